```python
import math
import jax
import jax.numpy as jnp
from jax import lax
import numpy as np

D_MODEL = 1024
BATCH = 4
SEQ = 4096
DEPTH = 1

D_MIX = D_MODEL
D_S5 = D_MIX // 2
S5_GROUP = 16
N_S5_GROUPS = D_S5 // S5_GROUP
S5_STATE = 64
D_GDN = D_MIX - D_S5
GDN_HEAD_DIM = 128
N_GDN_HEADS = D_GDN // GDN_HEAD_DIM
CONV_WIDTH = 4
CHUNK = 64
D_IN = D_S5 + 4 * D_GDN + 2 * N_GDN_HEADS
N_EXPERT_GROUPS = 4
EXPERTS_PER_GROUP = 8
N_EXPERTS = N_EXPERT_GROUPS * EXPERTS_PER_GROUP
TOP_K_IN_GROUP = 2
D_EXPERT = D_MODEL // 4
N_MOD = 6
EPS = 1e-6

kernel_name = 'hymba_s5_gdn_hmoe_adaln_block'


def rmsnorm(x, g):
    x32 = x.astype(jnp.float32)
    y = x32 * lax.rsqrt(jnp.mean(x32 * x32, axis=-1, keepdims=True) + EPS)
    return (y * g.astype(jnp.float32)).astype(x.dtype)


def l2norm(x):
    return x * lax.rsqrt(jnp.sum(x * x, axis=-1, keepdims=True) + EPS)


def modulate(h, shift, scale):
    return h * (1.0 + scale[:, None, :]) + shift[:, None, :]


def _linear_recurrence_combine(e1, e2):
    a1, b1 = e1
    a2, b2 = e2
    return a1 * a2, a2 * b1 + b2


def s5_mixer(u, lam_re, lam_im, log_step, b_re, b_im, c_re, c_im, d_skip, w_glu, b_glu):
    f32 = jnp.float32
    bsz, seq, _ = u.shape
    u32 = u.astype(f32)
    ug = u32.reshape(bsz, seq, N_S5_GROUPS, S5_GROUP)
    lam = lax.complex(lam_re.astype(f32), lam_im.astype(f32))
    step = jnp.exp(log_step.astype(f32))[:, None]
    lam_bar = jnp.exp(lam * step)
    b_c = lax.complex(b_re.astype(f32), b_im.astype(f32))
    b_bar = ((lam_bar - 1.0) / lam)[..., None] * b_c
    bu = lax.complex(jnp.einsum('gph,blgh->blgp', b_bar.real, ug),
                     jnp.einsum('gph,blgh->blgp', b_bar.imag, ug))
    a = jnp.broadcast_to(lam_bar, bu.shape)
    _, states = lax.associative_scan(_linear_recurrence_combine, (a, bu), axis=1)
    c_c = lax.complex(c_re.astype(f32), c_im.astype(f32))
    y = jnp.einsum('ghp,blgp->blgh', c_c, states).real.reshape(bsz, seq, D_S5)
    y = y + d_skip.astype(f32) * u32
    y = jax.nn.gelu(y)
    y = y * jax.nn.sigmoid(y @ w_glu.astype(f32) + b_glu.astype(f32))
    return y.astype(u.dtype)


def causal_depthwise_conv(x, w):
    k, ch = w.shape
    return lax.conv_general_dilated(x, w[:, None, :], window_strides=(1,),
                                    padding=[(k - 1, 0)],
                                    dimension_numbers=('NWC', 'WIO', 'NWC'),
                                    feature_group_count=ch)


def gated_delta_rule(q, k, v, g, beta):
    bsz, nh, seq, dk = q.shape
    dv = v.shape[-1]
    n_chunks = seq // CHUNK
    q = q.reshape(bsz, nh, n_chunks, CHUNK, dk)
    k = k.reshape(bsz, nh, n_chunks, CHUNK, dk)
    v = v.reshape(bsz, nh, n_chunks, CHUNK, dv)
    beta = beta.reshape(bsz, nh, n_chunks, CHUNK)
    g = jnp.cumsum(g.reshape(bsz, nh, n_chunks, CHUNK), axis=-1)
    idx = jnp.arange(CHUNK)
    causal = idx[:, None] >= idx[None, :]
    strict = idx[:, None] > idx[None, :]
    decay = jnp.exp(jnp.where(causal, g[..., :, None] - g[..., None, :], -jnp.inf))
    k_beta = k * beta[..., None]
    v_beta = v * beta[..., None]
    kkt = jnp.einsum('bhncd,bhnsd->bhncs', k_beta, k) * decay
    a_mat = jnp.where(strict, kkt, 0.0) + jnp.eye(CHUNK, dtype=q.dtype)
    rhs = jnp.concatenate([v_beta, k_beta * jnp.exp(g)[..., None]], axis=-1)
    sol = lax.linalg.triangular_solve(a_mat, rhs, left_side=True, lower=True,
                                      unit_diagonal=True)
    value, k_cumdecay = sol[..., :dv], sol[..., dv:]
    qk = jnp.einsum('bhncd,bhnsd->bhncs', q, k) * decay
    q_decay = q * jnp.exp(g)[..., None]
    k_tail = k * jnp.exp(g[..., -1:] - g)[..., None]
    g_last = jnp.exp(g[..., -1])

    def chunk_step(state, inp):
        qk_i, qd_i, kc_i, val_i, kt_i, gl_i = inp
        v_new = val_i - jnp.einsum('bhcd,bhde->bhce', kc_i, state)
        o_i = (jnp.einsum('bhcd,bhde->bhce', qd_i, state)
               + jnp.einsum('bhcs,bhse->bhce', qk_i, v_new))
        state = state * gl_i[..., None, None] + jnp.einsum('bhcd,bhce->bhde', kt_i, v_new)
        return state, o_i

    xs = tuple(jnp.moveaxis(t, 2, 0) for t in (qk, q_decay, k_cumdecay, value, k_tail, g_last))
    state0 = jnp.zeros((bsz, nh, dk, dv), q.dtype)
    _, o = lax.scan(chunk_step, state0, xs)
    return jnp.moveaxis(o, 0, 2).reshape(bsz, nh, seq, dv)


def gdn_mixer(qkv, z, a_gate, b_gate, conv_w, a_log, dt_bias, gdn_norm_g):
    f32 = jnp.float32
    bsz, seq, _ = qkv.shape
    qkv = jax.nn.silu(causal_depthwise_conv(qkv, conv_w)).astype(f32)
    q, k, v = jnp.split(qkv, 3, axis=-1)

    def heads(t):
        return t.reshape(bsz, seq, N_GDN_HEADS, GDN_HEAD_DIM).transpose(0, 2, 1, 3)

    q = l2norm(heads(q)) * (GDN_HEAD_DIM ** -0.5)
    k = l2norm(heads(k))
    v = heads(v)
    beta = jax.nn.sigmoid(b_gate.astype(f32)).transpose(0, 2, 1)
    g = (-jnp.exp(a_log.astype(f32))
         * jax.nn.softplus(a_gate.astype(f32) + dt_bias.astype(f32))).transpose(0, 2, 1)
    o = gated_delta_rule(q, k, v, g, beta).transpose(0, 2, 1, 3)
    zh = z.astype(f32).reshape(bsz, seq, N_GDN_HEADS, GDN_HEAD_DIM)
    o = rmsnorm(o, gdn_norm_g) * jax.nn.silu(zh)
    return o.reshape(bsz, seq, D_GDN).astype(z.dtype)


def hybrid_mixer(h, w_in, lam_re, lam_im, log_step, s5_b_re, s5_b_im, s5_c_re, s5_c_im,
                 s5_d, w_glu, b_glu, conv_w, a_log, dt_bias, gdn_norm_g, w_out):
    proj = h @ w_in
    splits = [D_S5, D_S5 + 3 * D_GDN, D_S5 + 4 * D_GDN, D_S5 + 4 * D_GDN + N_GDN_HEADS]
    u_s5, qkv, z, a_gate, b_gate = jnp.split(proj, splits, axis=-1)
    y_s5 = s5_mixer(u_s5, lam_re, lam_im, log_step, s5_b_re, s5_b_im, s5_c_re, s5_c_im,
                    s5_d, w_glu, b_glu)
    y_gdn = gdn_mixer(qkv, z, a_gate, b_gate, conv_w, a_log, dt_bias, gdn_norm_g)
    return jnp.concatenate([y_s5, y_gdn], axis=-1) @ w_out


def hier_moe(h, w_router_grp, w_router_exp, w_gate, w_up, w_down):
    f32 = jnp.float32
    bsz, seq, d = h.shape
    hf = h.reshape(bsz * seq, d)
    grp_logits = (hf @ w_router_grp).astype(f32)
    grp_prob = jax.nn.softmax(grp_logits, axis=-1)
    grp_idx = jnp.argmax(grp_logits, axis=-1)
    grp_onehot = jax.nn.one_hot(grp_idx, N_EXPERT_GROUPS, dtype=f32)
    p_grp = jnp.sum(grp_prob * grp_onehot, axis=-1)
    exp_logits = (hf @ w_router_exp).astype(f32).reshape(-1, N_EXPERT_GROUPS, EXPERTS_PER_GROUP)
    sel_logits = jnp.einsum('tg,tge->te', grp_onehot, exp_logits)
    top_val, top_idx = lax.top_k(sel_logits, TOP_K_IN_GROUP)
    w_k = jax.nn.softmax(top_val, axis=-1) * p_grp[:, None]
    expert_id = grp_idx[:, None] * EXPERTS_PER_GROUP + top_idx
    combine = jnp.einsum('tk,tke->te', w_k,
                         jax.nn.one_hot(expert_id, N_EXPERTS, dtype=f32)).astype(h.dtype)

    def expert_step(acc, ws):
        wg, wu, wd, gate_col = ws
        y = (jax.nn.silu(hf @ wg) * (hf @ wu)) @ wd
        return acc + gate_col[:, None] * y, None

    acc0 = jnp.zeros_like(hf)
    out, _ = lax.scan(expert_step, acc0, (w_gate, w_up, w_down, combine.T))
    return out.reshape(bsz, seq, d)


def setup_inputs(seed: int = 0) -> dict:
    key = jax.random.key(seed)
    ks = jax.random.split(key, 32)
    f32 = jnp.float32

    def nrm(k, shape, scale):
        return jax.random.normal(k, shape, f32) * scale

    nl = DEPTH
    gp = (nl, N_S5_GROUPS, S5_STATE)
    n_idx = jnp.arange(S5_STATE, dtype=f32)
    dt_init = jnp.exp(jax.random.uniform(ks[21], (nl, N_GDN_HEADS), f32,
                                         math.log(1e-3), math.log(1e-1)))
    return {
        'x': nrm(ks[0], (BATCH, SEQ, D_MODEL), 1.0),
        'c': nrm(ks[1], (BATCH, D_MODEL), 1.0),
        'norm1_g': 1.0 + nrm(ks[2], (nl, D_MODEL), 0.02),
        'norm2_g': 1.0 + nrm(ks[3], (nl, D_MODEL), 0.02),
        'w_ada': nrm(ks[4], (nl, D_MODEL, N_MOD * D_MODEL), 0.5 * D_MODEL ** -0.5),
        'b_ada': nrm(ks[5], (nl, N_MOD * D_MODEL), 0.02),
        'w_in': nrm(ks[6], (nl, D_MODEL, D_IN), D_MODEL ** -0.5),
        'lam_re': -0.5 + nrm(ks[7], gp, 0.01),
        'lam_im': math.pi * n_idx + nrm(ks[8], gp, 0.01),
        'log_step': jax.random.uniform(ks[9], (nl, N_S5_GROUPS), f32,
                                       math.log(1e-3), math.log(1e-1)),
        's5_b_re': nrm(ks[10], (nl, N_S5_GROUPS, S5_STATE, S5_GROUP), (2 * S5_GROUP) ** -0.5),
        's5_b_im': nrm(ks[11], (nl, N_S5_GROUPS, S5_STATE, S5_GROUP), (2 * S5_GROUP) ** -0.5),
        's5_c_re': nrm(ks[12], (nl, N_S5_GROUPS, S5_GROUP, S5_STATE), S5_STATE ** -0.5),
        's5_c_im': nrm(ks[13], (nl, N_S5_GROUPS, S5_GROUP, S5_STATE), S5_STATE ** -0.5),
        's5_d': nrm(ks[14], (nl, D_S5), 0.5),
        'w_glu': nrm(ks[15], (nl, D_S5, D_S5), D_S5 ** -0.5),
        'b_glu': nrm(ks[16], (nl, D_S5), 0.02),
        'conv_w': nrm(ks[17], (nl, CONV_WIDTH, 3 * D_GDN), CONV_WIDTH ** -0.5),
        'a_log': jnp.log(jax.random.uniform(ks[18], (nl, N_GDN_HEADS), f32, 1.0, 16.0)),
        'dt_bias': dt_init + jnp.log(-jnp.expm1(-dt_init)),
        'gdn_norm_g': 1.0 + nrm(ks[19], (nl, GDN_HEAD_DIM), 0.02),
        'w_out': nrm(ks[20], (nl, D_MIX, D_MODEL), D_MIX ** -0.5),
        'w_router_grp': nrm(ks[22], (nl, D_MODEL, N_EXPERT_GROUPS), D_MODEL ** -0.5),
        'w_router_exp': nrm(ks[23], (nl, D_MODEL, N_EXPERTS), D_MODEL ** -0.5),
        'w_gate': nrm(ks[24], (nl, N_EXPERTS, D_MODEL, D_EXPERT), D_MODEL ** -0.5),
        'w_up': nrm(ks[25], (nl, N_EXPERTS, D_MODEL, D_EXPERT), D_MODEL ** -0.5),
        'w_down': nrm(ks[26], (nl, N_EXPERTS, D_EXPERT, D_MODEL), D_EXPERT ** -0.5),
        'normf_g': 1.0 + nrm(ks[27], (D_MODEL,), 0.02),
    }


def reference(x, c, norm1_g, norm2_g, w_ada, b_ada, w_in, lam_re, lam_im, log_step,
              s5_b_re, s5_b_im, s5_c_re, s5_c_im, s5_d, w_glu, b_glu, conv_w, a_log,
              dt_bias, gdn_norm_g, w_out, w_router_grp, w_router_exp, w_gate, w_up,
              w_down, normf_g):
    h = x
    c_act = jax.nn.silu(c)
    for l in range(DEPTH):
        mod = c_act @ w_ada[l] + b_ada[l]
        sh1, sc1, gt1, sh2, sc2, gt2 = jnp.split(mod, N_MOD, axis=-1)
        a_in = modulate(rmsnorm(h, norm1_g[l]), sh1, sc1)
        mix = hybrid_mixer(a_in, w_in[l], lam_re[l], lam_im[l], log_step[l], s5_b_re[l],
                           s5_b_im[l], s5_c_re[l], s5_c_im[l], s5_d[l], w_glu[l], b_glu[l],
                           conv_w[l], a_log[l], dt_bias[l], gdn_norm_g[l], w_out[l])
        h = h + gt1[:, None, :] * mix
        m_in = modulate(rmsnorm(h, norm2_g[l]), sh2, sc2)
        moe = hier_moe(m_in, w_router_grp[l], w_router_exp[l], w_gate[l], w_up[l], w_down[l])
        h = h + gt2[:, None, :] * moe
    return rmsnorm(h, normf_g)
```

```python
import functools
import math

import jax
import jax.numpy as jnp
from jax import lax
from jax.experimental import pallas as pl
from jax.experimental.pallas import tpu as pltpu

F32 = jnp.float32
BF16 = jnp.bfloat16
HIGHEST = lax.Precision.HIGHEST
EPS = 1e-6

S5_GROUP = 16
S5_STATE = 64
GDN_HEAD_DIM = 128
CONV_WIDTH = 4
N_EXPERT_GROUPS = 4
EXPERTS_PER_GROUP = 8
N_EXPERTS = N_EXPERT_GROUPS * EXPERTS_PER_GROUP
N_MOD = 6

LANES = 128
SUBLANES = 8
VMEM_LIMIT = 56 * 1024 * 1024

S5_CHUNK = SUBLANES
S5_TILE = 2048
GDN_CHUNK = 64
ROW_TILE = 512
MOE_TILE = 1024
ROUTE_OFF = N_EXPERT_GROUPS


def _dot(a, b, **kw):
    return jnp.dot(a, b, preferred_element_type=F32, **kw)


def _silu(v):
    return v * jax.nn.sigmoid(v)


def _rms(v):
    return v * lax.rsqrt(jnp.mean(v * v, axis=-1, keepdims=True) + EPS)


def _ada_kernel(c_ref, w_ref, b_ref, o_ref):
    o_ref[...] = _dot(_silu(c_ref[...]), w_ref[...], precision=HIGHEST) + b_ref[...]


def _ada(c, w, b):
    bsz, d = c.shape
    n = w.shape[1]
    return pl.pallas_call(
        _ada_kernel,
        grid=(n // d,),
        in_specs=[pl.BlockSpec((bsz, d), lambda j: (0, 0)),
                  pl.BlockSpec((d, d), lambda j: (0, j)),
                  pl.BlockSpec((1, d), lambda j: (0, j))],
        out_specs=pl.BlockSpec((bsz, d), lambda j: (0, j)),
        out_shape=jax.ShapeDtypeStruct((bsz, n), F32),
        compiler_params=pltpu.CompilerParams(vmem_limit_bytes=VMEM_LIMIT),
        name="ada",
    )(c, w, b.reshape(1, n))


def _inproj_kernel(x_ref, mod_ref, g_ref, wu_ref, wqkv_ref, wz_ref, wab_ref,
                   u_ref, qkv_ref, z_ref, ab_ref):
    x = x_ref[0]
    y = _rms(x) * g_ref[...]
    h = (y * (1.0 + mod_ref[0, 1:2, :]) + mod_ref[0, 0:1, :]).astype(BF16)
    u_ref[0] = _dot(h, wu_ref[...])
    qkv_ref[0] = _dot(h, wqkv_ref[...]).astype(BF16)
    z_ref[0] = _dot(h, wz_ref[...]).astype(BF16)
    ab_ref[0] = _dot(h, wab_ref[...])


def _inproj(x, mod, g, wu, wqkv, wz, wab):
    bsz, seq, d = x.shape
    tm = ROW_TILE
    full = lambda a: pl.BlockSpec(a.shape, lambda b, i: (0,) * a.ndim)
    row = lambda n: pl.BlockSpec((1, tm, n), lambda b, i: (b, i, 0))
    return pl.pallas_call(
        _inproj_kernel,
        grid=(bsz, seq // tm),
        in_specs=[row(d), pl.BlockSpec((1, N_MOD, d), lambda b, i: (b, 0, 0)), full(g),
                  full(wu), full(wqkv), full(wz), full(wab)],
        out_specs=[row(wu.shape[1]), row(wqkv.shape[1]), row(wz.shape[1]), row(LANES)],
        out_shape=[jax.ShapeDtypeStruct((bsz, seq, wu.shape[1]), F32),
                   jax.ShapeDtypeStruct((bsz, seq, wqkv.shape[1]), BF16),
                   jax.ShapeDtypeStruct((bsz, seq, wz.shape[1]), BF16),
                   jax.ShapeDtypeStruct((bsz, seq, LANES), F32)],
        compiler_params=pltpu.CompilerParams(
            dimension_semantics=("parallel", "parallel"), vmem_limit_bytes=VMEM_LIMIT),
        name="inproj",
    )(x, mod, g, wu, wqkv, wz, wab)


def _s5_tables(lam_re, lam_im, log_step, b_re, b_im, c_re, c_im):
    ch = S5_CHUNK
    n_grp, n_st = lam_re.shape
    gpb = LANES // S5_GROUP
    nblk = n_grp // gpb
    step = jnp.exp(log_step.astype(F32))[:, None]
    lr = lam_re.astype(F32)
    li = lam_im.astype(F32)

    def lam_pow(j):
        mag = jnp.exp(j * lr * step)
        ang = j * li * step
        return mag * jnp.cos(ang), mag * jnp.sin(ang)

    ar, ai = lam_pow(1.0)
    den = lr * lr + li * li
    fr = ((ar - 1.0) * lr + ai * li) / den
    fi = (ai * lr - (ar - 1.0) * li) / den
    bbr = fr[..., None] * b_re - fi[..., None] * b_im
    bbi = fr[..., None] * b_im + fi[..., None] * b_re
    pows = [lam_pow(float(j)) for j in range(ch + 1)]
    pr = jnp.stack([p[0] for p in pows])
    pi = jnp.stack([p[1] for p in pows])
    lbr = pr[:ch, :, :, None] * bbr[None] - pi[:ch, :, :, None] * bbi[None]
    lbi = pr[:ch, :, :, None] * bbi[None] + pi[:ch, :, :, None] * bbr[None]
    eye_q = jnp.eye(gpb, dtype=F32)

    kmat = (jnp.einsum('ghp,jgpk->jghk', c_re, lbr, precision=HIGHEST)
            - jnp.einsum('ghp,jgpk->jghk', c_im, lbi, precision=HIGHEST))
    s_idx = jnp.arange(ch)[:, None]
    t_idx = jnp.arange(ch)[None, :]
    lag = jnp.clip(t_idx - s_idx, 0, ch - 1)
    kst = jnp.where((t_idx >= s_idx)[..., None, None, None], kmat[lag], 0.0)
    kst = kst.reshape(ch, ch, nblk, gpb, S5_GROUP, S5_GROUP)
    tmat = jnp.einsum('stcqhk,qr->csqktrh', kst, eye_q).reshape(nblk, ch * LANES, ch * LANES)

    wri = jnp.stack([lbr[::-1], lbi[::-1]]).reshape(2, ch, nblk, gpb, n_st, S5_GROUP)
    win = jnp.einsum('iscqpk,qr->csqkirp', wri, eye_q).reshape(nblk, ch * LANES, 2 * gpb * n_st)

    pr1 = pr[1:, :, None, :]
    pi1 = pi[1:, :, None, :]
    clr = c_re[None] * pr1 - c_im[None] * pi1
    cli = c_re[None] * pi1 + c_im[None] * pr1
    wo = jnp.stack([clr, -cli]).reshape(2, ch, nblk, gpb, S5_GROUP, n_st)
    wout = jnp.einsum('itcqhp,qr->ciqptrh', wo, eye_q).reshape(nblk, 2 * gpb * n_st, ch * LANES)

    amat = jnp.stack([pr[ch].reshape(nblk, gpb * n_st), pi[ch].reshape(nblk, gpb * n_st)], axis=1)
    return tmat.astype(BF16), win.astype(BF16), wout.astype(BF16), amat


def _s5_kernel(u_ref, t_ref, win_ref, wout_ref, a_ref, d_ref, y_ref, st_ref, v_ref, xp_ref):
    ch = S5_CHUNK
    n = u_ref.shape[1] // ch
    half = st_ref.shape[1]

    @pl.when(pl.program_id(2) == 0)
    def _():
        st_ref[...] = jnp.zeros_like(st_ref)

    slabs = [u_ref[0, pl.ds(s, n, stride=ch), :] for s in range(ch)]
    ucat = jnp.concatenate(slabs, axis=1).astype(BF16)
    y = _dot(ucat, t_ref[0])
    v_ref[...] = _dot(ucat, win_ref[0])
    a_r = a_ref[0, 0:1, :]
    a_i = a_ref[0, 1:2, :]

    def body(r, carry):
        x_r, x_i = carry
        xp_ref[pl.ds(r, 1), 0:half] = x_r
        xp_ref[pl.ds(r, 1), half:2 * half] = x_i
        v_r = v_ref[pl.ds(r, 1), 0:half]
        v_i = v_ref[pl.ds(r, 1), half:2 * half]
        return a_r * x_r - a_i * x_i + v_r, a_r * x_i + a_i * x_r + v_i

    x_r, x_i = lax.fori_loop(0, n, body, (st_ref[0:1, :], st_ref[1:2, :]))
    st_ref[0:1, :] = x_r
    st_ref[1:2, :] = x_i
    y = y + _dot(xp_ref[...].astype(BF16), wout_ref[0])
    d = d_ref[0]
    for t in range(ch):
        y_ref[0, pl.ds(t, n, stride=ch), :] = y[:, t * LANES:(t + 1) * LANES] + d * slabs[t]


def _s5(u, tmat, win, wout, amat, d_skip):
    bsz, seq, dch = u.shape
    nblk = dch // LANES
    tt = min(S5_TILE, seq)
    n = tt // S5_CHUNK
    kdim = S5_CHUNK * LANES
    sdim = win.shape[2]
    wspec = lambda a: pl.BlockSpec((1,) + a.shape[1:], lambda b, c, i: (c, 0, 0))
    return pl.pallas_call(
        _s5_kernel,
        grid=(bsz, nblk, seq // tt),
        in_specs=[pl.BlockSpec((1, tt, LANES), lambda b, c, i: (b, i, c)),
                  wspec(tmat), wspec(win), wspec(wout), wspec(amat),
                  pl.BlockSpec((1, 1, LANES), lambda b, c, i: (c, 0, 0))],
        out_specs=pl.BlockSpec((1, tt, LANES), lambda b, c, i: (b, i, c)),
        out_shape=jax.ShapeDtypeStruct((bsz, seq, dch), F32),
        scratch_shapes=[pltpu.VMEM((2, sdim // 2), F32),
                        pltpu.VMEM((n, sdim), F32),
                        pltpu.VMEM((n, sdim), F32)],
        compiler_params=pltpu.CompilerParams(
            dimension_semantics=("parallel", "parallel", "arbitrary"),
            vmem_limit_bytes=VMEM_LIMIT),
        name="s5",
    )(u, tmat, win, wout, amat, d_skip.reshape(nblk, 1, LANES))


def _cumsum_rows(v):
    n = v.shape[0]
    row = lax.broadcasted_iota(jnp.int32, v.shape, 0)
    sh = 1
    while sh < n:
        v = v + jnp.where(row >= sh, pltpu.roll(v, sh, axis=0), 0.0)
        sh *= 2
    return v


def _unit_lower_inverse(lm):
    n = lm.shape[0]
    ri = lax.broadcasted_iota(jnp.int32, (n, n), 0)
    ci = lax.broadcasted_iota(jnp.int32, (n, n), 1)
    eye = jnp.where(ri == ci, 1.0, 0.0).astype(F32)
    m = 1
    d = eye
    while m < n:
        mask = ((ri // (2 * m)) == (ci // (2 * m))) & (((ri // m) % 2) == 1) & (((ci // m) % 2) == 0)
        e = jnp.where(mask, lm, 0.0)
        if m == 1:
            d = eye - e
        else:
            de = _dot(d.astype(BF16), e.astype(BF16))
            d = d - _dot(de.astype(BF16), d.astype(BF16))
        m *= 2
    return d


def _gdn_kernel(qkv_ref, z_ref, ab_ref, cw_ref, gp_ref, ng_ref, o_ref, s_ref, xp_ref, *, n_heads):
    ck = GDN_CHUNK
    dh = GDN_HEAD_DIM
    bsz = qkv_ref.shape[0]
    dg = n_heads * dh
    halo = SUBLANES

    @pl.when(pl.program_id(0) == 0)
    def _():
        s_ref[...] = jnp.zeros_like(s_ref)
        xp_ref[:, 0:halo, :] = jnp.zeros((bsz, halo, 3 * dg), F32)

    ri = lax.broadcasted_iota(jnp.int32, (ck, ck), 0)
    ci = lax.broadcasted_iota(jnp.int32, (ck, ck), 1)
    causal = ri >= ci
    strict = ri > ci
    lane = lax.broadcasted_iota(jnp.int32, (ck, LANES), 1)
    a_log = gp_ref[0:1, :]
    dt_bias = gp_ref[1:2, :]
    ng = ng_ref[...]

    for b in range(bsz):
        xp_ref[b, halo:halo + ck, :] = qkv_ref[b].astype(F32)
        conv = cw_ref[0:1, :] * xp_ref[b, halo - 3:halo - 3 + ck, :]
        for j in range(1, CONV_WIDTH):
            conv = conv + cw_ref[j:j + 1, :] * xp_ref[b, halo - 3 + j:halo - 3 + j + ck, :]
        xp_ref[b, 0:halo, :] = xp_ref[b, ck:ck + halo, :]
        act = _silu(conv)

        ab = ab_ref[b]
        sp = jnp.maximum(ab + dt_bias, 0.0) + jnp.log1p(jnp.exp(-jnp.abs(ab + dt_bias)))
        g = jnp.where(lane < n_heads, -jnp.exp(a_log) * sp, 0.0)
        gc = _cumsum_rows(g)
        gct = gc.T
        beta_all = jax.nn.sigmoid(ab)
        zb = z_ref[b].astype(F32)

        for h in range(n_heads):
            q = act[:, h * dh:(h + 1) * dh]
            k = act[:, dg + h * dh:dg + (h + 1) * dh]
            v = act[:, 2 * dg + h * dh:2 * dg + (h + 1) * dh]
            q = q * lax.rsqrt(jnp.sum(q * q, axis=-1, keepdims=True) + EPS) * (dh ** -0.5)
            k = k * lax.rsqrt(jnp.sum(k * k, axis=-1, keepdims=True) + EPS)
            beta = beta_all[:, n_heads + h:n_heads + h + 1]
            gcol = gc[:, h:h + 1]
            grow = gct[h:h + 1, :]
            glast = gc[ck - 1:ck, h:h + 1]
            decay = jnp.exp(jnp.where(causal, gcol - grow, -1e30))
            kb = k * beta
            kq = jnp.concatenate([kb, q], axis=0).astype(BF16)
            m1 = lax.dot_general(kq, k.astype(BF16), (((1,), (1,)), ((), ())),
                                 preferred_element_type=F32)
            lm = jnp.where(strict, m1[0:ck] * decay, 0.0)
            qk = m1[ck:2 * ck] * decay
            tinv = _unit_lower_inverse(lm)
            eg = jnp.exp(gcol)
            rhs = jnp.concatenate([v * beta, kb * eg], axis=1).astype(BF16)
            w = _dot(tinv.astype(BF16), rhs)
            value = w[:, 0:dh]
            kcd = w[:, dh:2 * dh]
            qd = q * eg
            kt = k * jnp.exp(glast - gcol)
            st = s_ref[b * n_heads + h]
            m2 = _dot(jnp.concatenate([kcd, qd], axis=0).astype(BF16), st.astype(BF16))
            v_new = value - m2[0:ck]
            vnb = v_new.astype(BF16)
            o = m2[ck:2 * ck] + _dot(qk.astype(BF16), vnb)
            s_ref[b * n_heads + h] = st * jnp.exp(glast) + lax.dot_general(
                kt.astype(BF16), vnb, (((0,), (0,)), ((), ())), preferred_element_type=F32)
            zh = zb[:, h * dh:(h + 1) * dh]
            o_ref[b, :, h * dh:(h + 1) * dh] = (_rms(o) * ng * _silu(zh)).astype(o_ref.dtype)


def _gdn(qkv, z, ab, conv_w, a_log, dt_bias, norm_g):
    bsz, seq, _ = qkv.shape
    dg = z.shape[2]
    n_heads = dg // GDN_HEAD_DIM
    ck = GDN_CHUNK
    gp = jnp.zeros((2, LANES), F32).at[0, :n_heads].set(a_log).at[1, :n_heads].set(dt_bias)
    blk = lambda n: pl.BlockSpec((bsz, ck, n), lambda i: (0, i, 0))
    full = lambda a: pl.BlockSpec(a.shape, lambda i: (0,) * a.ndim)
    ng = norm_g.reshape(1, GDN_HEAD_DIM)
    return pl.pallas_call(
        functools.partial(_gdn_kernel, n_heads=n_heads),
        grid=(seq // ck,),
        in_specs=[blk(3 * dg), blk(dg), blk(LANES), full(conv_w), full(gp), full(ng)],
        out_specs=blk(dg),
        out_shape=jax.ShapeDtypeStruct((bsz, seq, dg), BF16),
        scratch_shapes=[pltpu.VMEM((bsz * n_heads, GDN_HEAD_DIM, GDN_HEAD_DIM), F32),
                        pltpu.VMEM((bsz, ck + SUBLANES, 3 * dg), F32)],
        compiler_params=pltpu.CompilerParams(
            dimension_semantics=("arbitrary",), vmem_limit_bytes=VMEM_LIMIT),
        name="gdn",
    )(qkv, z, ab, conv_w, gp, ng)


def _outproj_kernel(ys_ref, yg_ref, x_ref, mod_ref, wglu_ref, bglu_ref, wos_ref, wog_ref,
                    g2_ref, wr_ref, h_ref, m_ref, comb_ref):
    y = jax.nn.gelu(ys_ref[0])
    gate = jax.nn.sigmoid(_dot(y.astype(BF16), wglu_ref[...]) + bglu_ref[...])
    y = (y * gate).astype(BF16)
    mix = _dot(y, wos_ref[...]) + _dot(yg_ref[0], wog_ref[...])
    h = x_ref[0] + mod_ref[0, 2:3, :] * mix
    h_ref[0] = h
    m = _rms(h) * g2_ref[...] * (1.0 + mod_ref[0, 4:5, :]) + mod_ref[0, 3:4, :]
    m_ref[0] = m.astype(BF16)

    logits = _dot(m, wr_ref[...], precision=HIGHEST)
    lane = lax.broadcasted_iota(jnp.int32, logits.shape, 1)
    neg = -1e30
    is_grp = lane < N_EXPERT_GROUPS
    gl = jnp.where(is_grp, logits, neg)
    gmax = jnp.max(gl, axis=-1, keepdims=True)
    gidx = jnp.min(jnp.where(gl == gmax, lane, LANES), axis=-1, keepdims=True)
    p_grp = 1.0 / jnp.sum(jnp.where(is_grp, jnp.exp(gl - gmax), 0.0), axis=-1, keepdims=True)
    lo = ROUTE_OFF + gidx * EXPERTS_PER_GROUP
    el = jnp.where((lane >= lo) & (lane < lo + EXPERTS_PER_GROUP), logits, neg)
    v1 = jnp.max(el, axis=-1, keepdims=True)
    i1 = jnp.min(jnp.where(el == v1, lane, LANES), axis=-1, keepdims=True)
    el2 = jnp.where(lane == i1, neg, el)
    v2 = jnp.max(el2, axis=-1, keepdims=True)
    i2 = jnp.min(jnp.where(el2 == v2, lane, LANES), axis=-1, keepdims=True)
    t = jnp.exp(v2 - v1)
    w1 = p_grp / (1.0 + t)
    w2 = w1 * t
    comb_ref[0] = jnp.where(lane == i1, w1, 0.0) + jnp.where(lane == i2, w2, 0.0)


def _outproj(ys, yg, x, mod, wglu, bglu, wos, wog, g2, wr):
    bsz, seq, d = x.shape
    tm = ROW_TILE
    full = lambda a: pl.BlockSpec(a.shape, lambda b, i: (0,) * a.ndim)
    row = lambda n: pl.BlockSpec((1, tm, n), lambda b, i: (b, i, 0))
    return pl.pallas_call(
        _outproj_kernel,
        grid=(bsz, seq // tm),
        in_specs=[row(ys.shape[2]), row(yg.shape[2]), row(d),
                  pl.BlockSpec((1, N_MOD, d), lambda b, i: (b, 0, 0)),
                  full(wglu), full(bglu), full(wos), full(wog), full(g2), full(wr)],
        out_specs=[row(d), row(d), row(LANES)],
        out_shape=[jax.ShapeDtypeStruct((bsz, seq, d), F32),
                   jax.ShapeDtypeStruct((bsz, seq, d), BF16),
                   jax.ShapeDtypeStruct((bsz, seq, LANES), F32)],
        compiler_params=pltpu.CompilerParams(
            dimension_semantics=("parallel", "parallel"), vmem_limit_bytes=VMEM_LIMIT),
        name="outproj",
    )(ys, yg, x, mod, wglu, bglu, wos, wog, g2, wr)


def _moe_kernel(m_ref, comb_ref, wg_ref, wu_ref, wd_ref, h_ref, mod_ref, nf_ref, o_ref, acc_ref,
                *, final_norm):
    e = pl.program_id(2)

    @pl.when(e == 0)
    def _():
        acc_ref[...] = jnp.zeros_like(acc_ref)

    m = m_ref[0]
    mid = (_silu(_dot(m, wg_ref[0])) * _dot(m, wu_ref[0])).astype(BF16)
    y = _dot(mid, wd_ref[0])
    comb = comb_ref[0]
    lane = lax.broadcasted_iota(jnp.int32, comb.shape, 1)
    col = jnp.sum(jnp.where(lane == e + ROUTE_OFF, comb, 0.0), axis=-1, keepdims=True)
    acc_ref[...] += col * y

    @pl.when(e == pl.num_programs(2) - 1)
    def _():
        h = h_ref[0] + mod_ref[0, 5:6, :] * acc_ref[...]
        if final_norm:
            h = _rms(h) * nf_ref[...]
        o_ref[0] = h


def _moe(m, comb, wg, wu, wd, h, mod, nf, final_norm):
    bsz, seq, d = h.shape
    tm = min(MOE_TILE, seq)
    n_exp, _, de = wg.shape
    row = lambda n: pl.BlockSpec((1, tm, n), lambda b, i, e: (b, i, 0))
    return pl.pallas_call(
        functools.partial(_moe_kernel, final_norm=final_norm),
        grid=(bsz, seq // tm, n_exp),
        in_specs=[row(d), row(LANES),
                  pl.BlockSpec((1, d, de), lambda b, i, e: (e, 0, 0)),
                  pl.BlockSpec((1, d, de), lambda b, i, e: (e, 0, 0)),
                  pl.BlockSpec((1, de, d), lambda b, i, e: (e, 0, 0)),
                  row(d), pl.BlockSpec((1, N_MOD, d), lambda b, i, e: (b, 0, 0)),
                  pl.BlockSpec(nf.shape, lambda b, i, e: (0, 0))],
        out_specs=row(d),
        out_shape=jax.ShapeDtypeStruct((bsz, seq, d), F32),
        scratch_shapes=[pltpu.VMEM((tm, d), F32)],
        compiler_params=pltpu.CompilerParams(
            dimension_semantics=("parallel", "parallel", "arbitrary"),
            vmem_limit_bytes=VMEM_LIMIT),
        name="moe",
    )(m, comb, wg, wu, wd, h, mod, nf)


def kernel(x, c, norm1_g, norm2_g, w_ada, b_ada, w_in, lam_re, lam_im, log_step, s5_b_re, s5_b_im,
           s5_c_re, s5_c_im, s5_d, w_glu, b_glu, conv_w, a_log, dt_bias, gdn_norm_g, w_out,
           w_router_grp, w_router_exp, w_gate, w_up, w_down, normf_g):
    bsz, seq, d = x.shape
    depth = w_ada.shape[0]
    d_s5 = s5_d.shape[1]
    d_gdn = w_out.shape[1] - d_s5
    n_heads = d_gdn // GDN_HEAD_DIM
    assert seq % S5_TILE == 0 or seq < S5_TILE
    assert seq % ROW_TILE == 0 and seq % GDN_CHUNK == 0 and d_s5 % LANES == 0
    assert 2 * n_heads <= LANES and N_EXPERT_GROUPS + N_EXPERTS <= LANES

    h = x
    for l in range(depth):
        mod = _ada(c, w_ada[l], b_ada[l]).reshape(bsz, N_MOD, d)
        wi = w_in[l]
        c0, c1, c2 = d_s5, d_s5 + 3 * d_gdn, d_s5 + 4 * d_gdn
        wab = jnp.zeros((d, LANES), F32).at[:, :2 * n_heads].set(wi[:, c2:c2 + 2 * n_heads])
        u, qkv, z, ab = _inproj(h, mod, norm1_g[l].reshape(1, d), wi[:, :c0].astype(BF16),
                                wi[:, c0:c1].astype(BF16), wi[:, c1:c2].astype(BF16), wab.astype(BF16))
        tabs = _s5_tables(lam_re[l], lam_im[l], log_step[l], s5_b_re[l], s5_b_im[l],
                          s5_c_re[l], s5_c_im[l])
        ys = _s5(u, *tabs, s5_d[l])
        yg = _gdn(qkv, z, ab, conv_w[l], a_log[l], dt_bias[l], gdn_norm_g[l])
        wr = jnp.zeros((d, LANES), F32)
        wr = wr.at[:, :N_EXPERT_GROUPS].set(w_router_grp[l])
        wr = wr.at[:, ROUTE_OFF:ROUTE_OFF + N_EXPERTS].set(w_router_exp[l])
        h1, m, comb = _outproj(ys, yg, h, mod, w_glu[l].astype(BF16), b_glu[l].reshape(1, d_s5),
                               w_out[l, :d_s5].astype(BF16), w_out[l, d_s5:].astype(BF16),
                               norm2_g[l].reshape(1, d), wr)
        h = _moe(m, comb, w_gate[l].astype(BF16), w_up[l].astype(BF16), w_down[l].astype(BF16),
                 h1, mod, normf_g.reshape(1, d), final_norm=(l == depth - 1))
    return h
```

```python
import functools
import math

import jax
import jax.numpy as jnp
from jax import lax
from jax.experimental import pallas as pl
from jax.experimental.pallas import tpu as pltpu

F32 = jnp.float32
BF16 = jnp.bfloat16
HIGHEST = lax.Precision.HIGHEST
EPS = 1e-6

S5_GROUP = 16
S5_STATE = 64
GDN_HEAD_DIM = 128
CONV_WIDTH = 4
N_EXPERT_GROUPS = 4
EXPERTS_PER_GROUP = 8
N_EXPERTS = N_EXPERT_GROUPS * EXPERTS_PER_GROUP
N_MOD = 6

LANES = 128
SUBLANES = 8
VMEM_LIMIT = 56 * 1024 * 1024

S5_CHUNK = SUBLANES
S5_TILE = 2048
GDN_CHUNK = 64
GDN_BLOCK = 256
ROW_TILE = 512
MOE_TILE = 1024
ROUTE_OFF = N_EXPERT_GROUPS


def _dot(a, b, **kw):
    return jnp.dot(a, b, preferred_element_type=F32, **kw)


def _silu(v):
    return v * jax.nn.sigmoid(v)


def _rms(v):
    return v * lax.rsqrt(jnp.mean(v * v, axis=-1, keepdims=True) + EPS)


def _ada_kernel(c_ref, w_ref, b_ref, o_ref):
    o_ref[...] = _dot(_silu(c_ref[...]), w_ref[...], precision=HIGHEST) + b_ref[...]


def _ada(c, w, b):
    bsz, d = c.shape
    n = w.shape[1]
    return pl.pallas_call(
        _ada_kernel,
        grid=(n // d,),
        in_specs=[pl.BlockSpec((bsz, d), lambda j: (0, 0)),
                  pl.BlockSpec((d, d), lambda j: (0, j)),
                  pl.BlockSpec((1, d), lambda j: (0, j))],
        out_specs=pl.BlockSpec((bsz, d), lambda j: (0, j)),
        out_shape=jax.ShapeDtypeStruct((bsz, n), F32),
        compiler_params=pltpu.CompilerParams(vmem_limit_bytes=VMEM_LIMIT),
        name="ada",
    )(c, w, b.reshape(1, n))


def _inproj_kernel(x_ref, mod_ref, g_ref, wu_ref, wqkv_ref, wz_ref, wab_ref,
                   u_ref, qkv_ref, z_ref, ab_ref):
    x = x_ref[0]
    y = _rms(x) * g_ref[...]
    h = (y * (1.0 + mod_ref[0, 1:2, :]) + mod_ref[0, 0:1, :]).astype(BF16)
    u_ref[0] = _dot(h, wu_ref[...])
    qkv_ref[0] = _dot(h, wqkv_ref[...]).astype(BF16)
    z_ref[0] = _dot(h, wz_ref[...]).astype(BF16)
    ab_ref[0] = _dot(h, wab_ref[...])


def _inproj(x, mod, g, wu, wqkv, wz, wab):
    bsz, seq, d = x.shape
    tm = ROW_TILE
    full = lambda a: pl.BlockSpec(a.shape, lambda b, i: (0,) * a.ndim)
    row = lambda n: pl.BlockSpec((1, tm, n), lambda b, i: (b, i, 0))
    return pl.pallas_call(
        _inproj_kernel,
        grid=(bsz, seq // tm),
        in_specs=[row(d), pl.BlockSpec((1, N_MOD, d), lambda b, i: (b, 0, 0)), full(g),
                  full(wu), full(wqkv), full(wz), full(wab)],
        out_specs=[row(wu.shape[1]), row(wqkv.shape[1]), row(wz.shape[1]), row(LANES)],
        out_shape=[jax.ShapeDtypeStruct((bsz, seq, wu.shape[1]), F32),
                   jax.ShapeDtypeStruct((bsz, seq, wqkv.shape[1]), BF16),
                   jax.ShapeDtypeStruct((bsz, seq, wz.shape[1]), BF16),
                   jax.ShapeDtypeStruct((bsz, seq, LANES), F32)],
        compiler_params=pltpu.CompilerParams(
            dimension_semantics=("parallel", "parallel"), vmem_limit_bytes=VMEM_LIMIT),
        name="inproj",
    )(x, mod, g, wu, wqkv, wz, wab)


def _s5_tables(lam_re, lam_im, log_step, b_re, b_im, c_re, c_im):
    ch = S5_CHUNK
    n_grp, n_st = lam_re.shape
    gpb = LANES // S5_GROUP
    nblk = n_grp // gpb
    step = jnp.exp(log_step.astype(F32))[:, None]
    lr = lam_re.astype(F32)
    li = lam_im.astype(F32)

    def lam_pow(j):
        mag = jnp.exp(j * lr * step)
        ang = j * li * step
        return mag * jnp.cos(ang), mag * jnp.sin(ang)

    ar, ai = lam_pow(1.0)
    den = lr * lr + li * li
    fr = ((ar - 1.0) * lr + ai * li) / den
    fi = (ai * lr - (ar - 1.0) * li) / den
    bbr = fr[..., None] * b_re - fi[..., None] * b_im
    bbi = fr[..., None] * b_im + fi[..., None] * b_re
    pows = [lam_pow(float(j)) for j in range(ch + 1)]
    pr = jnp.stack([p[0] for p in pows])
    pi = jnp.stack([p[1] for p in pows])
    lbr = pr[:ch, :, :, None] * bbr[None] - pi[:ch, :, :, None] * bbi[None]
    lbi = pr[:ch, :, :, None] * bbi[None] + pi[:ch, :, :, None] * bbr[None]
    eye_q = jnp.eye(gpb, dtype=F32)

    kmat = (jnp.einsum('ghp,jgpk->jghk', c_re, lbr, precision=HIGHEST)
            - jnp.einsum('ghp,jgpk->jghk', c_im, lbi, precision=HIGHEST))
    s_idx = jnp.arange(ch)[:, None]
    t_idx = jnp.arange(ch)[None, :]
    lag = jnp.clip(t_idx - s_idx, 0, ch - 1)
    kst = jnp.where((t_idx >= s_idx)[..., None, None, None], kmat[lag], 0.0)
    kst = kst.reshape(ch, ch, nblk, gpb, S5_GROUP, S5_GROUP)
    tmat = jnp.einsum('stcqhk,qr->csqktrh', kst, eye_q).reshape(nblk, ch * LANES, ch * LANES)

    wri = jnp.stack([lbr[::-1], lbi[::-1]]).reshape(2, ch, nblk, gpb, n_st, S5_GROUP)
    win = jnp.einsum('iscqpk,qr->csqkirp', wri, eye_q).reshape(nblk, ch * LANES, 2 * gpb * n_st)

    pr1 = pr[1:, :, None, :]
    pi1 = pi[1:, :, None, :]
    clr = c_re[None] * pr1 - c_im[None] * pi1
    cli = c_re[None] * pi1 + c_im[None] * pr1
    wo = jnp.stack([clr, -cli]).reshape(2, ch, nblk, gpb, S5_GROUP, n_st)
    wout = jnp.einsum('itcqhp,qr->ciqptrh', wo, eye_q).reshape(nblk, 2 * gpb * n_st, ch * LANES)

    amat = jnp.stack([pr[ch].reshape(nblk, gpb * n_st), pi[ch].reshape(nblk, gpb * n_st)], axis=1)
    return tmat.astype(BF16), win.astype(BF16), wout.astype(BF16), amat


def _s5_kernel(u_ref, t_ref, win_ref, wout_ref, a_ref, d_ref, y_ref, st_ref, v_ref, xp_ref):
    ch = S5_CHUNK
    n = u_ref.shape[1] // ch
    half = st_ref.shape[1]

    @pl.when(pl.program_id(2) == 0)
    def _():
        st_ref[...] = jnp.zeros_like(st_ref)

    slabs = [u_ref[0, pl.ds(s, n, stride=ch), :] for s in range(ch)]
    ucat = jnp.concatenate(slabs, axis=1).astype(BF16)
    y = _dot(ucat, t_ref[0])
    v_ref[...] = _dot(ucat, win_ref[0])
    a_r = a_ref[0, 0:1, :]
    a_i = a_ref[0, 1:2, :]

    def body(r, carry):
        x_r, x_i = carry
        xp_ref[pl.ds(r, 1), 0:half] = x_r
        xp_ref[pl.ds(r, 1), half:2 * half] = x_i
        v_r = v_ref[pl.ds(r, 1), 0:half]
        v_i = v_ref[pl.ds(r, 1), half:2 * half]
        return a_r * x_r - a_i * x_i + v_r, a_r * x_i + a_i * x_r + v_i

    x_r, x_i = lax.fori_loop(0, n, body, (st_ref[0:1, :], st_ref[1:2, :]))
    st_ref[0:1, :] = x_r
    st_ref[1:2, :] = x_i
    y = y + _dot(xp_ref[...].astype(BF16), wout_ref[0])
    d = d_ref[0]
    for t in range(ch):
        y_ref[0, pl.ds(t, n, stride=ch), :] = y[:, t * LANES:(t + 1) * LANES] + d * slabs[t]


def _s5(u, tmat, win, wout, amat, d_skip):
    bsz, seq, dch = u.shape
    nblk = dch // LANES
    tt = min(S5_TILE, seq)
    n = tt // S5_CHUNK
    kdim = S5_CHUNK * LANES
    sdim = win.shape[2]
    wspec = lambda a: pl.BlockSpec((1,) + a.shape[1:], lambda b, c, i: (c, 0, 0))
    return pl.pallas_call(
        _s5_kernel,
        grid=(bsz, nblk, seq // tt),
        in_specs=[pl.BlockSpec((1, tt, LANES), lambda b, c, i: (b, i, c)),
                  wspec(tmat), wspec(win), wspec(wout), wspec(amat),
                  pl.BlockSpec((1, 1, LANES), lambda b, c, i: (c, 0, 0))],
        out_specs=pl.BlockSpec((1, tt, LANES), lambda b, c, i: (b, i, c)),
        out_shape=jax.ShapeDtypeStruct((bsz, seq, dch), F32),
        scratch_shapes=[pltpu.VMEM((2, sdim // 2), F32),
                        pltpu.VMEM((n, sdim), F32),
                        pltpu.VMEM((n, sdim), F32)],
        compiler_params=pltpu.CompilerParams(
            dimension_semantics=("parallel", "parallel", "arbitrary"),
            vmem_limit_bytes=VMEM_LIMIT),
        name="s5",
    )(u, tmat, win, wout, amat, d_skip.reshape(nblk, 1, LANES))


def _cumsum_rows(v):
    n = v.shape[0]
    row = lax.broadcasted_iota(jnp.int32, v.shape, 0)
    sh = 1
    while sh < n:
        v = v + jnp.where(row >= sh, pltpu.roll(v, sh, axis=0), 0.0)
        sh *= 2
    return v


def _unit_lower_inverse(lm):
    n = lm.shape[0]
    ri = lax.broadcasted_iota(jnp.int32, (n, n), 0)
    ci = lax.broadcasted_iota(jnp.int32, (n, n), 1)
    eye = jnp.where(ri == ci, 1.0, 0.0).astype(F32)
    m = 1
    d = eye
    while m < n:
        mask = ((ri // (2 * m)) == (ci // (2 * m))) & (((ri // m) % 2) == 1) & (((ci // m) % 2) == 0)
        e = jnp.where(mask, lm, 0.0)
        if m == 1:
            d = eye - e
        else:
            de = _dot(d.astype(BF16), e.astype(BF16))
            d = d - _dot(de.astype(BF16), d.astype(BF16))
        m *= 2
    return d


def _gdn_kernel(qkv_ref, z_ref, ab_ref, cw_ref, gp_ref, ng_ref, o_ref, s_ref, xp_ref, *, n_heads):
    ck = GDN_CHUNK
    dh = GDN_HEAD_DIM
    bsz = qkv_ref.shape[0]
    dg = n_heads * dh
    halo = SUBLANES

    @pl.when(pl.program_id(0) == 0)
    def _():
        s_ref[...] = jnp.zeros_like(s_ref)
        xp_ref[:, 0:halo, :] = jnp.zeros((bsz, halo, 3 * dg), F32)

    ri = lax.broadcasted_iota(jnp.int32, (ck, ck), 0)
    ci = lax.broadcasted_iota(jnp.int32, (ck, ck), 1)
    causal = ri >= ci
    strict = ri > ci
    lane = lax.broadcasted_iota(jnp.int32, (ck, LANES), 1)
    a_log = gp_ref[0:1, :]
    dt_bias = gp_ref[1:2, :]
    ng = ng_ref[...]

    for b in range(bsz):
        xp_ref[b, halo:halo + ck, :] = qkv_ref[b].astype(F32)
        conv = cw_ref[0:1, :] * xp_ref[b, halo - 3:halo - 3 + ck, :]
        for j in range(1, CONV_WIDTH):
            conv = conv + cw_ref[j:j + 1, :] * xp_ref[b, halo - 3 + j:halo - 3 + j + ck, :]
        xp_ref[b, 0:halo, :] = xp_ref[b, ck:ck + halo, :]
        act = _silu(conv)

        ab = ab_ref[b]
        sp = jnp.maximum(ab + dt_bias, 0.0) + jnp.log1p(jnp.exp(-jnp.abs(ab + dt_bias)))
        g = jnp.where(lane < n_heads, -jnp.exp(a_log) * sp, 0.0)
        gc = _cumsum_rows(g)
        gct = gc.T
        beta_all = jax.nn.sigmoid(ab)
        zb = z_ref[b].astype(F32)

        for h in range(n_heads):
            q = act[:, h * dh:(h + 1) * dh]
            k = act[:, dg + h * dh:dg + (h + 1) * dh]
            v = act[:, 2 * dg + h * dh:2 * dg + (h + 1) * dh]
            q = q * lax.rsqrt(jnp.sum(q * q, axis=-1, keepdims=True) + EPS) * (dh ** -0.5)
            k = k * lax.rsqrt(jnp.sum(k * k, axis=-1, keepdims=True) + EPS)
            beta = beta_all[:, n_heads + h:n_heads + h + 1]
            gcol = gc[:, h:h + 1]
            grow = gct[h:h + 1, :]
            glast = gc[ck - 1:ck, h:h + 1]
            decay = jnp.exp(jnp.where(causal, gcol - grow, -1e30))
            kb = k * beta
            kq = jnp.concatenate([kb, q], axis=0).astype(BF16)
            m1 = lax.dot_general(kq, k.astype(BF16), (((1,), (1,)), ((), ())),
                                 preferred_element_type=F32)
            lm = jnp.where(strict, m1[0:ck] * decay, 0.0)
            qk = m1[ck:2 * ck] * decay
            tinv = _unit_lower_inverse(lm)
            eg = jnp.exp(gcol)
            rhs = jnp.concatenate([v * beta, kb * eg], axis=1).astype(BF16)
            w = _dot(tinv.astype(BF16), rhs)
            value = w[:, 0:dh]
            kcd = w[:, dh:2 * dh]
            qd = q * eg
            kt = k * jnp.exp(glast - gcol)
            st = s_ref[b * n_heads + h]
            m2 = _dot(jnp.concatenate([kcd, qd], axis=0).astype(BF16), st.astype(BF16))
            v_new = value - m2[0:ck]
            vnb = v_new.astype(BF16)
            o = m2[ck:2 * ck] + _dot(qk.astype(BF16), vnb)
            s_ref[b * n_heads + h] = st * jnp.exp(glast) + lax.dot_general(
                kt.astype(BF16), vnb, (((0,), (0,)), ((), ())), preferred_element_type=F32)
            zh = zb[:, h * dh:(h + 1) * dh]
            o_ref[b, :, h * dh:(h + 1) * dh] = (_rms(o) * ng * _silu(zh)).astype(o_ref.dtype)


def _gdn(qkv, z, ab, conv_w, a_log, dt_bias, norm_g):
    bsz, seq, _ = qkv.shape
    dg = z.shape[2]
    n_heads = dg // GDN_HEAD_DIM
    ck = GDN_CHUNK
    gp = jnp.zeros((2, LANES), F32).at[0, :n_heads].set(a_log).at[1, :n_heads].set(dt_bias)
    blk = lambda n: pl.BlockSpec((bsz, ck, n), lambda i: (0, i, 0))
    full = lambda a: pl.BlockSpec(a.shape, lambda i: (0,) * a.ndim)
    ng = norm_g.reshape(1, GDN_HEAD_DIM)
    return pl.pallas_call(
        functools.partial(_gdn_kernel, n_heads=n_heads),
        grid=(seq // ck,),
        in_specs=[blk(3 * dg), blk(dg), blk(LANES), full(conv_w), full(gp), full(ng)],
        out_specs=blk(dg),
        out_shape=jax.ShapeDtypeStruct((bsz, seq, dg), BF16),
        scratch_shapes=[pltpu.VMEM((bsz * n_heads, GDN_HEAD_DIM, GDN_HEAD_DIM), F32),
                        pltpu.VMEM((bsz, ck + SUBLANES, 3 * dg), F32)],
        compiler_params=pltpu.CompilerParams(
            dimension_semantics=("arbitrary",), vmem_limit_bytes=VMEM_LIMIT),
        name="gdn",
    )(qkv, z, ab, conv_w, gp, ng)


def _gdn2_kernel(qkv_ref, z_ref, ab_ref, cw_ref, gp_ref, ng_ref, o_ref,
                 s_ref, xp_ref, gc_ref, gct_ref, kbq_ref, kn_ref, rhs_ref, cq_ref, qkt_ref,
                 lm_ref, d_ref, de_ref, val_ref, vn_ref, *, n_heads):
    tt = qkv_ref.shape[1]
    dh = GDN_HEAD_DIM
    bsz = qkv_ref.shape[0]
    dg = n_heads * dh
    halo = SUBLANES
    pairs = [(b, h) for b in range(bsz) for h in range(n_heads)]

    @pl.when(pl.program_id(0) == 0)
    def _():
        s_ref[...] = jnp.zeros_like(s_ref)
        xp_ref[:, 0:halo, :] = jnp.zeros((bsz, halo, 3 * dg), F32)

    ri = lax.broadcasted_iota(jnp.int32, (tt, tt), 0)
    ci = lax.broadcasted_iota(jnp.int32, (tt, tt), 1)
    lane = lax.broadcasted_iota(jnp.int32, (tt, LANES), 1)
    a_log = gp_ref[0:1, :]
    dt_bias = gp_ref[1:2, :]

    for b in range(bsz):
        xp_ref[b, halo:halo + tt, :] = qkv_ref[b].astype(F32)
        conv = cw_ref[0:1, :] * xp_ref[b, halo - 3:halo - 3 + tt, :]
        for j in range(1, CONV_WIDTH):
            conv = conv + cw_ref[j:j + 1, :] * xp_ref[b, halo - 3 + j:halo - 3 + j + tt, :]
        xp_ref[b, 0:halo, :] = xp_ref[b, tt:tt + halo, :]
        act = _silu(conv)

        ab = ab_ref[b]
        sp = jnp.maximum(ab + dt_bias, 0.0) + jnp.log1p(jnp.exp(-jnp.abs(ab + dt_bias)))
        g = jnp.where(lane < n_heads, -jnp.exp(a_log) * sp, 0.0)
        gc = _cumsum_rows(g)
        gc_ref[b] = gc
        gct_ref[b] = gc.T
        glast = gc[tt - 1:tt, :]
        egc = jnp.exp(gc)
        ekt = jnp.exp(glast - gc)
        beta_all = jax.nn.sigmoid(ab)
        for h in range(n_heads):
            p = b * n_heads + h
            q = act[:, h * dh:(h + 1) * dh]
            k = act[:, dg + h * dh:dg + (h + 1) * dh]
            v = act[:, 2 * dg + h * dh:2 * dg + (h + 1) * dh]
            q = q * lax.rsqrt(jnp.sum(q * q, axis=-1, keepdims=True) + EPS) * (dh ** -0.5)
            k = k * lax.rsqrt(jnp.sum(k * k, axis=-1, keepdims=True) + EPS)
            beta = beta_all[:, n_heads + h:n_heads + h + 1]
            eg = egc[:, h:h + 1]
            kb = k * beta
            kbq_ref[p, 0:tt, :] = kb.astype(BF16)
            kbq_ref[p, tt:2 * tt, :] = q.astype(BF16)
            kn_ref[p] = k.astype(BF16)
            rhs_ref[p, :, 0:dh] = (v * beta).astype(BF16)
            rhs_ref[p, :, dh:2 * dh] = (kb * eg).astype(BF16)
            cq_ref[p, tt:2 * tt, :] = (q * eg).astype(BF16)
            qkt_ref[p, tt:tt + dh, :] = (k * ekt[:, h:h + 1]).T.astype(BF16)

    for b, h in pairs:
        p = b * n_heads + h
        m1 = lax.dot_general(kbq_ref[p], kn_ref[p], (((1,), (1,)), ((), ())),
                             preferred_element_type=F32)
        gcol = gc_ref[b, :, h:h + 1]
        grow = gct_ref[b, h:h + 1, :]
        decay = jnp.exp(jnp.where(ri >= ci, gcol - grow, -1e30))
        lm = jnp.where(ri > ci, m1[0:tt] * decay, 0.0)
        lm_ref[p] = lm
        qkt_ref[p, 0:tt, :] = (m1[tt:2 * tt] * decay).astype(BF16)
        first = jnp.where(ri == ci + 1, jnp.where((ci & 1) == 0, lm, 0.0), 0.0)
        d_ref[p] = jnp.where(ri == ci, 1.0, 0.0) - first

    m = 2
    sh = 1
    while m < tt:
        rb = ri >> sh
        cb = ci >> sh
        for b, h in pairs:
            p = b * n_heads + h
            e = jnp.where(rb == cb + 1, jnp.where((cb & 1) == 0, lm_ref[p], 0.0), 0.0)
            de_ref[p] = _dot(d_ref[p].astype(BF16), e.astype(BF16)).astype(BF16)
        for b, h in pairs:
            p = b * n_heads + h
            d = d_ref[p]
            d_ref[p] = d - _dot(de_ref[p], d.astype(BF16))
        m *= 2
        sh += 1

    for b, h in pairs:
        p = b * n_heads + h
        w = _dot(d_ref[p].astype(BF16), rhs_ref[p])
        val_ref[p] = w[:, 0:dh]
        cq_ref[p, 0:tt, :] = w[:, dh:2 * dh].astype(BF16)

    for b, h in pairs:
        p = b * n_heads + h
        m2 = _dot(cq_ref[p], s_ref[p].astype(BF16))
        vn_ref[p] = (val_ref[p] - m2[0:tt]).astype(BF16)
        val_ref[p] = m2[tt:2 * tt]
    for b, h in pairs:
        p = b * n_heads + h
        r = _dot(qkt_ref[p], vn_ref[p])
        o = val_ref[p] + r[0:tt]
        egl = jnp.exp(gc_ref[b, tt - 1:tt, h:h + 1])
        s_ref[p] = s_ref[p] * egl + r[tt:tt + dh]
        zh = z_ref[b, :, h * dh:(h + 1) * dh].astype(F32)
        o_ref[b, :, h * dh:(h + 1) * dh] = (_rms(o) * ng_ref[...] * _silu(zh)).astype(o_ref.dtype)


def _gdn2(qkv, z, ab, conv_w, a_log, dt_bias, norm_g):
    bsz, seq, _ = qkv.shape
    dg = z.shape[2]
    dh = GDN_HEAD_DIM
    n_heads = dg // dh
    tt = GDN_BLOCK
    npair = bsz * n_heads
    gp = jnp.zeros((2, LANES), F32).at[0, :n_heads].set(a_log).at[1, :n_heads].set(dt_bias)
    blk = lambda n: pl.BlockSpec((bsz, tt, n), lambda i: (0, i, 0))
    full = lambda a: pl.BlockSpec(a.shape, lambda i: (0,) * a.ndim)
    ng = norm_g.reshape(1, dh)
    return pl.pallas_call(
        functools.partial(_gdn2_kernel, n_heads=n_heads),
        grid=(seq // tt,),
        in_specs=[blk(3 * dg), blk(dg), blk(LANES), full(conv_w), full(gp), full(ng)],
        out_specs=blk(dg),
        out_shape=jax.ShapeDtypeStruct((bsz, seq, dg), BF16),
        scratch_shapes=[pltpu.VMEM((npair, dh, dh), F32),
                        pltpu.VMEM((bsz, tt + SUBLANES, 3 * dg), F32),
                        pltpu.VMEM((bsz, tt, LANES), F32),
                        pltpu.VMEM((bsz, LANES, tt), F32),
                        pltpu.VMEM((npair, 2 * tt, dh), BF16),
                        pltpu.VMEM((npair, tt, dh), BF16),
                        pltpu.VMEM((npair, tt, 2 * dh), BF16),
                        pltpu.VMEM((npair, 2 * tt, dh), BF16),
                        pltpu.VMEM((npair, tt + dh, tt), BF16),
                        pltpu.VMEM((npair, tt, tt), F32),
                        pltpu.VMEM((npair, tt, tt), F32),
                        pltpu.VMEM((npair, tt, tt), BF16),
                        pltpu.VMEM((npair, tt, dh), F32),
                        pltpu.VMEM((npair, tt, dh), BF16)],
        compiler_params=pltpu.CompilerParams(
            dimension_semantics=("arbitrary",), vmem_limit_bytes=VMEM_LIMIT),
        name="gdn",
    )(qkv, z, ab, conv_w, gp, ng)


def _outproj_kernel(ys_ref, yg_ref, x_ref, mod_ref, wglu_ref, bglu_ref, wos_ref, wog_ref,
                    g2_ref, wr_ref, h_ref, m_ref, comb_ref):
    y = jax.nn.gelu(ys_ref[0])
    gate = jax.nn.sigmoid(_dot(y.astype(BF16), wglu_ref[...]) + bglu_ref[...])
    y = (y * gate).astype(BF16)
    mix = _dot(y, wos_ref[...]) + _dot(yg_ref[0], wog_ref[...])
    h = x_ref[0] + mod_ref[0, 2:3, :] * mix
    h_ref[0] = h
    m = _rms(h) * g2_ref[...] * (1.0 + mod_ref[0, 4:5, :]) + mod_ref[0, 3:4, :]
    m_ref[0] = m.astype(BF16)

    logits = _dot(m, wr_ref[...], precision=HIGHEST)
    lane = lax.broadcasted_iota(jnp.int32, logits.shape, 1)
    neg = -1e30
    is_grp = lane < N_EXPERT_GROUPS
    gl = jnp.where(is_grp, logits, neg)
    gmax = jnp.max(gl, axis=-1, keepdims=True)
    gidx = jnp.min(jnp.where(gl == gmax, lane, LANES), axis=-1, keepdims=True)
    p_grp = 1.0 / jnp.sum(jnp.where(is_grp, jnp.exp(gl - gmax), 0.0), axis=-1, keepdims=True)
    lo = ROUTE_OFF + gidx * EXPERTS_PER_GROUP
    el = jnp.where((lane >= lo) & (lane < lo + EXPERTS_PER_GROUP), logits, neg)
    v1 = jnp.max(el, axis=-1, keepdims=True)
    i1 = jnp.min(jnp.where(el == v1, lane, LANES), axis=-1, keepdims=True)
    el2 = jnp.where(lane == i1, neg, el)
    v2 = jnp.max(el2, axis=-1, keepdims=True)
    i2 = jnp.min(jnp.where(el2 == v2, lane, LANES), axis=-1, keepdims=True)
    t = jnp.exp(v2 - v1)
    w1 = p_grp / (1.0 + t)
    w2 = w1 * t
    comb_ref[0] = jnp.where(lane == i1, w1, 0.0) + jnp.where(lane == i2, w2, 0.0)


def _outproj(ys, yg, x, mod, wglu, bglu, wos, wog, g2, wr):
    bsz, seq, d = x.shape
    tm = ROW_TILE
    full = lambda a: pl.BlockSpec(a.shape, lambda b, i: (0,) * a.ndim)
    row = lambda n: pl.BlockSpec((1, tm, n), lambda b, i: (b, i, 0))
    return pl.pallas_call(
        _outproj_kernel,
        grid=(bsz, seq // tm),
        in_specs=[row(ys.shape[2]), row(yg.shape[2]), row(d),
                  pl.BlockSpec((1, N_MOD, d), lambda b, i: (b, 0, 0)),
                  full(wglu), full(bglu), full(wos), full(wog), full(g2), full(wr)],
        out_specs=[row(d), row(d), row(LANES)],
        out_shape=[jax.ShapeDtypeStruct((bsz, seq, d), F32),
                   jax.ShapeDtypeStruct((bsz, seq, d), BF16),
                   jax.ShapeDtypeStruct((bsz, seq, LANES), F32)],
        compiler_params=pltpu.CompilerParams(
            dimension_semantics=("parallel", "parallel"), vmem_limit_bytes=VMEM_LIMIT),
        name="outproj",
    )(ys, yg, x, mod, wglu, bglu, wos, wog, g2, wr)


def _moe_kernel(m_ref, comb_ref, wg_ref, wu_ref, wd_ref, h_ref, mod_ref, nf_ref, o_ref, acc_ref,
                *, final_norm):
    e = pl.program_id(2)

    @pl.when(e == 0)
    def _():
        acc_ref[...] = jnp.zeros_like(acc_ref)

    m = m_ref[0]
    mid = (_silu(_dot(m, wg_ref[0])) * _dot(m, wu_ref[0])).astype(BF16)
    y = _dot(mid, wd_ref[0])
    comb = comb_ref[0]
    lane = lax.broadcasted_iota(jnp.int32, comb.shape, 1)
    col = jnp.sum(jnp.where(lane == e + ROUTE_OFF, comb, 0.0), axis=-1, keepdims=True)
    acc_ref[...] += col * y

    @pl.when(e == pl.num_programs(2) - 1)
    def _():
        h = h_ref[0] + mod_ref[0, 5:6, :] * acc_ref[...]
        if final_norm:
            h = _rms(h) * nf_ref[...]
        o_ref[0] = h


def _moe(m, comb, wg, wu, wd, h, mod, nf, final_norm):
    bsz, seq, d = h.shape
    tm = min(MOE_TILE, seq)
    n_exp, _, de = wg.shape
    row = lambda n: pl.BlockSpec((1, tm, n), lambda b, i, e: (b, i, 0))
    return pl.pallas_call(
        functools.partial(_moe_kernel, final_norm=final_norm),
        grid=(bsz, seq // tm, n_exp),
        in_specs=[row(d), row(LANES),
                  pl.BlockSpec((1, d, de), lambda b, i, e: (e, 0, 0)),
                  pl.BlockSpec((1, d, de), lambda b, i, e: (e, 0, 0)),
                  pl.BlockSpec((1, de, d), lambda b, i, e: (e, 0, 0)),
                  row(d), pl.BlockSpec((1, N_MOD, d), lambda b, i, e: (b, 0, 0)),
                  pl.BlockSpec(nf.shape, lambda b, i, e: (0, 0))],
        out_specs=row(d),
        out_shape=jax.ShapeDtypeStruct((bsz, seq, d), F32),
        scratch_shapes=[pltpu.VMEM((tm, d), F32)],
        compiler_params=pltpu.CompilerParams(
            dimension_semantics=("parallel", "parallel", "arbitrary"),
            vmem_limit_bytes=VMEM_LIMIT),
        name="moe",
    )(m, comb, wg, wu, wd, h, mod, nf)


def kernel(x, c, norm1_g, norm2_g, w_ada, b_ada, w_in, lam_re, lam_im, log_step, s5_b_re, s5_b_im,
           s5_c_re, s5_c_im, s5_d, w_glu, b_glu, conv_w, a_log, dt_bias, gdn_norm_g, w_out,
           w_router_grp, w_router_exp, w_gate, w_up, w_down, normf_g):
    bsz, seq, d = x.shape
    depth = w_ada.shape[0]
    d_s5 = s5_d.shape[1]
    d_gdn = w_out.shape[1] - d_s5
    n_heads = d_gdn // GDN_HEAD_DIM
    assert seq % S5_TILE == 0 or seq < S5_TILE
    assert seq % ROW_TILE == 0 and seq % GDN_CHUNK == 0 and d_s5 % LANES == 0
    assert 2 * n_heads <= LANES and N_EXPERT_GROUPS + N_EXPERTS <= LANES

    h = x
    for l in range(depth):
        mod = _ada(c, w_ada[l], b_ada[l]).reshape(bsz, N_MOD, d)
        wi = w_in[l]
        c0, c1, c2 = d_s5, d_s5 + 3 * d_gdn, d_s5 + 4 * d_gdn
        wab = jnp.zeros((d, LANES), F32).at[:, :2 * n_heads].set(wi[:, c2:c2 + 2 * n_heads])
        u, qkv, z, ab = _inproj(h, mod, norm1_g[l].reshape(1, d), wi[:, :c0].astype(BF16),
                                wi[:, c0:c1].astype(BF16), wi[:, c1:c2].astype(BF16), wab.astype(BF16))
        tabs = _s5_tables(lam_re[l], lam_im[l], log_step[l], s5_b_re[l], s5_b_im[l],
                          s5_c_re[l], s5_c_im[l])
        ys = _s5(u, *tabs, s5_d[l])
        yg = _gdn2(qkv, z, ab, conv_w[l], a_log[l], dt_bias[l], gdn_norm_g[l])
        wr = jnp.zeros((d, LANES), F32)
        wr = wr.at[:, :N_EXPERT_GROUPS].set(w_router_grp[l])
        wr = wr.at[:, ROUTE_OFF:ROUTE_OFF + N_EXPERTS].set(w_router_exp[l])
        h1, m, comb = _outproj(ys, yg, h, mod, w_glu[l].astype(BF16), b_glu[l].reshape(1, d_s5),
                               w_out[l, :d_s5].astype(BF16), w_out[l, d_s5:].astype(BF16),
                               norm2_g[l].reshape(1, d), wr)
        h = _moe(m, comb, w_gate[l].astype(BF16), w_up[l].astype(BF16), w_down[l].astype(BF16),
                 h1, mod, normf_g.reshape(1, d), final_norm=(l == depth - 1))
    return h
```

```python
import functools
import math

import jax
import jax.numpy as jnp
from jax import lax
from jax.experimental import pallas as pl
from jax.experimental.pallas import tpu as pltpu

F32 = jnp.float32
BF16 = jnp.bfloat16
HIGHEST = lax.Precision.HIGHEST
EPS = 1e-6

S5_GROUP = 16
S5_STATE = 64
GDN_HEAD_DIM = 128
CONV_WIDTH = 4
N_EXPERT_GROUPS = 4
EXPERTS_PER_GROUP = 8
N_EXPERTS = N_EXPERT_GROUPS * EXPERTS_PER_GROUP
N_MOD = 6

LANES = 128
SUBLANES = 8
VMEM_LIMIT = 56 * 1024 * 1024

S5_CHUNK = SUBLANES
S5_TILE = 2048
GDN_CHUNK = 64
GDN_BLOCK = 256
ROW_TILE = 512
MOE_TILE = 1024
MOE_ROWS = 256
MOE_SCATTER_TILE = 2048
MOE_COMBINE_TILE = 1024
ROUTE_OFF = N_EXPERT_GROUPS


def _dot(a, b, **kw):
    return jnp.dot(a, b, preferred_element_type=F32, **kw)


def _silu(v):
    return v * jax.nn.sigmoid(v)


def _rms(v):
    return v * lax.rsqrt(jnp.mean(v * v, axis=-1, keepdims=True) + EPS)


def _ada_kernel(c_ref, w_ref, b_ref, o_ref):
    o_ref[...] = _dot(_silu(c_ref[...]), w_ref[...], precision=HIGHEST) + b_ref[...]


def _ada(c, w, b):
    bsz, d = c.shape
    n = w.shape[1]
    return pl.pallas_call(
        _ada_kernel,
        grid=(n // d,),
        in_specs=[pl.BlockSpec((bsz, d), lambda j: (0, 0)),
                  pl.BlockSpec((d, d), lambda j: (0, j)),
                  pl.BlockSpec((1, d), lambda j: (0, j))],
        out_specs=pl.BlockSpec((bsz, d), lambda j: (0, j)),
        out_shape=jax.ShapeDtypeStruct((bsz, n), F32),
        compiler_params=pltpu.CompilerParams(vmem_limit_bytes=VMEM_LIMIT),
        name="ada",
    )(c, w, b.reshape(1, n))


def _inproj_kernel(x_ref, mod_ref, g_ref, wu_ref, wqkv_ref, wz_ref, wab_ref,
                   u_ref, qkv_ref, z_ref, ab_ref):
    x = x_ref[0]
    y = _rms(x) * g_ref[...]
    h = (y * (1.0 + mod_ref[0, 1:2, :]) + mod_ref[0, 0:1, :]).astype(BF16)
    u_ref[0] = _dot(h, wu_ref[...])
    qkv_ref[0] = _dot(h, wqkv_ref[...]).astype(BF16)
    z_ref[0] = _dot(h, wz_ref[...]).astype(BF16)
    ab_ref[0] = _dot(h, wab_ref[...])


def _inproj(x, mod, g, wu, wqkv, wz, wab):
    bsz, seq, d = x.shape
    tm = ROW_TILE
    full = lambda a: pl.BlockSpec(a.shape, lambda b, i: (0,) * a.ndim)
    row = lambda n: pl.BlockSpec((1, tm, n), lambda b, i: (b, i, 0))
    return pl.pallas_call(
        _inproj_kernel,
        grid=(bsz, seq // tm),
        in_specs=[row(d), pl.BlockSpec((1, N_MOD, d), lambda b, i: (b, 0, 0)), full(g),
                  full(wu), full(wqkv), full(wz), full(wab)],
        out_specs=[row(wu.shape[1]), row(wqkv.shape[1]), row(wz.shape[1]), row(LANES)],
        out_shape=[jax.ShapeDtypeStruct((bsz, seq, wu.shape[1]), F32),
                   jax.ShapeDtypeStruct((bsz, seq, wqkv.shape[1]), BF16),
                   jax.ShapeDtypeStruct((bsz, seq, wz.shape[1]), BF16),
                   jax.ShapeDtypeStruct((bsz, seq, LANES), F32)],
        compiler_params=pltpu.CompilerParams(
            dimension_semantics=("parallel", "parallel"), vmem_limit_bytes=VMEM_LIMIT),
        name="inproj",
    )(x, mod, g, wu, wqkv, wz, wab)


def _s5_tables(lam_re, lam_im, log_step, b_re, b_im, c_re, c_im):
    ch = S5_CHUNK
    n_grp, n_st = lam_re.shape
    gpb = LANES // S5_GROUP
    nblk = n_grp // gpb
    step = jnp.exp(log_step.astype(F32))[:, None]
    lr = lam_re.astype(F32)
    li = lam_im.astype(F32)

    def lam_pow(j):
        mag = jnp.exp(j * lr * step)
        ang = j * li * step
        return mag * jnp.cos(ang), mag * jnp.sin(ang)

    ar, ai = lam_pow(1.0)
    den = lr * lr + li * li
    fr = ((ar - 1.0) * lr + ai * li) / den
    fi = (ai * lr - (ar - 1.0) * li) / den
    bbr = fr[..., None] * b_re - fi[..., None] * b_im
    bbi = fr[..., None] * b_im + fi[..., None] * b_re
    pows = [lam_pow(float(j)) for j in range(ch + 1)]
    pr = jnp.stack([p[0] for p in pows])
    pi = jnp.stack([p[1] for p in pows])
    lbr = pr[:ch, :, :, None] * bbr[None] - pi[:ch, :, :, None] * bbi[None]
    lbi = pr[:ch, :, :, None] * bbi[None] + pi[:ch, :, :, None] * bbr[None]
    eye_q = jnp.eye(gpb, dtype=F32)

    kmat = (jnp.einsum('ghp,jgpk->jghk', c_re, lbr, precision=HIGHEST)
            - jnp.einsum('ghp,jgpk->jghk', c_im, lbi, precision=HIGHEST))
    s_idx = jnp.arange(ch)[:, None]
    t_idx = jnp.arange(ch)[None, :]
    lag = jnp.clip(t_idx - s_idx, 0, ch - 1)
    kst = jnp.where((t_idx >= s_idx)[..., None, None, None], kmat[lag], 0.0)
    kst = kst.reshape(ch, ch, nblk, gpb, S5_GROUP, S5_GROUP)
    tmat = jnp.einsum('stcqhk,qr->csqktrh', kst, eye_q).reshape(nblk, ch * LANES, ch * LANES)

    wri = jnp.stack([lbr[::-1], lbi[::-1]]).reshape(2, ch, nblk, gpb, n_st, S5_GROUP)
    win = jnp.einsum('iscqpk,qr->csqkirp', wri, eye_q).reshape(nblk, ch * LANES, 2 * gpb * n_st)

    pr1 = pr[1:, :, None, :]
    pi1 = pi[1:, :, None, :]
    clr = c_re[None] * pr1 - c_im[None] * pi1
    cli = c_re[None] * pi1 + c_im[None] * pr1
    wo = jnp.stack([clr, -cli]).reshape(2, ch, nblk, gpb, S5_GROUP, n_st)
    wout = jnp.einsum('itcqhp,qr->ciqptrh', wo, eye_q).reshape(nblk, 2 * gpb * n_st, ch * LANES)

    amat = jnp.stack([pr[ch].reshape(nblk, gpb * n_st), pi[ch].reshape(nblk, gpb * n_st)], axis=1)
    return tmat.astype(BF16), win.astype(BF16), wout.astype(BF16), amat


def _s5_kernel(u_ref, t_ref, win_ref, wout_ref, a_ref, d_ref, y_ref, st_ref, v_ref, xp_ref):
    ch = S5_CHUNK
    n = u_ref.shape[1] // ch
    half = st_ref.shape[1]

    @pl.when(pl.program_id(2) == 0)
    def _():
        st_ref[...] = jnp.zeros_like(st_ref)

    slabs = [u_ref[0, pl.ds(s, n, stride=ch), :] for s in range(ch)]
    ucat = jnp.concatenate(slabs, axis=1).astype(BF16)
    y = _dot(ucat, t_ref[0])
    v_ref[...] = _dot(ucat, win_ref[0])
    a_r = a_ref[0, 0:1, :]
    a_i = a_ref[0, 1:2, :]

    def body(r, carry):
        x_r, x_i = carry
        xp_ref[pl.ds(r, 1), 0:half] = x_r
        xp_ref[pl.ds(r, 1), half:2 * half] = x_i
        v_r = v_ref[pl.ds(r, 1), 0:half]
        v_i = v_ref[pl.ds(r, 1), half:2 * half]
        return a_r * x_r - a_i * x_i + v_r, a_r * x_i + a_i * x_r + v_i

    x_r, x_i = lax.fori_loop(0, n, body, (st_ref[0:1, :], st_ref[1:2, :]))
    st_ref[0:1, :] = x_r
    st_ref[1:2, :] = x_i
    y = y + _dot(xp_ref[...].astype(BF16), wout_ref[0])
    d = d_ref[0]
    for t in range(ch):
        y_ref[0, pl.ds(t, n, stride=ch), :] = y[:, t * LANES:(t + 1) * LANES] + d * slabs[t]


def _s5(u, tmat, win, wout, amat, d_skip):
    bsz, seq, dch = u.shape
    nblk = dch // LANES
    tt = min(S5_TILE, seq)
    n = tt // S5_CHUNK
    kdim = S5_CHUNK * LANES
    sdim = win.shape[2]
    wspec = lambda a: pl.BlockSpec((1,) + a.shape[1:], lambda b, c, i: (c, 0, 0))
    return pl.pallas_call(
        _s5_kernel,
        grid=(bsz, nblk, seq // tt),
        in_specs=[pl.BlockSpec((1, tt, LANES), lambda b, c, i: (b, i, c)),
                  wspec(tmat), wspec(win), wspec(wout), wspec(amat),
                  pl.BlockSpec((1, 1, LANES), lambda b, c, i: (c, 0, 0))],
        out_specs=pl.BlockSpec((1, tt, LANES), lambda b, c, i: (b, i, c)),
        out_shape=jax.ShapeDtypeStruct((bsz, seq, dch), F32),
        scratch_shapes=[pltpu.VMEM((2, sdim // 2), F32),
                        pltpu.VMEM((n, sdim), F32),
                        pltpu.VMEM((n, sdim), F32)],
        compiler_params=pltpu.CompilerParams(
            dimension_semantics=("parallel", "parallel", "arbitrary"),
            vmem_limit_bytes=VMEM_LIMIT),
        name="s5",
    )(u, tmat, win, wout, amat, d_skip.reshape(nblk, 1, LANES))


def _cumsum_rows(v):
    n = v.shape[0]
    row = lax.broadcasted_iota(jnp.int32, v.shape, 0)
    sh = 1
    while sh < n:
        v = v + jnp.where(row >= sh, pltpu.roll(v, sh, axis=0), 0.0)
        sh *= 2
    return v


def _unit_lower_inverse(lm):
    n = lm.shape[0]
    ri = lax.broadcasted_iota(jnp.int32, (n, n), 0)
    ci = lax.broadcasted_iota(jnp.int32, (n, n), 1)
    eye = jnp.where(ri == ci, 1.0, 0.0).astype(F32)
    m = 1
    d = eye
    while m < n:
        mask = ((ri // (2 * m)) == (ci // (2 * m))) & (((ri // m) % 2) == 1) & (((ci // m) % 2) == 0)
        e = jnp.where(mask, lm, 0.0)
        if m == 1:
            d = eye - e
        else:
            de = _dot(d.astype(BF16), e.astype(BF16))
            d = d - _dot(de.astype(BF16), d.astype(BF16))
        m *= 2
    return d


def _gdn_kernel(qkv_ref, z_ref, ab_ref, cw_ref, gp_ref, ng_ref, o_ref, s_ref, xp_ref, *, n_heads):
    ck = GDN_CHUNK
    dh = GDN_HEAD_DIM
    bsz = qkv_ref.shape[0]
    dg = n_heads * dh
    halo = SUBLANES

    @pl.when(pl.program_id(0) == 0)
    def _():
        s_ref[...] = jnp.zeros_like(s_ref)
        xp_ref[:, 0:halo, :] = jnp.zeros((bsz, halo, 3 * dg), F32)

    ri = lax.broadcasted_iota(jnp.int32, (ck, ck), 0)
    ci = lax.broadcasted_iota(jnp.int32, (ck, ck), 1)
    causal = ri >= ci
    strict = ri > ci
    lane = lax.broadcasted_iota(jnp.int32, (ck, LANES), 1)
    a_log = gp_ref[0:1, :]
    dt_bias = gp_ref[1:2, :]
    ng = ng_ref[...]

    for b in range(bsz):
        xp_ref[b, halo:halo + ck, :] = qkv_ref[b].astype(F32)
        conv = cw_ref[0:1, :] * xp_ref[b, halo - 3:halo - 3 + ck, :]
        for j in range(1, CONV_WIDTH):
            conv = conv + cw_ref[j:j + 1, :] * xp_ref[b, halo - 3 + j:halo - 3 + j + ck, :]
        xp_ref[b, 0:halo, :] = xp_ref[b, ck:ck + halo, :]
        act = _silu(conv)

        ab = ab_ref[b]
        sp = jnp.maximum(ab + dt_bias, 0.0) + jnp.log1p(jnp.exp(-jnp.abs(ab + dt_bias)))
        g = jnp.where(lane < n_heads, -jnp.exp(a_log) * sp, 0.0)
        gc = _cumsum_rows(g)
        gct = gc.T
        beta_all = jax.nn.sigmoid(ab)
        zb = z_ref[b].astype(F32)

        for h in range(n_heads):
            q = act[:, h * dh:(h + 1) * dh]
            k = act[:, dg + h * dh:dg + (h + 1) * dh]
            v = act[:, 2 * dg + h * dh:2 * dg + (h + 1) * dh]
            q = q * lax.rsqrt(jnp.sum(q * q, axis=-1, keepdims=True) + EPS) * (dh ** -0.5)
            k = k * lax.rsqrt(jnp.sum(k * k, axis=-1, keepdims=True) + EPS)
            beta = beta_all[:, n_heads + h:n_heads + h + 1]
            gcol = gc[:, h:h + 1]
            grow = gct[h:h + 1, :]
            glast = gc[ck - 1:ck, h:h + 1]
            decay = jnp.exp(jnp.where(causal, gcol - grow, -1e30))
            kb = k * beta
            kq = jnp.concatenate([kb, q], axis=0).astype(BF16)
            m1 = lax.dot_general(kq, k.astype(BF16), (((1,), (1,)), ((), ())),
                                 preferred_element_type=F32)
            lm = jnp.where(strict, m1[0:ck] * decay, 0.0)
            qk = m1[ck:2 * ck] * decay
            tinv = _unit_lower_inverse(lm)
            eg = jnp.exp(gcol)
            rhs = jnp.concatenate([v * beta, kb * eg], axis=1).astype(BF16)
            w = _dot(tinv.astype(BF16), rhs)
            value = w[:, 0:dh]
            kcd = w[:, dh:2 * dh]
            qd = q * eg
            kt = k * jnp.exp(glast - gcol)
            st = s_ref[b * n_heads + h]
            m2 = _dot(jnp.concatenate([kcd, qd], axis=0).astype(BF16), st.astype(BF16))
            v_new = value - m2[0:ck]
            vnb = v_new.astype(BF16)
            o = m2[ck:2 * ck] + _dot(qk.astype(BF16), vnb)
            s_ref[b * n_heads + h] = st * jnp.exp(glast) + lax.dot_general(
                kt.astype(BF16), vnb, (((0,), (0,)), ((), ())), preferred_element_type=F32)
            zh = zb[:, h * dh:(h + 1) * dh]
            o_ref[b, :, h * dh:(h + 1) * dh] = (_rms(o) * ng * _silu(zh)).astype(o_ref.dtype)


def _gdn(qkv, z, ab, conv_w, a_log, dt_bias, norm_g):
    bsz, seq, _ = qkv.shape
    dg = z.shape[2]
    n_heads = dg // GDN_HEAD_DIM
    ck = GDN_CHUNK
    gp = jnp.zeros((2, LANES), F32).at[0, :n_heads].set(a_log).at[1, :n_heads].set(dt_bias)
    blk = lambda n: pl.BlockSpec((bsz, ck, n), lambda i: (0, i, 0))
    full = lambda a: pl.BlockSpec(a.shape, lambda i: (0,) * a.ndim)
    ng = norm_g.reshape(1, GDN_HEAD_DIM)
    return pl.pallas_call(
        functools.partial(_gdn_kernel, n_heads=n_heads),
        grid=(seq // ck,),
        in_specs=[blk(3 * dg), blk(dg), blk(LANES), full(conv_w), full(gp), full(ng)],
        out_specs=blk(dg),
        out_shape=jax.ShapeDtypeStruct((bsz, seq, dg), BF16),
        scratch_shapes=[pltpu.VMEM((bsz * n_heads, GDN_HEAD_DIM, GDN_HEAD_DIM), F32),
                        pltpu.VMEM((bsz, ck + SUBLANES, 3 * dg), F32)],
        compiler_params=pltpu.CompilerParams(
            dimension_semantics=("arbitrary",), vmem_limit_bytes=VMEM_LIMIT),
        name="gdn",
    )(qkv, z, ab, conv_w, gp, ng)


def _gdn2_kernel(qkv_ref, z_ref, ab_ref, cw_ref, gp_ref, ng_ref, o_ref,
                 s_ref, xp_ref, gc_ref, gct_ref, kbq_ref, kn_ref, rhs_ref, cq_ref, qkt_ref,
                 lm_ref, d_ref, de_ref, val_ref, vn_ref, *, n_heads):
    tt = qkv_ref.shape[1]
    dh = GDN_HEAD_DIM
    bsz = qkv_ref.shape[0]
    dg = n_heads * dh
    halo = SUBLANES
    pairs = [(b, h) for b in range(bsz) for h in range(n_heads)]

    @pl.when(pl.program_id(0) == 0)
    def _():
        s_ref[...] = jnp.zeros_like(s_ref)
        xp_ref[:, 0:halo, :] = jnp.zeros((bsz, halo, 3 * dg), F32)

    ri = lax.broadcasted_iota(jnp.int32, (tt, tt), 0)
    ci = lax.broadcasted_iota(jnp.int32, (tt, tt), 1)
    lane = lax.broadcasted_iota(jnp.int32, (tt, LANES), 1)
    a_log = gp_ref[0:1, :]
    dt_bias = gp_ref[1:2, :]

    for b in range(bsz):
        xp_ref[b, halo:halo + tt, :] = qkv_ref[b].astype(F32)
        conv = cw_ref[0:1, :] * xp_ref[b, halo - 3:halo - 3 + tt, :]
        for j in range(1, CONV_WIDTH):
            conv = conv + cw_ref[j:j + 1, :] * xp_ref[b, halo - 3 + j:halo - 3 + j + tt, :]
        xp_ref[b, 0:halo, :] = xp_ref[b, tt:tt + halo, :]
        act = _silu(conv)

        ab = ab_ref[b]
        sp = jnp.maximum(ab + dt_bias, 0.0) + jnp.log1p(jnp.exp(-jnp.abs(ab + dt_bias)))
        g = jnp.where(lane < n_heads, -jnp.exp(a_log) * sp, 0.0)
        gc = _cumsum_rows(g)
        gc_ref[b] = gc
        gct_ref[b] = gc.T
        glast = gc[tt - 1:tt, :]
        egc = jnp.exp(gc)
        ekt = jnp.exp(glast - gc)
        beta_all = jax.nn.sigmoid(ab)
        for h in range(n_heads):
            p = b * n_heads + h
            q = act[:, h * dh:(h + 1) * dh]
            k = act[:, dg + h * dh:dg + (h + 1) * dh]
            v = act[:, 2 * dg + h * dh:2 * dg + (h + 1) * dh]
            q = q * lax.rsqrt(jnp.sum(q * q, axis=-1, keepdims=True) + EPS) * (dh ** -0.5)
            k = k * lax.rsqrt(jnp.sum(k * k, axis=-1, keepdims=True) + EPS)
            beta = beta_all[:, n_heads + h:n_heads + h + 1]
            eg = egc[:, h:h + 1]
            kb = k * beta
            kbq_ref[p, 0:tt, :] = kb.astype(BF16)
            kbq_ref[p, tt:2 * tt, :] = q.astype(BF16)
            kn_ref[p] = k.astype(BF16)
            rhs_ref[p, :, 0:dh] = (v * beta).astype(BF16)
            rhs_ref[p, :, dh:2 * dh] = (kb * eg).astype(BF16)
            cq_ref[p, tt:2 * tt, :] = (q * eg).astype(BF16)
            qkt_ref[p, tt:tt + dh, :] = (k * ekt[:, h:h + 1]).T.astype(BF16)

    for b, h in pairs:
        p = b * n_heads + h
        m1 = lax.dot_general(kbq_ref[p], kn_ref[p], (((1,), (1,)), ((), ())),
                             preferred_element_type=F32)
        gcol = gc_ref[b, :, h:h + 1]
        grow = gct_ref[b, h:h + 1, :]
        decay = jnp.exp(jnp.where(ri >= ci, gcol - grow, -1e30))
        lm = jnp.where(ri > ci, m1[0:tt] * decay, 0.0)
        lm_ref[p] = lm
        qkt_ref[p, 0:tt, :] = (m1[tt:2 * tt] * decay).astype(BF16)
        first = jnp.where(ri == ci + 1, jnp.where((ci & 1) == 0, lm, 0.0), 0.0)
        d_ref[p] = jnp.where(ri == ci, 1.0, 0.0) - first

    m = 2
    sh = 1
    while m < tt:
        rb = ri >> sh
        cb = ci >> sh
        for b, h in pairs:
            p = b * n_heads + h
            e = jnp.where(rb == cb + 1, jnp.where((cb & 1) == 0, lm_ref[p], 0.0), 0.0)
            de_ref[p] = _dot(d_ref[p].astype(BF16), e.astype(BF16)).astype(BF16)
        for b, h in pairs:
            p = b * n_heads + h
            d = d_ref[p]
            d_ref[p] = d - _dot(de_ref[p], d.astype(BF16))
        m *= 2
        sh += 1

    for b, h in pairs:
        p = b * n_heads + h
        w = _dot(d_ref[p].astype(BF16), rhs_ref[p])
        val_ref[p] = w[:, 0:dh]
        cq_ref[p, 0:tt, :] = w[:, dh:2 * dh].astype(BF16)

    for b, h in pairs:
        p = b * n_heads + h
        m2 = _dot(cq_ref[p], s_ref[p].astype(BF16))
        vn_ref[p] = (val_ref[p] - m2[0:tt]).astype(BF16)
        val_ref[p] = m2[tt:2 * tt]
    for b, h in pairs:
        p = b * n_heads + h
        r = _dot(qkt_ref[p], vn_ref[p])
        o = val_ref[p] + r[0:tt]
        egl = jnp.exp(gc_ref[b, tt - 1:tt, h:h + 1])
        s_ref[p] = s_ref[p] * egl + r[tt:tt + dh]
        zh = z_ref[b, :, h * dh:(h + 1) * dh].astype(F32)
        o_ref[b, :, h * dh:(h + 1) * dh] = (_rms(o) * ng_ref[...] * _silu(zh)).astype(o_ref.dtype)


def _gdn2(qkv, z, ab, conv_w, a_log, dt_bias, norm_g):
    bsz, seq, _ = qkv.shape
    dg = z.shape[2]
    dh = GDN_HEAD_DIM
    n_heads = dg // dh
    tt = GDN_BLOCK
    npair = bsz * n_heads
    gp = jnp.zeros((2, LANES), F32).at[0, :n_heads].set(a_log).at[1, :n_heads].set(dt_bias)
    blk = lambda n: pl.BlockSpec((bsz, tt, n), lambda i: (0, i, 0))
    full = lambda a: pl.BlockSpec(a.shape, lambda i: (0,) * a.ndim)
    ng = norm_g.reshape(1, dh)
    return pl.pallas_call(
        functools.partial(_gdn2_kernel, n_heads=n_heads),
        grid=(seq // tt,),
        in_specs=[blk(3 * dg), blk(dg), blk(LANES), full(conv_w), full(gp), full(ng)],
        out_specs=blk(dg),
        out_shape=jax.ShapeDtypeStruct((bsz, seq, dg), BF16),
        scratch_shapes=[pltpu.VMEM((npair, dh, dh), F32),
                        pltpu.VMEM((bsz, tt + SUBLANES, 3 * dg), F32),
                        pltpu.VMEM((bsz, tt, LANES), F32),
                        pltpu.VMEM((bsz, LANES, tt), F32),
                        pltpu.VMEM((npair, 2 * tt, dh), BF16),
                        pltpu.VMEM((npair, tt, dh), BF16),
                        pltpu.VMEM((npair, tt, 2 * dh), BF16),
                        pltpu.VMEM((npair, 2 * tt, dh), BF16),
                        pltpu.VMEM((npair, tt + dh, tt), BF16),
                        pltpu.VMEM((npair, tt, tt), F32),
                        pltpu.VMEM((npair, tt, tt), F32),
                        pltpu.VMEM((npair, tt, tt), BF16),
                        pltpu.VMEM((npair, tt, dh), F32),
                        pltpu.VMEM((npair, tt, dh), BF16)],
        compiler_params=pltpu.CompilerParams(
            dimension_semantics=("arbitrary",), vmem_limit_bytes=VMEM_LIMIT),
        name="gdn",
    )(qkv, z, ab, conv_w, gp, ng)


def _outproj_kernel(ys_ref, yg_ref, x_ref, mod_ref, wglu_ref, bglu_ref, wos_ref, wog_ref,
                    g2_ref, wr_ref, h_ref, m_ref, route_ref, count_ref, cnt_ref):
    y = jax.nn.gelu(ys_ref[0])
    gate = jax.nn.sigmoid(_dot(y.astype(BF16), wglu_ref[...]) + bglu_ref[...])
    y = (y * gate).astype(BF16)
    mix = _dot(y, wos_ref[...]) + _dot(yg_ref[0], wog_ref[...])
    h = x_ref[0] + mod_ref[0, 2:3, :] * mix
    h_ref[0] = h
    m = _rms(h) * g2_ref[...] * (1.0 + mod_ref[0, 4:5, :]) + mod_ref[0, 3:4, :]
    m_ref[0] = m

    logits = _dot(m, wr_ref[...], precision=HIGHEST)
    lane = lax.broadcasted_iota(jnp.int32, logits.shape, 1)
    neg = -1e30
    is_grp = lane < N_EXPERT_GROUPS
    gl = jnp.where(is_grp, logits, neg)
    gmax = jnp.max(gl, axis=-1, keepdims=True)
    gidx = jnp.min(jnp.where(gl == gmax, lane, LANES), axis=-1, keepdims=True)
    p_grp = 1.0 / jnp.sum(jnp.where(is_grp, jnp.exp(gl - gmax), 0.0), axis=-1, keepdims=True)
    lo = ROUTE_OFF + gidx * EXPERTS_PER_GROUP
    el = jnp.where((lane >= lo) & (lane < lo + EXPERTS_PER_GROUP), logits, neg)
    v1 = jnp.max(el, axis=-1, keepdims=True)
    i1 = jnp.min(jnp.where(el == v1, lane, LANES), axis=-1, keepdims=True)
    el2 = jnp.where(lane == i1, neg, el)
    v2 = jnp.max(el2, axis=-1, keepdims=True)
    i2 = jnp.min(jnp.where(el2 == v2, lane, LANES), axis=-1, keepdims=True)
    t = jnp.exp(v2 - v1)
    w1 = p_grp / (1.0 + t)
    w2 = w1 * t

    @pl.when((pl.program_id(0) == 0) & (pl.program_id(1) == 0))
    def _():
        cnt_ref[...] = jnp.zeros_like(cnt_ref)

    tm = logits.shape[0]
    oh1 = lane == i1
    oh2 = lane == i2
    cnt = jnp.where(oh1, 1.0, 0.0) + jnp.where(oh2, 1.0, 0.0)
    earlier = (lax.broadcasted_iota(jnp.int32, (tm, tm), 0) > lax.broadcasted_iota(jnp.int32, (tm, tm), 1))
    before = _dot(jnp.where(earlier, 1.0, 0.0).astype(BF16), cnt.astype(BF16)) + cnt_ref[...]
    r1 = jnp.sum(jnp.where(oh1, before, 0.0), axis=-1, keepdims=True)
    r2 = jnp.sum(jnp.where(oh2, before, 0.0), axis=-1, keepdims=True)
    cnt_ref[...] += jnp.sum(cnt, axis=0, keepdims=True)
    count_ref[0] = cnt_ref[...]
    cols = [(i1 - ROUTE_OFF).astype(F32), (i2 - ROUTE_OFF).astype(F32), w1, w2, r1, r2]
    route = jnp.zeros_like(logits)
    for j, col in enumerate(cols):
        route = jnp.where(lane == j, col, route)
    route_ref[0] = route


def _outproj(ys, yg, x, mod, wglu, bglu, wos, wog, g2, wr):
    bsz, seq, d = x.shape
    tm = ROW_TILE
    full = lambda a: pl.BlockSpec(a.shape, lambda b, i: (0,) * a.ndim)
    row = lambda n: pl.BlockSpec((1, tm, n), lambda b, i: (b, i, 0))
    return pl.pallas_call(
        _outproj_kernel,
        grid=(bsz, seq // tm),
        in_specs=[row(ys.shape[2]), row(yg.shape[2]), row(d),
                  pl.BlockSpec((1, N_MOD, d), lambda b, i: (b, 0, 0)),
                  full(wglu), full(bglu), full(wos), full(wog), full(g2), full(wr)],
        out_specs=[row(d), row(d), row(LANES),
                   pl.BlockSpec((1, 1, LANES), lambda b, i: (b * (seq // tm) + i, 0, 0))],
        out_shape=[jax.ShapeDtypeStruct((bsz, seq, d), F32),
                   jax.ShapeDtypeStruct((bsz, seq, d), F32),
                   jax.ShapeDtypeStruct((bsz, seq, LANES), F32),
                   jax.ShapeDtypeStruct((bsz * (seq // tm), 1, LANES), F32)],
        scratch_shapes=[pltpu.VMEM((1, LANES), F32)],
        compiler_params=pltpu.CompilerParams(
            dimension_semantics=("arbitrary", "arbitrary"), vmem_limit_bytes=VMEM_LIMIT),
        name="outproj",
    )(ys, yg, x, mod, wglu, bglu, wos, wog, g2, wr)


def _moe_kernel(m_ref, comb_ref, wg_ref, wu_ref, wd_ref, h_ref, mod_ref, nf_ref, o_ref, acc_ref,
                *, final_norm):
    e = pl.program_id(2)

    @pl.when(e == 0)
    def _():
        acc_ref[...] = jnp.zeros_like(acc_ref)

    m = m_ref[0]
    mid = (_silu(_dot(m, wg_ref[0])) * _dot(m, wu_ref[0])).astype(BF16)
    y = _dot(mid, wd_ref[0])
    comb = comb_ref[0]
    lane = lax.broadcasted_iota(jnp.int32, comb.shape, 1)
    col = jnp.sum(jnp.where(lane == e + ROUTE_OFF, comb, 0.0), axis=-1, keepdims=True)
    acc_ref[...] += col * y

    @pl.when(e == pl.num_programs(2) - 1)
    def _():
        h = h_ref[0] + mod_ref[0, 5:6, :] * acc_ref[...]
        if final_norm:
            h = _rms(h) * nf_ref[...]
        o_ref[0] = h


def _moe(m, comb, wg, wu, wd, h, mod, nf, final_norm):
    bsz, seq, d = h.shape
    tm = min(MOE_TILE, seq)
    n_exp, _, de = wg.shape
    row = lambda n: pl.BlockSpec((1, tm, n), lambda b, i, e: (b, i, 0))
    return pl.pallas_call(
        functools.partial(_moe_kernel, final_norm=final_norm),
        grid=(bsz, seq // tm, n_exp),
        in_specs=[row(d), row(LANES),
                  pl.BlockSpec((1, d, de), lambda b, i, e: (e, 0, 0)),
                  pl.BlockSpec((1, d, de), lambda b, i, e: (e, 0, 0)),
                  pl.BlockSpec((1, de, d), lambda b, i, e: (e, 0, 0)),
                  row(d), pl.BlockSpec((1, N_MOD, d), lambda b, i, e: (b, 0, 0)),
                  pl.BlockSpec(nf.shape, lambda b, i, e: (0, 0))],
        out_specs=row(d),
        out_shape=jax.ShapeDtypeStruct((bsz, seq, d), F32),
        scratch_shapes=[pltpu.VMEM((tm, d), F32)],
        compiler_params=pltpu.CompilerParams(
            dimension_semantics=("parallel", "parallel", "arbitrary"),
            vmem_limit_bytes=VMEM_LIMIT),
        name="moe",
    )(m, comb, wg, wu, wd, h, mod, nf)


def _moe_scatter_kernel(d1_ref, d2_ref, pad_ref, m_hbm, xs_hbm, zero_ref, sem, *, n_exp):
    i = pl.program_id(0)
    ts = d1_ref.shape[2]
    base = i * ts

    tmr = zero_ref.shape[0]

    def fill_copy(r):
        return pltpu.make_async_copy(zero_ref.at[pl.ds(0, 1)], xs_hbm.at[pl.ds(r, 1)], sem.at[0])

    def tile_copy(j):
        return pltpu.make_async_copy(zero_ref, xs_hbm.at[pl.ds(pl.multiple_of(j * tmr, tmr), tmr)], sem.at[0])

    def run(lo, hi, copy):
        def fill(r, carry):
            copy(r).start()
            return carry

        def drain(r, carry):
            copy(r).wait()
            return carry

        lax.fori_loop(lo, hi, fill, 0)
        lax.fori_loop(lo, hi, drain, 0)

    @pl.when(i == 0)
    def _():
        zero_ref[...] = jnp.zeros_like(zero_ref)
        for e in range(n_exp):
            run(pad_ref[0, e], pad_ref[1, e], fill_copy)
        run(pad_ref[0, n_exp], pad_ref[1, n_exp], tile_copy)

    def row_copy(t, dst):
        return pltpu.make_async_copy(m_hbm.at[pl.ds(base + t, 1)], xs_hbm.at[pl.ds(dst, 1)], sem.at[0])

    def issue(t, carry):
        row_copy(t, d1_ref[0, 0, t]).start()
        row_copy(t, d2_ref[0, 0, t]).start()
        return carry

    def drain(t, carry):
        row_copy(t, d1_ref[0, 0, t]).wait()
        row_copy(t, d2_ref[0, 0, t]).wait()
        return carry

    lax.fori_loop(0, ts, issue, 0, unroll=8)
    lax.fori_loop(0, ts, drain, 0, unroll=8)


def _moe_scatter(m, dest1, dest2, pads, n_rows):
    n_tok, d = m.shape
    ts = MOE_SCATTER_TILE
    n_exp = pads.shape[1] - 1
    dspec = pl.BlockSpec((1, 1, ts), lambda i: (i, 0, 0), memory_space=pltpu.SMEM)
    return pl.pallas_call(
        functools.partial(_moe_scatter_kernel, n_exp=n_exp),
        grid=(n_tok // ts,),
        in_specs=[dspec, dspec,
                  pl.BlockSpec(pads.shape, lambda i: (0, 0), memory_space=pltpu.SMEM),
                  pl.BlockSpec(memory_space=pl.ANY)],
        out_specs=pl.BlockSpec(memory_space=pl.ANY),
        out_shape=jax.ShapeDtypeStruct((n_rows, d), m.dtype),
        scratch_shapes=[pltpu.VMEM((MOE_ROWS, d), m.dtype), pltpu.SemaphoreType.DMA((1,))],
        compiler_params=pltpu.CompilerParams(
            dimension_semantics=("arbitrary",), vmem_limit_bytes=VMEM_LIMIT, has_side_effects=True),
        name="moe_scatter",
    )(dest1.reshape(n_tok // ts, 1, ts), dest2.reshape(n_tok // ts, 1, ts), pads, m)


def _moe_ffn_kernel(te_ref, na_ref, xs_ref, wg_ref, wu_ref, wd_ref, ys_ref, wgb_ref, wub_ref, wdb_ref):
    i = pl.program_id(0)
    active = i < na_ref[0]
    changed = (i == 0) | (te_ref[i] != te_ref[jnp.maximum(i - 1, 0)])

    @pl.when(active & changed)
    def _():
        wgb_ref[...] = wg_ref[0].astype(BF16)
        wub_ref[...] = wu_ref[0].astype(BF16)
        wdb_ref[...] = wd_ref[0].astype(BF16)

    @pl.when(active)
    def _():
        x = xs_ref[...].astype(BF16)
        mid = (_silu(_dot(x, wgb_ref[...])) * _dot(x, wub_ref[...])).astype(BF16)
        ys_ref[...] = _dot(mid, wdb_ref[...])

    @pl.when(jnp.logical_not(active))
    def _():
        ys_ref[...] = jnp.zeros_like(ys_ref)


def _moe_ffn(xs, tile_expert, n_active, wg, wu, wd):
    n_rows, d = xs.shape
    tmr = MOE_ROWS
    _, _, de = wg.shape
    grid_spec = pltpu.PrefetchScalarGridSpec(
        num_scalar_prefetch=2,
        grid=(n_rows // tmr,),
        in_specs=[pl.BlockSpec((tmr, d), lambda i, te, na: (jnp.minimum(i, na[0] - 1), 0)),
                  pl.BlockSpec((1, d, de), lambda i, te, na: (te[i], 0, 0)),
                  pl.BlockSpec((1, d, de), lambda i, te, na: (te[i], 0, 0)),
                  pl.BlockSpec((1, de, d), lambda i, te, na: (te[i], 0, 0))],
        out_specs=pl.BlockSpec((tmr, d), lambda i, te, na: (i, 0)),
        scratch_shapes=[pltpu.VMEM((d, de), BF16), pltpu.VMEM((d, de), BF16), pltpu.VMEM((de, d), BF16)],
    )
    return pl.pallas_call(
        _moe_ffn_kernel,
        grid_spec=grid_spec,
        out_shape=jax.ShapeDtypeStruct((n_rows, d), F32),
        compiler_params=pltpu.CompilerParams(
            dimension_semantics=("arbitrary",), vmem_limit_bytes=VMEM_LIMIT),
        name="moe_ffn",
    )(tile_expert, n_active, xs, wg, wu, wd)


def _moe_combine_kernel(d1_ref, d2_ref, h_ref, route_ref, mod_ref, nf_ref, ys_hbm, o_ref,
                        b1_ref, b2_ref, sem, *, final_norm):
    tm = h_ref.shape[1]

    def copy1(t):
        return pltpu.make_async_copy(ys_hbm.at[pl.ds(d1_ref[0, 0, t], 1)], b1_ref.at[pl.ds(t, 1)], sem.at[0])

    def copy2(t):
        return pltpu.make_async_copy(ys_hbm.at[pl.ds(d2_ref[0, 0, t], 1)], b2_ref.at[pl.ds(t, 1)], sem.at[1])

    def issue(t, carry):
        copy1(t).start()
        copy2(t).start()
        return carry

    def drain(t, carry):
        copy1(t).wait()
        copy2(t).wait()
        return carry

    lax.fori_loop(0, tm, issue, 0, unroll=8)
    lax.fori_loop(0, tm, drain, 0, unroll=8)
    route = route_ref[0]
    moe = route[:, 2:3] * b1_ref[...] + route[:, 3:4] * b2_ref[...]
    h = h_ref[0] + mod_ref[0, 5:6, :] * moe
    if final_norm:
        h = _rms(h) * nf_ref[...]
    o_ref[0] = h


def _moe_combine(ys, dest1, dest2, h, route, mod, nf, final_norm):
    bsz, seq, d = h.shape
    tm = MOE_COMBINE_TILE
    nt = seq // tm
    dspec = pl.BlockSpec((1, 1, tm), lambda b, i: (b * nt + i, 0, 0), memory_space=pltpu.SMEM)
    row = lambda n: pl.BlockSpec((1, tm, n), lambda b, i: (b, i, 0))
    return pl.pallas_call(
        functools.partial(_moe_combine_kernel, final_norm=final_norm),
        grid=(bsz, nt),
        in_specs=[dspec, dspec, row(d), row(LANES),
                  pl.BlockSpec((1, N_MOD, d), lambda b, i: (b, 0, 0)),
                  pl.BlockSpec(nf.shape, lambda b, i: (0, 0)),
                  pl.BlockSpec(memory_space=pl.ANY)],
        out_specs=row(d),
        out_shape=jax.ShapeDtypeStruct((bsz, seq, d), F32),
        scratch_shapes=[pltpu.VMEM((tm, d), F32), pltpu.VMEM((tm, d), F32),
                        pltpu.SemaphoreType.DMA((2,))],
        compiler_params=pltpu.CompilerParams(
            dimension_semantics=("arbitrary", "arbitrary"), vmem_limit_bytes=VMEM_LIMIT),
        name="moe_combine",
    )(dest1.reshape(bsz * nt, 1, tm), dest2.reshape(bsz * nt, 1, tm), h, route, mod, nf, ys)


def _moe_sparse(m, route, counts, wg, wu, wd, h, mod, nf, final_norm):
    bsz, seq, d = h.shape
    n_tok = bsz * seq
    tmr = MOE_ROWS
    n_exp = wg.shape[0]
    n_tiles = (2 * n_tok) // tmr + n_exp
    i32 = jnp.int32
    r = route.reshape(n_tok, LANES)
    e1, e2, k1, k2 = (r[:, j].astype(i32) for j in (0, 1, 4, 5))
    cnt = counts[-1, 0, ROUTE_OFF:ROUTE_OFF + n_exp].astype(i32)
    ptiles = (cnt + tmr - 1) // tmr
    tend = jnp.cumsum(ptiles)
    gstart = (tend - ptiles) * tmr
    eids = jnp.arange(n_exp, dtype=i32)
    dest1 = k1 + jnp.sum(jnp.where(e1[:, None] == eids[None, :], gstart[None, :], 0), axis=1)
    dest2 = k2 + jnp.sum(jnp.where(e2[:, None] == eids[None, :], gstart[None, :], 0), axis=1)
    n_active = tend[-1:]
    tile_expert = jnp.sum(jnp.arange(n_tiles, dtype=i32)[:, None] >= tend[None, :], axis=1).astype(i32)
    last_expert = jnp.sum(n_active - 1 >= tend).astype(i32)
    tile_expert = jnp.minimum(tile_expert, last_expert)
    pads = jnp.stack([jnp.concatenate([gstart + cnt, n_active]),
                      jnp.concatenate([gstart + ptiles * tmr, jnp.full((1,), n_tiles, i32)])]).astype(i32)
    xs = _moe_scatter(m.reshape(n_tok, d), dest1, dest2, pads, n_tiles * tmr)
    ys = _moe_ffn(xs, tile_expert, n_active.astype(i32), wg, wu, wd)
    return _moe_combine(ys, dest1, dest2, h, route, mod, nf, final_norm)


def kernel(x, c, norm1_g, norm2_g, w_ada, b_ada, w_in, lam_re, lam_im, log_step, s5_b_re, s5_b_im,
           s5_c_re, s5_c_im, s5_d, w_glu, b_glu, conv_w, a_log, dt_bias, gdn_norm_g, w_out,
           w_router_grp, w_router_exp, w_gate, w_up, w_down, normf_g):
    bsz, seq, d = x.shape
    depth = w_ada.shape[0]
    d_s5 = s5_d.shape[1]
    d_gdn = w_out.shape[1] - d_s5
    n_heads = d_gdn // GDN_HEAD_DIM
    assert seq % S5_TILE == 0 or seq < S5_TILE
    assert seq % ROW_TILE == 0 and seq % GDN_CHUNK == 0 and d_s5 % LANES == 0
    assert 2 * n_heads <= LANES and N_EXPERT_GROUPS + N_EXPERTS <= LANES

    h = x
    for l in range(depth):
        mod = _ada(c, w_ada[l], b_ada[l]).reshape(bsz, N_MOD, d)
        wi = w_in[l]
        c0, c1, c2 = d_s5, d_s5 + 3 * d_gdn, d_s5 + 4 * d_gdn
        wab = jnp.zeros((d, LANES), F32).at[:, :2 * n_heads].set(wi[:, c2:c2 + 2 * n_heads])
        u, qkv, z, ab = _inproj(h, mod, norm1_g[l].reshape(1, d), wi[:, :c0].astype(BF16),
                                wi[:, c0:c1].astype(BF16), wi[:, c1:c2].astype(BF16), wab.astype(BF16))
        tabs = _s5_tables(lam_re[l], lam_im[l], log_step[l], s5_b_re[l], s5_b_im[l],
                          s5_c_re[l], s5_c_im[l])
        ys = _s5(u, *tabs, s5_d[l])
        yg = _gdn2(qkv, z, ab, conv_w[l], a_log[l], dt_bias[l], gdn_norm_g[l])
        wr = jnp.zeros((d, LANES), F32)
        wr = wr.at[:, :N_EXPERT_GROUPS].set(w_router_grp[l])
        wr = wr.at[:, ROUTE_OFF:ROUTE_OFF + N_EXPERTS].set(w_router_exp[l])
        h1, m, route, counts = _outproj(ys, yg, h, mod, w_glu[l].astype(BF16), b_glu[l].reshape(1, d_s5),
                                        w_out[l, :d_s5].astype(BF16), w_out[l, d_s5:].astype(BF16),
                                        norm2_g[l].reshape(1, d), wr)
        h = _moe_sparse(m, route, counts, w_gate[l], w_up[l], w_down[l], h1, mod,
                        normf_g.reshape(1, d), final_norm=(l == depth - 1))
    return h
```

```python
import functools
import math

import jax
import jax.numpy as jnp
from jax import lax
from jax.experimental import pallas as pl
from jax.experimental.pallas import tpu as pltpu

F32 = jnp.float32
BF16 = jnp.bfloat16
HIGHEST = lax.Precision.HIGHEST
EPS = 1e-6

S5_GROUP = 16
S5_STATE = 64
GDN_HEAD_DIM = 128
CONV_WIDTH = 4
N_EXPERT_GROUPS = 4
EXPERTS_PER_GROUP = 8
N_EXPERTS = N_EXPERT_GROUPS * EXPERTS_PER_GROUP
N_MOD = 6

LANES = 128
SUBLANES = 8
VMEM_LIMIT = 56 * 1024 * 1024

S5_CHUNK = SUBLANES
S5_TILE = 2048
GDN_CHUNK = 64
GDN_BLOCK = 256
ROW_TILE = 512
MOE_TILE = 1024
MOE_ROWS = 256
MOE_SCATTER_TILE = 1024
MOE_COMBINE_TILE = 1024
ROUTE_OFF = N_EXPERT_GROUPS


def _dot(a, b, **kw):
    return jnp.dot(a, b, preferred_element_type=F32, **kw)


def _silu(v):
    return v * jax.nn.sigmoid(v)


def _rms(v):
    return v * lax.rsqrt(jnp.mean(v * v, axis=-1, keepdims=True) + EPS)


def _ada_kernel(c_ref, w_ref, b_ref, o_ref):
    o_ref[...] = _dot(_silu(c_ref[...]), w_ref[...], precision=HIGHEST) + b_ref[...]


def _ada(c, w, b):
    bsz, d = c.shape
    n = w.shape[1]
    return pl.pallas_call(
        _ada_kernel,
        grid=(n // d,),
        in_specs=[pl.BlockSpec((bsz, d), lambda j: (0, 0)),
                  pl.BlockSpec((d, d), lambda j: (0, j)),
                  pl.BlockSpec((1, d), lambda j: (0, j))],
        out_specs=pl.BlockSpec((bsz, d), lambda j: (0, j)),
        out_shape=jax.ShapeDtypeStruct((bsz, n), F32),
        compiler_params=pltpu.CompilerParams(vmem_limit_bytes=VMEM_LIMIT),
        name="ada",
    )(c, w, b.reshape(1, n))


def _inproj_kernel(x_ref, mod_ref, g_ref, wu_ref, wqkv_ref, wz_ref, wab_ref,
                   u_ref, qkv_ref, z_ref, ab_ref):
    x = x_ref[0]
    y = _rms(x) * g_ref[...]
    h = (y * (1.0 + mod_ref[0, 1:2, :]) + mod_ref[0, 0:1, :]).astype(BF16)
    u_ref[0] = _dot(h, wu_ref[...])
    qkv_ref[0] = _dot(h, wqkv_ref[...]).astype(BF16)
    z_ref[0] = _dot(h, wz_ref[...]).astype(BF16)
    ab_ref[0] = _dot(h, wab_ref[...])


def _inproj(x, mod, g, wu, wqkv, wz, wab):
    bsz, seq, d = x.shape
    tm = ROW_TILE
    full = lambda a: pl.BlockSpec(a.shape, lambda b, i: (0,) * a.ndim)
    row = lambda n: pl.BlockSpec((1, tm, n), lambda b, i: (b, i, 0))
    return pl.pallas_call(
        _inproj_kernel,
        grid=(bsz, seq // tm),
        in_specs=[row(d), pl.BlockSpec((1, N_MOD, d), lambda b, i: (b, 0, 0)), full(g),
                  full(wu), full(wqkv), full(wz), full(wab)],
        out_specs=[row(wu.shape[1]), row(wqkv.shape[1]), row(wz.shape[1]), row(LANES)],
        out_shape=[jax.ShapeDtypeStruct((bsz, seq, wu.shape[1]), F32),
                   jax.ShapeDtypeStruct((bsz, seq, wqkv.shape[1]), BF16),
                   jax.ShapeDtypeStruct((bsz, seq, wz.shape[1]), BF16),
                   jax.ShapeDtypeStruct((bsz, seq, LANES), F32)],
        compiler_params=pltpu.CompilerParams(
            dimension_semantics=("parallel", "parallel"), vmem_limit_bytes=VMEM_LIMIT),
        name="inproj",
    )(x, mod, g, wu, wqkv, wz, wab)


def _s5_tables(lam_re, lam_im, log_step, b_re, b_im, c_re, c_im):
    ch = S5_CHUNK
    n_grp, n_st = lam_re.shape
    gpb = LANES // S5_GROUP
    nblk = n_grp // gpb
    step = jnp.exp(log_step.astype(F32))[:, None]
    lr = lam_re.astype(F32)
    li = lam_im.astype(F32)

    def lam_pow(j):
        mag = jnp.exp(j * lr * step)
        ang = j * li * step
        return mag * jnp.cos(ang), mag * jnp.sin(ang)

    ar, ai = lam_pow(1.0)
    den = lr * lr + li * li
    fr = ((ar - 1.0) * lr + ai * li) / den
    fi = (ai * lr - (ar - 1.0) * li) / den
    bbr = fr[..., None] * b_re - fi[..., None] * b_im
    bbi = fr[..., None] * b_im + fi[..., None] * b_re
    pows = [lam_pow(float(j)) for j in range(ch + 1)]
    pr = jnp.stack([p[0] for p in pows])
    pi = jnp.stack([p[1] for p in pows])
    lbr = pr[:ch, :, :, None] * bbr[None] - pi[:ch, :, :, None] * bbi[None]
    lbi = pr[:ch, :, :, None] * bbi[None] + pi[:ch, :, :, None] * bbr[None]
    kdim = ch * LANES
    sdim = 2 * gpb * n_st

    def expand(small, lead, row_grp, col_grp, n_rows):
        n_cols = small.shape[-1]
        inner = small.shape[-2]
        full = jnp.broadcast_to(small[..., None, :, :], (nblk,) + lead + (gpb, inner, n_cols))
        full = full.reshape(nblk, n_rows, n_cols)
        rq = (jnp.arange(n_rows) // row_grp) % gpb
        cq = (jnp.arange(n_cols) // col_grp) % gpb
        return jnp.where(rq[:, None] == cq[None, :], full, 0.0).astype(BF16)

    kmat = (jnp.einsum('ghp,jgpk->jghk', c_re, lbr, precision=HIGHEST)
            - jnp.einsum('ghp,jgpk->jghk', c_im, lbi, precision=HIGHEST))
    s_idx = jnp.arange(ch)[:, None]
    t_idx = jnp.arange(ch)[None, :]
    lag = jnp.clip(t_idx - s_idx, 0, ch - 1)
    kst = jnp.where((t_idx >= s_idx)[..., None, None, None], kmat[lag], 0.0)
    kst = kst.reshape(ch, ch, nblk, gpb, S5_GROUP, S5_GROUP)
    t_small = kst.transpose(2, 0, 5, 1, 3, 4).reshape(nblk, ch, S5_GROUP, kdim)
    tmat = expand(t_small, (ch,), S5_GROUP, S5_GROUP, kdim)

    wri = jnp.stack([lbr[::-1], lbi[::-1]]).reshape(2, ch, nblk, gpb, n_st, S5_GROUP)
    w_small = wri.transpose(2, 1, 5, 0, 3, 4).reshape(nblk, ch, S5_GROUP, sdim)
    win = expand(w_small, (ch,), S5_GROUP, n_st, kdim)

    pr1 = pr[1:, :, None, :]
    pi1 = pi[1:, :, None, :]
    clr = c_re[None] * pr1 - c_im[None] * pi1
    cli = c_re[None] * pi1 + c_im[None] * pr1
    wo = jnp.stack([clr, -cli]).reshape(2, ch, nblk, gpb, S5_GROUP, n_st)
    o_small = wo.transpose(2, 0, 5, 1, 3, 4).reshape(nblk, 2, n_st, kdim)
    wout = expand(o_small, (2,), n_st, S5_GROUP, sdim)

    amat = jnp.stack([pr[ch].reshape(nblk, gpb * n_st), pi[ch].reshape(nblk, gpb * n_st)], axis=1)
    return tmat, win, wout, amat


def _s5_kernel(u_ref, t_ref, win_ref, wout_ref, a_ref, d_ref, y_ref, st_ref, v_ref, xp_ref):
    ch = S5_CHUNK
    n = u_ref.shape[1] // ch
    half = st_ref.shape[1]

    @pl.when(pl.program_id(2) == 0)
    def _():
        st_ref[...] = jnp.zeros_like(st_ref)

    slabs = [u_ref[0, pl.ds(s, n, stride=ch), :] for s in range(ch)]
    ucat = jnp.concatenate(slabs, axis=1).astype(BF16)
    y = _dot(ucat, t_ref[0])
    v_ref[...] = _dot(ucat, win_ref[0])
    a_r = a_ref[0, 0:1, :]
    a_i = a_ref[0, 1:2, :]

    def body(r, carry):
        x_r, x_i = carry
        xp_ref[pl.ds(r, 1), 0:half] = x_r
        xp_ref[pl.ds(r, 1), half:2 * half] = x_i
        v_r = v_ref[pl.ds(r, 1), 0:half]
        v_i = v_ref[pl.ds(r, 1), half:2 * half]
        return a_r * x_r - a_i * x_i + v_r, a_r * x_i + a_i * x_r + v_i

    x_r, x_i = lax.fori_loop(0, n, body, (st_ref[0:1, :], st_ref[1:2, :]))
    st_ref[0:1, :] = x_r
    st_ref[1:2, :] = x_i
    y = y + _dot(xp_ref[...].astype(BF16), wout_ref[0])
    d = d_ref[0]
    for t in range(ch):
        y_ref[0, pl.ds(t, n, stride=ch), :] = y[:, t * LANES:(t + 1) * LANES] + d * slabs[t]


def _s5(u, tmat, win, wout, amat, d_skip):
    bsz, seq, dch = u.shape
    nblk = dch // LANES
    tt = min(S5_TILE, seq)
    n = tt // S5_CHUNK
    kdim = S5_CHUNK * LANES
    sdim = win.shape[2]
    wspec = lambda a: pl.BlockSpec((1,) + a.shape[1:], lambda b, c, i: (c, 0, 0))
    return pl.pallas_call(
        _s5_kernel,
        grid=(bsz, nblk, seq // tt),
        in_specs=[pl.BlockSpec((1, tt, LANES), lambda b, c, i: (b, i, c)),
                  wspec(tmat), wspec(win), wspec(wout), wspec(amat),
                  pl.BlockSpec((1, 1, LANES), lambda b, c, i: (c, 0, 0))],
        out_specs=pl.BlockSpec((1, tt, LANES), lambda b, c, i: (b, i, c)),
        out_shape=jax.ShapeDtypeStruct((bsz, seq, dch), F32),
        scratch_shapes=[pltpu.VMEM((2, sdim // 2), F32),
                        pltpu.VMEM((n, sdim), F32),
                        pltpu.VMEM((n, sdim), F32)],
        compiler_params=pltpu.CompilerParams(
            dimension_semantics=("parallel", "parallel", "arbitrary"),
            vmem_limit_bytes=VMEM_LIMIT),
        name="s5",
    )(u, tmat, win, wout, amat, d_skip.reshape(nblk, 1, LANES))


def _cumsum_rows(v):
    n = v.shape[0]
    row = lax.broadcasted_iota(jnp.int32, v.shape, 0)
    sh = 1
    while sh < n:
        v = v + jnp.where(row >= sh, pltpu.roll(v, sh, axis=0), 0.0)
        sh *= 2
    return v


def _unit_lower_inverse(lm):
    n = lm.shape[0]
    ri = lax.broadcasted_iota(jnp.int32, (n, n), 0)
    ci = lax.broadcasted_iota(jnp.int32, (n, n), 1)
    eye = jnp.where(ri == ci, 1.0, 0.0).astype(F32)
    m = 1
    d = eye
    while m < n:
        mask = ((ri // (2 * m)) == (ci // (2 * m))) & (((ri // m) % 2) == 1) & (((ci // m) % 2) == 0)
        e = jnp.where(mask, lm, 0.0)
        if m == 1:
            d = eye - e
        else:
            de = _dot(d.astype(BF16), e.astype(BF16))
            d = d - _dot(de.astype(BF16), d.astype(BF16))
        m *= 2
    return d


def _gdn_kernel(qkv_ref, z_ref, ab_ref, cw_ref, gp_ref, ng_ref, o_ref, s_ref, xp_ref, *, n_heads):
    ck = GDN_CHUNK
    dh = GDN_HEAD_DIM
    bsz = qkv_ref.shape[0]
    dg = n_heads * dh
    halo = SUBLANES

    @pl.when(pl.program_id(0) == 0)
    def _():
        s_ref[...] = jnp.zeros_like(s_ref)
        xp_ref[:, 0:halo, :] = jnp.zeros((bsz, halo, 3 * dg), F32)

    ri = lax.broadcasted_iota(jnp.int32, (ck, ck), 0)
    ci = lax.broadcasted_iota(jnp.int32, (ck, ck), 1)
    causal = ri >= ci
    strict = ri > ci
    lane = lax.broadcasted_iota(jnp.int32, (ck, LANES), 1)
    a_log = gp_ref[0:1, :]
    dt_bias = gp_ref[1:2, :]
    ng = ng_ref[...]

    for b in range(bsz):
        xp_ref[b, halo:halo + ck, :] = qkv_ref[b].astype(F32)
        conv = cw_ref[0:1, :] * xp_ref[b, halo - 3:halo - 3 + ck, :]
        for j in range(1, CONV_WIDTH):
            conv = conv + cw_ref[j:j + 1, :] * xp_ref[b, halo - 3 + j:halo - 3 + j + ck, :]
        xp_ref[b, 0:halo, :] = xp_ref[b, ck:ck + halo, :]
        act = _silu(conv)

        ab = ab_ref[b]
        sp = jnp.maximum(ab + dt_bias, 0.0) + jnp.log1p(jnp.exp(-jnp.abs(ab + dt_bias)))
        g = jnp.where(lane < n_heads, -jnp.exp(a_log) * sp, 0.0)
        gc = _cumsum_rows(g)
        gct = gc.T
        beta_all = jax.nn.sigmoid(ab)
        zb = z_ref[b].astype(F32)

        for h in range(n_heads):
            q = act[:, h * dh:(h + 1) * dh]
            k = act[:, dg + h * dh:dg + (h + 1) * dh]
            v = act[:, 2 * dg + h * dh:2 * dg + (h + 1) * dh]
            q = q * lax.rsqrt(jnp.sum(q * q, axis=-1, keepdims=True) + EPS) * (dh ** -0.5)
            k = k * lax.rsqrt(jnp.sum(k * k, axis=-1, keepdims=True) + EPS)
            beta = beta_all[:, n_heads + h:n_heads + h + 1]
            gcol = gc[:, h:h + 1]
            grow = gct[h:h + 1, :]
            glast = gc[ck - 1:ck, h:h + 1]
            decay = jnp.exp(jnp.where(causal, gcol - grow, -1e30))
            kb = k * beta
            kq = jnp.concatenate([kb, q], axis=0).astype(BF16)
            m1 = lax.dot_general(kq, k.astype(BF16), (((1,), (1,)), ((), ())),
                                 preferred_element_type=F32)
            lm = jnp.where(strict, m1[0:ck] * decay, 0.0)
            qk = m1[ck:2 * ck] * decay
            tinv = _unit_lower_inverse(lm)
            eg = jnp.exp(gcol)
            rhs = jnp.concatenate([v * beta, kb * eg], axis=1).astype(BF16)
            w = _dot(tinv.astype(BF16), rhs)
            value = w[:, 0:dh]
            kcd = w[:, dh:2 * dh]
            qd = q * eg
            kt = k * jnp.exp(glast - gcol)
            st = s_ref[b * n_heads + h]
            m2 = _dot(jnp.concatenate([kcd, qd], axis=0).astype(BF16), st.astype(BF16))
            v_new = value - m2[0:ck]
            vnb = v_new.astype(BF16)
            o = m2[ck:2 * ck] + _dot(qk.astype(BF16), vnb)
            s_ref[b * n_heads + h] = st * jnp.exp(glast) + lax.dot_general(
                kt.astype(BF16), vnb, (((0,), (0,)), ((), ())), preferred_element_type=F32)
            zh = zb[:, h * dh:(h + 1) * dh]
            o_ref[b, :, h * dh:(h + 1) * dh] = (_rms(o) * ng * _silu(zh)).astype(o_ref.dtype)


def _gdn(qkv, z, ab, conv_w, a_log, dt_bias, norm_g):
    bsz, seq, _ = qkv.shape
    dg = z.shape[2]
    n_heads = dg // GDN_HEAD_DIM
    ck = GDN_CHUNK
    gp = jnp.zeros((2, LANES), F32).at[0, :n_heads].set(a_log).at[1, :n_heads].set(dt_bias)
    blk = lambda n: pl.BlockSpec((bsz, ck, n), lambda i: (0, i, 0))
    full = lambda a: pl.BlockSpec(a.shape, lambda i: (0,) * a.ndim)
    ng = norm_g.reshape(1, GDN_HEAD_DIM)
    return pl.pallas_call(
        functools.partial(_gdn_kernel, n_heads=n_heads),
        grid=(seq // ck,),
        in_specs=[blk(3 * dg), blk(dg), blk(LANES), full(conv_w), full(gp), full(ng)],
        out_specs=blk(dg),
        out_shape=jax.ShapeDtypeStruct((bsz, seq, dg), BF16),
        scratch_shapes=[pltpu.VMEM((bsz * n_heads, GDN_HEAD_DIM, GDN_HEAD_DIM), F32),
                        pltpu.VMEM((bsz, ck + SUBLANES, 3 * dg), F32)],
        compiler_params=pltpu.CompilerParams(
            dimension_semantics=("arbitrary",), vmem_limit_bytes=VMEM_LIMIT),
        name="gdn",
    )(qkv, z, ab, conv_w, gp, ng)


def _gdn2_kernel(qkv_ref, z_ref, ab_ref, cw_ref, gp_ref, ng_ref, o_ref,
                 s_ref, xp_ref, gc_ref, gct_ref, kbq_ref, kn_ref, rhs_ref, cq_ref, qkt_ref,
                 lm_ref, d_ref, de_ref, val_ref, vn_ref, *, n_heads):
    tt = qkv_ref.shape[1]
    dh = GDN_HEAD_DIM
    bsz = qkv_ref.shape[0]
    dg = n_heads * dh
    halo = SUBLANES
    pairs = [(b, h) for b in range(bsz) for h in range(n_heads)]

    @pl.when(pl.program_id(0) == 0)
    def _():
        s_ref[...] = jnp.zeros_like(s_ref)
        xp_ref[:, 0:halo, :] = jnp.zeros((bsz, halo, 3 * dg), F32)

    ri = lax.broadcasted_iota(jnp.int32, (tt, tt), 0)
    ci = lax.broadcasted_iota(jnp.int32, (tt, tt), 1)
    lane = lax.broadcasted_iota(jnp.int32, (tt, LANES), 1)
    a_log = gp_ref[0:1, :]
    dt_bias = gp_ref[1:2, :]

    for b in range(bsz):
        xp_ref[b, halo:halo + tt, :] = qkv_ref[b].astype(F32)
        conv = cw_ref[0:1, :] * xp_ref[b, halo - 3:halo - 3 + tt, :]
        for j in range(1, CONV_WIDTH):
            conv = conv + cw_ref[j:j + 1, :] * xp_ref[b, halo - 3 + j:halo - 3 + j + tt, :]
        xp_ref[b, 0:halo, :] = xp_ref[b, tt:tt + halo, :]
        act = _silu(conv)

        ab = ab_ref[b]
        sp = jnp.maximum(ab + dt_bias, 0.0) + jnp.log1p(jnp.exp(-jnp.abs(ab + dt_bias)))
        g = jnp.where(lane < n_heads, -jnp.exp(a_log) * sp, 0.0)
        gc = _cumsum_rows(g)
        gc_ref[b] = gc
        gct_ref[b] = gc.T
        glast = gc[tt - 1:tt, :]
        egc = jnp.exp(gc)
        ekt = jnp.exp(glast - gc)
        beta_all = jax.nn.sigmoid(ab)
        for h in range(n_heads):
            p = b * n_heads + h
            q = act[:, h * dh:(h + 1) * dh]
            k = act[:, dg + h * dh:dg + (h + 1) * dh]
            v = act[:, 2 * dg + h * dh:2 * dg + (h + 1) * dh]
            q = q * lax.rsqrt(jnp.sum(q * q, axis=-1, keepdims=True) + EPS) * (dh ** -0.5)
            k = k * lax.rsqrt(jnp.sum(k * k, axis=-1, keepdims=True) + EPS)
            beta = beta_all[:, n_heads + h:n_heads + h + 1]
            eg = egc[:, h:h + 1]
            kb = k * beta
            kbq_ref[p, 0:tt, :] = kb.astype(BF16)
            kbq_ref[p, tt:2 * tt, :] = q.astype(BF16)
            kn_ref[p] = k.astype(BF16)
            rhs_ref[p, :, 0:dh] = (v * beta).astype(BF16)
            rhs_ref[p, :, dh:2 * dh] = (kb * eg).astype(BF16)
            cq_ref[p, tt:2 * tt, :] = (q * eg).astype(BF16)
            qkt_ref[p, tt:tt + dh, :] = (k * ekt[:, h:h + 1]).T.astype(BF16)

    for b, h in pairs:
        p = b * n_heads + h
        m1 = lax.dot_general(kbq_ref[p], kn_ref[p], (((1,), (1,)), ((), ())),
                             preferred_element_type=F32)
        gcol = gc_ref[b, :, h:h + 1]
        grow = gct_ref[b, h:h + 1, :]
        decay = jnp.exp(jnp.where(ri >= ci, gcol - grow, -1e30))
        lm = jnp.where(ri > ci, m1[0:tt] * decay, 0.0)
        lm_ref[p] = lm
        qkt_ref[p, 0:tt, :] = (m1[tt:2 * tt] * decay).astype(BF16)
        first = jnp.where(ri == ci + 1, jnp.where((ci & 1) == 0, lm, 0.0), 0.0)
        d_ref[p] = jnp.where(ri == ci, 1.0, 0.0) - first

    m = 2
    sh = 1
    while m < tt:
        rb = ri >> sh
        cb = ci >> sh
        for b, h in pairs:
            p = b * n_heads + h
            e = jnp.where(rb == cb + 1, jnp.where((cb & 1) == 0, lm_ref[p], 0.0), 0.0)
            de_ref[p] = _dot(d_ref[p].astype(BF16), e.astype(BF16)).astype(BF16)
        for b, h in pairs:
            p = b * n_heads + h
            d = d_ref[p]
            d_ref[p] = d - _dot(de_ref[p], d.astype(BF16))
        m *= 2
        sh += 1

    for b, h in pairs:
        p = b * n_heads + h
        w = _dot(d_ref[p].astype(BF16), rhs_ref[p])
        val_ref[p] = w[:, 0:dh]
        cq_ref[p, 0:tt, :] = w[:, dh:2 * dh].astype(BF16)

    for b, h in pairs:
        p = b * n_heads + h
        m2 = _dot(cq_ref[p], s_ref[p].astype(BF16))
        vn_ref[p] = (val_ref[p] - m2[0:tt]).astype(BF16)
        val_ref[p] = m2[tt:2 * tt]
    for b, h in pairs:
        p = b * n_heads + h
        r = _dot(qkt_ref[p], vn_ref[p])
        o = val_ref[p] + r[0:tt]
        egl = jnp.exp(gc_ref[b, tt - 1:tt, h:h + 1])
        s_ref[p] = s_ref[p] * egl + r[tt:tt + dh]
        zh = z_ref[b, :, h * dh:(h + 1) * dh].astype(F32)
        o_ref[b, :, h * dh:(h + 1) * dh] = (_rms(o) * ng_ref[...] * _silu(zh)).astype(o_ref.dtype)


def _gdn2(qkv, z, ab, conv_w, a_log, dt_bias, norm_g):
    bsz, seq, _ = qkv.shape
    dg = z.shape[2]
    dh = GDN_HEAD_DIM
    n_heads = dg // dh
    tt = GDN_BLOCK
    npair = bsz * n_heads
    gp = jnp.zeros((2, LANES), F32).at[0, :n_heads].set(a_log).at[1, :n_heads].set(dt_bias)
    blk = lambda n: pl.BlockSpec((bsz, tt, n), lambda i: (0, i, 0))
    full = lambda a: pl.BlockSpec(a.shape, lambda i: (0,) * a.ndim)
    ng = norm_g.reshape(1, dh)
    return pl.pallas_call(
        functools.partial(_gdn2_kernel, n_heads=n_heads),
        grid=(seq // tt,),
        in_specs=[blk(3 * dg), blk(dg), blk(LANES), full(conv_w), full(gp), full(ng)],
        out_specs=blk(dg),
        out_shape=jax.ShapeDtypeStruct((bsz, seq, dg), BF16),
        scratch_shapes=[pltpu.VMEM((npair, dh, dh), F32),
                        pltpu.VMEM((bsz, tt + SUBLANES, 3 * dg), F32),
                        pltpu.VMEM((bsz, tt, LANES), F32),
                        pltpu.VMEM((bsz, LANES, tt), F32),
                        pltpu.VMEM((npair, 2 * tt, dh), BF16),
                        pltpu.VMEM((npair, tt, dh), BF16),
                        pltpu.VMEM((npair, tt, 2 * dh), BF16),
                        pltpu.VMEM((npair, 2 * tt, dh), BF16),
                        pltpu.VMEM((npair, tt + dh, tt), BF16),
                        pltpu.VMEM((npair, tt, tt), F32),
                        pltpu.VMEM((npair, tt, tt), F32),
                        pltpu.VMEM((npair, tt, tt), BF16),
                        pltpu.VMEM((npair, tt, dh), F32),
                        pltpu.VMEM((npair, tt, dh), BF16)],
        compiler_params=pltpu.CompilerParams(
            dimension_semantics=("arbitrary",), vmem_limit_bytes=VMEM_LIMIT),
        name="gdn",
    )(qkv, z, ab, conv_w, gp, ng)


def _outproj_kernel(ys_ref, yg_ref, x_ref, mod_ref, wglu_ref, bglu_ref, wos_ref, wog_ref,
                    g2_ref, wr_ref, h_ref, m_ref, route_ref, count_ref, cnt_ref):
    y = jax.nn.gelu(ys_ref[0])
    gate = jax.nn.sigmoid(_dot(y.astype(BF16), wglu_ref[...]) + bglu_ref[...])
    y = (y * gate).astype(BF16)
    mix = _dot(y, wos_ref[...]) + _dot(yg_ref[0], wog_ref[...])
    h = x_ref[0] + mod_ref[0, 2:3, :] * mix
    h_ref[0] = h
    m = _rms(h) * g2_ref[...] * (1.0 + mod_ref[0, 4:5, :]) + mod_ref[0, 3:4, :]
    m_ref[0] = m

    m_hi = m.astype(BF16)
    m_lo = (m - m_hi.astype(F32)).astype(BF16)
    part = _dot(m_hi, wr_ref[...])
    logits = part[:, 0:LANES] + part[:, LANES:2 * LANES] + _dot(m_lo, wr_ref[:, 0:LANES])
    lane = lax.broadcasted_iota(jnp.int32, logits.shape, 1)
    neg = -1e30
    is_grp = lane < N_EXPERT_GROUPS
    gl = jnp.where(is_grp, logits, neg)
    gmax = jnp.max(gl, axis=-1, keepdims=True)
    gidx = jnp.min(jnp.where(gl == gmax, lane, LANES), axis=-1, keepdims=True)
    p_grp = 1.0 / jnp.sum(jnp.where(is_grp, jnp.exp(gl - gmax), 0.0), axis=-1, keepdims=True)
    lo = ROUTE_OFF + gidx * EXPERTS_PER_GROUP
    el = jnp.where((lane >= lo) & (lane < lo + EXPERTS_PER_GROUP), logits, neg)
    v1 = jnp.max(el, axis=-1, keepdims=True)
    i1 = jnp.min(jnp.where(el == v1, lane, LANES), axis=-1, keepdims=True)
    el2 = jnp.where(lane == i1, neg, el)
    v2 = jnp.max(el2, axis=-1, keepdims=True)
    i2 = jnp.min(jnp.where(el2 == v2, lane, LANES), axis=-1, keepdims=True)
    t = jnp.exp(v2 - v1)
    w1 = p_grp / (1.0 + t)
    w2 = w1 * t

    @pl.when((pl.program_id(0) == 0) & (pl.program_id(1) == 0))
    def _():
        cnt_ref[...] = jnp.zeros_like(cnt_ref)

    tm = logits.shape[0]
    oh1 = lane == i1
    oh2 = lane == i2
    cnt = jnp.where(oh1, 1.0, 0.0) + jnp.where(oh2, 1.0, 0.0)
    earlier = (lax.broadcasted_iota(jnp.int32, (tm, tm), 0) > lax.broadcasted_iota(jnp.int32, (tm, tm), 1))
    before = _dot(jnp.where(earlier, 1.0, 0.0).astype(BF16), cnt.astype(BF16)) + cnt_ref[...]
    r1 = jnp.sum(jnp.where(oh1, before, 0.0), axis=-1, keepdims=True)
    r2 = jnp.sum(jnp.where(oh2, before, 0.0), axis=-1, keepdims=True)
    cnt_ref[...] += jnp.sum(cnt, axis=0, keepdims=True)
    count_ref[0] = cnt_ref[...]
    cols = [(i1 - ROUTE_OFF).astype(F32), (i2 - ROUTE_OFF).astype(F32), w1, w2, r1, r2]
    route = jnp.zeros_like(logits)
    for j, col in enumerate(cols):
        route = jnp.where(lane == j, col, route)
    route_ref[0] = route


def _outproj(ys, yg, x, mod, wglu, bglu, wos, wog, g2, wr):
    bsz, seq, d = x.shape
    tm = ROW_TILE
    full = lambda a: pl.BlockSpec(a.shape, lambda b, i: (0,) * a.ndim)
    row = lambda n: pl.BlockSpec((1, tm, n), lambda b, i: (b, i, 0))
    return pl.pallas_call(
        _outproj_kernel,
        grid=(bsz, seq // tm),
        in_specs=[row(ys.shape[2]), row(yg.shape[2]), row(d),
                  pl.BlockSpec((1, N_MOD, d), lambda b, i: (b, 0, 0)),
                  full(wglu), full(bglu), full(wos), full(wog), full(g2), full(wr)],
        out_specs=[row(d), row(d), row(LANES),
                   pl.BlockSpec((1, 1, LANES), lambda b, i: (b * (seq // tm) + i, 0, 0))],
        out_shape=[jax.ShapeDtypeStruct((bsz, seq, d), F32),
                   jax.ShapeDtypeStruct((bsz, seq, d), F32),
                   jax.ShapeDtypeStruct((bsz, seq, LANES), F32),
                   jax.ShapeDtypeStruct((bsz * (seq // tm), 1, LANES), F32)],
        scratch_shapes=[pltpu.VMEM((1, LANES), F32)],
        compiler_params=pltpu.CompilerParams(
            dimension_semantics=("arbitrary", "arbitrary"), vmem_limit_bytes=VMEM_LIMIT),
        name="outproj",
    )(ys, yg, x, mod, wglu, bglu, wos, wog, g2, wr)


def _moe_kernel(m_ref, comb_ref, wg_ref, wu_ref, wd_ref, h_ref, mod_ref, nf_ref, o_ref, acc_ref,
                *, final_norm):
    e = pl.program_id(2)

    @pl.when(e == 0)
    def _():
        acc_ref[...] = jnp.zeros_like(acc_ref)

    m = m_ref[0]
    mid = (_silu(_dot(m, wg_ref[0])) * _dot(m, wu_ref[0])).astype(BF16)
    y = _dot(mid, wd_ref[0])
    comb = comb_ref[0]
    lane = lax.broadcasted_iota(jnp.int32, comb.shape, 1)
    col = jnp.sum(jnp.where(lane == e + ROUTE_OFF, comb, 0.0), axis=-1, keepdims=True)
    acc_ref[...] += col * y

    @pl.when(e == pl.num_programs(2) - 1)
    def _():
        h = h_ref[0] + mod_ref[0, 5:6, :] * acc_ref[...]
        if final_norm:
            h = _rms(h) * nf_ref[...]
        o_ref[0] = h


def _moe(m, comb, wg, wu, wd, h, mod, nf, final_norm):
    bsz, seq, d = h.shape
    tm = min(MOE_TILE, seq)
    n_exp, _, de = wg.shape
    row = lambda n: pl.BlockSpec((1, tm, n), lambda b, i, e: (b, i, 0))
    return pl.pallas_call(
        functools.partial(_moe_kernel, final_norm=final_norm),
        grid=(bsz, seq // tm, n_exp),
        in_specs=[row(d), row(LANES),
                  pl.BlockSpec((1, d, de), lambda b, i, e: (e, 0, 0)),
                  pl.BlockSpec((1, d, de), lambda b, i, e: (e, 0, 0)),
                  pl.BlockSpec((1, de, d), lambda b, i, e: (e, 0, 0)),
                  row(d), pl.BlockSpec((1, N_MOD, d), lambda b, i, e: (b, 0, 0)),
                  pl.BlockSpec(nf.shape, lambda b, i, e: (0, 0))],
        out_specs=row(d),
        out_shape=jax.ShapeDtypeStruct((bsz, seq, d), F32),
        scratch_shapes=[pltpu.VMEM((tm, d), F32)],
        compiler_params=pltpu.CompilerParams(
            dimension_semantics=("parallel", "parallel", "arbitrary"),
            vmem_limit_bytes=VMEM_LIMIT),
        name="moe",
    )(m, comb, wg, wu, wd, h, mod, nf)


def _moe_scatter_kernel(d1_ref, d2_ref, pad_ref, m_ref, xs_hbm, zero_ref, sem, *, n_exp):
    i = pl.program_id(0)
    ts = d1_ref.shape[2]
    tmr = zero_ref.shape[0]

    def fill_copy(r):
        return pltpu.make_async_copy(zero_ref.at[pl.ds(0, 1)], xs_hbm.at[pl.ds(r, 1)], sem.at[0])

    def tile_copy(j):
        return pltpu.make_async_copy(zero_ref, xs_hbm.at[pl.ds(pl.multiple_of(j * tmr, tmr), tmr)], sem.at[0])

    def run(lo, hi, copy):
        def fill(r, carry):
            copy(r).start()
            return carry

        def drain(r, carry):
            copy(r).wait()
            return carry

        lax.fori_loop(lo, hi, fill, 0)
        lax.fori_loop(lo, hi, drain, 0)

    @pl.when(i == 0)
    def _():
        zero_ref[...] = jnp.zeros_like(zero_ref)
        for e in range(n_exp):
            run(pad_ref[0, e], pad_ref[1, e], fill_copy)
        run(pad_ref[0, n_exp], pad_ref[1, n_exp], tile_copy)

    def row_copy(t, dst):
        return pltpu.make_async_copy(m_ref.at[pl.ds(t, 1)], xs_hbm.at[pl.ds(dst, 1)], sem.at[0])

    def issue(t, carry):
        row_copy(t, d1_ref[0, 0, t]).start()
        row_copy(t, d2_ref[0, 0, t]).start()
        return carry

    def drain(t, carry):
        row_copy(t, d1_ref[0, 0, t]).wait()
        row_copy(t, d2_ref[0, 0, t]).wait()
        return carry

    lax.fori_loop(0, ts, issue, 0, unroll=8)
    lax.fori_loop(0, ts, drain, 0, unroll=8)


def _moe_scatter(m, dest1, dest2, pads, n_rows):
    n_tok, d = m.shape
    ts = MOE_SCATTER_TILE
    n_exp = pads.shape[1] - 1
    dspec = pl.BlockSpec((1, 1, ts), lambda i: (i, 0, 0), memory_space=pltpu.SMEM)
    return pl.pallas_call(
        functools.partial(_moe_scatter_kernel, n_exp=n_exp),
        grid=(n_tok // ts,),
        in_specs=[dspec, dspec,
                  pl.BlockSpec(pads.shape, lambda i: (0, 0), memory_space=pltpu.SMEM),
                  pl.BlockSpec((ts, d), lambda i: (i, 0))],
        out_specs=pl.BlockSpec(memory_space=pl.ANY),
        out_shape=jax.ShapeDtypeStruct((n_rows, d), m.dtype),
        scratch_shapes=[pltpu.VMEM((MOE_ROWS, d), m.dtype), pltpu.SemaphoreType.DMA((1,))],
        compiler_params=pltpu.CompilerParams(
            dimension_semantics=("arbitrary",), vmem_limit_bytes=VMEM_LIMIT, has_side_effects=True),
        name="moe_scatter",
    )(dest1.reshape(n_tok // ts, 1, ts), dest2.reshape(n_tok // ts, 1, ts), pads, m)


def _moe_ffn_kernel(te_ref, na_ref, xs_ref, wg_ref, wu_ref, wd_ref, ys_ref, wgb_ref, wub_ref, wdb_ref):
    i = pl.program_id(0)
    active = i < na_ref[0]
    changed = (i == 0) | (te_ref[i] != te_ref[jnp.maximum(i - 1, 0)])

    @pl.when(active & changed)
    def _():
        wgb_ref[...] = wg_ref[0].astype(BF16)
        wub_ref[...] = wu_ref[0].astype(BF16)
        wdb_ref[...] = wd_ref[0].astype(BF16)

    @pl.when(active)
    def _():
        x = xs_ref[...].astype(BF16)
        mid = (_silu(_dot(x, wgb_ref[...])) * _dot(x, wub_ref[...])).astype(BF16)
        ys_ref[...] = _dot(mid, wdb_ref[...])

    @pl.when(jnp.logical_not(active))
    def _():
        ys_ref[...] = jnp.zeros_like(ys_ref)


def _moe_ffn(xs, tile_expert, n_active, wg, wu, wd):
    n_rows, dp = xs.shape
    tmr = MOE_ROWS
    _, d, de = wg.shape
    grid_spec = pltpu.PrefetchScalarGridSpec(
        num_scalar_prefetch=2,
        grid=(n_rows // tmr,),
        in_specs=[pl.BlockSpec((tmr, dp), lambda i, te, na: (jnp.minimum(i, na[0] - 1), 0)),
                  pl.BlockSpec((1, d, de), lambda i, te, na: (te[i], 0, 0)),
                  pl.BlockSpec((1, d, de), lambda i, te, na: (te[i], 0, 0)),
                  pl.BlockSpec((1, de, d), lambda i, te, na: (te[i], 0, 0))],
        out_specs=pl.BlockSpec((tmr, dp), lambda i, te, na: (i, 0)),
        scratch_shapes=[pltpu.VMEM((d, de), BF16), pltpu.VMEM((d, de), BF16), pltpu.VMEM((de, d), BF16)],
    )
    return pl.pallas_call(
        _moe_ffn_kernel,
        grid_spec=grid_spec,
        out_shape=jax.ShapeDtypeStruct((n_rows, dp), F32),
        compiler_params=pltpu.CompilerParams(
            dimension_semantics=("arbitrary",), vmem_limit_bytes=VMEM_LIMIT),
        name="moe_ffn",
    )(tile_expert, n_active, xs, wg, wu, wd)


def _moe_combine_kernel(d1_ref, d2_ref, h_ref, route_ref, mod_ref, nf_ref, ys_hbm, o_ref,
                        b1_ref, b2_ref, sem, *, final_norm):
    tm = h_ref.shape[1]

    def copy1(t):
        return pltpu.make_async_copy(ys_hbm.at[pl.ds(d1_ref[0, 0, t], 1)], b1_ref.at[pl.ds(t, 1)], sem.at[0])

    def copy2(t):
        return pltpu.make_async_copy(ys_hbm.at[pl.ds(d2_ref[0, 0, t], 1)], b2_ref.at[pl.ds(t, 1)], sem.at[1])

    def issue(t, carry):
        copy1(t).start()
        copy2(t).start()
        return carry

    def drain(t, carry):
        copy1(t).wait()
        copy2(t).wait()
        return carry

    lax.fori_loop(0, tm, issue, 0, unroll=8)
    lax.fori_loop(0, tm, drain, 0, unroll=8)
    route = route_ref[0]
    moe = route[:, 2:3] * b1_ref[...] + route[:, 3:4] * b2_ref[...]
    h = h_ref[0] + mod_ref[0, 5:6, :] * moe
    if final_norm:
        h = _rms(h) * nf_ref[...]
    o_ref[0] = h


def _moe_combine(ys, dest1, dest2, h, route, mod, nf, final_norm):
    bsz, seq, d = h.shape
    tm = MOE_COMBINE_TILE
    nt = seq // tm
    dspec = pl.BlockSpec((1, 1, tm), lambda b, i: (b * nt + i, 0, 0), memory_space=pltpu.SMEM)
    row = lambda n: pl.BlockSpec((1, tm, n), lambda b, i: (b, i, 0))
    return pl.pallas_call(
        functools.partial(_moe_combine_kernel, final_norm=final_norm),
        grid=(bsz, nt),
        in_specs=[dspec, dspec, row(d), row(LANES),
                  pl.BlockSpec((1, N_MOD, d), lambda b, i: (b, 0, 0)),
                  pl.BlockSpec(nf.shape, lambda b, i: (0, 0)),
                  pl.BlockSpec(memory_space=pl.ANY)],
        out_specs=row(d),
        out_shape=jax.ShapeDtypeStruct((bsz, seq, d), F32),
        scratch_shapes=[pltpu.VMEM((tm, ys.shape[1]), ys.dtype), pltpu.VMEM((tm, ys.shape[1]), ys.dtype),
                        pltpu.SemaphoreType.DMA((2,))],
        compiler_params=pltpu.CompilerParams(
            dimension_semantics=("arbitrary", "arbitrary"), vmem_limit_bytes=VMEM_LIMIT),
        name="moe_combine",
    )(dest1.reshape(bsz * nt, 1, tm), dest2.reshape(bsz * nt, 1, tm), h, route, mod, nf, ys)


def _moe_sparse(m, route, counts, wg, wu, wd, h, mod, nf, final_norm):
    bsz, seq, d = h.shape
    n_tok = bsz * seq
    tmr = MOE_ROWS
    n_exp = wg.shape[0]
    n_tiles = (2 * n_tok) // tmr + n_exp
    i32 = jnp.int32
    r = route.reshape(n_tok, LANES)
    e1, e2, k1, k2 = (r[:, j].astype(i32) for j in (0, 1, 4, 5))
    cnt = counts[-1, 0, ROUTE_OFF:ROUTE_OFF + n_exp].astype(i32)
    ptiles = (cnt + tmr - 1) // tmr
    tend = jnp.cumsum(ptiles)
    gstart = (tend - ptiles) * tmr
    eids = jnp.arange(n_exp, dtype=i32)
    dest1 = k1 + jnp.sum(jnp.where(e1[:, None] == eids[None, :], gstart[None, :], 0), axis=1)
    dest2 = k2 + jnp.sum(jnp.where(e2[:, None] == eids[None, :], gstart[None, :], 0), axis=1)
    n_active = tend[-1:]
    tile_expert = jnp.sum(jnp.arange(n_tiles, dtype=i32)[:, None] >= tend[None, :], axis=1).astype(i32)
    last_expert = jnp.sum(n_active - 1 >= tend).astype(i32)
    tile_expert = jnp.minimum(tile_expert, last_expert)
    pads = jnp.stack([jnp.concatenate([gstart + cnt, n_active]),
                      jnp.concatenate([gstart + ptiles * tmr, jnp.full((1,), n_tiles, i32)])]).astype(i32)
    xs = _moe_scatter(m.reshape(n_tok, m.shape[2]), dest1, dest2, pads, n_tiles * tmr)
    ys = _moe_ffn(xs, tile_expert, n_active.astype(i32), wg, wu, wd)
    return _moe_combine(ys, dest1, dest2, h, route, mod, nf, final_norm)


def kernel(x, c, norm1_g, norm2_g, w_ada, b_ada, w_in, lam_re, lam_im, log_step, s5_b_re, s5_b_im,
           s5_c_re, s5_c_im, s5_d, w_glu, b_glu, conv_w, a_log, dt_bias, gdn_norm_g, w_out,
           w_router_grp, w_router_exp, w_gate, w_up, w_down, normf_g):
    bsz, seq, d = x.shape
    depth = w_ada.shape[0]
    d_s5 = s5_d.shape[1]
    d_gdn = w_out.shape[1] - d_s5
    n_heads = d_gdn // GDN_HEAD_DIM
    assert seq % S5_TILE == 0 or seq < S5_TILE
    assert seq % ROW_TILE == 0 and seq % GDN_CHUNK == 0 and d_s5 % LANES == 0
    assert 2 * n_heads <= LANES and N_EXPERT_GROUPS + N_EXPERTS <= LANES

    h = x
    for l in range(depth):
        mod = _ada(c, w_ada[l], b_ada[l]).reshape(bsz, N_MOD, d)
        wi = w_in[l]
        c0, c1, c2 = d_s5, d_s5 + 3 * d_gdn, d_s5 + 4 * d_gdn
        wab = jnp.zeros((d, LANES), F32).at[:, :2 * n_heads].set(wi[:, c2:c2 + 2 * n_heads])
        u, qkv, z, ab = _inproj(h, mod, norm1_g[l].reshape(1, d), wi[:, :c0].astype(BF16),
                                wi[:, c0:c1].astype(BF16), wi[:, c1:c2].astype(BF16), wab.astype(BF16))
        tabs = _s5_tables(lam_re[l], lam_im[l], log_step[l], s5_b_re[l], s5_b_im[l],
                          s5_c_re[l], s5_c_im[l])
        ys = _s5(u, *tabs, s5_d[l])
        yg = _gdn2(qkv, z, ab, conv_w[l], a_log[l], dt_bias[l], gdn_norm_g[l])
        wr = jnp.zeros((d, LANES), F32)
        wr = wr.at[:, :N_EXPERT_GROUPS].set(w_router_grp[l])
        wr = wr.at[:, ROUTE_OFF:ROUTE_OFF + N_EXPERTS].set(w_router_exp[l])
        wr_hi = wr.astype(BF16)
        wr = jnp.concatenate([wr_hi, (wr - wr_hi.astype(F32)).astype(BF16)], axis=1)
        h1, m, route, counts = _outproj(ys, yg, h, mod, w_glu[l].astype(BF16), b_glu[l].reshape(1, d_s5),
                                        w_out[l, :d_s5].astype(BF16), w_out[l, d_s5:].astype(BF16),
                                        norm2_g[l].reshape(1, d), wr)
        h = _moe_sparse(m, route, counts, w_gate[l], w_up[l], w_down[l], h1, mod,
                        normf_g.reshape(1, d), final_norm=(l == depth - 1))
    return h
```

```python
import functools
import math

import jax
import jax.numpy as jnp
from jax import lax
from jax.experimental import pallas as pl
from jax.experimental.pallas import tpu as pltpu

F32 = jnp.float32
BF16 = jnp.bfloat16
HIGHEST = lax.Precision.HIGHEST
EPS = 1e-6

S5_GROUP = 16
S5_STATE = 64
GDN_HEAD_DIM = 128
CONV_WIDTH = 4
N_EXPERT_GROUPS = 4
EXPERTS_PER_GROUP = 8
N_EXPERTS = N_EXPERT_GROUPS * EXPERTS_PER_GROUP
N_MOD = 6

LANES = 128
SUBLANES = 8
VMEM_LIMIT = 56 * 1024 * 1024

S5_CHUNK = SUBLANES
S5_TILE = 2048
GDN_CHUNK = 64
GDN_BLOCK = 256
ROW_TILE = 512
MOE_TILE = 1024
MOE_ROWS = 256
MOE_SCATTER_TILE = 1024
MOE_COMBINE_TILE = 1024
ROUTE_OFF = N_EXPERT_GROUPS


def _dot(a, b, **kw):
    return jnp.dot(a, b, preferred_element_type=F32, **kw)


def _silu(v):
    return v * jax.nn.sigmoid(v)


def _rms(v):
    return v * lax.rsqrt(jnp.mean(v * v, axis=-1, keepdims=True) + EPS)


def _ada_kernel(c_ref, w_ref, b_ref, o_ref):
    o_ref[...] = _dot(_silu(c_ref[...]), w_ref[...], precision=HIGHEST) + b_ref[...]


def _ada(c, w, b):
    bsz, d = c.shape
    n = w.shape[1]
    return pl.pallas_call(
        _ada_kernel,
        grid=(n // d,),
        in_specs=[pl.BlockSpec((bsz, d), lambda j: (0, 0)),
                  pl.BlockSpec((d, d), lambda j: (0, j)),
                  pl.BlockSpec((1, d), lambda j: (0, j))],
        out_specs=pl.BlockSpec((bsz, d), lambda j: (0, j)),
        out_shape=jax.ShapeDtypeStruct((bsz, n), F32),
        compiler_params=pltpu.CompilerParams(vmem_limit_bytes=VMEM_LIMIT),
        name="ada",
    )(c, w, b.reshape(1, n))


def _inproj_kernel(x_ref, mod_ref, g_ref, wu_ref, wqkv_ref, wz_ref, wab_ref,
                   u_ref, qkv_ref, z_ref, ab_ref):
    x = x_ref[0]
    y = _rms(x) * g_ref[...]
    h = (y * (1.0 + mod_ref[0, 1:2, :]) + mod_ref[0, 0:1, :]).astype(BF16)
    u_ref[0] = _dot(h, wu_ref[...])
    qkv_ref[0] = _dot(h, wqkv_ref[...]).astype(BF16)
    z_ref[0] = _dot(h, wz_ref[...]).astype(BF16)
    ab_ref[0] = _dot(h, wab_ref[...])


def _inproj(x, mod, g, wu, wqkv, wz, wab):
    bsz, seq, d = x.shape
    tm = ROW_TILE
    full = lambda a: pl.BlockSpec(a.shape, lambda b, i: (0,) * a.ndim)
    row = lambda n: pl.BlockSpec((1, tm, n), lambda b, i: (b, i, 0))
    return pl.pallas_call(
        _inproj_kernel,
        grid=(bsz, seq // tm),
        in_specs=[row(d), pl.BlockSpec((1, N_MOD, d), lambda b, i: (b, 0, 0)), full(g),
                  full(wu), full(wqkv), full(wz), full(wab)],
        out_specs=[row(wu.shape[1]), row(wqkv.shape[1]), row(wz.shape[1]), row(LANES)],
        out_shape=[jax.ShapeDtypeStruct((bsz, seq, wu.shape[1]), F32),
                   jax.ShapeDtypeStruct((bsz, seq, wqkv.shape[1]), BF16),
                   jax.ShapeDtypeStruct((bsz, seq, wz.shape[1]), BF16),
                   jax.ShapeDtypeStruct((bsz, seq, LANES), F32)],
        compiler_params=pltpu.CompilerParams(
            dimension_semantics=("parallel", "parallel"), vmem_limit_bytes=VMEM_LIMIT),
        name="inproj",
    )(x, mod, g, wu, wqkv, wz, wab)


def _s5_tables(lam_re, lam_im, log_step, b_re, b_im, c_re, c_im):
    ch = S5_CHUNK
    n_grp, n_st = lam_re.shape
    gpb = LANES // S5_GROUP
    nblk = n_grp // gpb
    step = jnp.exp(log_step.astype(F32))[:, None]
    lr = lam_re.astype(F32)
    li = lam_im.astype(F32)

    def lam_pow(j):
        mag = jnp.exp(j * lr * step)
        ang = j * li * step
        return mag * jnp.cos(ang), mag * jnp.sin(ang)

    ar, ai = lam_pow(1.0)
    den = lr * lr + li * li
    fr = ((ar - 1.0) * lr + ai * li) / den
    fi = (ai * lr - (ar - 1.0) * li) / den
    bbr = fr[..., None] * b_re - fi[..., None] * b_im
    bbi = fr[..., None] * b_im + fi[..., None] * b_re
    pows = [lam_pow(float(j)) for j in range(ch + 1)]
    pr = jnp.stack([p[0] for p in pows])
    pi = jnp.stack([p[1] for p in pows])
    lbr = pr[:ch, :, :, None] * bbr[None] - pi[:ch, :, :, None] * bbi[None]
    lbi = pr[:ch, :, :, None] * bbi[None] + pi[:ch, :, :, None] * bbr[None]
    kdim = ch * LANES
    sdim = 2 * gpb * n_st

    def expand(small, lead, row_grp, col_grp, n_rows):
        n_cols = small.shape[-1]
        inner = small.shape[-2]
        full = jnp.broadcast_to(small[..., None, :, :], (nblk,) + lead + (gpb, inner, n_cols))
        full = full.reshape(nblk, n_rows, n_cols)
        rq = (jnp.arange(n_rows) // row_grp) % gpb
        cq = (jnp.arange(n_cols) // col_grp) % gpb
        return jnp.where(rq[:, None] == cq[None, :], full, 0.0).astype(BF16)

    kmat = (jnp.einsum('ghp,jgpk->jghk', c_re, lbr, precision=HIGHEST)
            - jnp.einsum('ghp,jgpk->jghk', c_im, lbi, precision=HIGHEST))
    s_idx = jnp.arange(ch)[:, None]
    t_idx = jnp.arange(ch)[None, :]
    lag = jnp.clip(t_idx - s_idx, 0, ch - 1)
    kst = jnp.where((t_idx >= s_idx)[..., None, None, None], kmat[lag], 0.0)
    kst = kst.reshape(ch, ch, nblk, gpb, S5_GROUP, S5_GROUP)
    t_small = kst.transpose(2, 0, 5, 1, 3, 4).reshape(nblk, ch, S5_GROUP, kdim)
    tmat = expand(t_small, (ch,), S5_GROUP, S5_GROUP, kdim)

    wri = jnp.stack([lbr[::-1], lbi[::-1]]).reshape(2, ch, nblk, gpb, n_st, S5_GROUP)
    w_small = wri.transpose(2, 1, 5, 0, 3, 4).reshape(nblk, ch, S5_GROUP, sdim)
    win = expand(w_small, (ch,), S5_GROUP, n_st, kdim)

    pr1 = pr[1:, :, None, :]
    pi1 = pi[1:, :, None, :]
    clr = c_re[None] * pr1 - c_im[None] * pi1
    cli = c_re[None] * pi1 + c_im[None] * pr1
    wo = jnp.stack([clr, -cli]).reshape(2, ch, nblk, gpb, S5_GROUP, n_st)
    o_small = wo.transpose(2, 0, 5, 1, 3, 4).reshape(nblk, 2, n_st, kdim)
    wout = expand(o_small, (2,), n_st, S5_GROUP, sdim)

    amat = jnp.stack([pr[ch].reshape(nblk, gpb * n_st), pi[ch].reshape(nblk, gpb * n_st)], axis=1)
    return tmat, win, wout, amat


def _s5_kernel(u_ref, t_ref, win_ref, wout_ref, a_ref, d_ref, y_ref, st_ref, v_ref, xp_ref):
    ch = S5_CHUNK
    n = u_ref.shape[1] // ch
    half = st_ref.shape[1]

    @pl.when(pl.program_id(2) == 0)
    def _():
        st_ref[...] = jnp.zeros_like(st_ref)

    slabs = [u_ref[0, pl.ds(s, n, stride=ch), :] for s in range(ch)]
    ucat = jnp.concatenate(slabs, axis=1).astype(BF16)
    y = _dot(ucat, t_ref[0])
    v_ref[...] = _dot(ucat, win_ref[0])
    a_r = a_ref[0, 0:1, :]
    a_i = a_ref[0, 1:2, :]

    def body(r, carry):
        x_r, x_i = carry
        xp_ref[pl.ds(r, 1), 0:half] = x_r
        xp_ref[pl.ds(r, 1), half:2 * half] = x_i
        v_r = v_ref[pl.ds(r, 1), 0:half]
        v_i = v_ref[pl.ds(r, 1), half:2 * half]
        return a_r * x_r - a_i * x_i + v_r, a_r * x_i + a_i * x_r + v_i

    x_r, x_i = lax.fori_loop(0, n, body, (st_ref[0:1, :], st_ref[1:2, :]))
    st_ref[0:1, :] = x_r
    st_ref[1:2, :] = x_i
    y = y + _dot(xp_ref[...].astype(BF16), wout_ref[0])
    d = d_ref[0]
    for t in range(ch):
        y_ref[0, pl.ds(t, n, stride=ch), :] = y[:, t * LANES:(t + 1) * LANES] + d * slabs[t]


def _s5(u, tmat, win, wout, amat, d_skip):
    bsz, seq, dch = u.shape
    nblk = dch // LANES
    tt = min(S5_TILE, seq)
    n = tt // S5_CHUNK
    kdim = S5_CHUNK * LANES
    sdim = win.shape[2]
    wspec = lambda a: pl.BlockSpec((1,) + a.shape[1:], lambda b, c, i: (c, 0, 0))
    return pl.pallas_call(
        _s5_kernel,
        grid=(bsz, nblk, seq // tt),
        in_specs=[pl.BlockSpec((1, tt, LANES), lambda b, c, i: (b, i, c)),
                  wspec(tmat), wspec(win), wspec(wout), wspec(amat),
                  pl.BlockSpec((1, 1, LANES), lambda b, c, i: (c, 0, 0))],
        out_specs=pl.BlockSpec((1, tt, LANES), lambda b, c, i: (b, i, c)),
        out_shape=jax.ShapeDtypeStruct((bsz, seq, dch), F32),
        scratch_shapes=[pltpu.VMEM((2, sdim // 2), F32),
                        pltpu.VMEM((n, sdim), F32),
                        pltpu.VMEM((n, sdim), F32)],
        compiler_params=pltpu.CompilerParams(
            dimension_semantics=("parallel", "parallel", "arbitrary"),
            vmem_limit_bytes=VMEM_LIMIT),
        name="s5",
    )(u, tmat, win, wout, amat, d_skip.reshape(nblk, 1, LANES))


def _cumsum_rows(v):
    n = v.shape[0]
    row = lax.broadcasted_iota(jnp.int32, v.shape, 0)
    sh = 1
    while sh < n:
        v = v + jnp.where(row >= sh, pltpu.roll(v, sh, axis=0), 0.0)
        sh *= 2
    return v


def _unit_lower_inverse(lm):
    n = lm.shape[0]
    ri = lax.broadcasted_iota(jnp.int32, (n, n), 0)
    ci = lax.broadcasted_iota(jnp.int32, (n, n), 1)
    eye = jnp.where(ri == ci, 1.0, 0.0).astype(F32)
    m = 1
    d = eye
    while m < n:
        mask = ((ri // (2 * m)) == (ci // (2 * m))) & (((ri // m) % 2) == 1) & (((ci // m) % 2) == 0)
        e = jnp.where(mask, lm, 0.0)
        if m == 1:
            d = eye - e
        else:
            de = _dot(d.astype(BF16), e.astype(BF16))
            d = d - _dot(de.astype(BF16), d.astype(BF16))
        m *= 2
    return d


def _gdn_kernel(qkv_ref, z_ref, ab_ref, cw_ref, gp_ref, ng_ref, o_ref, s_ref, xp_ref, *, n_heads):
    ck = GDN_CHUNK
    dh = GDN_HEAD_DIM
    bsz = qkv_ref.shape[0]
    dg = n_heads * dh
    halo = SUBLANES

    @pl.when(pl.program_id(0) == 0)
    def _():
        s_ref[...] = jnp.zeros_like(s_ref)
        xp_ref[:, 0:halo, :] = jnp.zeros((bsz, halo, 3 * dg), F32)

    ri = lax.broadcasted_iota(jnp.int32, (ck, ck), 0)
    ci = lax.broadcasted_iota(jnp.int32, (ck, ck), 1)
    causal = ri >= ci
    strict = ri > ci
    lane = lax.broadcasted_iota(jnp.int32, (ck, LANES), 1)
    a_log = gp_ref[0:1, :]
    dt_bias = gp_ref[1:2, :]
    ng = ng_ref[...]

    for b in range(bsz):
        xp_ref[b, halo:halo + ck, :] = qkv_ref[b].astype(F32)
        conv = cw_ref[0:1, :] * xp_ref[b, halo - 3:halo - 3 + ck, :]
        for j in range(1, CONV_WIDTH):
            conv = conv + cw_ref[j:j + 1, :] * xp_ref[b, halo - 3 + j:halo - 3 + j + ck, :]
        xp_ref[b, 0:halo, :] = xp_ref[b, ck:ck + halo, :]
        act = _silu(conv)

        ab = ab_ref[b]
        sp = jnp.maximum(ab + dt_bias, 0.0) + jnp.log1p(jnp.exp(-jnp.abs(ab + dt_bias)))
        g = jnp.where(lane < n_heads, -jnp.exp(a_log) * sp, 0.0)
        gc = _cumsum_rows(g)
        gct = gc.T
        beta_all = jax.nn.sigmoid(ab)
        zb = z_ref[b].astype(F32)

        for h in range(n_heads):
            q = act[:, h * dh:(h + 1) * dh]
            k = act[:, dg + h * dh:dg + (h + 1) * dh]
            v = act[:, 2 * dg + h * dh:2 * dg + (h + 1) * dh]
            q = q * lax.rsqrt(jnp.sum(q * q, axis=-1, keepdims=True) + EPS) * (dh ** -0.5)
            k = k * lax.rsqrt(jnp.sum(k * k, axis=-1, keepdims=True) + EPS)
            beta = beta_all[:, n_heads + h:n_heads + h + 1]
            gcol = gc[:, h:h + 1]
            grow = gct[h:h + 1, :]
            glast = gc[ck - 1:ck, h:h + 1]
            decay = jnp.exp(jnp.where(causal, gcol - grow, -1e30))
            kb = k * beta
            kq = jnp.concatenate([kb, q], axis=0).astype(BF16)
            m1 = lax.dot_general(kq, k.astype(BF16), (((1,), (1,)), ((), ())),
                                 preferred_element_type=F32)
            lm = jnp.where(strict, m1[0:ck] * decay, 0.0)
            qk = m1[ck:2 * ck] * decay
            tinv = _unit_lower_inverse(lm)
            eg = jnp.exp(gcol)
            rhs = jnp.concatenate([v * beta, kb * eg], axis=1).astype(BF16)
            w = _dot(tinv.astype(BF16), rhs)
            value = w[:, 0:dh]
            kcd = w[:, dh:2 * dh]
            qd = q * eg
            kt = k * jnp.exp(glast - gcol)
            st = s_ref[b * n_heads + h]
            m2 = _dot(jnp.concatenate([kcd, qd], axis=0).astype(BF16), st.astype(BF16))
            v_new = value - m2[0:ck]
            vnb = v_new.astype(BF16)
            o = m2[ck:2 * ck] + _dot(qk.astype(BF16), vnb)
            s_ref[b * n_heads + h] = st * jnp.exp(glast) + lax.dot_general(
                kt.astype(BF16), vnb, (((0,), (0,)), ((), ())), preferred_element_type=F32)
            zh = zb[:, h * dh:(h + 1) * dh]
            o_ref[b, :, h * dh:(h + 1) * dh] = (_rms(o) * ng * _silu(zh)).astype(o_ref.dtype)


def _gdn(qkv, z, ab, conv_w, a_log, dt_bias, norm_g):
    bsz, seq, _ = qkv.shape
    dg = z.shape[2]
    n_heads = dg // GDN_HEAD_DIM
    ck = GDN_CHUNK
    gp = jnp.zeros((2, LANES), F32).at[0, :n_heads].set(a_log).at[1, :n_heads].set(dt_bias)
    blk = lambda n: pl.BlockSpec((bsz, ck, n), lambda i: (0, i, 0))
    full = lambda a: pl.BlockSpec(a.shape, lambda i: (0,) * a.ndim)
    ng = norm_g.reshape(1, GDN_HEAD_DIM)
    return pl.pallas_call(
        functools.partial(_gdn_kernel, n_heads=n_heads),
        grid=(seq // ck,),
        in_specs=[blk(3 * dg), blk(dg), blk(LANES), full(conv_w), full(gp), full(ng)],
        out_specs=blk(dg),
        out_shape=jax.ShapeDtypeStruct((bsz, seq, dg), BF16),
        scratch_shapes=[pltpu.VMEM((bsz * n_heads, GDN_HEAD_DIM, GDN_HEAD_DIM), F32),
                        pltpu.VMEM((bsz, ck + SUBLANES, 3 * dg), F32)],
        compiler_params=pltpu.CompilerParams(
            dimension_semantics=("arbitrary",), vmem_limit_bytes=VMEM_LIMIT),
        name="gdn",
    )(qkv, z, ab, conv_w, gp, ng)


def _gdn2_kernel(qkv_ref, z_ref, ab_ref, cw_ref, gp_ref, ng_ref, o_ref,
                 s_ref, xp_ref, gc_ref, gct_ref, kbq_ref, kn_ref, rhs_ref, cq_ref, qkt_ref,
                 lm_ref, d_ref, db_ref, de_ref, val_ref, vn_ref, *, n_heads):
    tt = qkv_ref.shape[1]
    dh = GDN_HEAD_DIM
    bsz = qkv_ref.shape[0]
    dg = n_heads * dh
    halo = SUBLANES
    pairs = [(b, h) for b in range(bsz) for h in range(n_heads)]

    @pl.when(pl.program_id(0) == 0)
    def _():
        s_ref[...] = jnp.zeros_like(s_ref)
        xp_ref[...] = jnp.zeros_like(xp_ref)

    ri = lax.broadcasted_iota(jnp.int32, (tt, tt), 0)
    ci = lax.broadcasted_iota(jnp.int32, (tt, tt), 1)
    lane = lax.broadcasted_iota(jnp.int32, (tt, LANES), 1)
    a_log = gp_ref[0:1, :]
    dt_bias = gp_ref[1:2, :]
    shift_op = jnp.concatenate(
        [jnp.where(ri == ci + j, 1.0, 0.0) for j in range(1, CONV_WIDTH)], axis=0).astype(BF16)

    for b in range(bsz):
        x = qkv_ref[b]
        x32 = x.astype(F32)
        shifted = _dot(shift_op, x)
        edge = jnp.concatenate([xp_ref[b], x32[0:halo]], axis=0)
        conv = cw_ref[CONV_WIDTH - 1:CONV_WIDTH, :] * x32
        for j in range(1, CONV_WIDTH):
            sh_j = jnp.concatenate([edge[halo - j:2 * halo - j],
                                    shifted[(j - 1) * tt + halo:j * tt]], axis=0)
            conv = conv + cw_ref[CONV_WIDTH - 1 - j:CONV_WIDTH - j, :] * sh_j
        xp_ref[b] = x32[tt - halo:tt]
        act = _silu(conv)

        ab = ab_ref[b]
        sp = jnp.maximum(ab + dt_bias, 0.0) + jnp.log1p(jnp.exp(-jnp.abs(ab + dt_bias)))
        g = jnp.where(lane < n_heads, -jnp.exp(a_log) * sp, 0.0)
        gc = _cumsum_rows(g)
        gc_ref[b] = gc
        gct_ref[b] = gc.T
        glast = gc[tt - 1:tt, :]
        egc = jnp.exp(gc)
        ekt = jnp.exp(glast - gc)
        beta_all = jax.nn.sigmoid(ab)
        for h in range(n_heads):
            p = b * n_heads + h
            q = act[:, h * dh:(h + 1) * dh]
            k = act[:, dg + h * dh:dg + (h + 1) * dh]
            v = act[:, 2 * dg + h * dh:2 * dg + (h + 1) * dh]
            q = q * lax.rsqrt(jnp.sum(q * q, axis=-1, keepdims=True) + EPS) * (dh ** -0.5)
            k = k * lax.rsqrt(jnp.sum(k * k, axis=-1, keepdims=True) + EPS)
            beta = beta_all[:, n_heads + h:n_heads + h + 1]
            eg = egc[:, h:h + 1]
            kb = k * beta
            kbq_ref[p, 0:tt, :] = kb.astype(BF16)
            kbq_ref[p, tt:2 * tt, :] = q.astype(BF16)
            kn_ref[p] = k.astype(BF16)
            rhs_ref[p, :, 0:dh] = (v * beta).astype(BF16)
            rhs_ref[p, :, dh:2 * dh] = (kb * eg).astype(BF16)
            cq_ref[p, tt:2 * tt, :] = (q * eg).astype(BF16)
            qkt_ref[p, tt:tt + dh, :] = (k * ekt[:, h:h + 1]).T.astype(BF16)

    for b, h in pairs:
        p = b * n_heads + h
        m1 = lax.dot_general(kbq_ref[p], kn_ref[p], (((1,), (1,)), ((), ())),
                             preferred_element_type=F32)
        gcol = gc_ref[b, :, h:h + 1]
        grow = gct_ref[b, h:h + 1, :]
        decay = jnp.exp(jnp.where(ri >= ci, gcol - grow, -1e30))
        lm = jnp.where(ri > ci, m1[0:tt] * decay, 0.0)
        lm_ref[p] = lm.astype(BF16)
        qkt_ref[p, 0:tt, :] = (m1[tt:2 * tt] * decay).astype(BF16)
        first = jnp.where(ri == ci + 1, jnp.where((ci & 1) == 0, lm, 0.0), 0.0)
        d = jnp.where(ri == ci, 1.0, 0.0) - first
        d_ref[p] = d
        db_ref[p] = d.astype(BF16)

    m = 2
    sh = 1
    while m < tt:
        rb = ri >> sh
        cb = ci >> sh
        sel = jnp.where(rb == cb + 1, jnp.where((cb & 1) == 0, 1.0, 0.0), 0.0).astype(BF16)
        for b, h in pairs:
            p = b * n_heads + h
            de_ref[p] = _dot(db_ref[p], lm_ref[p] * sel).astype(BF16)
        for b, h in pairs:
            p = b * n_heads + h
            d = d_ref[p] - _dot(de_ref[p], db_ref[p])
            d_ref[p] = d
            db_ref[p] = d.astype(BF16)
        m *= 2
        sh += 1

    for b, h in pairs:
        p = b * n_heads + h
        w = _dot(db_ref[p], rhs_ref[p])
        val_ref[p] = w[:, 0:dh]
        cq_ref[p, 0:tt, :] = w[:, dh:2 * dh].astype(BF16)

    for b, h in pairs:
        p = b * n_heads + h
        m2 = _dot(cq_ref[p], s_ref[p].astype(BF16))
        vn_ref[p] = (val_ref[p] - m2[0:tt]).astype(BF16)
        val_ref[p] = m2[tt:2 * tt]
    for b, h in pairs:
        p = b * n_heads + h
        r = _dot(qkt_ref[p], vn_ref[p])
        o = val_ref[p] + r[0:tt]
        egl = jnp.exp(gc_ref[b, tt - 1:tt, h:h + 1])
        s_ref[p] = s_ref[p] * egl + r[tt:tt + dh]
        zh = z_ref[b, :, h * dh:(h + 1) * dh].astype(F32)
        o_ref[b, :, h * dh:(h + 1) * dh] = (_rms(o) * ng_ref[...] * _silu(zh)).astype(o_ref.dtype)


def _gdn2(qkv, z, ab, conv_w, a_log, dt_bias, norm_g):
    bsz, seq, _ = qkv.shape
    dg = z.shape[2]
    dh = GDN_HEAD_DIM
    n_heads = dg // dh
    tt = GDN_BLOCK
    npair = bsz * n_heads
    gp = jnp.zeros((2, LANES), F32).at[0, :n_heads].set(a_log).at[1, :n_heads].set(dt_bias)
    blk = lambda n: pl.BlockSpec((bsz, tt, n), lambda i: (0, i, 0))
    full = lambda a: pl.BlockSpec(a.shape, lambda i: (0,) * a.ndim)
    ng = norm_g.reshape(1, dh)
    return pl.pallas_call(
        functools.partial(_gdn2_kernel, n_heads=n_heads),
        grid=(seq // tt,),
        in_specs=[blk(3 * dg), blk(dg), blk(LANES), full(conv_w), full(gp), full(ng)],
        out_specs=blk(dg),
        out_shape=jax.ShapeDtypeStruct((bsz, seq, dg), BF16),
        scratch_shapes=[pltpu.VMEM((npair, dh, dh), F32),
                        pltpu.VMEM((bsz, SUBLANES, 3 * dg), F32),
                        pltpu.VMEM((bsz, tt, LANES), F32),
                        pltpu.VMEM((bsz, LANES, tt), F32),
                        pltpu.VMEM((npair, 2 * tt, dh), BF16),
                        pltpu.VMEM((npair, tt, dh), BF16),
                        pltpu.VMEM((npair, tt, 2 * dh), BF16),
                        pltpu.VMEM((npair, 2 * tt, dh), BF16),
                        pltpu.VMEM((npair, tt + dh, tt), BF16),
                        pltpu.VMEM((npair, tt, tt), BF16),
                        pltpu.VMEM((npair, tt, tt), F32),
                        pltpu.VMEM((npair, tt, tt), BF16),
                        pltpu.VMEM((npair, tt, tt), BF16),
                        pltpu.VMEM((npair, tt, dh), F32),
                        pltpu.VMEM((npair, tt, dh), BF16)],
        compiler_params=pltpu.CompilerParams(
            dimension_semantics=("arbitrary",), vmem_limit_bytes=VMEM_LIMIT),
        name="gdn",
    )(qkv, z, ab, conv_w, gp, ng)


def _outproj_kernel(ys_ref, yg_ref, x_ref, mod_ref, wglu_ref, bglu_ref, wos_ref, wog_ref,
                    g2_ref, wr_ref, h_ref, m_ref, route_ref, count_ref, cnt_ref):
    y = jax.nn.gelu(ys_ref[0])
    gate = jax.nn.sigmoid(_dot(y.astype(BF16), wglu_ref[...]) + bglu_ref[...])
    y = (y * gate).astype(BF16)
    mix = _dot(y, wos_ref[...]) + _dot(yg_ref[0], wog_ref[...])
    h = x_ref[0] + mod_ref[0, 2:3, :] * mix
    h_ref[0] = h
    m = _rms(h) * g2_ref[...] * (1.0 + mod_ref[0, 4:5, :]) + mod_ref[0, 3:4, :]
    m_ref[0] = m

    m_hi = m.astype(BF16)
    m_lo = (m - m_hi.astype(F32)).astype(BF16)
    part = _dot(m_hi, wr_ref[...])
    logits = part[:, 0:LANES] + part[:, LANES:2 * LANES] + _dot(m_lo, wr_ref[:, 0:LANES])
    lane = lax.broadcasted_iota(jnp.int32, logits.shape, 1)
    neg = -1e30
    is_grp = lane < N_EXPERT_GROUPS
    gl = jnp.where(is_grp, logits, neg)
    gmax = jnp.max(gl, axis=-1, keepdims=True)
    gidx = jnp.min(jnp.where(gl == gmax, lane, LANES), axis=-1, keepdims=True)
    p_grp = 1.0 / jnp.sum(jnp.where(is_grp, jnp.exp(gl - gmax), 0.0), axis=-1, keepdims=True)
    lo = ROUTE_OFF + gidx * EXPERTS_PER_GROUP
    el = jnp.where((lane >= lo) & (lane < lo + EXPERTS_PER_GROUP), logits, neg)
    v1 = jnp.max(el, axis=-1, keepdims=True)
    i1 = jnp.min(jnp.where(el == v1, lane, LANES), axis=-1, keepdims=True)
    el2 = jnp.where(lane == i1, neg, el)
    v2 = jnp.max(el2, axis=-1, keepdims=True)
    i2 = jnp.min(jnp.where(el2 == v2, lane, LANES), axis=-1, keepdims=True)
    t = jnp.exp(v2 - v1)
    w1 = p_grp / (1.0 + t)
    w2 = w1 * t

    @pl.when((pl.program_id(0) == 0) & (pl.program_id(1) == 0))
    def _():
        cnt_ref[...] = jnp.zeros_like(cnt_ref)

    tm = logits.shape[0]
    oh1 = lane == i1
    oh2 = lane == i2
    cnt = jnp.where(oh1, 1.0, 0.0) + jnp.where(oh2, 1.0, 0.0)
    earlier = (lax.broadcasted_iota(jnp.int32, (tm, tm), 0) > lax.broadcasted_iota(jnp.int32, (tm, tm), 1))
    before = _dot(jnp.where(earlier, 1.0, 0.0).astype(BF16), cnt.astype(BF16)) + cnt_ref[...]
    r1 = jnp.sum(jnp.where(oh1, before, 0.0), axis=-1, keepdims=True)
    r2 = jnp.sum(jnp.where(oh2, before, 0.0), axis=-1, keepdims=True)
    cnt_ref[...] += jnp.sum(cnt, axis=0, keepdims=True)
    count_ref[0] = cnt_ref[...]
    cols = [(i1 - ROUTE_OFF).astype(F32), (i2 - ROUTE_OFF).astype(F32), w1, w2, r1, r2]
    route = jnp.zeros_like(logits)
    for j, col in enumerate(cols):
        route = jnp.where(lane == j, col, route)
    route_ref[0] = route


def _outproj(ys, yg, x, mod, wglu, bglu, wos, wog, g2, wr):
    bsz, seq, d = x.shape
    tm = ROW_TILE
    full = lambda a: pl.BlockSpec(a.shape, lambda b, i: (0,) * a.ndim)
    row = lambda n: pl.BlockSpec((1, tm, n), lambda b, i: (b, i, 0))
    return pl.pallas_call(
        _outproj_kernel,
        grid=(bsz, seq // tm),
        in_specs=[row(ys.shape[2]), row(yg.shape[2]), row(d),
                  pl.BlockSpec((1, N_MOD, d), lambda b, i: (b, 0, 0)),
                  full(wglu), full(bglu), full(wos), full(wog), full(g2), full(wr)],
        out_specs=[row(d), row(d), row(LANES),
                   pl.BlockSpec((1, 1, LANES), lambda b, i: (b * (seq // tm) + i, 0, 0))],
        out_shape=[jax.ShapeDtypeStruct((bsz, seq, d), F32),
                   jax.ShapeDtypeStruct((bsz, seq, d), F32),
                   jax.ShapeDtypeStruct((bsz, seq, LANES), F32),
                   jax.ShapeDtypeStruct((bsz * (seq // tm), 1, LANES), F32)],
        scratch_shapes=[pltpu.VMEM((1, LANES), F32)],
        compiler_params=pltpu.CompilerParams(
            dimension_semantics=("arbitrary", "arbitrary"), vmem_limit_bytes=VMEM_LIMIT),
        name="outproj",
    )(ys, yg, x, mod, wglu, bglu, wos, wog, g2, wr)


def _moe_kernel(m_ref, comb_ref, wg_ref, wu_ref, wd_ref, h_ref, mod_ref, nf_ref, o_ref, acc_ref,
                *, final_norm):
    e = pl.program_id(2)

    @pl.when(e == 0)
    def _():
        acc_ref[...] = jnp.zeros_like(acc_ref)

    m = m_ref[0]
    mid = (_silu(_dot(m, wg_ref[0])) * _dot(m, wu_ref[0])).astype(BF16)
    y = _dot(mid, wd_ref[0])
    comb = comb_ref[0]
    lane = lax.broadcasted_iota(jnp.int32, comb.shape, 1)
    col = jnp.sum(jnp.where(lane == e + ROUTE_OFF, comb, 0.0), axis=-1, keepdims=True)
    acc_ref[...] += col * y

    @pl.when(e == pl.num_programs(2) - 1)
    def _():
        h = h_ref[0] + mod_ref[0, 5:6, :] * acc_ref[...]
        if final_norm:
            h = _rms(h) * nf_ref[...]
        o_ref[0] = h


def _moe(m, comb, wg, wu, wd, h, mod, nf, final_norm):
    bsz, seq, d = h.shape
    tm = min(MOE_TILE, seq)
    n_exp, _, de = wg.shape
    row = lambda n: pl.BlockSpec((1, tm, n), lambda b, i, e: (b, i, 0))
    return pl.pallas_call(
        functools.partial(_moe_kernel, final_norm=final_norm),
        grid=(bsz, seq // tm, n_exp),
        in_specs=[row(d), row(LANES),
                  pl.BlockSpec((1, d, de), lambda b, i, e: (e, 0, 0)),
                  pl.BlockSpec((1, d, de), lambda b, i, e: (e, 0, 0)),
                  pl.BlockSpec((1, de, d), lambda b, i, e: (e, 0, 0)),
                  row(d), pl.BlockSpec((1, N_MOD, d), lambda b, i, e: (b, 0, 0)),
                  pl.BlockSpec(nf.shape, lambda b, i, e: (0, 0))],
        out_specs=row(d),
        out_shape=jax.ShapeDtypeStruct((bsz, seq, d), F32),
        scratch_shapes=[pltpu.VMEM((tm, d), F32)],
        compiler_params=pltpu.CompilerParams(
            dimension_semantics=("parallel", "parallel", "arbitrary"),
            vmem_limit_bytes=VMEM_LIMIT),
        name="moe",
    )(m, comb, wg, wu, wd, h, mod, nf)


def _moe_scatter_kernel(d1_ref, d2_ref, pad_ref, m_ref, xs_hbm, zero_ref, sem, *, n_exp):
    i = pl.program_id(0)
    ts = d1_ref.shape[2]
    tmr = zero_ref.shape[0]

    def fill_copy(r):
        return pltpu.make_async_copy(zero_ref.at[pl.ds(0, 1)], xs_hbm.at[pl.ds(r, 1)], sem.at[0])

    def tile_copy(j):
        return pltpu.make_async_copy(zero_ref, xs_hbm.at[pl.ds(pl.multiple_of(j * tmr, tmr), tmr)], sem.at[0])

    def run(lo, hi, copy):
        def fill(r, carry):
            copy(r).start()
            return carry

        def drain(r, carry):
            copy(r).wait()
            return carry

        lax.fori_loop(lo, hi, fill, 0)
        lax.fori_loop(lo, hi, drain, 0)

    @pl.when(i == 0)
    def _():
        zero_ref[...] = jnp.zeros_like(zero_ref)
        for e in range(n_exp):
            run(pad_ref[0, e], pad_ref[1, e], fill_copy)
        run(pad_ref[0, n_exp], pad_ref[1, n_exp], tile_copy)

    def row_copy(t, dst, slot):
        return pltpu.make_async_copy(m_ref.at[pl.ds(t, 1)], xs_hbm.at[pl.ds(dst, 1)], sem.at[1 + slot])

    def issue(t, carry):
        row_copy(t, d1_ref[0, 0, t], 0).start(priority=0)
        row_copy(t, d2_ref[0, 0, t], 1).start(priority=1)
        return carry

    lax.fori_loop(0, ts, issue, 0, unroll=8)
    for slot in range(2):
        pltpu.make_async_copy(m_ref, xs_hbm.at[pl.ds(0, ts)], sem.at[1 + slot]).wait()


def _moe_scatter(m, dest1, dest2, pads, n_rows):
    n_tok, d = m.shape
    ts = MOE_SCATTER_TILE
    n_exp = pads.shape[1] - 1
    dspec = pl.BlockSpec((1, 1, ts), lambda i: (i, 0, 0), memory_space=pltpu.SMEM)
    return pl.pallas_call(
        functools.partial(_moe_scatter_kernel, n_exp=n_exp),
        grid=(n_tok // ts,),
        in_specs=[dspec, dspec,
                  pl.BlockSpec(pads.shape, lambda i: (0, 0), memory_space=pltpu.SMEM),
                  pl.BlockSpec((ts, d), lambda i: (i, 0))],
        out_specs=pl.BlockSpec(memory_space=pl.ANY),
        out_shape=jax.ShapeDtypeStruct((n_rows, d), m.dtype),
        scratch_shapes=[pltpu.VMEM((MOE_ROWS, d), m.dtype), pltpu.SemaphoreType.DMA((3,))],
        compiler_params=pltpu.CompilerParams(
            dimension_semantics=("arbitrary",), vmem_limit_bytes=VMEM_LIMIT, has_side_effects=True),
        name="moe_scatter",
    )(dest1.reshape(n_tok // ts, 1, ts), dest2.reshape(n_tok // ts, 1, ts), pads, m)


def _moe_ffn_kernel(te_ref, na_ref, xs_ref, wg_ref, wu_ref, wd_ref, ys_ref, wgb_ref, wub_ref, wdb_ref):
    i = pl.program_id(0)
    active = i < na_ref[0]
    changed = (i == 0) | (te_ref[i] != te_ref[jnp.maximum(i - 1, 0)])

    @pl.when(active & changed)
    def _():
        wgb_ref[...] = wg_ref[0].astype(BF16)
        wub_ref[...] = wu_ref[0].astype(BF16)
        wdb_ref[...] = wd_ref[0].astype(BF16)

    @pl.when(active)
    def _():
        x = xs_ref[...].astype(BF16)
        mid = (_silu(_dot(x, wgb_ref[...])) * _dot(x, wub_ref[...])).astype(BF16)
        ys_ref[...] = _dot(mid, wdb_ref[...])

    @pl.when(jnp.logical_not(active))
    def _():
        ys_ref[...] = jnp.zeros_like(ys_ref)


def _moe_ffn(xs, tile_expert, n_active, wg, wu, wd):
    n_rows, dp = xs.shape
    tmr = MOE_ROWS
    _, d, de = wg.shape
    grid_spec = pltpu.PrefetchScalarGridSpec(
        num_scalar_prefetch=2,
        grid=(n_rows // tmr,),
        in_specs=[pl.BlockSpec((tmr, dp), lambda i, te, na: (jnp.minimum(i, na[0] - 1), 0)),
                  pl.BlockSpec((1, d, de), lambda i, te, na: (te[i], 0, 0)),
                  pl.BlockSpec((1, d, de), lambda i, te, na: (te[i], 0, 0)),
                  pl.BlockSpec((1, de, d), lambda i, te, na: (te[i], 0, 0))],
        out_specs=pl.BlockSpec((tmr, dp), lambda i, te, na: (i, 0)),
        scratch_shapes=[pltpu.VMEM((d, de), BF16), pltpu.VMEM((d, de), BF16), pltpu.VMEM((de, d), BF16)],
    )
    return pl.pallas_call(
        _moe_ffn_kernel,
        grid_spec=grid_spec,
        out_shape=jax.ShapeDtypeStruct((n_rows, dp), F32),
        compiler_params=pltpu.CompilerParams(
            dimension_semantics=("arbitrary",), vmem_limit_bytes=VMEM_LIMIT),
        name="moe_ffn",
    )(tile_expert, n_active, xs, wg, wu, wd)


def _moe_combine_kernel(d1_ref, d2_ref, h_ref, route_ref, mod_ref, nf_ref, ys_hbm, o_ref,
                        b1_ref, b2_ref, sem, *, final_norm):
    tm = h_ref.shape[1]

    def copy1(t):
        return pltpu.make_async_copy(ys_hbm.at[pl.ds(d1_ref[0, 0, t], 1)], b1_ref.at[pl.ds(t, 1)], sem.at[0])

    def copy2(t):
        return pltpu.make_async_copy(ys_hbm.at[pl.ds(d2_ref[0, 0, t], 1)], b2_ref.at[pl.ds(t, 1)], sem.at[1])

    def issue(t, carry):
        copy1(t).start(priority=0)
        copy2(t).start(priority=1)
        return carry

    lax.fori_loop(0, tm, issue, 0, unroll=8)
    pltpu.make_async_copy(ys_hbm.at[pl.ds(0, tm)], b1_ref, sem.at[0]).wait()
    pltpu.make_async_copy(ys_hbm.at[pl.ds(0, tm)], b2_ref, sem.at[1]).wait()
    route = route_ref[0]
    moe = route[:, 2:3] * b1_ref[...] + route[:, 3:4] * b2_ref[...]
    h = h_ref[0] + mod_ref[0, 5:6, :] * moe
    if final_norm:
        h = _rms(h) * nf_ref[...]
    o_ref[0] = h


def _moe_combine(ys, dest1, dest2, h, route, mod, nf, final_norm):
    bsz, seq, d = h.shape
    tm = MOE_COMBINE_TILE
    nt = seq // tm
    dspec = pl.BlockSpec((1, 1, tm), lambda b, i: (b * nt + i, 0, 0), memory_space=pltpu.SMEM)
    row = lambda n: pl.BlockSpec((1, tm, n), lambda b, i: (b, i, 0))
    return pl.pallas_call(
        functools.partial(_moe_combine_kernel, final_norm=final_norm),
        grid=(bsz, nt),
        in_specs=[dspec, dspec, row(d), row(LANES),
                  pl.BlockSpec((1, N_MOD, d), lambda b, i: (b, 0, 0)),
                  pl.BlockSpec(nf.shape, lambda b, i: (0, 0)),
                  pl.BlockSpec(memory_space=pl.ANY)],
        out_specs=row(d),
        out_shape=jax.ShapeDtypeStruct((bsz, seq, d), F32),
        scratch_shapes=[pltpu.VMEM((tm, ys.shape[1]), ys.dtype), pltpu.VMEM((tm, ys.shape[1]), ys.dtype),
                        pltpu.SemaphoreType.DMA((2,))],
        compiler_params=pltpu.CompilerParams(
            dimension_semantics=("arbitrary", "arbitrary"), vmem_limit_bytes=VMEM_LIMIT),
        name="moe_combine",
    )(dest1.reshape(bsz * nt, 1, tm), dest2.reshape(bsz * nt, 1, tm), h, route, mod, nf, ys)


def _moe_sparse(m, route, counts, wg, wu, wd, h, mod, nf, final_norm):
    bsz, seq, d = h.shape
    n_tok = bsz * seq
    tmr = MOE_ROWS
    n_exp = wg.shape[0]
    n_tiles = (2 * n_tok) // tmr + n_exp
    i32 = jnp.int32
    r = route.reshape(n_tok, LANES)
    e1, e2, k1, k2 = (r[:, j].astype(i32) for j in (0, 1, 4, 5))
    cnt = counts[-1, 0, ROUTE_OFF:ROUTE_OFF + n_exp].astype(i32)
    ptiles = (cnt + tmr - 1) // tmr
    tend = jnp.cumsum(ptiles)
    gstart = (tend - ptiles) * tmr
    eids = jnp.arange(n_exp, dtype=i32)
    dest1 = k1 + jnp.sum(jnp.where(e1[:, None] == eids[None, :], gstart[None, :], 0), axis=1)
    dest2 = k2 + jnp.sum(jnp.where(e2[:, None] == eids[None, :], gstart[None, :], 0), axis=1)
    n_active = tend[-1:]
    tile_expert = jnp.sum(jnp.arange(n_tiles, dtype=i32)[:, None] >= tend[None, :], axis=1).astype(i32)
    last_expert = jnp.sum(n_active - 1 >= tend).astype(i32)
    tile_expert = jnp.minimum(tile_expert, last_expert)
    pads = jnp.stack([jnp.concatenate([gstart + cnt, n_active]),
                      jnp.concatenate([gstart + ptiles * tmr, jnp.full((1,), n_tiles, i32)])]).astype(i32)
    xs = _moe_scatter(m.reshape(n_tok, m.shape[2]), dest1, dest2, pads, n_tiles * tmr)
    ys = _moe_ffn(xs, tile_expert, n_active.astype(i32), wg, wu, wd)
    return _moe_combine(ys, dest1, dest2, h, route, mod, nf, final_norm)


def kernel(x, c, norm1_g, norm2_g, w_ada, b_ada, w_in, lam_re, lam_im, log_step, s5_b_re, s5_b_im,
           s5_c_re, s5_c_im, s5_d, w_glu, b_glu, conv_w, a_log, dt_bias, gdn_norm_g, w_out,
           w_router_grp, w_router_exp, w_gate, w_up, w_down, normf_g):
    bsz, seq, d = x.shape
    depth = w_ada.shape[0]
    d_s5 = s5_d.shape[1]
    d_gdn = w_out.shape[1] - d_s5
    n_heads = d_gdn // GDN_HEAD_DIM
    assert seq % S5_TILE == 0 or seq < S5_TILE
    assert seq % ROW_TILE == 0 and seq % GDN_CHUNK == 0 and d_s5 % LANES == 0
    assert 2 * n_heads <= LANES and N_EXPERT_GROUPS + N_EXPERTS <= LANES

    h = x
    for l in range(depth):
        mod = _ada(c, w_ada[l], b_ada[l]).reshape(bsz, N_MOD, d)
        wi = w_in[l]
        c0, c1, c2 = d_s5, d_s5 + 3 * d_gdn, d_s5 + 4 * d_gdn
        wab = jnp.zeros((d, LANES), F32).at[:, :2 * n_heads].set(wi[:, c2:c2 + 2 * n_heads])
        u, qkv, z, ab = _inproj(h, mod, norm1_g[l].reshape(1, d), wi[:, :c0].astype(BF16),
                                wi[:, c0:c1].astype(BF16), wi[:, c1:c2].astype(BF16), wab.astype(BF16))
        tabs = _s5_tables(lam_re[l], lam_im[l], log_step[l], s5_b_re[l], s5_b_im[l],
                          s5_c_re[l], s5_c_im[l])
        ys = _s5(u, *tabs, s5_d[l])
        yg = _gdn2(qkv, z, ab, conv_w[l], a_log[l], dt_bias[l], gdn_norm_g[l])
        wr = jnp.zeros((d, LANES), F32)
        wr = wr.at[:, :N_EXPERT_GROUPS].set(w_router_grp[l])
        wr = wr.at[:, ROUTE_OFF:ROUTE_OFF + N_EXPERTS].set(w_router_exp[l])
        wr_hi = wr.astype(BF16)
        wr = jnp.concatenate([wr_hi, (wr - wr_hi.astype(F32)).astype(BF16)], axis=1)
        h1, m, route, counts = _outproj(ys, yg, h, mod, w_glu[l].astype(BF16), b_glu[l].reshape(1, d_s5),
                                        w_out[l, :d_s5].astype(BF16), w_out[l, d_s5:].astype(BF16),
                                        norm2_g[l].reshape(1, d), wr)
        h = _moe_sparse(m, route, counts, w_gate[l], w_up[l], w_down[l], h1, mod,
                        normf_g.reshape(1, d), final_norm=(l == depth - 1))
    return h
```

```python
import functools
import math

import jax
import jax.numpy as jnp
from jax import lax
from jax.experimental import pallas as pl
from jax.experimental.pallas import tpu as pltpu

F32 = jnp.float32
BF16 = jnp.bfloat16
HIGHEST = lax.Precision.HIGHEST
EPS = 1e-6

S5_GROUP = 16
S5_STATE = 64
GDN_HEAD_DIM = 128
CONV_WIDTH = 4
N_EXPERT_GROUPS = 4
EXPERTS_PER_GROUP = 8
N_EXPERTS = N_EXPERT_GROUPS * EXPERTS_PER_GROUP
N_MOD = 6

LANES = 128
SUBLANES = 8
VMEM_LIMIT = 56 * 1024 * 1024

S5_CHUNK = SUBLANES
S5_TILE = 2048
GDN_CHUNK = 64
GDN_BLOCK = 256
ROW_TILE = 512
MOE_TILE = 1024
MOE_ROWS = 256
MOE_SCATTER_TILE = 1024
MOE_COMBINE_TILE = 1024
ROUTE_OFF = N_EXPERT_GROUPS


def _dot(a, b, **kw):
    return jnp.dot(a, b, preferred_element_type=F32, **kw)


def _silu(v):
    return v * jax.nn.sigmoid(v)


def _rms(v):
    return v * lax.rsqrt(jnp.mean(v * v, axis=-1, keepdims=True) + EPS)


def _store_token_tiles(ref, idx, v):
    n, width = v.shape
    k = width // LANES
    for s in range(k):
        ref[idx + (pl.ds(s, n, stride=k), slice(None))] = v[:, s * LANES:(s + 1) * LANES]


def _load_token_tiles(ref, idx, n, k):
    return jnp.concatenate([ref[idx + (pl.ds(s, n, stride=k), slice(None))] for s in range(k)], axis=1)


def _ada_kernel(c_ref, w_ref, b_ref, o_ref):
    o_ref[...] = _dot(_silu(c_ref[...]), w_ref[...], precision=HIGHEST) + b_ref[...]


def _ada(c, w, b):
    bsz, d = c.shape
    n = w.shape[1]
    return pl.pallas_call(
        _ada_kernel,
        grid=(n // d,),
        in_specs=[pl.BlockSpec((bsz, d), lambda j: (0, 0)),
                  pl.BlockSpec((d, d), lambda j: (0, j)),
                  pl.BlockSpec((1, d), lambda j: (0, j))],
        out_specs=pl.BlockSpec((bsz, d), lambda j: (0, j)),
        out_shape=jax.ShapeDtypeStruct((bsz, n), F32),
        compiler_params=pltpu.CompilerParams(vmem_limit_bytes=VMEM_LIMIT),
        name="ada",
    )(c, w, b.reshape(1, n))


def _inproj_kernel(x_ref, mod_ref, g_ref, wu_ref, wqkv_ref, wz_ref, wab_ref,
                   u_ref, qkv_ref, z_ref, ab_ref):
    x = x_ref[0]
    y = _rms(x) * g_ref[...]
    h = (y * (1.0 + mod_ref[0, 1:2, :]) + mod_ref[0, 0:1, :]).astype(BF16)
    u_ref[0] = _dot(h, wu_ref[...])
    qkv_ref[0] = _dot(h, wqkv_ref[...]).astype(BF16)
    z_ref[0] = _dot(h, wz_ref[...]).astype(BF16)
    ab_ref[0] = _dot(h, wab_ref[...])


def _inproj(x, mod, g, wu, wqkv, wz, wab):
    bsz, seq, d = x.shape
    tm = ROW_TILE
    full = lambda a: pl.BlockSpec(a.shape, lambda b, i: (0,) * a.ndim)
    row = lambda n: pl.BlockSpec((1, tm, n), lambda b, i: (b, i, 0))
    return pl.pallas_call(
        _inproj_kernel,
        grid=(bsz, seq // tm),
        in_specs=[row(d), pl.BlockSpec((1, N_MOD, d), lambda b, i: (b, 0, 0)), full(g),
                  full(wu), full(wqkv), full(wz), full(wab)],
        out_specs=[row(wu.shape[1]), row(wqkv.shape[1]), row(wz.shape[1]), row(LANES)],
        out_shape=[jax.ShapeDtypeStruct((bsz, seq, wu.shape[1]), F32),
                   jax.ShapeDtypeStruct((bsz, seq, wqkv.shape[1]), BF16),
                   jax.ShapeDtypeStruct((bsz, seq, wz.shape[1]), BF16),
                   jax.ShapeDtypeStruct((bsz, seq, LANES), F32)],
        compiler_params=pltpu.CompilerParams(
            dimension_semantics=("parallel", "parallel"), vmem_limit_bytes=VMEM_LIMIT),
        name="inproj",
    )(x, mod, g, wu, wqkv, wz, wab)


def _s5_tables(lam_re, lam_im, log_step, b_re, b_im, c_re, c_im, seg_len):
    ch = S5_CHUNK
    n_grp, n_st = lam_re.shape
    gpb = LANES // S5_GROUP
    nblk = n_grp // gpb
    step = jnp.exp(log_step.astype(F32))[:, None]
    lr = lam_re.astype(F32)
    li = lam_im.astype(F32)

    def lam_pow(j):
        mag = jnp.exp(j * lr * step)
        ang = j * li * step
        return mag * jnp.cos(ang), mag * jnp.sin(ang)

    ar, ai = lam_pow(1.0)
    den = lr * lr + li * li
    fr = ((ar - 1.0) * lr + ai * li) / den
    fi = (ai * lr - (ar - 1.0) * li) / den
    bbr = fr[..., None] * b_re - fi[..., None] * b_im
    bbi = fr[..., None] * b_im + fi[..., None] * b_re
    pows = [lam_pow(float(j)) for j in range(ch + 1)]
    pr = jnp.stack([p[0] for p in pows])
    pi = jnp.stack([p[1] for p in pows])
    lbr = pr[:ch, :, :, None] * bbr[None] - pi[:ch, :, :, None] * bbi[None]
    lbi = pr[:ch, :, :, None] * bbi[None] + pi[:ch, :, :, None] * bbr[None]
    kdim = ch * LANES
    sdim = 2 * gpb * n_st

    def expand(small, lead, row_grp, col_grp, n_rows):
        n_cols = small.shape[-1]
        inner = small.shape[-2]
        full = jnp.broadcast_to(small[..., None, :, :], (nblk,) + lead + (gpb, inner, n_cols))
        full = full.reshape(nblk, n_rows, n_cols)
        rq = (jnp.arange(n_rows) // row_grp) % gpb
        cq = (jnp.arange(n_cols) // col_grp) % gpb
        return jnp.where(rq[:, None] == cq[None, :], full, 0.0).astype(BF16)

    kmat = (jnp.einsum('ghp,jgpk->jghk', c_re, lbr, precision=HIGHEST)
            - jnp.einsum('ghp,jgpk->jghk', c_im, lbi, precision=HIGHEST))
    s_idx = jnp.arange(ch)[:, None]
    t_idx = jnp.arange(ch)[None, :]
    lag = jnp.clip(t_idx - s_idx, 0, ch - 1)
    kst = jnp.where((t_idx >= s_idx)[..., None, None, None], kmat[lag], 0.0)
    kst = kst.reshape(ch, ch, nblk, gpb, S5_GROUP, S5_GROUP)
    t_small = kst.transpose(2, 0, 5, 1, 3, 4).reshape(nblk, ch, S5_GROUP, kdim)
    tmat = expand(t_small, (ch,), S5_GROUP, S5_GROUP, kdim)

    wri = jnp.stack([lbr[::-1], lbi[::-1]]).reshape(2, ch, nblk, gpb, n_st, S5_GROUP)
    w_small = wri.transpose(2, 1, 5, 0, 3, 4).reshape(nblk, ch, S5_GROUP, sdim)
    win = expand(w_small, (ch,), S5_GROUP, n_st, kdim)

    pr1 = pr[1:, :, None, :]
    pi1 = pi[1:, :, None, :]
    clr = c_re[None] * pr1 - c_im[None] * pi1
    cli = c_re[None] * pi1 + c_im[None] * pr1
    wo = jnp.stack([clr, -cli]).reshape(2, ch, nblk, gpb, S5_GROUP, n_st)
    o_small = wo.transpose(2, 0, 5, 1, 3, 4).reshape(nblk, 2, n_st, kdim)
    wout = expand(o_small, (2,), n_st, S5_GROUP, sdim)

    half = gpb * n_st
    amat = jnp.stack([pr[ch].reshape(nblk, half), pi[ch].reshape(nblk, half)], axis=1)
    jb = (ch * jnp.arange(seg_len + 1, dtype=F32))[:, None, None]
    mag = jnp.exp(jb * (lr * step)[None])
    ang = jb * (li * step)[None]
    apow = jnp.concatenate([(mag * jnp.cos(ang)).reshape(seg_len + 1, nblk, half),
                            (mag * jnp.sin(ang)).reshape(seg_len + 1, nblk, half)], axis=-1)
    return tmat, win, wout, amat, apow.transpose(1, 0, 2)


def _s5_kernel(u_ref, t_ref, win_ref, wout_ref, a_ref, ap_ref, d_ref, y_ref, st_ref, v_ref, xp_ref):
    ch = S5_CHUNK
    n = u_ref.shape[1] // ch
    half = st_ref.shape[1]
    nseg = SUBLANES
    seg = n // nseg

    @pl.when(pl.program_id(2) == 0)
    def _():
        st_ref[...] = jnp.zeros_like(st_ref)

    slabs = [u_ref[0, pl.ds(s, n, stride=ch), :] for s in range(ch)]
    ucat = jnp.concatenate(slabs, axis=1).astype(BF16)
    y = _dot(ucat, t_ref[0])
    v = _dot(ucat, win_ref[0])
    nlb = v.shape[1] // LANES
    hlb = nlb // 2
    for j in range(nlb):
        v_ref[j] = v[:, j * LANES:(j + 1) * LANES]

    a_r = jnp.broadcast_to(a_ref[0, 0:1, :], (nseg, half))
    a_i = jnp.broadcast_to(a_ref[0, 1:2, :], (nseg, half))

    def body(b, carry):
        x_r, x_i = carry
        rows = pl.ds(b, nseg, stride=seg)
        for j in range(hlb):
            xp_ref[j, rows, :] = x_r[:, j * LANES:(j + 1) * LANES]
            xp_ref[hlb + j, rows, :] = x_i[:, j * LANES:(j + 1) * LANES]
        v_r = jnp.concatenate([v_ref[j, rows, :] for j in range(hlb)], axis=1)
        v_i = jnp.concatenate([v_ref[hlb + j, rows, :] for j in range(hlb)], axis=1)
        return a_r * x_r - a_i * x_i + v_r, a_r * x_i + a_i * x_r + v_i

    zero = jnp.zeros((nseg, half), F32)
    e_r, e_i = lax.fori_loop(0, seg, body, (zero, zero))
    p_r = ap_ref[0, 0:seg, 0:half]
    p_i = ap_ref[0, 0:seg, half:2 * half]
    s_r = ap_ref[0, seg:seg + 1, 0:half]
    s_i = ap_ref[0, seg:seg + 1, half:2 * half]
    x_r = st_ref[0:1, :]
    x_i = st_ref[1:2, :]
    for a in range(nseg):
        rows = slice(a * seg, (a + 1) * seg)
        c_r = p_r * x_r - p_i * x_i
        c_i = p_r * x_i + p_i * x_r
        for j in range(hlb):
            xp_ref[j, rows, :] += c_r[:, j * LANES:(j + 1) * LANES]
            xp_ref[hlb + j, rows, :] += c_i[:, j * LANES:(j + 1) * LANES]
        x_r, x_i = (s_r * x_r - s_i * x_i + e_r[a:a + 1], s_r * x_i + s_i * x_r + e_i[a:a + 1])
    st_ref[0:1, :] = x_r
    st_ref[1:2, :] = x_i
    xp = jnp.concatenate([xp_ref[j] for j in range(nlb)], axis=1)
    y = y + _dot(xp.astype(BF16), wout_ref[0])
    d = d_ref[0]
    for t in range(ch):
        y_ref[0, pl.ds(t, n, stride=ch), :] = y[:, t * LANES:(t + 1) * LANES] + d * slabs[t]


def _s5_seg_len(seq):
    return min(S5_TILE, seq) // S5_CHUNK // SUBLANES


def _s5(u, tmat, win, wout, amat, apow, d_skip):
    bsz, seq, dch = u.shape
    nblk = dch // LANES
    tt = min(S5_TILE, seq)
    n = tt // S5_CHUNK
    assert n % SUBLANES == 0 and apow.shape[1] == n // SUBLANES + 1
    kdim = S5_CHUNK * LANES
    sdim = win.shape[2]
    wspec = lambda a: pl.BlockSpec((1,) + a.shape[1:], lambda b, c, i: (c, 0, 0))
    return pl.pallas_call(
        _s5_kernel,
        grid=(bsz, nblk, seq // tt),
        in_specs=[pl.BlockSpec((1, tt, LANES), lambda b, c, i: (b, i, c)),
                  wspec(tmat), wspec(win), wspec(wout), wspec(amat), wspec(apow),
                  pl.BlockSpec((1, 1, LANES), lambda b, c, i: (c, 0, 0))],
        out_specs=pl.BlockSpec((1, tt, LANES), lambda b, c, i: (b, i, c)),
        out_shape=jax.ShapeDtypeStruct((bsz, seq, dch), F32),
        scratch_shapes=[pltpu.VMEM((2, sdim // 2), F32),
                        pltpu.VMEM((sdim // LANES, n, LANES), F32),
                        pltpu.VMEM((sdim // LANES, n, LANES), F32)],
        compiler_params=pltpu.CompilerParams(
            dimension_semantics=("parallel", "parallel", "arbitrary"),
            vmem_limit_bytes=VMEM_LIMIT),
        name="s5",
    )(u, tmat, win, wout, amat, apow, d_skip.reshape(nblk, 1, LANES))


def _cumsum_rows(v):
    n = v.shape[0]
    row = lax.broadcasted_iota(jnp.int32, v.shape, 0)
    sh = 1
    while sh < n:
        v = v + jnp.where(row >= sh, pltpu.roll(v, sh, axis=0), 0.0)
        sh *= 2
    return v


def _unit_lower_inverse(lm):
    n = lm.shape[0]
    ri = lax.broadcasted_iota(jnp.int32, (n, n), 0)
    ci = lax.broadcasted_iota(jnp.int32, (n, n), 1)
    eye = jnp.where(ri == ci, 1.0, 0.0).astype(F32)
    m = 1
    d = eye
    while m < n:
        mask = ((ri // (2 * m)) == (ci // (2 * m))) & (((ri // m) % 2) == 1) & (((ci // m) % 2) == 0)
        e = jnp.where(mask, lm, 0.0)
        if m == 1:
            d = eye - e
        else:
            de = _dot(d.astype(BF16), e.astype(BF16))
            d = d - _dot(de.astype(BF16), d.astype(BF16))
        m *= 2
    return d


def _gdn_kernel(qkv_ref, z_ref, ab_ref, cw_ref, gp_ref, ng_ref, o_ref, s_ref, xp_ref, *, n_heads):
    ck = GDN_CHUNK
    dh = GDN_HEAD_DIM
    bsz = qkv_ref.shape[0]
    dg = n_heads * dh
    halo = SUBLANES

    @pl.when(pl.program_id(0) == 0)
    def _():
        s_ref[...] = jnp.zeros_like(s_ref)
        xp_ref[:, 0:halo, :] = jnp.zeros((bsz, halo, 3 * dg), F32)

    ri = lax.broadcasted_iota(jnp.int32, (ck, ck), 0)
    ci = lax.broadcasted_iota(jnp.int32, (ck, ck), 1)
    causal = ri >= ci
    strict = ri > ci
    lane = lax.broadcasted_iota(jnp.int32, (ck, LANES), 1)
    a_log = gp_ref[0:1, :]
    dt_bias = gp_ref[1:2, :]
    ng = ng_ref[...]

    for b in range(bsz):
        xp_ref[b, halo:halo + ck, :] = qkv_ref[b].astype(F32)
        conv = cw_ref[0:1, :] * xp_ref[b, halo - 3:halo - 3 + ck, :]
        for j in range(1, CONV_WIDTH):
            conv = conv + cw_ref[j:j + 1, :] * xp_ref[b, halo - 3 + j:halo - 3 + j + ck, :]
        xp_ref[b, 0:halo, :] = xp_ref[b, ck:ck + halo, :]
        act = _silu(conv)

        ab = ab_ref[b]
        sp = jnp.maximum(ab + dt_bias, 0.0) + jnp.log1p(jnp.exp(-jnp.abs(ab + dt_bias)))
        g = jnp.where(lane < n_heads, -jnp.exp(a_log) * sp, 0.0)
        gc = _cumsum_rows(g)
        gct = gc.T
        beta_all = jax.nn.sigmoid(ab)
        zb = z_ref[b].astype(F32)

        for h in range(n_heads):
            q = act[:, h * dh:(h + 1) * dh]
            k = act[:, dg + h * dh:dg + (h + 1) * dh]
            v = act[:, 2 * dg + h * dh:2 * dg + (h + 1) * dh]
            q = q * lax.rsqrt(jnp.sum(q * q, axis=-1, keepdims=True) + EPS) * (dh ** -0.5)
            k = k * lax.rsqrt(jnp.sum(k * k, axis=-1, keepdims=True) + EPS)
            beta = beta_all[:, n_heads + h:n_heads + h + 1]
            gcol = gc[:, h:h + 1]
            grow = gct[h:h + 1, :]
            glast = gc[ck - 1:ck, h:h + 1]
            decay = jnp.exp(jnp.where(causal, gcol - grow, -1e30))
            kb = k * beta
            kq = jnp.concatenate([kb, q], axis=0).astype(BF16)
            m1 = lax.dot_general(kq, k.astype(BF16), (((1,), (1,)), ((), ())),
                                 preferred_element_type=F32)
            lm = jnp.where(strict, m1[0:ck] * decay, 0.0)
            qk = m1[ck:2 * ck] * decay
            tinv = _unit_lower_inverse(lm)
            eg = jnp.exp(gcol)
            rhs = jnp.concatenate([v * beta, kb * eg], axis=1).astype(BF16)
            w = _dot(tinv.astype(BF16), rhs)
            value = w[:, 0:dh]
            kcd = w[:, dh:2 * dh]
            qd = q * eg
            kt = k * jnp.exp(glast - gcol)
            st = s_ref[b * n_heads + h]
            m2 = _dot(jnp.concatenate([kcd, qd], axis=0).astype(BF16), st.astype(BF16))
            v_new = value - m2[0:ck]
            vnb = v_new.astype(BF16)
            o = m2[ck:2 * ck] + _dot(qk.astype(BF16), vnb)
            s_ref[b * n_heads + h] = st * jnp.exp(glast) + lax.dot_general(
                kt.astype(BF16), vnb, (((0,), (0,)), ((), ())), preferred_element_type=F32)
            zh = zb[:, h * dh:(h + 1) * dh]
            o_ref[b, :, h * dh:(h + 1) * dh] = (_rms(o) * ng * _silu(zh)).astype(o_ref.dtype)


def _gdn(qkv, z, ab, conv_w, a_log, dt_bias, norm_g):
    bsz, seq, _ = qkv.shape
    dg = z.shape[2]
    n_heads = dg // GDN_HEAD_DIM
    ck = GDN_CHUNK
    gp = jnp.zeros((2, LANES), F32).at[0, :n_heads].set(a_log).at[1, :n_heads].set(dt_bias)
    blk = lambda n: pl.BlockSpec((bsz, ck, n), lambda i: (0, i, 0))
    full = lambda a: pl.BlockSpec(a.shape, lambda i: (0,) * a.ndim)
    ng = norm_g.reshape(1, GDN_HEAD_DIM)
    return pl.pallas_call(
        functools.partial(_gdn_kernel, n_heads=n_heads),
        grid=(seq // ck,),
        in_specs=[blk(3 * dg), blk(dg), blk(LANES), full(conv_w), full(gp), full(ng)],
        out_specs=blk(dg),
        out_shape=jax.ShapeDtypeStruct((bsz, seq, dg), BF16),
        scratch_shapes=[pltpu.VMEM((bsz * n_heads, GDN_HEAD_DIM, GDN_HEAD_DIM), F32),
                        pltpu.VMEM((bsz, ck + SUBLANES, 3 * dg), F32)],
        compiler_params=pltpu.CompilerParams(
            dimension_semantics=("arbitrary",), vmem_limit_bytes=VMEM_LIMIT),
        name="gdn",
    )(qkv, z, ab, conv_w, gp, ng)


def _gdn2_kernel(qkv_ref, z_ref, ab_ref, cw_ref, gp_ref, ng_ref, o_ref,
                 s_ref, xp_ref, gc_ref, gct_ref, kbq_ref, kn_ref, rhs_ref, cq_ref, qkt_ref,
                 lm_ref, d_ref, db_ref, de_ref, val_ref, vn_ref, *, n_heads):
    tt = qkv_ref.shape[1]
    dh = GDN_HEAD_DIM
    bsz = qkv_ref.shape[0]
    dg = n_heads * dh
    halo = SUBLANES
    pairs = [(b, h) for b in range(bsz) for h in range(n_heads)]

    @pl.when(pl.program_id(0) == 0)
    def _():
        s_ref[...] = jnp.zeros_like(s_ref)
        xp_ref[...] = jnp.zeros_like(xp_ref)

    ri = lax.broadcasted_iota(jnp.int32, (tt, tt), 0)
    ci = lax.broadcasted_iota(jnp.int32, (tt, tt), 1)
    lane = lax.broadcasted_iota(jnp.int32, (tt, LANES), 1)
    a_log = gp_ref[0:1, :]
    dt_bias = gp_ref[1:2, :]
    shift_op = jnp.concatenate(
        [jnp.where(ri == ci + j, 1.0, 0.0) for j in range(1, CONV_WIDTH)], axis=0).astype(BF16)

    for b in range(bsz):
        x = qkv_ref[b]
        x32 = x.astype(F32)
        shifted = _dot(shift_op, x)
        edge = jnp.concatenate([xp_ref[b], x32[0:halo]], axis=0)
        conv = cw_ref[CONV_WIDTH - 1:CONV_WIDTH, :] * x32
        for j in range(1, CONV_WIDTH):
            sh_j = jnp.concatenate([edge[halo - j:2 * halo - j],
                                    shifted[(j - 1) * tt + halo:j * tt]], axis=0)
            conv = conv + cw_ref[CONV_WIDTH - 1 - j:CONV_WIDTH - j, :] * sh_j
        xp_ref[b] = x32[tt - halo:tt]
        act = _silu(conv)

        ab = ab_ref[b]
        sp = jnp.maximum(ab + dt_bias, 0.0) + jnp.log1p(jnp.exp(-jnp.abs(ab + dt_bias)))
        g = jnp.where(lane < n_heads, -jnp.exp(a_log) * sp, 0.0)
        gc = _cumsum_rows(g)
        gc_ref[b] = gc
        gct_ref[b] = gc.T
        glast = gc[tt - 1:tt, :]
        egc = jnp.exp(gc)
        ekt = jnp.exp(glast - gc)
        beta_all = jax.nn.sigmoid(ab)
        for h in range(n_heads):
            p = b * n_heads + h
            q = act[:, h * dh:(h + 1) * dh]
            k = act[:, dg + h * dh:dg + (h + 1) * dh]
            v = act[:, 2 * dg + h * dh:2 * dg + (h + 1) * dh]
            q = q * lax.rsqrt(jnp.sum(q * q, axis=-1, keepdims=True) + EPS) * (dh ** -0.5)
            k = k * lax.rsqrt(jnp.sum(k * k, axis=-1, keepdims=True) + EPS)
            beta = beta_all[:, n_heads + h:n_heads + h + 1]
            eg = egc[:, h:h + 1]
            kb = k * beta
            kbq_ref[p, 0:tt, :] = kb.astype(BF16)
            kbq_ref[p, tt:2 * tt, :] = q.astype(BF16)
            kn_ref[p] = k.astype(BF16)
            rhs_ref[p, :, 0:dh] = (v * beta).astype(BF16)
            rhs_ref[p, :, dh:2 * dh] = (kb * eg).astype(BF16)
            cq_ref[p, tt:2 * tt, :] = (q * eg).astype(BF16)
            qkt_ref[p, tt:tt + dh, :] = (k * ekt[:, h:h + 1]).T.astype(BF16)

    for b, h in pairs:
        p = b * n_heads + h
        m1 = lax.dot_general(kbq_ref[p], kn_ref[p], (((1,), (1,)), ((), ())),
                             preferred_element_type=F32)
        gcol = gc_ref[b, :, h:h + 1]
        grow = gct_ref[b, h:h + 1, :]
        decay = jnp.exp(jnp.where(ri >= ci, gcol - grow, -1e30))
        lm = jnp.where(ri > ci, m1[0:tt] * decay, 0.0)
        lm_ref[p] = lm.astype(BF16)
        qkt_ref[p, 0:tt, :] = (m1[tt:2 * tt] * decay).astype(BF16)
        first = jnp.where(ri == ci + 1, jnp.where((ci & 1) == 0, lm, 0.0), 0.0)
        d = jnp.where(ri == ci, 1.0, 0.0) - first
        d_ref[p] = d
        db_ref[p] = d.astype(BF16)

    m = 2
    sh = 1
    while m < tt:
        rb = ri >> sh
        cb = ci >> sh
        sel = jnp.where(rb == cb + 1, jnp.where((cb & 1) == 0, 1.0, 0.0), 0.0).astype(BF16)
        for b, h in pairs:
            p = b * n_heads + h
            de_ref[p] = _dot(db_ref[p], lm_ref[p] * sel).astype(BF16)
        for b, h in pairs:
            p = b * n_heads + h
            d = d_ref[p] - _dot(de_ref[p], db_ref[p])
            d_ref[p] = d
            db_ref[p] = d.astype(BF16)
        m *= 2
        sh += 1

    for b, h in pairs:
        p = b * n_heads + h
        w = _dot(db_ref[p], rhs_ref[p])
        val_ref[p] = w[:, 0:dh]
        cq_ref[p, 0:tt, :] = w[:, dh:2 * dh].astype(BF16)

    for b, h in pairs:
        p = b * n_heads + h
        m2 = _dot(cq_ref[p], s_ref[p].astype(BF16))
        vn_ref[p] = (val_ref[p] - m2[0:tt]).astype(BF16)
        val_ref[p] = m2[tt:2 * tt]
    for b, h in pairs:
        p = b * n_heads + h
        r = _dot(qkt_ref[p], vn_ref[p])
        o = val_ref[p] + r[0:tt]
        egl = jnp.exp(gc_ref[b, tt - 1:tt, h:h + 1])
        s_ref[p] = s_ref[p] * egl + r[tt:tt + dh]
        zh = z_ref[b, :, h * dh:(h + 1) * dh].astype(F32)
        o_ref[b, :, h * dh:(h + 1) * dh] = (_rms(o) * ng_ref[...] * _silu(zh)).astype(o_ref.dtype)


def _gdn2(qkv, z, ab, conv_w, a_log, dt_bias, norm_g):
    bsz, seq, _ = qkv.shape
    dg = z.shape[2]
    dh = GDN_HEAD_DIM
    n_heads = dg // dh
    tt = GDN_BLOCK
    npair = bsz * n_heads
    gp = jnp.zeros((2, LANES), F32).at[0, :n_heads].set(a_log).at[1, :n_heads].set(dt_bias)
    blk = lambda n: pl.BlockSpec((bsz, tt, n), lambda i: (0, i, 0))
    full = lambda a: pl.BlockSpec(a.shape, lambda i: (0,) * a.ndim)
    ng = norm_g.reshape(1, dh)
    return pl.pallas_call(
        functools.partial(_gdn2_kernel, n_heads=n_heads),
        grid=(seq // tt,),
        in_specs=[blk(3 * dg), blk(dg), blk(LANES), full(conv_w), full(gp), full(ng)],
        out_specs=blk(dg),
        out_shape=jax.ShapeDtypeStruct((bsz, seq, dg), BF16),
        scratch_shapes=[pltpu.VMEM((npair, dh, dh), F32),
                        pltpu.VMEM((bsz, SUBLANES, 3 * dg), F32),
                        pltpu.VMEM((bsz, tt, LANES), F32),
                        pltpu.VMEM((bsz, LANES, tt), F32),
                        pltpu.VMEM((npair, 2 * tt, dh), BF16),
                        pltpu.VMEM((npair, tt, dh), BF16),
                        pltpu.VMEM((npair, tt, 2 * dh), BF16),
                        pltpu.VMEM((npair, 2 * tt, dh), BF16),
                        pltpu.VMEM((npair, tt + dh, tt), BF16),
                        pltpu.VMEM((npair, tt, tt), BF16),
                        pltpu.VMEM((npair, tt, tt), F32),
                        pltpu.VMEM((npair, tt, tt), BF16),
                        pltpu.VMEM((npair, tt, tt), BF16),
                        pltpu.VMEM((npair, tt, dh), F32),
                        pltpu.VMEM((npair, tt, dh), BF16)],
        compiler_params=pltpu.CompilerParams(
            dimension_semantics=("arbitrary",), vmem_limit_bytes=VMEM_LIMIT),
        name="gdn",
    )(qkv, z, ab, conv_w, gp, ng)


def _outproj_kernel(ys_ref, yg_ref, x_ref, mod_ref, wglu_ref, bglu_ref, wos_ref, wog_ref,
                    g2_ref, wr_ref, h_ref, m_ref, route_ref, count_ref, cnt_ref):
    y = jax.nn.gelu(ys_ref[0])
    gate = jax.nn.sigmoid(_dot(y.astype(BF16), wglu_ref[...]) + bglu_ref[...])
    y = (y * gate).astype(BF16)
    mix = _dot(y, wos_ref[...]) + _dot(yg_ref[0], wog_ref[...])
    h = x_ref[0] + mod_ref[0, 2:3, :] * mix
    h_ref[0] = h
    m = _rms(h) * g2_ref[...] * (1.0 + mod_ref[0, 4:5, :]) + mod_ref[0, 3:4, :]
    _store_token_tiles(m_ref, (0,), m)

    m_hi = m.astype(BF16)
    m_lo = (m - m_hi.astype(F32)).astype(BF16)
    part = _dot(m_hi, wr_ref[...])
    logits = part[:, 0:LANES] + part[:, LANES:2 * LANES] + _dot(m_lo, wr_ref[:, 0:LANES])
    lane = lax.broadcasted_iota(jnp.int32, logits.shape, 1)
    neg = -1e30
    is_grp = lane < N_EXPERT_GROUPS
    gl = jnp.where(is_grp, logits, neg)
    gmax = jnp.max(gl, axis=-1, keepdims=True)
    gidx = jnp.min(jnp.where(gl == gmax, lane, LANES), axis=-1, keepdims=True)
    p_grp = 1.0 / jnp.sum(jnp.where(is_grp, jnp.exp(gl - gmax), 0.0), axis=-1, keepdims=True)
    lo = ROUTE_OFF + gidx * EXPERTS_PER_GROUP
    el = jnp.where((lane >= lo) & (lane < lo + EXPERTS_PER_GROUP), logits, neg)
    v1 = jnp.max(el, axis=-1, keepdims=True)
    i1 = jnp.min(jnp.where(el == v1, lane, LANES), axis=-1, keepdims=True)
    el2 = jnp.where(lane == i1, neg, el)
    v2 = jnp.max(el2, axis=-1, keepdims=True)
    i2 = jnp.min(jnp.where(el2 == v2, lane, LANES), axis=-1, keepdims=True)
    t = jnp.exp(v2 - v1)
    w1 = p_grp / (1.0 + t)
    w2 = w1 * t

    @pl.when((pl.program_id(0) == 0) & (pl.program_id(1) == 0))
    def _():
        cnt_ref[...] = jnp.zeros_like(cnt_ref)

    tm = logits.shape[0]
    oh1 = lane == i1
    oh2 = lane == i2
    cnt = jnp.where(oh1, 1.0, 0.0) + jnp.where(oh2, 1.0, 0.0)
    earlier = (lax.broadcasted_iota(jnp.int32, (tm, tm), 0) > lax.broadcasted_iota(jnp.int32, (tm, tm), 1))
    before = _dot(jnp.where(earlier, 1.0, 0.0).astype(BF16), cnt.astype(BF16)) + cnt_ref[...]
    r1 = jnp.sum(jnp.where(oh1, before, 0.0), axis=-1, keepdims=True)
    r2 = jnp.sum(jnp.where(oh2, before, 0.0), axis=-1, keepdims=True)
    cnt_ref[...] += jnp.sum(cnt, axis=0, keepdims=True)
    count_ref[0] = cnt_ref[...]
    cols = [(i1 - ROUTE_OFF).astype(F32), (i2 - ROUTE_OFF).astype(F32), w1, w2, r1, r2]
    route = jnp.zeros_like(logits)
    for j, col in enumerate(cols):
        route = jnp.where(lane == j, col, route)
    route_ref[0] = route


def _outproj(ys, yg, x, mod, wglu, bglu, wos, wog, g2, wr):
    bsz, seq, d = x.shape
    tm = ROW_TILE
    full = lambda a: pl.BlockSpec(a.shape, lambda b, i: (0,) * a.ndim)
    row = lambda n: pl.BlockSpec((1, tm, n), lambda b, i: (b, i, 0))
    return pl.pallas_call(
        _outproj_kernel,
        grid=(bsz, seq // tm),
        in_specs=[row(ys.shape[2]), row(yg.shape[2]), row(d),
                  pl.BlockSpec((1, N_MOD, d), lambda b, i: (b, 0, 0)),
                  full(wglu), full(bglu), full(wos), full(wog), full(g2), full(wr)],
        out_specs=[row(d), pl.BlockSpec((1, tm * (d // LANES), LANES), lambda b, i: (b, i, 0)), row(LANES),
                   pl.BlockSpec((1, 1, LANES), lambda b, i: (b * (seq // tm) + i, 0, 0))],
        out_shape=[jax.ShapeDtypeStruct((bsz, seq, d), F32),
                   jax.ShapeDtypeStruct((bsz, seq * (d // LANES), LANES), F32),
                   jax.ShapeDtypeStruct((bsz, seq, LANES), F32),
                   jax.ShapeDtypeStruct((bsz * (seq // tm), 1, LANES), F32)],
        scratch_shapes=[pltpu.VMEM((1, LANES), F32)],
        compiler_params=pltpu.CompilerParams(
            dimension_semantics=("arbitrary", "arbitrary"), vmem_limit_bytes=VMEM_LIMIT),
        name="outproj",
    )(ys, yg, x, mod, wglu, bglu, wos, wog, g2, wr)


def _moe_kernel(m_ref, comb_ref, wg_ref, wu_ref, wd_ref, h_ref, mod_ref, nf_ref, o_ref, acc_ref,
                *, final_norm):
    e = pl.program_id(2)

    @pl.when(e == 0)
    def _():
        acc_ref[...] = jnp.zeros_like(acc_ref)

    m = m_ref[0]
    mid = (_silu(_dot(m, wg_ref[0])) * _dot(m, wu_ref[0])).astype(BF16)
    y = _dot(mid, wd_ref[0])
    comb = comb_ref[0]
    lane = lax.broadcasted_iota(jnp.int32, comb.shape, 1)
    col = jnp.sum(jnp.where(lane == e + ROUTE_OFF, comb, 0.0), axis=-1, keepdims=True)
    acc_ref[...] += col * y

    @pl.when(e == pl.num_programs(2) - 1)
    def _():
        h = h_ref[0] + mod_ref[0, 5:6, :] * acc_ref[...]
        if final_norm:
            h = _rms(h) * nf_ref[...]
        o_ref[0] = h


def _moe(m, comb, wg, wu, wd, h, mod, nf, final_norm):
    bsz, seq, d = h.shape
    tm = min(MOE_TILE, seq)
    n_exp, _, de = wg.shape
    row = lambda n: pl.BlockSpec((1, tm, n), lambda b, i, e: (b, i, 0))
    return pl.pallas_call(
        functools.partial(_moe_kernel, final_norm=final_norm),
        grid=(bsz, seq // tm, n_exp),
        in_specs=[row(d), row(LANES),
                  pl.BlockSpec((1, d, de), lambda b, i, e: (e, 0, 0)),
                  pl.BlockSpec((1, d, de), lambda b, i, e: (e, 0, 0)),
                  pl.BlockSpec((1, de, d), lambda b, i, e: (e, 0, 0)),
                  row(d), pl.BlockSpec((1, N_MOD, d), lambda b, i, e: (b, 0, 0)),
                  pl.BlockSpec(nf.shape, lambda b, i, e: (0, 0))],
        out_specs=row(d),
        out_shape=jax.ShapeDtypeStruct((bsz, seq, d), F32),
        scratch_shapes=[pltpu.VMEM((tm, d), F32)],
        compiler_params=pltpu.CompilerParams(
            dimension_semantics=("parallel", "parallel", "arbitrary"),
            vmem_limit_bytes=VMEM_LIMIT),
        name="moe",
    )(m, comb, wg, wu, wd, h, mod, nf)


def _moe_scatter_kernel(d1_ref, d2_ref, pad_ref, m_ref, xs_hbm, zero_ref, sem, *, n_exp, k):
    i = pl.program_id(0)
    ts = d1_ref.shape[2]
    trows = zero_ref.shape[0]

    def tok(ref, t):
        return ref.at[pl.ds(pl.multiple_of(t * k, k), k)]

    def fill_copy(r):
        return pltpu.make_async_copy(tok(zero_ref, 0), tok(xs_hbm, r), sem.at[0])

    def tile_copy(j):
        return pltpu.make_async_copy(
            zero_ref, xs_hbm.at[pl.ds(pl.multiple_of(j * trows, trows), trows)], sem.at[0])

    def run(lo, hi, copy):
        def fill(r, carry):
            copy(r).start()
            return carry

        def drain(r, carry):
            copy(r).wait()
            return carry

        lax.fori_loop(lo, hi, fill, 0)
        lax.fori_loop(lo, hi, drain, 0)

    @pl.when(i == 0)
    def _():
        zero_ref[...] = jnp.zeros_like(zero_ref)
        for e in range(n_exp):
            run(pad_ref[0, e], pad_ref[1, e], fill_copy)
        run(pad_ref[0, n_exp], pad_ref[1, n_exp], tile_copy)

    def row_copy(t, dst, slot):
        return pltpu.make_async_copy(tok(m_ref, t), tok(xs_hbm, dst), sem.at[1 + slot])

    def issue(t, carry):
        row_copy(t, d1_ref[0, 0, t], 0).start(priority=0)
        row_copy(t, d2_ref[0, 0, t], 1).start(priority=1)
        return carry

    lax.fori_loop(0, ts, issue, 0, unroll=8)
    for slot in range(2):
        pltpu.make_async_copy(m_ref, xs_hbm.at[pl.ds(0, ts * k)], sem.at[1 + slot]).wait()


def _moe_scatter(m, dest1, dest2, pads, n_rows, k):
    n_tok = m.shape[0] // k
    ts = MOE_SCATTER_TILE
    n_exp = pads.shape[1] - 1
    dspec = pl.BlockSpec((1, 1, ts), lambda i: (i, 0, 0), memory_space=pltpu.SMEM)
    return pl.pallas_call(
        functools.partial(_moe_scatter_kernel, n_exp=n_exp, k=k),
        grid=(n_tok // ts,),
        in_specs=[dspec, dspec,
                  pl.BlockSpec(pads.shape, lambda i: (0, 0), memory_space=pltpu.SMEM),
                  pl.BlockSpec((ts * k, LANES), lambda i: (i, 0))],
        out_specs=pl.BlockSpec(memory_space=pl.ANY),
        out_shape=jax.ShapeDtypeStruct((n_rows * k, LANES), m.dtype),
        scratch_shapes=[pltpu.VMEM((MOE_ROWS * k, LANES), m.dtype), pltpu.SemaphoreType.DMA((3,))],
        compiler_params=pltpu.CompilerParams(
            dimension_semantics=("arbitrary",), vmem_limit_bytes=VMEM_LIMIT, has_side_effects=True),
        name="moe_scatter",
    )(dest1.reshape(n_tok // ts, 1, ts), dest2.reshape(n_tok // ts, 1, ts), pads, m)


def _moe_ffn_kernel(te_ref, na_ref, xs_ref, wg_ref, wu_ref, wd_ref, ys_ref, wgb_ref, wub_ref, wdb_ref):
    i = pl.program_id(0)
    active = i < na_ref[0]
    changed = (i == 0) | (te_ref[i] != te_ref[jnp.maximum(i - 1, 0)])

    @pl.when(active & changed)
    def _():
        wgb_ref[...] = wg_ref[0].astype(BF16)
        wub_ref[...] = wu_ref[0].astype(BF16)
        wdb_ref[...] = wd_ref[0].astype(BF16)

    @pl.when(active)
    def _():
        k = wgb_ref.shape[0] // LANES
        x = _load_token_tiles(xs_ref, (), xs_ref.shape[0] // k, k).astype(BF16)
        mid = (_silu(_dot(x, wgb_ref[...])) * _dot(x, wub_ref[...])).astype(BF16)
        _store_token_tiles(ys_ref, (), _dot(mid, wdb_ref[...]))

    @pl.when(jnp.logical_not(active))
    def _():
        ys_ref[...] = jnp.zeros_like(ys_ref)


def _moe_ffn(xs, tile_expert, n_active, wg, wu, wd):
    _, d, de = wg.shape
    tmr = MOE_ROWS * (d // LANES)
    n_rows, dp = xs.shape
    grid_spec = pltpu.PrefetchScalarGridSpec(
        num_scalar_prefetch=2,
        grid=(n_rows // tmr,),
        in_specs=[pl.BlockSpec((tmr, dp), lambda i, te, na: (jnp.minimum(i, na[0] - 1), 0)),
                  pl.BlockSpec((1, d, de), lambda i, te, na: (te[i], 0, 0)),
                  pl.BlockSpec((1, d, de), lambda i, te, na: (te[i], 0, 0)),
                  pl.BlockSpec((1, de, d), lambda i, te, na: (te[i], 0, 0))],
        out_specs=pl.BlockSpec((tmr, dp), lambda i, te, na: (i, 0)),
        scratch_shapes=[pltpu.VMEM((d, de), BF16), pltpu.VMEM((d, de), BF16), pltpu.VMEM((de, d), BF16)],
    )
    return pl.pallas_call(
        _moe_ffn_kernel,
        grid_spec=grid_spec,
        out_shape=jax.ShapeDtypeStruct((n_rows, dp), F32),
        compiler_params=pltpu.CompilerParams(
            dimension_semantics=("arbitrary",), vmem_limit_bytes=VMEM_LIMIT),
        name="moe_ffn",
    )(tile_expert, n_active, xs, wg, wu, wd)


def _moe_combine_kernel(d1_ref, d2_ref, h_ref, route_ref, mod_ref, nf_ref, ys_hbm, o_ref,
                        b1_ref, b2_ref, sem, *, final_norm):
    tm = h_ref.shape[1]
    k = h_ref.shape[2] // LANES

    def tok(ref, t):
        return ref.at[pl.ds(pl.multiple_of(t * k, k), k)]

    def copy1(t):
        return pltpu.make_async_copy(tok(ys_hbm, d1_ref[0, 0, t]), tok(b1_ref, t), sem.at[0])

    def copy2(t):
        return pltpu.make_async_copy(tok(ys_hbm, d2_ref[0, 0, t]), tok(b2_ref, t), sem.at[1])

    def issue(t, carry):
        copy1(t).start(priority=0)
        copy2(t).start(priority=1)
        return carry

    lax.fori_loop(0, tm, issue, 0, unroll=8)
    pltpu.make_async_copy(ys_hbm.at[pl.ds(0, tm * k)], b1_ref, sem.at[0]).wait()
    pltpu.make_async_copy(ys_hbm.at[pl.ds(0, tm * k)], b2_ref, sem.at[1]).wait()
    route = route_ref[0]
    moe = (route[:, 2:3] * _load_token_tiles(b1_ref, (), tm, k)
           + route[:, 3:4] * _load_token_tiles(b2_ref, (), tm, k))
    h = h_ref[0] + mod_ref[0, 5:6, :] * moe
    if final_norm:
        h = _rms(h) * nf_ref[...]
    o_ref[0] = h


def _moe_combine(ys, dest1, dest2, h, route, mod, nf, final_norm):
    bsz, seq, d = h.shape
    tm = MOE_COMBINE_TILE
    nt = seq // tm
    dspec = pl.BlockSpec((1, 1, tm), lambda b, i: (b * nt + i, 0, 0), memory_space=pltpu.SMEM)
    row = lambda n: pl.BlockSpec((1, tm, n), lambda b, i: (b, i, 0))
    return pl.pallas_call(
        functools.partial(_moe_combine_kernel, final_norm=final_norm),
        grid=(bsz, nt),
        in_specs=[dspec, dspec, row(d), row(LANES),
                  pl.BlockSpec((1, N_MOD, d), lambda b, i: (b, 0, 0)),
                  pl.BlockSpec(nf.shape, lambda b, i: (0, 0)),
                  pl.BlockSpec(memory_space=pl.ANY)],
        out_specs=row(d),
        out_shape=jax.ShapeDtypeStruct((bsz, seq, d), F32),
        scratch_shapes=[pltpu.VMEM((tm * (d // LANES), LANES), ys.dtype),
                        pltpu.VMEM((tm * (d // LANES), LANES), ys.dtype),
                        pltpu.SemaphoreType.DMA((2,))],
        compiler_params=pltpu.CompilerParams(
            dimension_semantics=("arbitrary", "arbitrary"), vmem_limit_bytes=VMEM_LIMIT),
        name="moe_combine",
    )(dest1.reshape(bsz * nt, 1, tm), dest2.reshape(bsz * nt, 1, tm), h, route, mod, nf, ys)


def _moe_sparse(m, route, counts, wg, wu, wd, h, mod, nf, final_norm):
    bsz, seq, d = h.shape
    n_tok = bsz * seq
    tmr = MOE_ROWS
    n_exp = wg.shape[0]
    n_tiles = (2 * n_tok) // tmr + n_exp
    i32 = jnp.int32
    r = route.reshape(n_tok, LANES)
    e1, e2, k1, k2 = (r[:, j].astype(i32) for j in (0, 1, 4, 5))
    cnt = counts[-1, 0, ROUTE_OFF:ROUTE_OFF + n_exp].astype(i32)
    ptiles = (cnt + tmr - 1) // tmr
    tend = jnp.cumsum(ptiles)
    gstart = (tend - ptiles) * tmr
    eids = jnp.arange(n_exp, dtype=i32)
    dest1 = k1 + jnp.sum(jnp.where(e1[:, None] == eids[None, :], gstart[None, :], 0), axis=1)
    dest2 = k2 + jnp.sum(jnp.where(e2[:, None] == eids[None, :], gstart[None, :], 0), axis=1)
    n_active = tend[-1:]
    tile_expert = jnp.sum(jnp.arange(n_tiles, dtype=i32)[:, None] >= tend[None, :], axis=1).astype(i32)
    last_expert = jnp.sum(n_active - 1 >= tend).astype(i32)
    tile_expert = jnp.minimum(tile_expert, last_expert)
    pads = jnp.stack([jnp.concatenate([gstart + cnt, n_active]),
                      jnp.concatenate([gstart + ptiles * tmr, jnp.full((1,), n_tiles, i32)])]).astype(i32)
    k = d // LANES
    xs = _moe_scatter(m.reshape(n_tok * k, LANES), dest1, dest2, pads, n_tiles * tmr, k)
    ys = _moe_ffn(xs, tile_expert, n_active.astype(i32), wg, wu, wd)
    return _moe_combine(ys, dest1, dest2, h, route, mod, nf, final_norm)


def kernel(x, c, norm1_g, norm2_g, w_ada, b_ada, w_in, lam_re, lam_im, log_step, s5_b_re, s5_b_im,
           s5_c_re, s5_c_im, s5_d, w_glu, b_glu, conv_w, a_log, dt_bias, gdn_norm_g, w_out,
           w_router_grp, w_router_exp, w_gate, w_up, w_down, normf_g):
    bsz, seq, d = x.shape
    depth = w_ada.shape[0]
    d_s5 = s5_d.shape[1]
    d_gdn = w_out.shape[1] - d_s5
    n_heads = d_gdn // GDN_HEAD_DIM
    assert seq % S5_TILE == 0 or seq < S5_TILE
    assert seq % ROW_TILE == 0 and seq % GDN_CHUNK == 0 and d_s5 % LANES == 0
    assert 2 * n_heads <= LANES and N_EXPERT_GROUPS + N_EXPERTS <= LANES

    h = x
    for l in range(depth):
        mod = _ada(c, w_ada[l], b_ada[l]).reshape(bsz, N_MOD, d)
        wi = w_in[l]
        c0, c1, c2 = d_s5, d_s5 + 3 * d_gdn, d_s5 + 4 * d_gdn
        wab = jnp.zeros((d, LANES), F32).at[:, :2 * n_heads].set(wi[:, c2:c2 + 2 * n_heads])
        u, qkv, z, ab = _inproj(h, mod, norm1_g[l].reshape(1, d), wi[:, :c0].astype(BF16),
                                wi[:, c0:c1].astype(BF16), wi[:, c1:c2].astype(BF16), wab.astype(BF16))
        tabs = _s5_tables(lam_re[l], lam_im[l], log_step[l], s5_b_re[l], s5_b_im[l],
                          s5_c_re[l], s5_c_im[l], _s5_seg_len(seq))
        ys = _s5(u, *tabs, s5_d[l])
        yg = _gdn2(qkv, z, ab, conv_w[l], a_log[l], dt_bias[l], gdn_norm_g[l])
        wr = jnp.zeros((d, LANES), F32)
        wr = wr.at[:, :N_EXPERT_GROUPS].set(w_router_grp[l])
        wr = wr.at[:, ROUTE_OFF:ROUTE_OFF + N_EXPERTS].set(w_router_exp[l])
        wr_hi = wr.astype(BF16)
        wr = jnp.concatenate([wr_hi, (wr - wr_hi.astype(F32)).astype(BF16)], axis=1)
        h1, m, route, counts = _outproj(ys, yg, h, mod, w_glu[l].astype(BF16), b_glu[l].reshape(1, d_s5),
                                        w_out[l, :d_s5].astype(BF16), w_out[l, d_s5:].astype(BF16),
                                        norm2_g[l].reshape(1, d), wr)
        h = _moe_sparse(m, route, counts, w_gate[l], w_up[l], w_down[l], h1, mod,
                        normf_g.reshape(1, d), final_norm=(l == depth - 1))
    return h
```

```python
import functools
import math

import jax
import jax.numpy as jnp
from jax import lax
from jax.experimental import pallas as pl
from jax.experimental.pallas import tpu as pltpu

F32 = jnp.float32
BF16 = jnp.bfloat16
HIGHEST = lax.Precision.HIGHEST
EPS = 1e-6

S5_GROUP = 16
S5_STATE = 64
GDN_HEAD_DIM = 128
CONV_WIDTH = 4
N_EXPERT_GROUPS = 4
EXPERTS_PER_GROUP = 8
N_EXPERTS = N_EXPERT_GROUPS * EXPERTS_PER_GROUP
N_MOD = 6

LANES = 128
SUBLANES = 8
VMEM_LIMIT = 56 * 1024 * 1024

S5_CHUNK = SUBLANES
S5_TILE = 2048
GDN_CHUNK = 64
GDN_BLOCK = 256
ROW_TILE = 512
MOE_TILE = 1024
MOE_ROWS = 256
MOE_SCATTER_TILE = 1024
MOE_COMBINE_TILE = 1024
ROUTE_OFF = N_EXPERT_GROUPS


def _dot(a, b, **kw):
    return jnp.dot(a, b, preferred_element_type=F32, **kw)


def _silu(v):
    return v * jax.nn.sigmoid(v)


def _rms(v):
    return v * lax.rsqrt(jnp.mean(v * v, axis=-1, keepdims=True) + EPS)


def _store_token_tiles(ref, idx, v):
    n, width = v.shape
    k = width // LANES
    for s in range(k):
        ref[idx + (pl.ds(s, n, stride=k), slice(None))] = v[:, s * LANES:(s + 1) * LANES]


def _load_token_tiles(ref, idx, n, k):
    return jnp.concatenate([ref[idx + (pl.ds(s, n, stride=k), slice(None))] for s in range(k)], axis=1)


def _ada_kernel(c_ref, w_ref, b_ref, o_ref):
    o_ref[...] = _dot(_silu(c_ref[...]), w_ref[...], precision=HIGHEST) + b_ref[...]


def _ada(c, w, b):
    bsz, d = c.shape
    n = w.shape[1]
    return pl.pallas_call(
        _ada_kernel,
        grid=(n // d,),
        in_specs=[pl.BlockSpec((bsz, d), lambda j: (0, 0)),
                  pl.BlockSpec((d, d), lambda j: (0, j)),
                  pl.BlockSpec((1, d), lambda j: (0, j))],
        out_specs=pl.BlockSpec((bsz, d), lambda j: (0, j)),
        out_shape=jax.ShapeDtypeStruct((bsz, n), F32),
        compiler_params=pltpu.CompilerParams(vmem_limit_bytes=VMEM_LIMIT),
        name="ada",
    )(c, w, b.reshape(1, n))


def _inproj_kernel(x_ref, mod_ref, g_ref, wu_ref, wqkv_ref, wz_ref, wab_ref,
                   u_ref, qkv_ref, z_ref, ab_ref):
    x = x_ref[0]
    y = _rms(x) * g_ref[...]
    h = (y * (1.0 + mod_ref[0, 1:2, :]) + mod_ref[0, 0:1, :]).astype(BF16)
    u_ref[0] = _dot(h, wu_ref[...])
    qkv_ref[0] = _dot(h, wqkv_ref[...]).astype(BF16)
    z_ref[0] = _dot(h, wz_ref[...]).astype(BF16)
    ab_ref[0] = _dot(h, wab_ref[...])


def _inproj(x, mod, g, wu, wqkv, wz, wab):
    bsz, seq, d = x.shape
    tm = ROW_TILE
    full = lambda a: pl.BlockSpec(a.shape, lambda b, i: (0,) * a.ndim)
    row = lambda n: pl.BlockSpec((1, tm, n), lambda b, i: (b, i, 0))
    return pl.pallas_call(
        _inproj_kernel,
        grid=(bsz, seq // tm),
        in_specs=[row(d), pl.BlockSpec((1, N_MOD, d), lambda b, i: (b, 0, 0)), full(g),
                  full(wu), full(wqkv), full(wz), full(wab)],
        out_specs=[row(wu.shape[1]), row(wqkv.shape[1]), row(wz.shape[1]), row(LANES)],
        out_shape=[jax.ShapeDtypeStruct((bsz, seq, wu.shape[1]), F32),
                   jax.ShapeDtypeStruct((bsz, seq, wqkv.shape[1]), BF16),
                   jax.ShapeDtypeStruct((bsz, seq, wz.shape[1]), BF16),
                   jax.ShapeDtypeStruct((bsz, seq, LANES), F32)],
        compiler_params=pltpu.CompilerParams(
            dimension_semantics=("parallel", "parallel"), vmem_limit_bytes=VMEM_LIMIT),
        name="inproj",
    )(x, mod, g, wu, wqkv, wz, wab)


def _s5_tables(lam_re, lam_im, log_step, b_re, b_im, c_re, c_im):
    ch = S5_CHUNK
    n_grp, n_st = lam_re.shape
    gpb = LANES // S5_GROUP
    nblk = n_grp // gpb
    step = jnp.exp(log_step.astype(F32))[:, None]
    lr = lam_re.astype(F32)
    li = lam_im.astype(F32)

    def lam_pow(j):
        mag = jnp.exp(j * lr * step)
        ang = j * li * step
        return mag * jnp.cos(ang), mag * jnp.sin(ang)

    ar, ai = lam_pow(1.0)
    den = lr * lr + li * li
    fr = ((ar - 1.0) * lr + ai * li) / den
    fi = (ai * lr - (ar - 1.0) * li) / den
    bbr = fr[..., None] * b_re - fi[..., None] * b_im
    bbi = fr[..., None] * b_im + fi[..., None] * b_re
    pows = [lam_pow(float(j)) for j in range(ch + 1)]
    pr = jnp.stack([p[0] for p in pows])
    pi = jnp.stack([p[1] for p in pows])
    lbr = pr[:ch, :, :, None] * bbr[None] - pi[:ch, :, :, None] * bbi[None]
    lbi = pr[:ch, :, :, None] * bbi[None] + pi[:ch, :, :, None] * bbr[None]
    kdim = ch * LANES
    sdim = 2 * gpb * n_st

    def expand(small, col_grp, n_rows):
        n_cols = small.shape[-1]
        rq = jnp.arange(gpb)[:, None, None]
        cq = ((jnp.arange(n_cols) // col_grp) % gpb)[None, None, :]
        full = jnp.where(rq == cq, small[:, :, None, :, :], 0.0).astype(BF16)
        return full.reshape(nblk, n_rows, n_cols)

    kmat = (jnp.einsum('ghp,jgpk->jghk', c_re, lbr, precision=HIGHEST)
            - jnp.einsum('ghp,jgpk->jghk', c_im, lbi, precision=HIGHEST))
    s_idx = jnp.arange(ch)[:, None]
    t_idx = jnp.arange(ch)[None, :]
    lag = jnp.clip(t_idx - s_idx, 0, ch - 1)
    kst = jnp.where((t_idx >= s_idx)[..., None, None, None], kmat[lag], 0.0)
    kst = kst.reshape(ch, ch, nblk, gpb, S5_GROUP, S5_GROUP)
    t_small = kst.transpose(2, 0, 5, 1, 3, 4).reshape(nblk, ch, S5_GROUP, kdim)
    tmat = expand(t_small, S5_GROUP, kdim)

    wri = jnp.stack([lbr[::-1], lbi[::-1]]).reshape(2, ch, nblk, gpb, n_st, S5_GROUP)
    w_small = wri.transpose(2, 1, 5, 0, 3, 4).reshape(nblk, ch, S5_GROUP, sdim)
    win = expand(w_small, n_st, kdim)

    pr1 = pr[1:, :, None, :]
    pi1 = pi[1:, :, None, :]
    clr = c_re[None] * pr1 - c_im[None] * pi1
    cli = c_re[None] * pi1 + c_im[None] * pr1
    wo = jnp.stack([clr, -cli]).reshape(2, ch, nblk, gpb, S5_GROUP, n_st)
    o_small = wo.transpose(2, 0, 5, 1, 3, 4).reshape(nblk, 2, n_st, kdim)
    wout = expand(o_small, S5_GROUP, sdim)

    half = gpb * n_st
    amat = jnp.stack([pr[ch].reshape(nblk, half), pi[ch].reshape(nblk, half)], axis=1)
    return tmat, win, wout, amat


def _s5_kernel(u_ref, t_ref, win_ref, wout_ref, a_ref, d_ref, y_ref, st_ref, v_ref, xp_ref):
    ch = S5_CHUNK
    n = u_ref.shape[1] // ch
    half = st_ref.shape[1]

    @pl.when(pl.program_id(2) == 0)
    def _():
        st_ref[...] = jnp.zeros_like(st_ref)

    slabs = [u_ref[0, pl.ds(s, n, stride=ch), :] for s in range(ch)]
    ucat = jnp.concatenate(slabs, axis=1).astype(BF16)
    y = _dot(ucat, t_ref[0])
    v_ref[...] = _dot(ucat, win_ref[0])
    a_r = a_ref[0, 0:1, :]
    a_i = a_ref[0, 1:2, :]

    def body(r, carry):
        x_r, x_i = carry
        xp_ref[pl.ds(r, 1), 0:half] = x_r
        xp_ref[pl.ds(r, 1), half:2 * half] = x_i
        v_r = v_ref[pl.ds(r, 1), 0:half]
        v_i = v_ref[pl.ds(r, 1), half:2 * half]
        return a_r * x_r - a_i * x_i + v_r, a_r * x_i + a_i * x_r + v_i

    x_r, x_i = lax.fori_loop(0, n, body, (st_ref[0:1, :], st_ref[1:2, :]), unroll=2)
    st_ref[0:1, :] = x_r
    st_ref[1:2, :] = x_i
    y = y + _dot(xp_ref[...].astype(BF16), wout_ref[0])
    d = d_ref[0]
    for t in range(ch):
        y_ref[0, pl.ds(t, n, stride=ch), :] = y[:, t * LANES:(t + 1) * LANES] + d * slabs[t]


def _s5(u, tmat, win, wout, amat, d_skip):
    bsz, seq, dch = u.shape
    nblk = dch // LANES
    tt = min(S5_TILE, seq)
    n = tt // S5_CHUNK
    kdim = S5_CHUNK * LANES
    sdim = win.shape[2]
    wspec = lambda a: pl.BlockSpec((1,) + a.shape[1:], lambda b, c, i: (c, 0, 0))
    return pl.pallas_call(
        _s5_kernel,
        grid=(bsz, nblk, seq // tt),
        in_specs=[pl.BlockSpec((1, tt, LANES), lambda b, c, i: (b, i, c)),
                  wspec(tmat), wspec(win), wspec(wout), wspec(amat),
                  pl.BlockSpec((1, 1, LANES), lambda b, c, i: (c, 0, 0))],
        out_specs=pl.BlockSpec((1, tt, LANES), lambda b, c, i: (b, i, c)),
        out_shape=jax.ShapeDtypeStruct((bsz, seq, dch), F32),
        scratch_shapes=[pltpu.VMEM((2, sdim // 2), F32),
                        pltpu.VMEM((n, sdim), F32),
                        pltpu.VMEM((n, sdim), F32)],
        compiler_params=pltpu.CompilerParams(
            dimension_semantics=("parallel", "parallel", "arbitrary"),
            vmem_limit_bytes=VMEM_LIMIT),
        name="s5",
    )(u, tmat, win, wout, amat, d_skip.reshape(nblk, 1, LANES))


def _cumsum_rows(v):
    n = v.shape[0]
    row = lax.broadcasted_iota(jnp.int32, v.shape, 0)
    sh = 1
    while sh < n:
        v = v + jnp.where(row >= sh, pltpu.roll(v, sh, axis=0), 0.0)
        sh *= 2
    return v


def _unit_lower_inverse(lm):
    n = lm.shape[0]
    ri = lax.broadcasted_iota(jnp.int32, (n, n), 0)
    ci = lax.broadcasted_iota(jnp.int32, (n, n), 1)
    eye = jnp.where(ri == ci, 1.0, 0.0).astype(F32)
    m = 1
    d = eye
    while m < n:
        mask = ((ri // (2 * m)) == (ci // (2 * m))) & (((ri // m) % 2) == 1) & (((ci // m) % 2) == 0)
        e = jnp.where(mask, lm, 0.0)
        if m == 1:
            d = eye - e
        else:
            de = _dot(d.astype(BF16), e.astype(BF16))
            d = d - _dot(de.astype(BF16), d.astype(BF16))
        m *= 2
    return d


def _gdn_kernel(qkv_ref, z_ref, ab_ref, cw_ref, gp_ref, ng_ref, o_ref, s_ref, xp_ref, *, n_heads):
    ck = GDN_CHUNK
    dh = GDN_HEAD_DIM
    bsz = qkv_ref.shape[0]
    dg = n_heads * dh
    halo = SUBLANES

    @pl.when(pl.program_id(0) == 0)
    def _():
        s_ref[...] = jnp.zeros_like(s_ref)
        xp_ref[:, 0:halo, :] = jnp.zeros((bsz, halo, 3 * dg), F32)

    ri = lax.broadcasted_iota(jnp.int32, (ck, ck), 0)
    ci = lax.broadcasted_iota(jnp.int32, (ck, ck), 1)
    causal = ri >= ci
    strict = ri > ci
    lane = lax.broadcasted_iota(jnp.int32, (ck, LANES), 1)
    a_log = gp_ref[0:1, :]
    dt_bias = gp_ref[1:2, :]
    ng = ng_ref[...]

    for b in range(bsz):
        xp_ref[b, halo:halo + ck, :] = qkv_ref[b].astype(F32)
        conv = cw_ref[0:1, :] * xp_ref[b, halo - 3:halo - 3 + ck, :]
        for j in range(1, CONV_WIDTH):
            conv = conv + cw_ref[j:j + 1, :] * xp_ref[b, halo - 3 + j:halo - 3 + j + ck, :]
        xp_ref[b, 0:halo, :] = xp_ref[b, ck:ck + halo, :]
        act = _silu(conv)

        ab = ab_ref[b]
        sp = jnp.maximum(ab + dt_bias, 0.0) + jnp.log1p(jnp.exp(-jnp.abs(ab + dt_bias)))
        g = jnp.where(lane < n_heads, -jnp.exp(a_log) * sp, 0.0)
        gc = _cumsum_rows(g)
        gct = gc.T
        beta_all = jax.nn.sigmoid(ab)
        zb = z_ref[b].astype(F32)

        for h in range(n_heads):
            q = act[:, h * dh:(h + 1) * dh]
            k = act[:, dg + h * dh:dg + (h + 1) * dh]
            v = act[:, 2 * dg + h * dh:2 * dg + (h + 1) * dh]
            q = q * lax.rsqrt(jnp.sum(q * q, axis=-1, keepdims=True) + EPS) * (dh ** -0.5)
            k = k * lax.rsqrt(jnp.sum(k * k, axis=-1, keepdims=True) + EPS)
            beta = beta_all[:, n_heads + h:n_heads + h + 1]
            gcol = gc[:, h:h + 1]
            grow = gct[h:h + 1, :]
            glast = gc[ck - 1:ck, h:h + 1]
            decay = jnp.exp(jnp.where(causal, gcol - grow, -1e30))
            kb = k * beta
            kq = jnp.concatenate([kb, q], axis=0).astype(BF16)
            m1 = lax.dot_general(kq, k.astype(BF16), (((1,), (1,)), ((), ())),
                                 preferred_element_type=F32)
            lm = jnp.where(strict, m1[0:ck] * decay, 0.0)
            qk = m1[ck:2 * ck] * decay
            tinv = _unit_lower_inverse(lm)
            eg = jnp.exp(gcol)
            rhs = jnp.concatenate([v * beta, kb * eg], axis=1).astype(BF16)
            w = _dot(tinv.astype(BF16), rhs)
            value = w[:, 0:dh]
            kcd = w[:, dh:2 * dh]
            qd = q * eg
            kt = k * jnp.exp(glast - gcol)
            st = s_ref[b * n_heads + h]
            m2 = _dot(jnp.concatenate([kcd, qd], axis=0).astype(BF16), st.astype(BF16))
            v_new = value - m2[0:ck]
            vnb = v_new.astype(BF16)
            o = m2[ck:2 * ck] + _dot(qk.astype(BF16), vnb)
            s_ref[b * n_heads + h] = st * jnp.exp(glast) + lax.dot_general(
                kt.astype(BF16), vnb, (((0,), (0,)), ((), ())), preferred_element_type=F32)
            zh = zb[:, h * dh:(h + 1) * dh]
            o_ref[b, :, h * dh:(h + 1) * dh] = (_rms(o) * ng * _silu(zh)).astype(o_ref.dtype)


def _gdn(qkv, z, ab, conv_w, a_log, dt_bias, norm_g):
    bsz, seq, _ = qkv.shape
    dg = z.shape[2]
    n_heads = dg // GDN_HEAD_DIM
    ck = GDN_CHUNK
    gp = jnp.zeros((2, LANES), F32).at[0, :n_heads].set(a_log).at[1, :n_heads].set(dt_bias)
    blk = lambda n: pl.BlockSpec((bsz, ck, n), lambda i: (0, i, 0))
    full = lambda a: pl.BlockSpec(a.shape, lambda i: (0,) * a.ndim)
    ng = norm_g.reshape(1, GDN_HEAD_DIM)
    return pl.pallas_call(
        functools.partial(_gdn_kernel, n_heads=n_heads),
        grid=(seq // ck,),
        in_specs=[blk(3 * dg), blk(dg), blk(LANES), full(conv_w), full(gp), full(ng)],
        out_specs=blk(dg),
        out_shape=jax.ShapeDtypeStruct((bsz, seq, dg), BF16),
        scratch_shapes=[pltpu.VMEM((bsz * n_heads, GDN_HEAD_DIM, GDN_HEAD_DIM), F32),
                        pltpu.VMEM((bsz, ck + SUBLANES, 3 * dg), F32)],
        compiler_params=pltpu.CompilerParams(
            dimension_semantics=("arbitrary",), vmem_limit_bytes=VMEM_LIMIT),
        name="gdn",
    )(qkv, z, ab, conv_w, gp, ng)


def _gdn2_kernel(qkv_ref, z_ref, ab_ref, cw_ref, gp_ref, ng_ref, o_ref,
                 s_ref, xp_ref, gc_ref, gct_ref, kbq_ref, kn_ref, rhs_ref, cq_ref, qkt_ref,
                 lm_ref, d_ref, db_ref, de_ref, val_ref, vn_ref, *, n_heads):
    tt = qkv_ref.shape[1]
    dh = GDN_HEAD_DIM
    bsz = qkv_ref.shape[0]
    dg = n_heads * dh
    halo = SUBLANES
    pairs = [(b, h) for b in range(bsz) for h in range(n_heads)]

    @pl.when(pl.program_id(0) == 0)
    def _():
        s_ref[...] = jnp.zeros_like(s_ref)
        xp_ref[...] = jnp.zeros_like(xp_ref)

    ri = lax.broadcasted_iota(jnp.int32, (tt, tt), 0)
    ci = lax.broadcasted_iota(jnp.int32, (tt, tt), 1)
    lane = lax.broadcasted_iota(jnp.int32, (tt, LANES), 1)
    a_log = gp_ref[0:1, :]
    dt_bias = gp_ref[1:2, :]
    shift_op = jnp.concatenate(
        [jnp.where(ri == ci + j, 1.0, 0.0) for j in range(1, CONV_WIDTH)], axis=0).astype(BF16)

    for b in range(bsz):
        x = qkv_ref[b]
        x32 = x.astype(F32)
        shifted = _dot(shift_op, x)
        edge = jnp.concatenate([xp_ref[b], x32[0:halo]], axis=0)
        conv = cw_ref[CONV_WIDTH - 1:CONV_WIDTH, :] * x32
        for j in range(1, CONV_WIDTH):
            sh_j = jnp.concatenate([edge[halo - j:2 * halo - j],
                                    shifted[(j - 1) * tt + halo:j * tt]], axis=0)
            conv = conv + cw_ref[CONV_WIDTH - 1 - j:CONV_WIDTH - j, :] * sh_j
        xp_ref[b] = x32[tt - halo:tt]
        act = _silu(conv)

        ab = ab_ref[b]
        sp = jnp.maximum(ab + dt_bias, 0.0) + jnp.log1p(jnp.exp(-jnp.abs(ab + dt_bias)))
        g = jnp.where(lane < n_heads, -jnp.exp(a_log) * sp, 0.0)
        gc = _cumsum_rows(g)
        gc_ref[b] = gc
        gct_ref[b] = gc.T
        glast = gc[tt - 1:tt, :]
        egc = jnp.exp(gc)
        ekt = jnp.exp(glast - gc)
        beta_all = jax.nn.sigmoid(ab)
        for h in range(n_heads):
            p = b * n_heads + h
            q = act[:, h * dh:(h + 1) * dh]
            k = act[:, dg + h * dh:dg + (h + 1) * dh]
            v = act[:, 2 * dg + h * dh:2 * dg + (h + 1) * dh]
            q = q * lax.rsqrt(jnp.sum(q * q, axis=-1, keepdims=True) + EPS) * (dh ** -0.5)
            k = k * lax.rsqrt(jnp.sum(k * k, axis=-1, keepdims=True) + EPS)
            beta = beta_all[:, n_heads + h:n_heads + h + 1]
            eg = egc[:, h:h + 1]
            kb = k * beta
            kbq_ref[p, 0:tt, :] = kb.astype(BF16)
            kbq_ref[p, tt:2 * tt, :] = q.astype(BF16)
            kn_ref[p] = k.astype(BF16)
            rhs_ref[p, :, 0:dh] = (v * beta).astype(BF16)
            rhs_ref[p, :, dh:2 * dh] = (kb * eg).astype(BF16)
            cq_ref[p, tt:2 * tt, :] = (q * eg).astype(BF16)
            qkt_ref[p, tt:tt + dh, :] = (k * ekt[:, h:h + 1]).T.astype(BF16)

    for b, h in pairs:
        p = b * n_heads + h
        m1 = lax.dot_general(kbq_ref[p], kn_ref[p], (((1,), (1,)), ((), ())),
                             preferred_element_type=F32)
        gcol = gc_ref[b, :, h:h + 1]
        grow = gct_ref[b, h:h + 1, :]
        decay = jnp.exp(jnp.where(ri >= ci, gcol - grow, -1e30))
        lm = jnp.where(ri > ci, m1[0:tt] * decay, 0.0)
        lm_ref[p] = lm.astype(BF16)
        qkt_ref[p, 0:tt, :] = (m1[tt:2 * tt] * decay).astype(BF16)
        first = jnp.where(ri == ci + 1, jnp.where((ci & 1) == 0, lm, 0.0), 0.0)
        d = jnp.where(ri == ci, 1.0, 0.0) - first
        d_ref[p] = d
        db_ref[p] = d.astype(BF16)

    m = 2
    sh = 1
    while m < tt:
        rb = ri >> sh
        cb = ci >> sh
        sel = jnp.where(rb == cb + 1, jnp.where((cb & 1) == 0, 1.0, 0.0), 0.0).astype(BF16)
        for b, h in pairs:
            p = b * n_heads + h
            de_ref[p] = _dot(db_ref[p], lm_ref[p] * sel).astype(BF16)
        for b, h in pairs:
            p = b * n_heads + h
            d = d_ref[p] - _dot(de_ref[p], db_ref[p])
            d_ref[p] = d
            db_ref[p] = d.astype(BF16)
        m *= 2
        sh += 1

    for b, h in pairs:
        p = b * n_heads + h
        w = _dot(db_ref[p], rhs_ref[p])
        val_ref[p] = w[:, 0:dh]
        cq_ref[p, 0:tt, :] = w[:, dh:2 * dh].astype(BF16)

    for b, h in pairs:
        p = b * n_heads + h
        m2 = _dot(cq_ref[p], s_ref[p].astype(BF16))
        vn_ref[p] = (val_ref[p] - m2[0:tt]).astype(BF16)
        val_ref[p] = m2[tt:2 * tt]
    for b, h in pairs:
        p = b * n_heads + h
        r = _dot(qkt_ref[p], vn_ref[p])
        o = val_ref[p] + r[0:tt]
        egl = jnp.exp(gc_ref[b, tt - 1:tt, h:h + 1])
        s_ref[p] = s_ref[p] * egl + r[tt:tt + dh]
        zh = z_ref[b, :, h * dh:(h + 1) * dh].astype(F32)
        o_ref[b, :, h * dh:(h + 1) * dh] = (_rms(o) * ng_ref[...] * _silu(zh)).astype(o_ref.dtype)


def _gdn2(qkv, z, ab, conv_w, a_log, dt_bias, norm_g):
    bsz, seq, _ = qkv.shape
    dg = z.shape[2]
    dh = GDN_HEAD_DIM
    n_heads = dg // dh
    tt = GDN_BLOCK
    npair = bsz * n_heads
    gp = jnp.zeros((2, LANES), F32).at[0, :n_heads].set(a_log).at[1, :n_heads].set(dt_bias)
    blk = lambda n: pl.BlockSpec((bsz, tt, n), lambda i: (0, i, 0))
    full = lambda a: pl.BlockSpec(a.shape, lambda i: (0,) * a.ndim)
    ng = norm_g.reshape(1, dh)
    return pl.pallas_call(
        functools.partial(_gdn2_kernel, n_heads=n_heads),
        grid=(seq // tt,),
        in_specs=[blk(3 * dg), blk(dg), blk(LANES), full(conv_w), full(gp), full(ng)],
        out_specs=blk(dg),
        out_shape=jax.ShapeDtypeStruct((bsz, seq, dg), BF16),
        scratch_shapes=[pltpu.VMEM((npair, dh, dh), F32),
                        pltpu.VMEM((bsz, SUBLANES, 3 * dg), F32),
                        pltpu.VMEM((bsz, tt, LANES), F32),
                        pltpu.VMEM((bsz, LANES, tt), F32),
                        pltpu.VMEM((npair, 2 * tt, dh), BF16),
                        pltpu.VMEM((npair, tt, dh), BF16),
                        pltpu.VMEM((npair, tt, 2 * dh), BF16),
                        pltpu.VMEM((npair, 2 * tt, dh), BF16),
                        pltpu.VMEM((npair, tt + dh, tt), BF16),
                        pltpu.VMEM((npair, tt, tt), BF16),
                        pltpu.VMEM((npair, tt, tt), F32),
                        pltpu.VMEM((npair, tt, tt), BF16),
                        pltpu.VMEM((npair, tt, tt), BF16),
                        pltpu.VMEM((npair, tt, dh), F32),
                        pltpu.VMEM((npair, tt, dh), BF16)],
        compiler_params=pltpu.CompilerParams(
            dimension_semantics=("arbitrary",), vmem_limit_bytes=VMEM_LIMIT),
        name="gdn",
    )(qkv, z, ab, conv_w, gp, ng)


def _outproj_kernel(ys_ref, yg_ref, x_ref, mod_ref, wglu_ref, bglu_ref, wos_ref, wog_ref,
                    g2_ref, wr_ref, h_ref, m_ref, route_ref, count_ref, cnt_ref):
    y = jax.nn.gelu(ys_ref[0])
    gate = jax.nn.sigmoid(_dot(y.astype(BF16), wglu_ref[...]) + bglu_ref[...])
    y = (y * gate).astype(BF16)
    mix = _dot(y, wos_ref[...]) + _dot(yg_ref[0], wog_ref[...])
    h = x_ref[0] + mod_ref[0, 2:3, :] * mix
    h_ref[0] = h
    m = _rms(h) * g2_ref[...] * (1.0 + mod_ref[0, 4:5, :]) + mod_ref[0, 3:4, :]
    _store_token_tiles(m_ref, (0,), m)

    m_hi = m.astype(BF16)
    m_lo = (m - m_hi.astype(F32)).astype(BF16)
    part = _dot(m_hi, wr_ref[...])
    logits = part[:, 0:LANES] + part[:, LANES:2 * LANES] + _dot(m_lo, wr_ref[:, 0:LANES])
    lane = lax.broadcasted_iota(jnp.int32, logits.shape, 1)
    neg = -1e30
    is_grp = lane < N_EXPERT_GROUPS
    gl = jnp.where(is_grp, logits, neg)
    gmax = jnp.max(gl, axis=-1, keepdims=True)
    gidx = jnp.min(jnp.where(gl == gmax, lane, LANES), axis=-1, keepdims=True)
    p_grp = 1.0 / jnp.sum(jnp.where(is_grp, jnp.exp(gl - gmax), 0.0), axis=-1, keepdims=True)
    lo = ROUTE_OFF + gidx * EXPERTS_PER_GROUP
    el = jnp.where((lane >= lo) & (lane < lo + EXPERTS_PER_GROUP), logits, neg)
    v1 = jnp.max(el, axis=-1, keepdims=True)
    i1 = jnp.min(jnp.where(el == v1, lane, LANES), axis=-1, keepdims=True)
    el2 = jnp.where(lane == i1, neg, el)
    v2 = jnp.max(el2, axis=-1, keepdims=True)
    i2 = jnp.min(jnp.where(el2 == v2, lane, LANES), axis=-1, keepdims=True)
    t = jnp.exp(v2 - v1)
    w1 = p_grp / (1.0 + t)
    w2 = w1 * t

    @pl.when((pl.program_id(0) == 0) & (pl.program_id(1) == 0))
    def _():
        cnt_ref[...] = jnp.zeros_like(cnt_ref)

    tm = logits.shape[0]
    oh1 = lane == i1
    oh2 = lane == i2
    cnt = jnp.where(oh1, 1.0, 0.0) + jnp.where(oh2, 1.0, 0.0)
    earlier = (lax.broadcasted_iota(jnp.int32, (tm, tm), 0) > lax.broadcasted_iota(jnp.int32, (tm, tm), 1))
    before = _dot(jnp.where(earlier, 1.0, 0.0).astype(BF16), cnt.astype(BF16)) + cnt_ref[...]
    r1 = jnp.sum(jnp.where(oh1, before, 0.0), axis=-1, keepdims=True)
    r2 = jnp.sum(jnp.where(oh2, before, 0.0), axis=-1, keepdims=True)
    cnt_ref[...] += jnp.sum(cnt, axis=0, keepdims=True)
    count_ref[0] = cnt_ref[...]
    cols = [(i1 - ROUTE_OFF).astype(F32), (i2 - ROUTE_OFF).astype(F32), w1, w2, r1, r2]
    route = jnp.zeros_like(logits)
    for j, col in enumerate(cols):
        route = jnp.where(lane == j, col, route)
    route_ref[0] = route


def _outproj(ys, yg, x, mod, wglu, bglu, wos, wog, g2, wr):
    bsz, seq, d = x.shape
    tm = ROW_TILE
    full = lambda a: pl.BlockSpec(a.shape, lambda b, i: (0,) * a.ndim)
    row = lambda n: pl.BlockSpec((1, tm, n), lambda b, i: (b, i, 0))
    return pl.pallas_call(
        _outproj_kernel,
        grid=(bsz, seq // tm),
        in_specs=[row(ys.shape[2]), row(yg.shape[2]), row(d),
                  pl.BlockSpec((1, N_MOD, d), lambda b, i: (b, 0, 0)),
                  full(wglu), full(bglu), full(wos), full(wog), full(g2), full(wr)],
        out_specs=[row(d), pl.BlockSpec((1, tm * (d // LANES), LANES), lambda b, i: (b, i, 0)), row(LANES),
                   pl.BlockSpec((1, 1, LANES), lambda b, i: (b * (seq // tm) + i, 0, 0))],
        out_shape=[jax.ShapeDtypeStruct((bsz, seq, d), F32),
                   jax.ShapeDtypeStruct((bsz, seq * (d // LANES), LANES), F32),
                   jax.ShapeDtypeStruct((bsz, seq, LANES), F32),
                   jax.ShapeDtypeStruct((bsz * (seq // tm), 1, LANES), F32)],
        scratch_shapes=[pltpu.VMEM((1, LANES), F32)],
        compiler_params=pltpu.CompilerParams(
            dimension_semantics=("arbitrary", "arbitrary"), vmem_limit_bytes=VMEM_LIMIT),
        name="outproj",
    )(ys, yg, x, mod, wglu, bglu, wos, wog, g2, wr)


def _moe_kernel(m_ref, comb_ref, wg_ref, wu_ref, wd_ref, h_ref, mod_ref, nf_ref, o_ref, acc_ref,
                *, final_norm):
    e = pl.program_id(2)

    @pl.when(e == 0)
    def _():
        acc_ref[...] = jnp.zeros_like(acc_ref)

    m = m_ref[0]
    mid = (_silu(_dot(m, wg_ref[0])) * _dot(m, wu_ref[0])).astype(BF16)
    y = _dot(mid, wd_ref[0])
    comb = comb_ref[0]
    lane = lax.broadcasted_iota(jnp.int32, comb.shape, 1)
    col = jnp.sum(jnp.where(lane == e + ROUTE_OFF, comb, 0.0), axis=-1, keepdims=True)
    acc_ref[...] += col * y

    @pl.when(e == pl.num_programs(2) - 1)
    def _():
        h = h_ref[0] + mod_ref[0, 5:6, :] * acc_ref[...]
        if final_norm:
            h = _rms(h) * nf_ref[...]
        o_ref[0] = h


def _moe(m, comb, wg, wu, wd, h, mod, nf, final_norm):
    bsz, seq, d = h.shape
    tm = min(MOE_TILE, seq)
    n_exp, _, de = wg.shape
    row = lambda n: pl.BlockSpec((1, tm, n), lambda b, i, e: (b, i, 0))
    return pl.pallas_call(
        functools.partial(_moe_kernel, final_norm=final_norm),
        grid=(bsz, seq // tm, n_exp),
        in_specs=[row(d), row(LANES),
                  pl.BlockSpec((1, d, de), lambda b, i, e: (e, 0, 0)),
                  pl.BlockSpec((1, d, de), lambda b, i, e: (e, 0, 0)),
                  pl.BlockSpec((1, de, d), lambda b, i, e: (e, 0, 0)),
                  row(d), pl.BlockSpec((1, N_MOD, d), lambda b, i, e: (b, 0, 0)),
                  pl.BlockSpec(nf.shape, lambda b, i, e: (0, 0))],
        out_specs=row(d),
        out_shape=jax.ShapeDtypeStruct((bsz, seq, d), F32),
        scratch_shapes=[pltpu.VMEM((tm, d), F32)],
        compiler_params=pltpu.CompilerParams(
            dimension_semantics=("parallel", "parallel", "arbitrary"),
            vmem_limit_bytes=VMEM_LIMIT),
        name="moe",
    )(m, comb, wg, wu, wd, h, mod, nf)


def _moe_scatter_kernel(d1_ref, d2_ref, pad_ref, m_ref, xs_hbm, zero_ref, sem, *, n_exp, k):
    i = pl.program_id(0)
    ts = d1_ref.shape[2]
    trows = zero_ref.shape[0]

    def tok(ref, t):
        return ref.at[pl.ds(pl.multiple_of(t * k, k), k)]

    def tile_copy(j):
        return pltpu.make_async_copy(
            zero_ref, xs_hbm.at[pl.ds(pl.multiple_of(j * trows, trows), trows)], sem.at[0])

    def pad_piece(e, p, act):
        lo = pad_ref[0, e]
        length = pad_ref[1, e] - lo
        start = lo + (length & ~(2 * p - 1))
        copy = pltpu.make_async_copy(
            zero_ref.at[pl.ds(0, p * k)], xs_hbm.at[pl.ds(pl.multiple_of(start * k, k), p * k)], sem.at[0])

        @pl.when((length & p) != 0)
        def _():
            act(copy)

    @pl.when(i == 0)
    def _():
        zero_ref[...] = jnp.zeros_like(zero_ref)
        pieces = [(e, p) for e in range(n_exp) for p in [1 << j for j in range((trows // k).bit_length() - 1)]]
        for e, p in pieces:
            pad_piece(e, p, lambda c: c.start())
        for e, p in pieces:
            pad_piece(e, p, lambda c: c.wait())

        def fill(j, carry):
            tile_copy(j).start()
            return carry

        def drain(j, carry):
            tile_copy(j).wait()
            return carry

        lax.fori_loop(pad_ref[0, n_exp], pad_ref[1, n_exp], fill, 0)
        lax.fori_loop(pad_ref[0, n_exp], pad_ref[1, n_exp], drain, 0)

    def row_copy(t, dst, slot):
        return pltpu.make_async_copy(tok(m_ref, t), tok(xs_hbm, dst), sem.at[1 + slot])

    def issue(t, carry):
        row_copy(t, d1_ref[0, 0, t], 0).start(priority=0)
        row_copy(t, d2_ref[0, 0, t], 1).start(priority=1)
        return carry

    lax.fori_loop(0, ts, issue, 0, unroll=8)
    for slot in range(2):
        pltpu.make_async_copy(m_ref, xs_hbm.at[pl.ds(0, ts * k)], sem.at[1 + slot]).wait()


def _moe_scatter(m, dest1, dest2, pads, n_rows, k):
    n_tok = m.shape[0] // k
    ts = MOE_SCATTER_TILE
    n_exp = pads.shape[1] - 1
    dspec = pl.BlockSpec((1, 1, ts), lambda i: (i, 0, 0), memory_space=pltpu.SMEM)
    return pl.pallas_call(
        functools.partial(_moe_scatter_kernel, n_exp=n_exp, k=k),
        grid=(n_tok // ts,),
        in_specs=[dspec, dspec,
                  pl.BlockSpec(pads.shape, lambda i: (0, 0), memory_space=pltpu.SMEM),
                  pl.BlockSpec((ts * k, LANES), lambda i: (i, 0))],
        out_specs=pl.BlockSpec(memory_space=pl.ANY),
        out_shape=jax.ShapeDtypeStruct((n_rows * k, LANES), m.dtype),
        scratch_shapes=[pltpu.VMEM((MOE_ROWS * k, LANES), m.dtype), pltpu.SemaphoreType.DMA((3,))],
        compiler_params=pltpu.CompilerParams(
            dimension_semantics=("arbitrary",), vmem_limit_bytes=VMEM_LIMIT, has_side_effects=True),
        name="moe_scatter",
    )(dest1.reshape(n_tok // ts, 1, ts), dest2.reshape(n_tok // ts, 1, ts), pads, m)


def _moe_ffn_kernel(te_ref, na_ref, xs_ref, wg_ref, wu_ref, wd_ref, ys_ref, wgb_ref, wub_ref, wdb_ref):
    i = pl.program_id(0)
    active = i < na_ref[0]
    changed = (i == 0) | (te_ref[i] != te_ref[jnp.maximum(i - 1, 0)])

    @pl.when(active & changed)
    def _():
        wgb_ref[...] = wg_ref[0].astype(BF16)
        wub_ref[...] = wu_ref[0].astype(BF16)
        wdb_ref[...] = wd_ref[0].astype(BF16)

    @pl.when(active)
    def _():
        k = wgb_ref.shape[0] // LANES
        x = _load_token_tiles(xs_ref, (), xs_ref.shape[0] // k, k).astype(BF16)
        mid = (_silu(_dot(x, wgb_ref[...])) * _dot(x, wub_ref[...])).astype(BF16)
        _store_token_tiles(ys_ref, (), _dot(mid, wdb_ref[...]))

    @pl.when(jnp.logical_not(active))
    def _():
        ys_ref[...] = jnp.zeros_like(ys_ref)


def _moe_ffn(xs, tile_expert, n_active, wg, wu, wd):
    _, d, de = wg.shape
    tmr = MOE_ROWS * (d // LANES)
    n_rows, dp = xs.shape
    grid_spec = pltpu.PrefetchScalarGridSpec(
        num_scalar_prefetch=2,
        grid=(n_rows // tmr,),
        in_specs=[pl.BlockSpec((tmr, dp), lambda i, te, na: (jnp.minimum(i, na[0] - 1), 0)),
                  pl.BlockSpec((1, d, de), lambda i, te, na: (te[i], 0, 0)),
                  pl.BlockSpec((1, d, de), lambda i, te, na: (te[i], 0, 0)),
                  pl.BlockSpec((1, de, d), lambda i, te, na: (te[i], 0, 0))],
        out_specs=pl.BlockSpec((tmr, dp), lambda i, te, na: (i, 0)),
        scratch_shapes=[pltpu.VMEM((d, de), BF16), pltpu.VMEM((d, de), BF16), pltpu.VMEM((de, d), BF16)],
    )
    return pl.pallas_call(
        _moe_ffn_kernel,
        grid_spec=grid_spec,
        out_shape=jax.ShapeDtypeStruct((n_rows, dp), F32),
        compiler_params=pltpu.CompilerParams(
            dimension_semantics=("arbitrary",), vmem_limit_bytes=VMEM_LIMIT),
        name="moe_ffn",
    )(tile_expert, n_active, xs, wg, wu, wd)


def _moe_combine_kernel(d1_ref, d2_ref, n1_ref, n2_ref, h_ref, route_ref, mod_ref, nf_ref, ys_hbm, o_ref,
                        b1_ref, b2_ref, sem, *, final_norm):
    tm = h_ref.shape[1]
    k = h_ref.shape[2] // LANES
    g = pl.program_id(0)
    slot = g % 2

    def tok(ref, t):
        return ref.at[pl.ds(pl.multiple_of(t * k, k), k)]

    def gather(i1_ref, i2_ref, s):
        def issue(t, carry):
            pltpu.make_async_copy(tok(ys_hbm, i1_ref[0, 0, t]), tok(b1_ref.at[s], t),
                                  sem.at[s, 0]).start(priority=0)
            pltpu.make_async_copy(tok(ys_hbm, i2_ref[0, 0, t]), tok(b2_ref.at[s], t),
                                  sem.at[s, 1]).start(priority=1)
            return carry

        lax.fori_loop(0, tm, issue, 0, unroll=8)

    @pl.when(g == 0)
    def _():
        gather(d1_ref, d2_ref, 0)

    @pl.when(g + 1 < pl.num_programs(0))
    def _():
        gather(n1_ref, n2_ref, 1 - slot)

    pltpu.make_async_copy(ys_hbm.at[pl.ds(0, tm * k)], b1_ref.at[slot], sem.at[slot, 0]).wait()
    pltpu.make_async_copy(ys_hbm.at[pl.ds(0, tm * k)], b2_ref.at[slot], sem.at[slot, 1]).wait()
    route = route_ref[0]
    moe = (route[:, 2:3] * _load_token_tiles(b1_ref, (slot,), tm, k)
           + route[:, 3:4] * _load_token_tiles(b2_ref, (slot,), tm, k))
    h = h_ref[0] + mod_ref[0, 5:6, :] * moe
    if final_norm:
        h = _rms(h) * nf_ref[...]
    o_ref[0] = h


def _moe_combine(ys, dest1, dest2, h, route, mod, nf, final_norm):
    bsz, seq, d = h.shape
    tm = MOE_COMBINE_TILE
    nt = seq // tm
    steps = bsz * nt
    dspec = pl.BlockSpec((1, 1, tm), lambda g: (g, 0, 0), memory_space=pltpu.SMEM)
    nspec = pl.BlockSpec((1, 1, tm), lambda g: (jnp.minimum(g + 1, steps - 1), 0, 0), memory_space=pltpu.SMEM)
    row = lambda n: pl.BlockSpec((1, tm, n), lambda g: (g // nt, g % nt, 0))
    d1 = dest1.reshape(steps, 1, tm)
    d2 = dest2.reshape(steps, 1, tm)
    return pl.pallas_call(
        functools.partial(_moe_combine_kernel, final_norm=final_norm),
        grid=(steps,),
        in_specs=[dspec, dspec, nspec, nspec, row(d), row(LANES),
                  pl.BlockSpec((1, N_MOD, d), lambda g: (g // nt, 0, 0)),
                  pl.BlockSpec(nf.shape, lambda g: (0, 0)),
                  pl.BlockSpec(memory_space=pl.ANY)],
        out_specs=row(d),
        out_shape=jax.ShapeDtypeStruct((bsz, seq, d), F32),
        scratch_shapes=[pltpu.VMEM((2, tm * (d // LANES), LANES), ys.dtype),
                        pltpu.VMEM((2, tm * (d // LANES), LANES), ys.dtype),
                        pltpu.SemaphoreType.DMA((2, 2))],
        compiler_params=pltpu.CompilerParams(
            dimension_semantics=("arbitrary",), vmem_limit_bytes=VMEM_LIMIT),
        name="moe_combine",
    )(d1, d2, d1, d2, h, route, mod, nf, ys)


def _moe_sparse(m, route, counts, wg, wu, wd, h, mod, nf, final_norm):
    bsz, seq, d = h.shape
    n_tok = bsz * seq
    tmr = MOE_ROWS
    n_exp = wg.shape[0]
    n_tiles = (2 * n_tok) // tmr + n_exp
    i32 = jnp.int32
    r = route.reshape(n_tok, LANES)
    e1, e2, k1, k2 = (r[:, j].astype(i32) for j in (0, 1, 4, 5))
    cnt = counts[-1, 0, ROUTE_OFF:ROUTE_OFF + n_exp].astype(i32)
    ptiles = (cnt + tmr - 1) // tmr
    tend = jnp.cumsum(ptiles)
    gstart = (tend - ptiles) * tmr
    eids = jnp.arange(n_exp, dtype=i32)
    dest1 = k1 + jnp.sum(jnp.where(e1[:, None] == eids[None, :], gstart[None, :], 0), axis=1)
    dest2 = k2 + jnp.sum(jnp.where(e2[:, None] == eids[None, :], gstart[None, :], 0), axis=1)
    n_active = tend[-1:]
    tile_expert = jnp.sum(jnp.arange(n_tiles, dtype=i32)[:, None] >= tend[None, :], axis=1).astype(i32)
    last_expert = jnp.sum(n_active - 1 >= tend).astype(i32)
    tile_expert = jnp.minimum(tile_expert, last_expert)
    pads = jnp.stack([jnp.concatenate([gstart + cnt, n_active]),
                      jnp.concatenate([gstart + ptiles * tmr, jnp.full((1,), n_tiles, i32)])]).astype(i32)
    k = d // LANES
    xs = _moe_scatter(m.reshape(n_tok * k, LANES), dest1, dest2, pads, n_tiles * tmr, k)
    ys = _moe_ffn(xs, tile_expert, n_active.astype(i32), wg, wu, wd)
    return _moe_combine(ys, dest1, dest2, h, route, mod, nf, final_norm)


def kernel(x, c, norm1_g, norm2_g, w_ada, b_ada, w_in, lam_re, lam_im, log_step, s5_b_re, s5_b_im,
           s5_c_re, s5_c_im, s5_d, w_glu, b_glu, conv_w, a_log, dt_bias, gdn_norm_g, w_out,
           w_router_grp, w_router_exp, w_gate, w_up, w_down, normf_g):
    bsz, seq, d = x.shape
    depth = w_ada.shape[0]
    d_s5 = s5_d.shape[1]
    d_gdn = w_out.shape[1] - d_s5
    n_heads = d_gdn // GDN_HEAD_DIM
    assert seq % S5_TILE == 0 or seq < S5_TILE
    assert seq % ROW_TILE == 0 and seq % GDN_CHUNK == 0 and d_s5 % LANES == 0
    assert 2 * n_heads <= LANES and N_EXPERT_GROUPS + N_EXPERTS <= LANES

    h = x
    for l in range(depth):
        mod = _ada(c, w_ada[l], b_ada[l]).reshape(bsz, N_MOD, d)
        wi = w_in[l]
        c0, c1, c2 = d_s5, d_s5 + 3 * d_gdn, d_s5 + 4 * d_gdn
        wab = jnp.zeros((d, LANES), F32).at[:, :2 * n_heads].set(wi[:, c2:c2 + 2 * n_heads])
        u, qkv, z, ab = _inproj(h, mod, norm1_g[l].reshape(1, d), wi[:, :c0].astype(BF16),
                                wi[:, c0:c1].astype(BF16), wi[:, c1:c2].astype(BF16), wab.astype(BF16))
        tabs = _s5_tables(lam_re[l], lam_im[l], log_step[l], s5_b_re[l], s5_b_im[l],
                          s5_c_re[l], s5_c_im[l])
        ys = _s5(u, *tabs, s5_d[l])
        yg = _gdn2(qkv, z, ab, conv_w[l], a_log[l], dt_bias[l], gdn_norm_g[l])
        wr = jnp.zeros((d, LANES), F32)
        wr = wr.at[:, :N_EXPERT_GROUPS].set(w_router_grp[l])
        wr = wr.at[:, ROUTE_OFF:ROUTE_OFF + N_EXPERTS].set(w_router_exp[l])
        wr_hi = wr.astype(BF16)
        wr = jnp.concatenate([wr_hi, (wr - wr_hi.astype(F32)).astype(BF16)], axis=1)
        h1, m, route, counts = _outproj(ys, yg, h, mod, w_glu[l].astype(BF16), b_glu[l].reshape(1, d_s5),
                                        w_out[l, :d_s5].astype(BF16), w_out[l, d_s5:].astype(BF16),
                                        norm2_g[l].reshape(1, d), wr)
        h = _moe_sparse(m, route, counts, w_gate[l], w_up[l], w_down[l], h1, mod,
                        normf_g.reshape(1, d), final_norm=(l == depth - 1))
    return h
```

```python
import functools
import math

import jax
import jax.numpy as jnp
from jax import lax
from jax.experimental import pallas as pl
from jax.experimental.pallas import tpu as pltpu

F32 = jnp.float32
BF16 = jnp.bfloat16
HIGHEST = lax.Precision.HIGHEST
EPS = 1e-6

S5_GROUP = 16
S5_STATE = 64
GDN_HEAD_DIM = 128
CONV_WIDTH = 4
N_EXPERT_GROUPS = 4
EXPERTS_PER_GROUP = 8
N_EXPERTS = N_EXPERT_GROUPS * EXPERTS_PER_GROUP
N_MOD = 6

LANES = 128
SUBLANES = 8
VMEM_LIMIT = 56 * 1024 * 1024

S5_CHUNK = SUBLANES
S5_TILE = 2048
GDN_CHUNK = 64
GDN_BLOCK = 256
GDN_SIDE = 64
ROW_TILE = 512
MOE_TILE = 1024
MOE_ROWS = 256
MOE_SCATTER_TILE = 1024
MOE_COMBINE_TILE = 1024
ROUTE_OFF = N_EXPERT_GROUPS


def _dot(a, b, **kw):
    return jnp.dot(a, b, preferred_element_type=F32, **kw)


def _silu(v):
    return v * jax.nn.sigmoid(v)


def _rms(v):
    return v * lax.rsqrt(jnp.mean(v * v, axis=-1, keepdims=True) + EPS)


def _store_token_tiles(ref, idx, v):
    n, width = v.shape
    k = width // LANES
    for s in range(k):
        ref[idx + (pl.ds(s, n, stride=k), slice(None))] = v[:, s * LANES:(s + 1) * LANES]


def _load_token_tiles(ref, idx, n, k):
    return jnp.concatenate([ref[idx + (pl.ds(s, n, stride=k), slice(None))] for s in range(k)], axis=1)


def _ada_kernel(c_ref, w_ref, b_ref, o_ref):
    o_ref[...] = _dot(_silu(c_ref[...]), w_ref[...], precision=HIGHEST) + b_ref[...]


def _ada(c, w, b):
    bsz, d = c.shape
    n = w.shape[1]
    return pl.pallas_call(
        _ada_kernel,
        grid=(n // d,),
        in_specs=[pl.BlockSpec((bsz, d), lambda j: (0, 0)),
                  pl.BlockSpec((d, d), lambda j: (0, j)),
                  pl.BlockSpec((1, d), lambda j: (0, j))],
        out_specs=pl.BlockSpec((bsz, d), lambda j: (0, j)),
        out_shape=jax.ShapeDtypeStruct((bsz, n), F32),
        compiler_params=pltpu.CompilerParams(vmem_limit_bytes=VMEM_LIMIT),
        name="ada",
    )(c, w, b.reshape(1, n))


def _inproj_kernel(x_ref, mod_ref, g_ref, wu_ref, wqkv_ref, wz_ref, wab_ref,
                   u_ref, qkv_ref, z_ref, ab_ref):
    x = x_ref[0]
    y = _rms(x) * g_ref[...]
    h = (y * (1.0 + mod_ref[0, 1:2, :]) + mod_ref[0, 0:1, :]).astype(BF16)
    u_ref[0] = _dot(h, wu_ref[...])
    qkv_ref[0] = _dot(h, wqkv_ref[...]).astype(BF16)
    z_ref[0] = _dot(h, wz_ref[...]).astype(BF16)
    ab_ref[0] = _dot(h, wab_ref[...])


def _inproj(x, mod, g, wu, wqkv, wz, wab):
    bsz, seq, d = x.shape
    tm = ROW_TILE
    full = lambda a: pl.BlockSpec(a.shape, lambda b, i: (0,) * a.ndim)
    row = lambda n: pl.BlockSpec((1, tm, n), lambda b, i: (b, i, 0))
    return pl.pallas_call(
        _inproj_kernel,
        grid=(bsz, seq // tm),
        in_specs=[row(d), pl.BlockSpec((1, N_MOD, d), lambda b, i: (b, 0, 0)), full(g),
                  full(wu), full(wqkv), full(wz), full(wab)],
        out_specs=[row(wu.shape[1]), row(wqkv.shape[1]), row(wz.shape[1]), row(LANES)],
        out_shape=[jax.ShapeDtypeStruct((bsz, seq, wu.shape[1]), F32),
                   jax.ShapeDtypeStruct((bsz, seq, wqkv.shape[1]), BF16),
                   jax.ShapeDtypeStruct((bsz, seq, wz.shape[1]), BF16),
                   jax.ShapeDtypeStruct((bsz, seq, LANES), F32)],
        compiler_params=pltpu.CompilerParams(
            dimension_semantics=("parallel", "parallel"), vmem_limit_bytes=VMEM_LIMIT),
        name="inproj",
    )(x, mod, g, wu, wqkv, wz, wab)


def _s5_tables(lam_re, lam_im, log_step, b_re, b_im, c_re, c_im):
    ch = S5_CHUNK
    n_grp, n_st = lam_re.shape
    gpb = LANES // S5_GROUP
    nblk = n_grp // gpb
    step = jnp.exp(log_step.astype(F32))[:, None]
    lr = lam_re.astype(F32)
    li = lam_im.astype(F32)

    def lam_pow(j):
        mag = jnp.exp(j * lr * step)
        ang = j * li * step
        return mag * jnp.cos(ang), mag * jnp.sin(ang)

    ar, ai = lam_pow(1.0)
    den = lr * lr + li * li
    fr = ((ar - 1.0) * lr + ai * li) / den
    fi = (ai * lr - (ar - 1.0) * li) / den
    bbr = fr[..., None] * b_re - fi[..., None] * b_im
    bbi = fr[..., None] * b_im + fi[..., None] * b_re
    pows = [lam_pow(float(j)) for j in range(ch + 1)]
    pr = jnp.stack([p[0] for p in pows])
    pi = jnp.stack([p[1] for p in pows])
    lbr = pr[:ch, :, :, None] * bbr[None] - pi[:ch, :, :, None] * bbi[None]
    lbi = pr[:ch, :, :, None] * bbi[None] + pi[:ch, :, :, None] * bbr[None]
    kdim = ch * LANES
    sdim = 2 * gpb * n_st

    def expand(small, col_grp, n_rows):
        n_cols = small.shape[-1]
        rq = jnp.arange(gpb)[:, None, None]
        cq = ((jnp.arange(n_cols) // col_grp) % gpb)[None, None, :]
        full = jnp.where(rq == cq, small[:, :, None, :, :], 0.0).astype(BF16)
        return full.reshape(nblk, n_rows, n_cols)

    kmat = (jnp.einsum('ghp,jgpk->jghk', c_re, lbr, precision=HIGHEST)
            - jnp.einsum('ghp,jgpk->jghk', c_im, lbi, precision=HIGHEST))
    s_idx = jnp.arange(ch)[:, None]
    t_idx = jnp.arange(ch)[None, :]
    lag = jnp.clip(t_idx - s_idx, 0, ch - 1)
    kst = jnp.where((t_idx >= s_idx)[..., None, None, None], kmat[lag], 0.0)
    kst = kst.reshape(ch, ch, nblk, gpb, S5_GROUP, S5_GROUP)
    t_small = kst.transpose(2, 0, 5, 1, 3, 4).reshape(nblk, ch, S5_GROUP, kdim)
    tmat = expand(t_small, S5_GROUP, kdim)

    wri = jnp.stack([lbr[::-1], lbi[::-1]]).reshape(2, ch, nblk, gpb, n_st, S5_GROUP)
    w_small = wri.transpose(2, 1, 5, 0, 3, 4).reshape(nblk, ch, S5_GROUP, sdim)
    win = expand(w_small, n_st, kdim)

    pr1 = pr[1:, :, None, :]
    pi1 = pi[1:, :, None, :]
    clr = c_re[None] * pr1 - c_im[None] * pi1
    cli = c_re[None] * pi1 + c_im[None] * pr1
    wo = jnp.stack([clr, -cli]).reshape(2, ch, nblk, gpb, S5_GROUP, n_st)
    o_small = wo.transpose(2, 0, 5, 1, 3, 4).reshape(nblk, 2, n_st, kdim)
    wout = expand(o_small, S5_GROUP, sdim)

    half = gpb * n_st
    amat = jnp.stack([pr[ch].reshape(nblk, half), pi[ch].reshape(nblk, half)], axis=1)
    return tmat, win, wout, amat


def _s5_kernel(u_ref, t_ref, win_ref, wout_ref, a_ref, d_ref, y_ref, st_ref, v_ref, xp_ref):
    ch = S5_CHUNK
    n = u_ref.shape[1] // ch
    half = st_ref.shape[1]

    @pl.when(pl.program_id(2) == 0)
    def _():
        st_ref[...] = jnp.zeros_like(st_ref)

    slabs = [u_ref[0, pl.ds(s, n, stride=ch), :] for s in range(ch)]
    ucat = jnp.concatenate(slabs, axis=1).astype(BF16)
    y = _dot(ucat, t_ref[0])
    v_ref[...] = _dot(ucat, win_ref[0])
    a_r = a_ref[0, 0:1, :]
    a_i = a_ref[0, 1:2, :]

    def body(r, carry):
        x_r, x_i = carry
        xp_ref[pl.ds(r, 1), 0:half] = x_r
        xp_ref[pl.ds(r, 1), half:2 * half] = x_i
        v_r = v_ref[pl.ds(r, 1), 0:half]
        v_i = v_ref[pl.ds(r, 1), half:2 * half]
        return a_r * x_r - a_i * x_i + v_r, a_r * x_i + a_i * x_r + v_i

    x_r, x_i = lax.fori_loop(0, n, body, (st_ref[0:1, :], st_ref[1:2, :]), unroll=4)
    st_ref[0:1, :] = x_r
    st_ref[1:2, :] = x_i
    y = y + _dot(xp_ref[...].astype(BF16), wout_ref[0])
    d = d_ref[0]
    for t in range(ch):
        y_ref[0, pl.ds(t, n, stride=ch), :] = y[:, t * LANES:(t + 1) * LANES] + d * slabs[t]


def _s5(u, tmat, win, wout, amat, d_skip):
    bsz, seq, dch = u.shape
    nblk = dch // LANES
    tt = min(S5_TILE, seq)
    n = tt // S5_CHUNK
    kdim = S5_CHUNK * LANES
    sdim = win.shape[2]
    wspec = lambda a: pl.BlockSpec((1,) + a.shape[1:], lambda b, c, i: (c, 0, 0))
    return pl.pallas_call(
        _s5_kernel,
        grid=(bsz, nblk, seq // tt),
        in_specs=[pl.BlockSpec((1, tt, LANES), lambda b, c, i: (b, i, c)),
                  wspec(tmat), wspec(win), wspec(wout), wspec(amat),
                  pl.BlockSpec((1, 1, LANES), lambda b, c, i: (c, 0, 0))],
        out_specs=pl.BlockSpec((1, tt, LANES), lambda b, c, i: (b, i, c)),
        out_shape=jax.ShapeDtypeStruct((bsz, seq, dch), F32),
        scratch_shapes=[pltpu.VMEM((2, sdim // 2), F32),
                        pltpu.VMEM((n, sdim), F32),
                        pltpu.VMEM((n, sdim), F32)],
        compiler_params=pltpu.CompilerParams(
            dimension_semantics=("parallel", "parallel", "arbitrary"),
            vmem_limit_bytes=VMEM_LIMIT),
        name="s5",
    )(u, tmat, win, wout, amat, d_skip.reshape(nblk, 1, LANES))


def _cumsum_rows(v):
    n = v.shape[0]
    row = lax.broadcasted_iota(jnp.int32, v.shape, 0)
    sh = 1
    while sh < n:
        v = v + jnp.where(row >= sh, pltpu.roll(v, sh, axis=0), 0.0)
        sh *= 2
    return v


def _unit_lower_inverse(lm):
    n = lm.shape[0]
    ri = lax.broadcasted_iota(jnp.int32, (n, n), 0)
    ci = lax.broadcasted_iota(jnp.int32, (n, n), 1)
    eye = jnp.where(ri == ci, 1.0, 0.0).astype(F32)
    m = 1
    d = eye
    while m < n:
        mask = ((ri // (2 * m)) == (ci // (2 * m))) & (((ri // m) % 2) == 1) & (((ci // m) % 2) == 0)
        e = jnp.where(mask, lm, 0.0)
        if m == 1:
            d = eye - e
        else:
            de = _dot(d.astype(BF16), e.astype(BF16))
            d = d - _dot(de.astype(BF16), d.astype(BF16))
        m *= 2
    return d


def _gdn_kernel(qkv_ref, z_ref, ab_ref, cw_ref, gp_ref, ng_ref, o_ref, s_ref, xp_ref, *, n_heads):
    ck = GDN_CHUNK
    dh = GDN_HEAD_DIM
    bsz = qkv_ref.shape[0]
    dg = n_heads * dh
    halo = SUBLANES

    @pl.when(pl.program_id(0) == 0)
    def _():
        s_ref[...] = jnp.zeros_like(s_ref)
        xp_ref[:, 0:halo, :] = jnp.zeros((bsz, halo, 3 * dg), F32)

    ri = lax.broadcasted_iota(jnp.int32, (ck, ck), 0)
    ci = lax.broadcasted_iota(jnp.int32, (ck, ck), 1)
    causal = ri >= ci
    strict = ri > ci
    lane = lax.broadcasted_iota(jnp.int32, (ck, LANES), 1)
    a_log = gp_ref[0:1, :]
    dt_bias = gp_ref[1:2, :]
    ng = ng_ref[...]

    for b in range(bsz):
        xp_ref[b, halo:halo + ck, :] = qkv_ref[b].astype(F32)
        conv = cw_ref[0:1, :] * xp_ref[b, halo - 3:halo - 3 + ck, :]
        for j in range(1, CONV_WIDTH):
            conv = conv + cw_ref[j:j + 1, :] * xp_ref[b, halo - 3 + j:halo - 3 + j + ck, :]
        xp_ref[b, 0:halo, :] = xp_ref[b, ck:ck + halo, :]
        act = _silu(conv)

        ab = ab_ref[b]
        sp = jnp.maximum(ab + dt_bias, 0.0) + jnp.log1p(jnp.exp(-jnp.abs(ab + dt_bias)))
        g = jnp.where(lane < n_heads, -jnp.exp(a_log) * sp, 0.0)
        gc = _cumsum_rows(g)
        gct = gc.T
        beta_all = jax.nn.sigmoid(ab)
        zb = z_ref[b].astype(F32)

        for h in range(n_heads):
            q = act[:, h * dh:(h + 1) * dh]
            k = act[:, dg + h * dh:dg + (h + 1) * dh]
            v = act[:, 2 * dg + h * dh:2 * dg + (h + 1) * dh]
            q = q * lax.rsqrt(jnp.sum(q * q, axis=-1, keepdims=True) + EPS) * (dh ** -0.5)
            k = k * lax.rsqrt(jnp.sum(k * k, axis=-1, keepdims=True) + EPS)
            beta = beta_all[:, n_heads + h:n_heads + h + 1]
            gcol = gc[:, h:h + 1]
            grow = gct[h:h + 1, :]
            glast = gc[ck - 1:ck, h:h + 1]
            decay = jnp.exp(jnp.where(causal, gcol - grow, -1e30))
            kb = k * beta
            kq = jnp.concatenate([kb, q], axis=0).astype(BF16)
            m1 = lax.dot_general(kq, k.astype(BF16), (((1,), (1,)), ((), ())),
                                 preferred_element_type=F32)
            lm = jnp.where(strict, m1[0:ck] * decay, 0.0)
            qk = m1[ck:2 * ck] * decay
            tinv = _unit_lower_inverse(lm)
            eg = jnp.exp(gcol)
            rhs = jnp.concatenate([v * beta, kb * eg], axis=1).astype(BF16)
            w = _dot(tinv.astype(BF16), rhs)
            value = w[:, 0:dh]
            kcd = w[:, dh:2 * dh]
            qd = q * eg
            kt = k * jnp.exp(glast - gcol)
            st = s_ref[b * n_heads + h]
            m2 = _dot(jnp.concatenate([kcd, qd], axis=0).astype(BF16), st.astype(BF16))
            v_new = value - m2[0:ck]
            vnb = v_new.astype(BF16)
            o = m2[ck:2 * ck] + _dot(qk.astype(BF16), vnb)
            s_ref[b * n_heads + h] = st * jnp.exp(glast) + lax.dot_general(
                kt.astype(BF16), vnb, (((0,), (0,)), ((), ())), preferred_element_type=F32)
            zh = zb[:, h * dh:(h + 1) * dh]
            o_ref[b, :, h * dh:(h + 1) * dh] = (_rms(o) * ng * _silu(zh)).astype(o_ref.dtype)


def _gdn(qkv, z, ab, conv_w, a_log, dt_bias, norm_g):
    bsz, seq, _ = qkv.shape
    dg = z.shape[2]
    n_heads = dg // GDN_HEAD_DIM
    ck = GDN_CHUNK
    gp = jnp.zeros((2, LANES), F32).at[0, :n_heads].set(a_log).at[1, :n_heads].set(dt_bias)
    blk = lambda n: pl.BlockSpec((bsz, ck, n), lambda i: (0, i, 0))
    full = lambda a: pl.BlockSpec(a.shape, lambda i: (0,) * a.ndim)
    ng = norm_g.reshape(1, GDN_HEAD_DIM)
    return pl.pallas_call(
        functools.partial(_gdn_kernel, n_heads=n_heads),
        grid=(seq // ck,),
        in_specs=[blk(3 * dg), blk(dg), blk(LANES), full(conv_w), full(gp), full(ng)],
        out_specs=blk(dg),
        out_shape=jax.ShapeDtypeStruct((bsz, seq, dg), BF16),
        scratch_shapes=[pltpu.VMEM((bsz * n_heads, GDN_HEAD_DIM, GDN_HEAD_DIM), F32),
                        pltpu.VMEM((bsz, ck + SUBLANES, 3 * dg), F32)],
        compiler_params=pltpu.CompilerParams(
            dimension_semantics=("arbitrary",), vmem_limit_bytes=VMEM_LIMIT),
        name="gdn",
    )(qkv, z, ab, conv_w, gp, ng)


def _gdn2_kernel(qkv_ref, z_ref, ab_ref, cw_ref, gp_ref, ng_ref, o_ref,
                 s_ref, xp_ref, gc_ref, gct_ref, kbq_ref, kn_ref, rhs_ref, cq_ref, qkt_ref,
                 lm_ref, d_ref, db_ref, de_ref, ds_ref, pe_ref, val_ref, vn_ref, *, n_heads):
    tt = qkv_ref.shape[1]
    dh = GDN_HEAD_DIM
    bsz = qkv_ref.shape[0]
    dg = n_heads * dh
    halo = SUBLANES
    sb = ds_ref.shape[1]
    nsb = tt // sb
    pairs = [(b, h) for b in range(bsz) for h in range(n_heads)]

    @pl.when(pl.program_id(0) == 0)
    def _():
        s_ref[...] = jnp.zeros_like(s_ref)
        xp_ref[...] = jnp.zeros_like(xp_ref)

    ri = lax.broadcasted_iota(jnp.int32, (tt, tt), 0)
    ci = lax.broadcasted_iota(jnp.int32, (tt, tt), 1)
    lane = lax.broadcasted_iota(jnp.int32, (tt, LANES), 1)
    a_log = gp_ref[0:1, :]
    dt_bias = gp_ref[1:2, :]
    shift_op = jnp.concatenate(
        [jnp.where(ri == ci + j, 1.0, 0.0) for j in range(1, CONV_WIDTH)], axis=0).astype(BF16)

    for b in range(bsz):
        x = qkv_ref[b]
        x32 = x.astype(F32)
        shifted = _dot(shift_op, x)
        edge = jnp.concatenate([xp_ref[b], x32[0:halo]], axis=0)
        conv = cw_ref[CONV_WIDTH - 1:CONV_WIDTH, :] * x32
        for j in range(1, CONV_WIDTH):
            sh_j = jnp.concatenate([edge[halo - j:2 * halo - j],
                                    shifted[(j - 1) * tt + halo:j * tt]], axis=0)
            conv = conv + cw_ref[CONV_WIDTH - 1 - j:CONV_WIDTH - j, :] * sh_j
        xp_ref[b] = x32[tt - halo:tt]
        act = _silu(conv)

        ab = ab_ref[b]
        sp = jnp.maximum(ab + dt_bias, 0.0) + jnp.log1p(jnp.exp(-jnp.abs(ab + dt_bias)))
        g = jnp.where(lane < n_heads, -jnp.exp(a_log) * sp, 0.0)
        gc = _cumsum_rows(g)
        gc_ref[b] = gc
        gct_ref[b] = gc.T
        glast = gc[tt - 1:tt, :]
        egc = jnp.exp(gc)
        ekt = jnp.exp(glast - gc)
        beta_all = jax.nn.sigmoid(ab)
        for h in range(n_heads):
            p = b * n_heads + h
            q = act[:, h * dh:(h + 1) * dh]
            k = act[:, dg + h * dh:dg + (h + 1) * dh]
            v = act[:, 2 * dg + h * dh:2 * dg + (h + 1) * dh]
            q = q * lax.rsqrt(jnp.sum(q * q, axis=-1, keepdims=True) + EPS) * (dh ** -0.5)
            k = k * lax.rsqrt(jnp.sum(k * k, axis=-1, keepdims=True) + EPS)
            beta = beta_all[:, n_heads + h:n_heads + h + 1]
            eg = egc[:, h:h + 1]
            kb = k * beta
            kbq_ref[p, 0:tt, :] = kb.astype(BF16)
            kbq_ref[p, tt:2 * tt, :] = q.astype(BF16)
            kn_ref[p] = k.astype(BF16)
            rhs_ref[p, :, 0:dh] = (v * beta).astype(BF16)
            rhs_ref[p, :, dh:2 * dh] = (kb * eg).astype(BF16)
            cq_ref[p, tt:2 * tt, :] = (q * eg).astype(BF16)
            qkt_ref[p, tt:tt + dh, :] = (k * ekt[:, h:h + 1]).T.astype(BF16)

    for b, h in pairs:
        p = b * n_heads + h
        m1 = lax.dot_general(kbq_ref[p], kn_ref[p], (((1,), (1,)), ((), ())),
                             preferred_element_type=F32)
        gcol = gc_ref[b, :, h:h + 1]
        grow = gct_ref[b, h:h + 1, :]
        decay = jnp.exp(jnp.where(ri >= ci, gcol - grow, -1e30))
        lm = jnp.where(ri > ci, m1[0:tt] * decay, 0.0)
        lm_ref[p] = lm.astype(BF16)
        qkt_ref[p, 0:tt, :] = (m1[tt:2 * tt] * decay).astype(BF16)
        first = jnp.where(ri == ci + 1, jnp.where((ci & 1) == 0, lm, 0.0), 0.0)
        d = jnp.where(ri == ci, 1.0, 0.0) - first
        ds_ref[p] = sum(d[j * sb:(j + 1) * sb] for j in range(1, nsb)) + d[0:sb]

    in_blk = (ri // sb) == (ci // sb)
    blk_b = jnp.where(in_blk, 1.0, 0.0).astype(BF16)
    m = 2
    sh = 1
    while m < tt:
        rb = ri >> sh
        cb = ci >> sh
        sel = jnp.where(rb == cb + 1, jnp.where((cb & 1) == 0, 1.0, 0.0), 0.0).astype(BF16)
        if 2 * m <= sb:
            for b, h in pairs:
                p = b * n_heads + h
                pe_ref[p] = _dot(ds_ref[p].astype(BF16), lm_ref[p] * sel).astype(BF16)
            for b, h in pairs:
                p = b * n_heads + h
                ds = ds_ref[p]
                dbd = jnp.concatenate([ds.astype(BF16)] * nsb, axis=0) * blk_b
                ds = ds - _dot(pe_ref[p], dbd)
                ds_ref[p] = ds
                if 4 * m > sb:
                    d = jnp.where(in_blk, jnp.concatenate([ds] * nsb, axis=0), 0.0)
                    d_ref[p] = d
                    db_ref[p] = d.astype(BF16)
        else:
            for b, h in pairs:
                p = b * n_heads + h
                de_ref[p] = _dot(db_ref[p], lm_ref[p] * sel).astype(BF16)
            for b, h in pairs:
                p = b * n_heads + h
                d = d_ref[p] - _dot(de_ref[p], db_ref[p])
                d_ref[p] = d
                db_ref[p] = d.astype(BF16)
        m *= 2
        sh += 1

    for b, h in pairs:
        p = b * n_heads + h
        w = _dot(db_ref[p], rhs_ref[p])
        val_ref[p] = w[:, 0:dh]
        cq_ref[p, 0:tt, :] = w[:, dh:2 * dh].astype(BF16)

    for b, h in pairs:
        p = b * n_heads + h
        m2 = _dot(cq_ref[p], s_ref[p].astype(BF16))
        vn_ref[p] = (val_ref[p] - m2[0:tt]).astype(BF16)
        val_ref[p] = m2[tt:2 * tt]
    for b, h in pairs:
        p = b * n_heads + h
        r = _dot(qkt_ref[p], vn_ref[p])
        o = val_ref[p] + r[0:tt]
        egl = jnp.exp(gc_ref[b, tt - 1:tt, h:h + 1])
        s_ref[p] = s_ref[p] * egl + r[tt:tt + dh]
        zh = z_ref[b, :, h * dh:(h + 1) * dh].astype(F32)
        o_ref[b, :, h * dh:(h + 1) * dh] = (_rms(o) * ng_ref[...] * _silu(zh)).astype(o_ref.dtype)


def _gdn2(qkv, z, ab, conv_w, a_log, dt_bias, norm_g):
    bsz, seq, _ = qkv.shape
    dg = z.shape[2]
    dh = GDN_HEAD_DIM
    n_heads = dg // dh
    tt = GDN_BLOCK
    npair = bsz * n_heads
    gp = jnp.zeros((2, LANES), F32).at[0, :n_heads].set(a_log).at[1, :n_heads].set(dt_bias)
    blk = lambda n: pl.BlockSpec((bsz, tt, n), lambda i: (0, i, 0))
    full = lambda a: pl.BlockSpec(a.shape, lambda i: (0,) * a.ndim)
    ng = norm_g.reshape(1, dh)
    return pl.pallas_call(
        functools.partial(_gdn2_kernel, n_heads=n_heads),
        grid=(seq // tt,),
        in_specs=[blk(3 * dg), blk(dg), blk(LANES), full(conv_w), full(gp), full(ng)],
        out_specs=blk(dg),
        out_shape=jax.ShapeDtypeStruct((bsz, seq, dg), BF16),
        scratch_shapes=[pltpu.VMEM((npair, dh, dh), F32),
                        pltpu.VMEM((bsz, SUBLANES, 3 * dg), F32),
                        pltpu.VMEM((bsz, tt, LANES), F32),
                        pltpu.VMEM((bsz, LANES, tt), F32),
                        pltpu.VMEM((npair, 2 * tt, dh), BF16),
                        pltpu.VMEM((npair, tt, dh), BF16),
                        pltpu.VMEM((npair, tt, 2 * dh), BF16),
                        pltpu.VMEM((npair, 2 * tt, dh), BF16),
                        pltpu.VMEM((npair, tt + dh, tt), BF16),
                        pltpu.VMEM((npair, tt, tt), BF16),
                        pltpu.VMEM((npair, tt, tt), F32),
                        pltpu.VMEM((npair, tt, tt), BF16),
                        pltpu.VMEM((npair, tt, tt), BF16),
                        pltpu.VMEM((npair, GDN_SIDE, tt), F32),
                        pltpu.VMEM((npair, GDN_SIDE, tt), BF16),
                        pltpu.VMEM((npair, tt, dh), F32),
                        pltpu.VMEM((npair, tt, dh), BF16)],
        compiler_params=pltpu.CompilerParams(
            dimension_semantics=("arbitrary",), vmem_limit_bytes=VMEM_LIMIT),
        name="gdn",
    )(qkv, z, ab, conv_w, gp, ng)


def _outproj_kernel(ys_ref, yg_ref, x_ref, mod_ref, wglu_ref, bglu_ref, wos_ref, wog_ref,
                    g2_ref, wr_ref, h_ref, m_ref, route_ref, count_ref, cnt_ref):
    y = jax.nn.gelu(ys_ref[0])
    gate = jax.nn.sigmoid(_dot(y.astype(BF16), wglu_ref[...]) + bglu_ref[...])
    y = (y * gate).astype(BF16)
    mix = _dot(y, wos_ref[...]) + _dot(yg_ref[0], wog_ref[...])
    h = x_ref[0] + mod_ref[0, 2:3, :] * mix
    h_ref[0] = h
    m = _rms(h) * g2_ref[...] * (1.0 + mod_ref[0, 4:5, :]) + mod_ref[0, 3:4, :]
    _store_token_tiles(m_ref, (0,), m)

    m_hi = m.astype(BF16)
    m_lo = (m - m_hi.astype(F32)).astype(BF16)
    part = _dot(m_hi, wr_ref[...])
    logits = part[:, 0:LANES] + part[:, LANES:2 * LANES] + _dot(m_lo, wr_ref[:, 0:LANES])
    lane = lax.broadcasted_iota(jnp.int32, logits.shape, 1)
    neg = -1e30
    is_grp = lane < N_EXPERT_GROUPS
    gl = jnp.where(is_grp, logits, neg)
    gmax = jnp.max(gl, axis=-1, keepdims=True)
    gidx = jnp.min(jnp.where(gl == gmax, lane, LANES), axis=-1, keepdims=True)
    p_grp = 1.0 / jnp.sum(jnp.where(is_grp, jnp.exp(gl - gmax), 0.0), axis=-1, keepdims=True)
    lo = ROUTE_OFF + gidx * EXPERTS_PER_GROUP
    el = jnp.where((lane >= lo) & (lane < lo + EXPERTS_PER_GROUP), logits, neg)
    v1 = jnp.max(el, axis=-1, keepdims=True)
    i1 = jnp.min(jnp.where(el == v1, lane, LANES), axis=-1, keepdims=True)
    el2 = jnp.where(lane == i1, neg, el)
    v2 = jnp.max(el2, axis=-1, keepdims=True)
    i2 = jnp.min(jnp.where(el2 == v2, lane, LANES), axis=-1, keepdims=True)
    t = jnp.exp(v2 - v1)
    w1 = p_grp / (1.0 + t)
    w2 = w1 * t

    @pl.when((pl.program_id(0) == 0) & (pl.program_id(1) == 0))
    def _():
        cnt_ref[...] = jnp.zeros_like(cnt_ref)

    tm = logits.shape[0]
    oh1 = lane == i1
    oh2 = lane == i2
    cnt = jnp.where(oh1, 1.0, 0.0) + jnp.where(oh2, 1.0, 0.0)
    earlier = (lax.broadcasted_iota(jnp.int32, (tm, tm), 0) > lax.broadcasted_iota(jnp.int32, (tm, tm), 1))
    before = _dot(jnp.where(earlier, 1.0, 0.0).astype(BF16), cnt.astype(BF16)) + cnt_ref[...]
    r1 = jnp.sum(jnp.where(oh1, before, 0.0), axis=-1, keepdims=True)
    r2 = jnp.sum(jnp.where(oh2, before, 0.0), axis=-1, keepdims=True)
    cnt_ref[...] += jnp.sum(cnt, axis=0, keepdims=True)
    count_ref[0] = cnt_ref[...]
    cols = [(i1 - ROUTE_OFF).astype(F32), (i2 - ROUTE_OFF).astype(F32), w1, w2, r1, r2]
    route = jnp.zeros_like(logits)
    for j, col in enumerate(cols):
        route = jnp.where(lane == j, col, route)
    route_ref[0] = route


def _outproj(ys, yg, x, mod, wglu, bglu, wos, wog, g2, wr):
    bsz, seq, d = x.shape
    tm = ROW_TILE
    full = lambda a: pl.BlockSpec(a.shape, lambda b, i: (0,) * a.ndim)
    row = lambda n: pl.BlockSpec((1, tm, n), lambda b, i: (b, i, 0))
    return pl.pallas_call(
        _outproj_kernel,
        grid=(bsz, seq // tm),
        in_specs=[row(ys.shape[2]), row(yg.shape[2]), row(d),
                  pl.BlockSpec((1, N_MOD, d), lambda b, i: (b, 0, 0)),
                  full(wglu), full(bglu), full(wos), full(wog), full(g2), full(wr)],
        out_specs=[row(d), pl.BlockSpec((1, tm * (d // LANES), LANES), lambda b, i: (b, i, 0)), row(LANES),
                   pl.BlockSpec((1, 1, LANES), lambda b, i: (b * (seq // tm) + i, 0, 0))],
        out_shape=[jax.ShapeDtypeStruct((bsz, seq, d), F32),
                   jax.ShapeDtypeStruct((bsz, seq * (d // LANES), LANES), F32),
                   jax.ShapeDtypeStruct((bsz, seq, LANES), F32),
                   jax.ShapeDtypeStruct((bsz * (seq // tm), 1, LANES), F32)],
        scratch_shapes=[pltpu.VMEM((1, LANES), F32)],
        compiler_params=pltpu.CompilerParams(
            dimension_semantics=("arbitrary", "arbitrary"), vmem_limit_bytes=VMEM_LIMIT),
        name="outproj",
    )(ys, yg, x, mod, wglu, bglu, wos, wog, g2, wr)


def _moe_kernel(m_ref, comb_ref, wg_ref, wu_ref, wd_ref, h_ref, mod_ref, nf_ref, o_ref, acc_ref,
                *, final_norm):
    e = pl.program_id(2)

    @pl.when(e == 0)
    def _():
        acc_ref[...] = jnp.zeros_like(acc_ref)

    m = m_ref[0]
    mid = (_silu(_dot(m, wg_ref[0])) * _dot(m, wu_ref[0])).astype(BF16)
    y = _dot(mid, wd_ref[0])
    comb = comb_ref[0]
    lane = lax.broadcasted_iota(jnp.int32, comb.shape, 1)
    col = jnp.sum(jnp.where(lane == e + ROUTE_OFF, comb, 0.0), axis=-1, keepdims=True)
    acc_ref[...] += col * y

    @pl.when(e == pl.num_programs(2) - 1)
    def _():
        h = h_ref[0] + mod_ref[0, 5:6, :] * acc_ref[...]
        if final_norm:
            h = _rms(h) * nf_ref[...]
        o_ref[0] = h


def _moe(m, comb, wg, wu, wd, h, mod, nf, final_norm):
    bsz, seq, d = h.shape
    tm = min(MOE_TILE, seq)
    n_exp, _, de = wg.shape
    row = lambda n: pl.BlockSpec((1, tm, n), lambda b, i, e: (b, i, 0))
    return pl.pallas_call(
        functools.partial(_moe_kernel, final_norm=final_norm),
        grid=(bsz, seq // tm, n_exp),
        in_specs=[row(d), row(LANES),
                  pl.BlockSpec((1, d, de), lambda b, i, e: (e, 0, 0)),
                  pl.BlockSpec((1, d, de), lambda b, i, e: (e, 0, 0)),
                  pl.BlockSpec((1, de, d), lambda b, i, e: (e, 0, 0)),
                  row(d), pl.BlockSpec((1, N_MOD, d), lambda b, i, e: (b, 0, 0)),
                  pl.BlockSpec(nf.shape, lambda b, i, e: (0, 0))],
        out_specs=row(d),
        out_shape=jax.ShapeDtypeStruct((bsz, seq, d), F32),
        scratch_shapes=[pltpu.VMEM((tm, d), F32)],
        compiler_params=pltpu.CompilerParams(
            dimension_semantics=("parallel", "parallel", "arbitrary"),
            vmem_limit_bytes=VMEM_LIMIT),
        name="moe",
    )(m, comb, wg, wu, wd, h, mod, nf)


def _moe_scatter_kernel(d1_ref, d2_ref, pad_ref, m_ref, xs_hbm, zero_ref, sem, *, n_exp, k):
    i = pl.program_id(0)
    ts = d1_ref.shape[2]
    trows = zero_ref.shape[0]

    def tok(ref, t):
        return ref.at[pl.ds(pl.multiple_of(t * k, k), k)]

    def tile_copy(j):
        return pltpu.make_async_copy(
            zero_ref, xs_hbm.at[pl.ds(pl.multiple_of(j * trows, trows), trows)], sem.at[0])

    def pad_piece(e, p, act):
        lo = pad_ref[0, e]
        length = pad_ref[1, e] - lo
        start = lo + (length & ~(2 * p - 1))
        copy = pltpu.make_async_copy(
            zero_ref.at[pl.ds(0, p * k)], xs_hbm.at[pl.ds(pl.multiple_of(start * k, k), p * k)], sem.at[0])

        @pl.when((length & p) != 0)
        def _():
            act(copy)

    @pl.when(i == 0)
    def _():
        zero_ref[...] = jnp.zeros_like(zero_ref)
        pieces = [(e, p) for e in range(n_exp) for p in [1 << j for j in range((trows // k).bit_length() - 1)]]
        for e, p in pieces:
            pad_piece(e, p, lambda c: c.start())
        for e, p in pieces:
            pad_piece(e, p, lambda c: c.wait())

        def fill(j, carry):
            tile_copy(j).start()
            return carry

        def drain(j, carry):
            tile_copy(j).wait()
            return carry

        lax.fori_loop(pad_ref[0, n_exp], pad_ref[1, n_exp], fill, 0)
        lax.fori_loop(pad_ref[0, n_exp], pad_ref[1, n_exp], drain, 0)

    def row_copy(t, dst, slot):
        return pltpu.make_async_copy(tok(m_ref, t), tok(xs_hbm, dst), sem.at[1 + slot])

    def issue(t, carry):
        row_copy(t, d1_ref[0, 0, t], 0).start(priority=0)
        row_copy(t, d2_ref[0, 0, t], 1).start(priority=1)
        return carry

    lax.fori_loop(0, ts, issue, 0, unroll=8)
    for slot in range(2):
        pltpu.make_async_copy(m_ref, xs_hbm.at[pl.ds(0, ts * k)], sem.at[1 + slot]).wait()


def _moe_scatter(m, dest1, dest2, pads, n_rows, k):
    n_tok = m.shape[0] // k
    ts = MOE_SCATTER_TILE
    n_exp = pads.shape[1] - 1
    dspec = pl.BlockSpec((1, 1, ts), lambda i: (i, 0, 0), memory_space=pltpu.SMEM)
    return pl.pallas_call(
        functools.partial(_moe_scatter_kernel, n_exp=n_exp, k=k),
        grid=(n_tok // ts,),
        in_specs=[dspec, dspec,
                  pl.BlockSpec(pads.shape, lambda i: (0, 0), memory_space=pltpu.SMEM),
                  pl.BlockSpec((ts * k, LANES), lambda i: (i, 0))],
        out_specs=pl.BlockSpec(memory_space=pl.ANY),
        out_shape=jax.ShapeDtypeStruct((n_rows * k, LANES), m.dtype),
        scratch_shapes=[pltpu.VMEM((MOE_ROWS * k, LANES), m.dtype), pltpu.SemaphoreType.DMA((3,))],
        compiler_params=pltpu.CompilerParams(
            dimension_semantics=("arbitrary",), vmem_limit_bytes=VMEM_LIMIT, has_side_effects=True),
        name="moe_scatter",
    )(dest1.reshape(n_tok // ts, 1, ts), dest2.reshape(n_tok // ts, 1, ts), pads, m)


def _moe_ffn_kernel(te_ref, na_ref, xs_ref, wg_ref, wu_ref, wd_ref, ys_ref, wgb_ref, wub_ref, wdb_ref):
    i = pl.program_id(0)
    active = i < na_ref[0]
    changed = (i == 0) | (te_ref[i] != te_ref[jnp.maximum(i - 1, 0)])

    @pl.when(active & changed)
    def _():
        wgb_ref[...] = wg_ref[0].astype(BF16)
        wub_ref[...] = wu_ref[0].astype(BF16)
        wdb_ref[...] = wd_ref[0].astype(BF16)

    @pl.when(active)
    def _():
        k = wgb_ref.shape[0] // LANES
        x = _load_token_tiles(xs_ref, (), xs_ref.shape[0] // k, k).astype(BF16)
        mid = (_silu(_dot(x, wgb_ref[...])) * _dot(x, wub_ref[...])).astype(BF16)
        _store_token_tiles(ys_ref, (), _dot(mid, wdb_ref[...]))


def _moe_ffn(xs, tile_expert, n_active, wg, wu, wd):
    _, d, de = wg.shape
    tmr = MOE_ROWS * (d // LANES)
    n_rows, dp = xs.shape
    grid_spec = pltpu.PrefetchScalarGridSpec(
        num_scalar_prefetch=2,
        grid=(n_rows // tmr,),
        in_specs=[pl.BlockSpec((tmr, dp), lambda i, te, na: (jnp.minimum(i, na[0] - 1), 0)),
                  pl.BlockSpec((1, d, de), lambda i, te, na: (te[i], 0, 0)),
                  pl.BlockSpec((1, d, de), lambda i, te, na: (te[i], 0, 0)),
                  pl.BlockSpec((1, de, d), lambda i, te, na: (te[i], 0, 0))],
        out_specs=pl.BlockSpec((tmr, dp), lambda i, te, na: (jnp.minimum(i, na[0] - 1), 0)),
        scratch_shapes=[pltpu.VMEM((d, de), BF16), pltpu.VMEM((d, de), BF16), pltpu.VMEM((de, d), BF16)],
    )
    return pl.pallas_call(
        _moe_ffn_kernel,
        grid_spec=grid_spec,
        out_shape=jax.ShapeDtypeStruct((n_rows, dp), F32),
        input_output_aliases={2: 0},
        compiler_params=pltpu.CompilerParams(
            dimension_semantics=("arbitrary",), vmem_limit_bytes=VMEM_LIMIT),
        name="moe_ffn",
    )(tile_expert, n_active, xs, wg, wu, wd)


def _moe_combine_kernel(d1_ref, d2_ref, n1_ref, n2_ref, h_ref, route_ref, mod_ref, nf_ref, ys_hbm, o_ref,
                        b1_ref, b2_ref, sem, *, final_norm):
    tm = h_ref.shape[1]
    k = h_ref.shape[2] // LANES
    g = pl.program_id(0)
    slot = g % 2

    def tok(ref, t):
        return ref.at[pl.ds(pl.multiple_of(t * k, k), k)]

    def gather(i1_ref, i2_ref, s):
        def issue(t, carry):
            pltpu.make_async_copy(tok(ys_hbm, i1_ref[0, 0, t]), tok(b1_ref.at[s], t),
                                  sem.at[s, 0]).start(priority=0)
            pltpu.make_async_copy(tok(ys_hbm, i2_ref[0, 0, t]), tok(b2_ref.at[s], t),
                                  sem.at[s, 1]).start(priority=1)
            return carry

        lax.fori_loop(0, tm, issue, 0, unroll=8)

    @pl.when(g == 0)
    def _():
        gather(d1_ref, d2_ref, 0)

    @pl.when(g + 1 < pl.num_programs(0))
    def _():
        gather(n1_ref, n2_ref, 1 - slot)

    pltpu.make_async_copy(ys_hbm.at[pl.ds(0, tm * k)], b1_ref.at[slot], sem.at[slot, 0]).wait()
    pltpu.make_async_copy(ys_hbm.at[pl.ds(0, tm * k)], b2_ref.at[slot], sem.at[slot, 1]).wait()
    route = route_ref[0]
    moe = (route[:, 2:3] * _load_token_tiles(b1_ref, (slot,), tm, k)
           + route[:, 3:4] * _load_token_tiles(b2_ref, (slot,), tm, k))
    h = h_ref[0] + mod_ref[0, 5:6, :] * moe
    if final_norm:
        h = _rms(h) * nf_ref[...]
    o_ref[0] = h


def _moe_combine(ys, dest1, dest2, h, route, mod, nf, final_norm):
    bsz, seq, d = h.shape
    tm = MOE_COMBINE_TILE
    nt = seq // tm
    steps = bsz * nt
    dspec = pl.BlockSpec((1, 1, tm), lambda g: (g, 0, 0), memory_space=pltpu.SMEM)
    nspec = pl.BlockSpec((1, 1, tm), lambda g: (jnp.minimum(g + 1, steps - 1), 0, 0), memory_space=pltpu.SMEM)
    row = lambda n: pl.BlockSpec((1, tm, n), lambda g: (g // nt, g % nt, 0))
    d1 = dest1.reshape(steps, 1, tm)
    d2 = dest2.reshape(steps, 1, tm)
    return pl.pallas_call(
        functools.partial(_moe_combine_kernel, final_norm=final_norm),
        grid=(steps,),
        in_specs=[dspec, dspec, nspec, nspec, row(d), row(LANES),
                  pl.BlockSpec((1, N_MOD, d), lambda g: (g // nt, 0, 0)),
                  pl.BlockSpec(nf.shape, lambda g: (0, 0)),
                  pl.BlockSpec(memory_space=pl.ANY)],
        out_specs=row(d),
        out_shape=jax.ShapeDtypeStruct((bsz, seq, d), F32),
        scratch_shapes=[pltpu.VMEM((2, tm * (d // LANES), LANES), ys.dtype),
                        pltpu.VMEM((2, tm * (d // LANES), LANES), ys.dtype),
                        pltpu.SemaphoreType.DMA((2, 2))],
        compiler_params=pltpu.CompilerParams(
            dimension_semantics=("arbitrary",), vmem_limit_bytes=VMEM_LIMIT),
        name="moe_combine",
    )(d1, d2, d1, d2, h, route, mod, nf, ys)


def _moe_sparse(m, route, counts, wg, wu, wd, h, mod, nf, final_norm):
    bsz, seq, d = h.shape
    n_tok = bsz * seq
    tmr = MOE_ROWS
    n_exp = wg.shape[0]
    n_tiles = (2 * n_tok) // tmr + n_exp
    i32 = jnp.int32
    r = route.reshape(n_tok, LANES)
    e1, e2, k1, k2 = (r[:, j].astype(i32) for j in (0, 1, 4, 5))
    cnt = counts[-1, 0, ROUTE_OFF:ROUTE_OFF + n_exp].astype(i32)
    ptiles = (cnt + tmr - 1) // tmr
    tend = jnp.cumsum(ptiles)
    gstart = (tend - ptiles) * tmr
    eids = jnp.arange(n_exp, dtype=i32)
    dest1 = k1 + jnp.sum(jnp.where(e1[:, None] == eids[None, :], gstart[None, :], 0), axis=1)
    dest2 = k2 + jnp.sum(jnp.where(e2[:, None] == eids[None, :], gstart[None, :], 0), axis=1)
    n_active = tend[-1:]
    tile_expert = jnp.sum(jnp.arange(n_tiles, dtype=i32)[:, None] >= tend[None, :], axis=1).astype(i32)
    last_expert = jnp.sum(n_active - 1 >= tend).astype(i32)
    tile_expert = jnp.minimum(tile_expert, last_expert)
    pads = jnp.stack([jnp.concatenate([gstart + cnt, n_active]),
                      jnp.concatenate([gstart + ptiles * tmr, jnp.full((1,), n_tiles, i32)])]).astype(i32)
    k = d // LANES
    xs = _moe_scatter(m.reshape(n_tok * k, LANES), dest1, dest2, pads, n_tiles * tmr, k)
    ys = _moe_ffn(xs, tile_expert, n_active.astype(i32), wg, wu, wd)
    return _moe_combine(ys, dest1, dest2, h, route, mod, nf, final_norm)


def kernel(x, c, norm1_g, norm2_g, w_ada, b_ada, w_in, lam_re, lam_im, log_step, s5_b_re, s5_b_im,
           s5_c_re, s5_c_im, s5_d, w_glu, b_glu, conv_w, a_log, dt_bias, gdn_norm_g, w_out,
           w_router_grp, w_router_exp, w_gate, w_up, w_down, normf_g):
    bsz, seq, d = x.shape
    depth = w_ada.shape[0]
    d_s5 = s5_d.shape[1]
    d_gdn = w_out.shape[1] - d_s5
    n_heads = d_gdn // GDN_HEAD_DIM
    assert seq % S5_TILE == 0 or seq < S5_TILE
    assert seq % ROW_TILE == 0 and seq % GDN_CHUNK == 0 and d_s5 % LANES == 0
    assert 2 * n_heads <= LANES and N_EXPERT_GROUPS + N_EXPERTS <= LANES

    h = x
    for l in range(depth):
        mod = _ada(c, w_ada[l], b_ada[l]).reshape(bsz, N_MOD, d)
        wi = w_in[l]
        c0, c1, c2 = d_s5, d_s5 + 3 * d_gdn, d_s5 + 4 * d_gdn
        wab = jnp.zeros((d, LANES), F32).at[:, :2 * n_heads].set(wi[:, c2:c2 + 2 * n_heads])
        u, qkv, z, ab = _inproj(h, mod, norm1_g[l].reshape(1, d), wi[:, :c0].astype(BF16),
                                wi[:, c0:c1].astype(BF16), wi[:, c1:c2].astype(BF16), wab.astype(BF16))
        tabs = _s5_tables(lam_re[l], lam_im[l], log_step[l], s5_b_re[l], s5_b_im[l],
                          s5_c_re[l], s5_c_im[l])
        ys = _s5(u, *tabs, s5_d[l])
        yg = _gdn2(qkv, z, ab, conv_w[l], a_log[l], dt_bias[l], gdn_norm_g[l])
        wr = jnp.zeros((d, LANES), F32)
        wr = wr.at[:, :N_EXPERT_GROUPS].set(w_router_grp[l])
        wr = wr.at[:, ROUTE_OFF:ROUTE_OFF + N_EXPERTS].set(w_router_exp[l])
        wr_hi = wr.astype(BF16)
        wr = jnp.concatenate([wr_hi, (wr - wr_hi.astype(F32)).astype(BF16)], axis=1)
        h1, m, route, counts = _outproj(ys, yg, h, mod, w_glu[l].astype(BF16), b_glu[l].reshape(1, d_s5),
                                        w_out[l, :d_s5].astype(BF16), w_out[l, d_s5:].astype(BF16),
                                        norm2_g[l].reshape(1, d), wr)
        h = _moe_sparse(m, route, counts, w_gate[l], w_up[l], w_down[l], h1, mod,
                        normf_g.reshape(1, d), final_norm=(l == depth - 1))
    return h
```

```python
import functools
import math

import jax
import jax.numpy as jnp
from jax import lax
from jax.experimental import pallas as pl
from jax.experimental.pallas import tpu as pltpu

F32 = jnp.float32
BF16 = jnp.bfloat16
HIGHEST = lax.Precision.HIGHEST
EPS = 1e-6

S5_GROUP = 16
S5_STATE = 64
GDN_HEAD_DIM = 128
CONV_WIDTH = 4
N_EXPERT_GROUPS = 4
EXPERTS_PER_GROUP = 8
N_EXPERTS = N_EXPERT_GROUPS * EXPERTS_PER_GROUP
N_MOD = 6

LANES = 128
SUBLANES = 8
VMEM_LIMIT = 56 * 1024 * 1024

S5_CHUNK = SUBLANES
S5_TILE = 2048
GDN_CHUNK = 64
GDN_BLOCK = 256
GDN_SIDE = 64
ROW_TILE = 512
MOE_TILE = 1024
MOE_ROWS = 256
MOE_SCATTER_TILE = 1024
MOE_COMBINE_TILE = 1024
ROUTE_OFF = N_EXPERT_GROUPS


def _dot(a, b, **kw):
    return jnp.dot(a, b, preferred_element_type=F32, **kw)


def _silu(v):
    return v * jax.nn.sigmoid(v)


def _rms(v):
    return v * lax.rsqrt(jnp.mean(v * v, axis=-1, keepdims=True) + EPS)


def _store_token_tiles(ref, idx, v):
    n, width = v.shape
    k = width // LANES
    for s in range(k):
        ref[idx + (pl.ds(s, n, stride=k), slice(None))] = v[:, s * LANES:(s + 1) * LANES]


def _load_token_tiles(ref, idx, n, k):
    return jnp.concatenate([ref[idx + (pl.ds(s, n, stride=k), slice(None))] for s in range(k)], axis=1)


def _ada_kernel(c_ref, w_ref, b_ref, o_ref):
    o_ref[...] = _dot(_silu(c_ref[...]), w_ref[...], precision=HIGHEST) + b_ref[...]


def _ada(c, w, b):
    bsz, d = c.shape
    n = w.shape[1]
    return pl.pallas_call(
        _ada_kernel,
        grid=(n // d,),
        in_specs=[pl.BlockSpec((bsz, d), lambda j: (0, 0)),
                  pl.BlockSpec((d, d), lambda j: (0, j)),
                  pl.BlockSpec((1, d), lambda j: (0, j))],
        out_specs=pl.BlockSpec((bsz, d), lambda j: (0, j)),
        out_shape=jax.ShapeDtypeStruct((bsz, n), F32),
        compiler_params=pltpu.CompilerParams(vmem_limit_bytes=VMEM_LIMIT),
        name="ada",
    )(c, w, b.reshape(1, n))


def _inproj_kernel(x_ref, mod_ref, g_ref, w_ref, u_ref, qkv_ref, z_ref, ab_ref,
                   wu_ref, wqkv_ref, wz_ref, wab_ref):
    @pl.when((pl.program_id(0) == 0) & (pl.program_id(1) == 0))
    def _():
        c0 = wu_ref.shape[1]
        c1 = c0 + wqkv_ref.shape[1]
        c2 = c1 + wz_ref.shape[1]
        n_gate = w_ref.shape[1] - c2
        wu_ref[...] = w_ref[:, 0:c0].astype(BF16)
        wqkv_ref[...] = w_ref[:, c0:c1].astype(BF16)
        wz_ref[...] = w_ref[:, c1:c2].astype(BF16)
        wab_ref[...] = jnp.zeros_like(wab_ref)
        wab_ref[:, 0:n_gate] = w_ref[:, c2:c2 + n_gate].astype(BF16)

    x = x_ref[0]
    y = _rms(x) * g_ref[...]
    h = (y * (1.0 + mod_ref[0, 1:2, :]) + mod_ref[0, 0:1, :]).astype(BF16)
    u_ref[0] = _dot(h, wu_ref[...])
    qkv_ref[0] = _dot(h, wqkv_ref[...]).astype(BF16)
    z_ref[0] = _dot(h, wz_ref[...]).astype(BF16)
    ab_ref[0] = _dot(h, wab_ref[...])


def _inproj(x, mod, g, w, n_u, n_qkv, n_z):
    bsz, seq, d = x.shape
    tm = ROW_TILE
    full = lambda a: pl.BlockSpec(a.shape, lambda b, i: (0,) * a.ndim)
    row = lambda n: pl.BlockSpec((1, tm, n), lambda b, i: (b, i, 0))
    return pl.pallas_call(
        _inproj_kernel,
        grid=(bsz, seq // tm),
        in_specs=[row(d), pl.BlockSpec((1, N_MOD, d), lambda b, i: (b, 0, 0)), full(g),
                  pl.BlockSpec(w.shape, lambda b, i: (0, 0), pipeline_mode=pl.Buffered(1))],
        out_specs=[row(n_u), row(n_qkv), row(n_z), row(LANES)],
        out_shape=[jax.ShapeDtypeStruct((bsz, seq, n_u), F32),
                   jax.ShapeDtypeStruct((bsz, seq, n_qkv), BF16),
                   jax.ShapeDtypeStruct((bsz, seq, n_z), BF16),
                   jax.ShapeDtypeStruct((bsz, seq, LANES), F32)],
        scratch_shapes=[pltpu.VMEM((d, n_u), BF16), pltpu.VMEM((d, n_qkv), BF16),
                        pltpu.VMEM((d, n_z), BF16), pltpu.VMEM((d, LANES), BF16)],
        compiler_params=pltpu.CompilerParams(
            dimension_semantics=("arbitrary", "arbitrary"), vmem_limit_bytes=VMEM_LIMIT),
        name="inproj",
    )(x, mod, g, w)


def _s5_tables(lam_re, lam_im, log_step, b_re, b_im, c_re, c_im):
    ch = S5_CHUNK
    n_grp, n_st = lam_re.shape
    gpb = LANES // S5_GROUP
    nblk = n_grp // gpb
    step = jnp.exp(log_step.astype(F32))[:, None]
    lr = lam_re.astype(F32)
    li = lam_im.astype(F32)

    def lam_pow(j):
        mag = jnp.exp(j * lr * step)
        ang = j * li * step
        return mag * jnp.cos(ang), mag * jnp.sin(ang)

    ar, ai = lam_pow(1.0)
    den = lr * lr + li * li
    fr = ((ar - 1.0) * lr + ai * li) / den
    fi = (ai * lr - (ar - 1.0) * li) / den
    bbr = fr[..., None] * b_re - fi[..., None] * b_im
    bbi = fr[..., None] * b_im + fi[..., None] * b_re
    pows = [lam_pow(float(j)) for j in range(ch + 1)]
    pr = jnp.stack([p[0] for p in pows])
    pi = jnp.stack([p[1] for p in pows])
    lbr = pr[:ch, :, :, None] * bbr[None] - pi[:ch, :, :, None] * bbi[None]
    lbi = pr[:ch, :, :, None] * bbi[None] + pi[:ch, :, :, None] * bbr[None]
    kdim = ch * LANES
    sdim = 2 * gpb * n_st

    def expand(small, col_grp, n_rows):
        n_cols = small.shape[-1]
        rq = jnp.arange(gpb)[:, None, None]
        cq = ((jnp.arange(n_cols) // col_grp) % gpb)[None, None, :]
        full = jnp.where(rq == cq, small[:, :, None, :, :], 0.0).astype(BF16)
        return full.reshape(nblk, n_rows, n_cols)

    kmat = (jnp.einsum('ghp,jgpk->jghk', c_re, lbr, precision=HIGHEST)
            - jnp.einsum('ghp,jgpk->jghk', c_im, lbi, precision=HIGHEST))
    s_idx = jnp.arange(ch)[:, None]
    t_idx = jnp.arange(ch)[None, :]
    lag = jnp.clip(t_idx - s_idx, 0, ch - 1)
    kst = jnp.where((t_idx >= s_idx)[..., None, None, None], kmat[lag], 0.0)
    kst = kst.reshape(ch, ch, nblk, gpb, S5_GROUP, S5_GROUP)
    t_small = kst.transpose(2, 0, 5, 1, 3, 4).reshape(nblk, ch, S5_GROUP, kdim)
    tmat = expand(t_small, S5_GROUP, kdim)

    wri = jnp.stack([lbr[::-1], lbi[::-1]]).reshape(2, ch, nblk, gpb, n_st, S5_GROUP)
    w_small = wri.transpose(2, 1, 5, 0, 3, 4).reshape(nblk, ch, S5_GROUP, sdim)
    win = expand(w_small, n_st, kdim)

    pr1 = pr[1:, :, None, :]
    pi1 = pi[1:, :, None, :]
    clr = c_re[None] * pr1 - c_im[None] * pi1
    cli = c_re[None] * pi1 + c_im[None] * pr1
    wo = jnp.stack([clr, -cli]).reshape(2, ch, nblk, gpb, S5_GROUP, n_st)
    o_small = wo.transpose(2, 0, 5, 1, 3, 4).reshape(nblk, 2, n_st, kdim)
    wout = expand(o_small, S5_GROUP, sdim)

    half = gpb * n_st
    amat = jnp.stack([pr[ch].reshape(nblk, half), pi[ch].reshape(nblk, half)], axis=1)
    return tmat, win, wout, amat


def _s5_kernel(u_ref, t_ref, win_ref, wout_ref, a_ref, d_ref, y_ref, st_ref, v_ref, xp_ref):
    ch = S5_CHUNK
    n = u_ref.shape[1] // ch
    half = st_ref.shape[1]

    @pl.when(pl.program_id(2) == 0)
    def _():
        st_ref[...] = jnp.zeros_like(st_ref)

    slabs = [u_ref[0, pl.ds(s, n, stride=ch), :] for s in range(ch)]
    ucat = jnp.concatenate(slabs, axis=1).astype(BF16)
    y = _dot(ucat, t_ref[0])
    v_ref[...] = _dot(ucat, win_ref[0])
    a_r = a_ref[0, 0:1, :]
    a_i = a_ref[0, 1:2, :]

    def body(r, carry):
        x_r, x_i = carry
        xp_ref[pl.ds(r, 1), 0:half] = x_r
        xp_ref[pl.ds(r, 1), half:2 * half] = x_i
        v_r = v_ref[pl.ds(r, 1), 0:half]
        v_i = v_ref[pl.ds(r, 1), half:2 * half]
        return a_r * x_r - a_i * x_i + v_r, a_r * x_i + a_i * x_r + v_i

    x_r, x_i = lax.fori_loop(0, n, body, (st_ref[0:1, :], st_ref[1:2, :]), unroll=4)
    st_ref[0:1, :] = x_r
    st_ref[1:2, :] = x_i
    y = y + _dot(xp_ref[...].astype(BF16), wout_ref[0])
    d = d_ref[0]
    for t in range(ch):
        y_ref[0, pl.ds(t, n, stride=ch), :] = y[:, t * LANES:(t + 1) * LANES] + d * slabs[t]


def _s5(u, tmat, win, wout, amat, d_skip):
    bsz, seq, dch = u.shape
    nblk = dch // LANES
    tt = min(S5_TILE, seq)
    n = tt // S5_CHUNK
    kdim = S5_CHUNK * LANES
    sdim = win.shape[2]
    wspec = lambda a: pl.BlockSpec((1,) + a.shape[1:], lambda b, c, i: (c, 0, 0))
    return pl.pallas_call(
        _s5_kernel,
        grid=(bsz, nblk, seq // tt),
        in_specs=[pl.BlockSpec((1, tt, LANES), lambda b, c, i: (b, i, c)),
                  wspec(tmat), wspec(win), wspec(wout), wspec(amat),
                  pl.BlockSpec((1, 1, LANES), lambda b, c, i: (c, 0, 0))],
        out_specs=pl.BlockSpec((1, tt, LANES), lambda b, c, i: (b, i, c)),
        out_shape=jax.ShapeDtypeStruct((bsz, seq, dch), F32),
        scratch_shapes=[pltpu.VMEM((2, sdim // 2), F32),
                        pltpu.VMEM((n, sdim), F32),
                        pltpu.VMEM((n, sdim), F32)],
        compiler_params=pltpu.CompilerParams(
            dimension_semantics=("parallel", "parallel", "arbitrary"),
            vmem_limit_bytes=VMEM_LIMIT),
        name="s5",
    )(u, tmat, win, wout, amat, d_skip.reshape(nblk, 1, LANES))


def _cumsum_rows(v):
    n = v.shape[0]
    row = lax.broadcasted_iota(jnp.int32, v.shape, 0)
    sh = 1
    while sh < n:
        v = v + jnp.where(row >= sh, pltpu.roll(v, sh, axis=0), 0.0)
        sh *= 2
    return v


def _unit_lower_inverse(lm):
    n = lm.shape[0]
    ri = lax.broadcasted_iota(jnp.int32, (n, n), 0)
    ci = lax.broadcasted_iota(jnp.int32, (n, n), 1)
    eye = jnp.where(ri == ci, 1.0, 0.0).astype(F32)
    m = 1
    d = eye
    while m < n:
        mask = ((ri // (2 * m)) == (ci // (2 * m))) & (((ri // m) % 2) == 1) & (((ci // m) % 2) == 0)
        e = jnp.where(mask, lm, 0.0)
        if m == 1:
            d = eye - e
        else:
            de = _dot(d.astype(BF16), e.astype(BF16))
            d = d - _dot(de.astype(BF16), d.astype(BF16))
        m *= 2
    return d


def _gdn_kernel(qkv_ref, z_ref, ab_ref, cw_ref, gp_ref, ng_ref, o_ref, s_ref, xp_ref, *, n_heads):
    ck = GDN_CHUNK
    dh = GDN_HEAD_DIM
    bsz = qkv_ref.shape[0]
    dg = n_heads * dh
    halo = SUBLANES

    @pl.when(pl.program_id(0) == 0)
    def _():
        s_ref[...] = jnp.zeros_like(s_ref)
        xp_ref[:, 0:halo, :] = jnp.zeros((bsz, halo, 3 * dg), F32)

    ri = lax.broadcasted_iota(jnp.int32, (ck, ck), 0)
    ci = lax.broadcasted_iota(jnp.int32, (ck, ck), 1)
    causal = ri >= ci
    strict = ri > ci
    lane = lax.broadcasted_iota(jnp.int32, (ck, LANES), 1)
    a_log = gp_ref[0:1, :]
    dt_bias = gp_ref[1:2, :]
    ng = ng_ref[...]

    for b in range(bsz):
        xp_ref[b, halo:halo + ck, :] = qkv_ref[b].astype(F32)
        conv = cw_ref[0:1, :] * xp_ref[b, halo - 3:halo - 3 + ck, :]
        for j in range(1, CONV_WIDTH):
            conv = conv + cw_ref[j:j + 1, :] * xp_ref[b, halo - 3 + j:halo - 3 + j + ck, :]
        xp_ref[b, 0:halo, :] = xp_ref[b, ck:ck + halo, :]
        act = _silu(conv)

        ab = ab_ref[b]
        sp = jnp.maximum(ab + dt_bias, 0.0) + jnp.log1p(jnp.exp(-jnp.abs(ab + dt_bias)))
        g = jnp.where(lane < n_heads, -jnp.exp(a_log) * sp, 0.0)
        gc = _cumsum_rows(g)
        gct = gc.T
        beta_all = jax.nn.sigmoid(ab)
        zb = z_ref[b].astype(F32)

        for h in range(n_heads):
            q = act[:, h * dh:(h + 1) * dh]
            k = act[:, dg + h * dh:dg + (h + 1) * dh]
            v = act[:, 2 * dg + h * dh:2 * dg + (h + 1) * dh]
            q = q * lax.rsqrt(jnp.sum(q * q, axis=-1, keepdims=True) + EPS) * (dh ** -0.5)
            k = k * lax.rsqrt(jnp.sum(k * k, axis=-1, keepdims=True) + EPS)
            beta = beta_all[:, n_heads + h:n_heads + h + 1]
            gcol = gc[:, h:h + 1]
            grow = gct[h:h + 1, :]
            glast = gc[ck - 1:ck, h:h + 1]
            decay = jnp.exp(jnp.where(causal, gcol - grow, -1e30))
            kb = k * beta
            kq = jnp.concatenate([kb, q], axis=0).astype(BF16)
            m1 = lax.dot_general(kq, k.astype(BF16), (((1,), (1,)), ((), ())),
                                 preferred_element_type=F32)
            lm = jnp.where(strict, m1[0:ck] * decay, 0.0)
            qk = m1[ck:2 * ck] * decay
            tinv = _unit_lower_inverse(lm)
            eg = jnp.exp(gcol)
            rhs = jnp.concatenate([v * beta, kb * eg], axis=1).astype(BF16)
            w = _dot(tinv.astype(BF16), rhs)
            value = w[:, 0:dh]
            kcd = w[:, dh:2 * dh]
            qd = q * eg
            kt = k * jnp.exp(glast - gcol)
            st = s_ref[b * n_heads + h]
            m2 = _dot(jnp.concatenate([kcd, qd], axis=0).astype(BF16), st.astype(BF16))
            v_new = value - m2[0:ck]
            vnb = v_new.astype(BF16)
            o = m2[ck:2 * ck] + _dot(qk.astype(BF16), vnb)
            s_ref[b * n_heads + h] = st * jnp.exp(glast) + lax.dot_general(
                kt.astype(BF16), vnb, (((0,), (0,)), ((), ())), preferred_element_type=F32)
            zh = zb[:, h * dh:(h + 1) * dh]
            o_ref[b, :, h * dh:(h + 1) * dh] = (_rms(o) * ng * _silu(zh)).astype(o_ref.dtype)


def _gdn(qkv, z, ab, conv_w, a_log, dt_bias, norm_g):
    bsz, seq, _ = qkv.shape
    dg = z.shape[2]
    n_heads = dg // GDN_HEAD_DIM
    ck = GDN_CHUNK
    gp = jnp.zeros((2, LANES), F32).at[0, :n_heads].set(a_log).at[1, :n_heads].set(dt_bias)
    blk = lambda n: pl.BlockSpec((bsz, ck, n), lambda i: (0, i, 0))
    full = lambda a: pl.BlockSpec(a.shape, lambda i: (0,) * a.ndim)
    ng = norm_g.reshape(1, GDN_HEAD_DIM)
    return pl.pallas_call(
        functools.partial(_gdn_kernel, n_heads=n_heads),
        grid=(seq // ck,),
        in_specs=[blk(3 * dg), blk(dg), blk(LANES), full(conv_w), full(gp), full(ng)],
        out_specs=blk(dg),
        out_shape=jax.ShapeDtypeStruct((bsz, seq, dg), BF16),
        scratch_shapes=[pltpu.VMEM((bsz * n_heads, GDN_HEAD_DIM, GDN_HEAD_DIM), F32),
                        pltpu.VMEM((bsz, ck + SUBLANES, 3 * dg), F32)],
        compiler_params=pltpu.CompilerParams(
            dimension_semantics=("arbitrary",), vmem_limit_bytes=VMEM_LIMIT),
        name="gdn",
    )(qkv, z, ab, conv_w, gp, ng)


def _gdn2_kernel(qkv_ref, z_ref, ab_ref, cw_ref, gp_ref, ng_ref, o_ref,
                 s_ref, xp_ref, gc_ref, gct_ref, kbq_ref, kn_ref, rhs_ref, cq_ref, qkt_ref,
                 lm_ref, d_ref, db_ref, de_ref, ds_ref, pe_ref, val_ref, vn_ref, *, n_heads):
    tt = qkv_ref.shape[1]
    dh = GDN_HEAD_DIM
    bsz = qkv_ref.shape[0]
    dg = n_heads * dh
    halo = SUBLANES
    sb = ds_ref.shape[1]
    nsb = tt // sb
    pairs = [(b, h) for b in range(bsz) for h in range(n_heads)]

    @pl.when(pl.program_id(0) == 0)
    def _():
        s_ref[...] = jnp.zeros_like(s_ref)
        xp_ref[...] = jnp.zeros_like(xp_ref)

    ri = lax.broadcasted_iota(jnp.int32, (tt, tt), 0)
    ci = lax.broadcasted_iota(jnp.int32, (tt, tt), 1)
    lane = lax.broadcasted_iota(jnp.int32, (tt, LANES), 1)
    a_log = gp_ref[0:1, :]
    dt_bias = gp_ref[1:2, :]
    shift_op = jnp.concatenate(
        [jnp.where(ri == ci + j, 1.0, 0.0) for j in range(1, CONV_WIDTH)], axis=0).astype(BF16)

    for b in range(bsz):
        x = qkv_ref[b]
        x32 = x.astype(F32)
        shifted = _dot(shift_op, x)
        edge = jnp.concatenate([xp_ref[b], x32[0:halo]], axis=0)
        conv = cw_ref[CONV_WIDTH - 1:CONV_WIDTH, :] * x32
        for j in range(1, CONV_WIDTH):
            sh_j = jnp.concatenate([edge[halo - j:2 * halo - j],
                                    shifted[(j - 1) * tt + halo:j * tt]], axis=0)
            conv = conv + cw_ref[CONV_WIDTH - 1 - j:CONV_WIDTH - j, :] * sh_j
        xp_ref[b] = x32[tt - halo:tt]
        act = _silu(conv)

        ab = ab_ref[b]
        sp = jnp.maximum(ab + dt_bias, 0.0) + jnp.log1p(jnp.exp(-jnp.abs(ab + dt_bias)))
        g = jnp.where(lane < n_heads, -jnp.exp(a_log) * sp, 0.0)
        gc = _cumsum_rows(g)
        gc_ref[b] = gc
        gct_ref[b] = gc.T
        glast = gc[tt - 1:tt, :]
        egc = jnp.exp(gc)
        ekt = jnp.exp(glast - gc)
        beta_all = jax.nn.sigmoid(ab)
        for h in range(n_heads):
            p = b * n_heads + h
            q = act[:, h * dh:(h + 1) * dh]
            k = act[:, dg + h * dh:dg + (h + 1) * dh]
            v = act[:, 2 * dg + h * dh:2 * dg + (h + 1) * dh]
            q = q * lax.rsqrt(jnp.sum(q * q, axis=-1, keepdims=True) + EPS) * (dh ** -0.5)
            k = k * lax.rsqrt(jnp.sum(k * k, axis=-1, keepdims=True) + EPS)
            beta = beta_all[:, n_heads + h:n_heads + h + 1]
            eg = egc[:, h:h + 1]
            kb = k * beta
            kbq_ref[p, 0:tt, :] = kb.astype(BF16)
            kbq_ref[p, tt:2 * tt, :] = q.astype(BF16)
            kn_ref[p] = k.astype(BF16)
            rhs_ref[p, :, 0:dh] = (v * beta).astype(BF16)
            rhs_ref[p, :, dh:2 * dh] = (kb * eg).astype(BF16)
            cq_ref[p, tt:2 * tt, :] = (q * eg).astype(BF16)
            qkt_ref[p, tt:tt + dh, :] = (k * ekt[:, h:h + 1]).T.astype(BF16)

    for b, h in pairs:
        p = b * n_heads + h
        m1 = lax.dot_general(kbq_ref[p], kn_ref[p], (((1,), (1,)), ((), ())),
                             preferred_element_type=F32)
        gcol = gc_ref[b, :, h:h + 1]
        grow = gct_ref[b, h:h + 1, :]
        decay = jnp.exp(jnp.where(ri >= ci, gcol - grow, -1e30))
        lm = jnp.where(ri > ci, m1[0:tt] * decay, 0.0)
        lm_ref[p] = lm.astype(BF16)
        qkt_ref[p, 0:tt, :] = (m1[tt:2 * tt] * decay).astype(BF16)
        first = jnp.where(ri == ci + 1, jnp.where((ci & 1) == 0, lm, 0.0), 0.0)
        d = jnp.where(ri == ci, 1.0, 0.0) - first
        ds_ref[p] = sum(d[j * sb:(j + 1) * sb] for j in range(1, nsb)) + d[0:sb]

    in_blk = (ri // sb) == (ci // sb)
    blk_b = jnp.where(in_blk, 1.0, 0.0).astype(BF16)
    m = 2
    sh = 1
    while m < tt:
        rb = ri >> sh
        cb = ci >> sh
        sel = jnp.where(rb == cb + 1, jnp.where((cb & 1) == 0, 1.0, 0.0), 0.0).astype(BF16)
        if 2 * m <= sb:
            for b, h in pairs:
                p = b * n_heads + h
                pe_ref[p] = _dot(ds_ref[p].astype(BF16), lm_ref[p] * sel).astype(BF16)
            for b, h in pairs:
                p = b * n_heads + h
                ds = ds_ref[p]
                dbd = jnp.concatenate([ds.astype(BF16)] * nsb, axis=0) * blk_b
                ds = ds - _dot(pe_ref[p], dbd)
                ds_ref[p] = ds
                if 4 * m > sb:
                    d = jnp.where(in_blk, jnp.concatenate([ds] * nsb, axis=0), 0.0)
                    d_ref[p] = d
                    db_ref[p] = d.astype(BF16)
        else:
            for b, h in pairs:
                p = b * n_heads + h
                de_ref[p] = _dot(db_ref[p], lm_ref[p] * sel).astype(BF16)
            for b, h in pairs:
                p = b * n_heads + h
                d = d_ref[p] - _dot(de_ref[p], db_ref[p])
                d_ref[p] = d
                db_ref[p] = d.astype(BF16)
        m *= 2
        sh += 1

    for b, h in pairs:
        p = b * n_heads + h
        w = _dot(db_ref[p], rhs_ref[p])
        val_ref[p] = w[:, 0:dh]
        cq_ref[p, 0:tt, :] = w[:, dh:2 * dh].astype(BF16)

    for b, h in pairs:
        p = b * n_heads + h
        m2 = _dot(cq_ref[p], s_ref[p].astype(BF16))
        vn_ref[p] = (val_ref[p] - m2[0:tt]).astype(BF16)
        val_ref[p] = m2[tt:2 * tt]
    for b, h in pairs:
        p = b * n_heads + h
        r = _dot(qkt_ref[p], vn_ref[p])
        o = val_ref[p] + r[0:tt]
        egl = jnp.exp(gc_ref[b, tt - 1:tt, h:h + 1])
        s_ref[p] = s_ref[p] * egl + r[tt:tt + dh]
        zh = z_ref[b, :, h * dh:(h + 1) * dh].astype(F32)
        o_ref[b, :, h * dh:(h + 1) * dh] = (_rms(o) * ng_ref[...] * _silu(zh)).astype(o_ref.dtype)


def _gdn2(qkv, z, ab, conv_w, a_log, dt_bias, norm_g):
    bsz, seq, _ = qkv.shape
    dg = z.shape[2]
    dh = GDN_HEAD_DIM
    n_heads = dg // dh
    tt = GDN_BLOCK
    npair = bsz * n_heads
    gp = jnp.zeros((2, LANES), F32).at[0, :n_heads].set(a_log).at[1, :n_heads].set(dt_bias)
    blk = lambda n: pl.BlockSpec((bsz, tt, n), lambda i: (0, i, 0))
    full = lambda a: pl.BlockSpec(a.shape, lambda i: (0,) * a.ndim)
    ng = norm_g.reshape(1, dh)
    return pl.pallas_call(
        functools.partial(_gdn2_kernel, n_heads=n_heads),
        grid=(seq // tt,),
        in_specs=[blk(3 * dg), blk(dg), blk(LANES), full(conv_w), full(gp), full(ng)],
        out_specs=blk(dg),
        out_shape=jax.ShapeDtypeStruct((bsz, seq, dg), BF16),
        scratch_shapes=[pltpu.VMEM((npair, dh, dh), F32),
                        pltpu.VMEM((bsz, SUBLANES, 3 * dg), F32),
                        pltpu.VMEM((bsz, tt, LANES), F32),
                        pltpu.VMEM((bsz, LANES, tt), F32),
                        pltpu.VMEM((npair, 2 * tt, dh), BF16),
                        pltpu.VMEM((npair, tt, dh), BF16),
                        pltpu.VMEM((npair, tt, 2 * dh), BF16),
                        pltpu.VMEM((npair, 2 * tt, dh), BF16),
                        pltpu.VMEM((npair, tt + dh, tt), BF16),
                        pltpu.VMEM((npair, tt, tt), BF16),
                        pltpu.VMEM((npair, tt, tt), F32),
                        pltpu.VMEM((npair, tt, tt), BF16),
                        pltpu.VMEM((npair, tt, tt), BF16),
                        pltpu.VMEM((npair, GDN_SIDE, tt), F32),
                        pltpu.VMEM((npair, GDN_SIDE, tt), BF16),
                        pltpu.VMEM((npair, tt, dh), F32),
                        pltpu.VMEM((npair, tt, dh), BF16)],
        compiler_params=pltpu.CompilerParams(
            dimension_semantics=("arbitrary",), vmem_limit_bytes=VMEM_LIMIT),
        name="gdn",
    )(qkv, z, ab, conv_w, gp, ng)


def _outproj_kernel(ys_ref, yg_ref, x_ref, mod_ref, wglu_ref, bglu_ref, wo_ref,
                    g2_ref, wr_ref, h_ref, m_ref, route_ref, count_ref, cnt_ref, wglub_ref, wob_ref):
    @pl.when((pl.program_id(0) == 0) & (pl.program_id(1) == 0))
    def _():
        wglub_ref[...] = wglu_ref[...].astype(BF16)
        wob_ref[...] = wo_ref[...].astype(BF16)

    d_s5 = wglub_ref.shape[0]
    y = jax.nn.gelu(ys_ref[0])
    gate = jax.nn.sigmoid(_dot(y.astype(BF16), wglub_ref[...]) + bglu_ref[...])
    y = (y * gate).astype(BF16)
    mix = _dot(y, wob_ref[0:d_s5, :]) + _dot(yg_ref[0], wob_ref[d_s5:, :])
    h = x_ref[0] + mod_ref[0, 2:3, :] * mix
    h_ref[0] = h
    m = _rms(h) * g2_ref[...] * (1.0 + mod_ref[0, 4:5, :]) + mod_ref[0, 3:4, :]
    _store_token_tiles(m_ref, (0,), m)

    m_hi = m.astype(BF16)
    m_lo = (m - m_hi.astype(F32)).astype(BF16)
    part = _dot(m_hi, wr_ref[...])
    logits = part[:, 0:LANES] + part[:, LANES:2 * LANES] + _dot(m_lo, wr_ref[:, 0:LANES])
    lane = lax.broadcasted_iota(jnp.int32, logits.shape, 1)
    neg = -1e30
    is_grp = lane < N_EXPERT_GROUPS
    gl = jnp.where(is_grp, logits, neg)
    gmax = jnp.max(gl, axis=-1, keepdims=True)
    gidx = jnp.min(jnp.where(gl == gmax, lane, LANES), axis=-1, keepdims=True)
    p_grp = 1.0 / jnp.sum(jnp.where(is_grp, jnp.exp(gl - gmax), 0.0), axis=-1, keepdims=True)
    lo = ROUTE_OFF + gidx * EXPERTS_PER_GROUP
    el = jnp.where((lane >= lo) & (lane < lo + EXPERTS_PER_GROUP), logits, neg)
    v1 = jnp.max(el, axis=-1, keepdims=True)
    i1 = jnp.min(jnp.where(el == v1, lane, LANES), axis=-1, keepdims=True)
    el2 = jnp.where(lane == i1, neg, el)
    v2 = jnp.max(el2, axis=-1, keepdims=True)
    i2 = jnp.min(jnp.where(el2 == v2, lane, LANES), axis=-1, keepdims=True)
    t = jnp.exp(v2 - v1)
    w1 = p_grp / (1.0 + t)
    w2 = w1 * t

    @pl.when((pl.program_id(0) == 0) & (pl.program_id(1) == 0))
    def _():
        cnt_ref[...] = jnp.zeros_like(cnt_ref)

    tm = logits.shape[0]
    oh1 = lane == i1
    oh2 = lane == i2
    cnt = jnp.where(oh1, 1.0, 0.0) + jnp.where(oh2, 1.0, 0.0)
    earlier = (lax.broadcasted_iota(jnp.int32, (tm, tm), 0) > lax.broadcasted_iota(jnp.int32, (tm, tm), 1))
    before = _dot(jnp.where(earlier, 1.0, 0.0).astype(BF16), cnt.astype(BF16)) + cnt_ref[...]
    r1 = jnp.sum(jnp.where(oh1, before, 0.0), axis=-1, keepdims=True)
    r2 = jnp.sum(jnp.where(oh2, before, 0.0), axis=-1, keepdims=True)
    cnt_ref[...] += jnp.sum(cnt, axis=0, keepdims=True)
    count_ref[0] = cnt_ref[...]
    cols = [(i1 - ROUTE_OFF).astype(F32), (i2 - ROUTE_OFF).astype(F32), w1, w2, r1, r2]
    route = jnp.zeros_like(logits)
    for j, col in enumerate(cols):
        route = jnp.where(lane == j, col, route)
    route_ref[0] = route


def _outproj(ys, yg, x, mod, wglu, bglu, wo, g2, wr):
    bsz, seq, d = x.shape
    tm = ROW_TILE
    full = lambda a: pl.BlockSpec(a.shape, lambda b, i: (0,) * a.ndim, pipeline_mode=pl.Buffered(1))
    row = lambda n: pl.BlockSpec((1, tm, n), lambda b, i: (b, i, 0))
    return pl.pallas_call(
        _outproj_kernel,
        grid=(bsz, seq // tm),
        in_specs=[row(ys.shape[2]), row(yg.shape[2]), row(d),
                  pl.BlockSpec((1, N_MOD, d), lambda b, i: (b, 0, 0)),
                  full(wglu), full(bglu), full(wo), full(g2), full(wr)],
        out_specs=[row(d), pl.BlockSpec((1, tm * (d // LANES), LANES), lambda b, i: (b, i, 0)), row(LANES),
                   pl.BlockSpec((1, 1, LANES), lambda b, i: (b * (seq // tm) + i, 0, 0))],
        out_shape=[jax.ShapeDtypeStruct((bsz, seq, d), F32),
                   jax.ShapeDtypeStruct((bsz, seq * (d // LANES), LANES), F32),
                   jax.ShapeDtypeStruct((bsz, seq, LANES), F32),
                   jax.ShapeDtypeStruct((bsz * (seq // tm), 1, LANES), F32)],
        scratch_shapes=[pltpu.VMEM((1, LANES), F32), pltpu.VMEM(wglu.shape, BF16), pltpu.VMEM(wo.shape, BF16)],
        compiler_params=pltpu.CompilerParams(
            dimension_semantics=("arbitrary", "arbitrary"), vmem_limit_bytes=VMEM_LIMIT),
        name="outproj",
    )(ys, yg, x, mod, wglu, bglu, wo, g2, wr)


def _moe_kernel(m_ref, comb_ref, wg_ref, wu_ref, wd_ref, h_ref, mod_ref, nf_ref, o_ref, acc_ref,
                *, final_norm):
    e = pl.program_id(2)

    @pl.when(e == 0)
    def _():
        acc_ref[...] = jnp.zeros_like(acc_ref)

    m = m_ref[0]
    mid = (_silu(_dot(m, wg_ref[0])) * _dot(m, wu_ref[0])).astype(BF16)
    y = _dot(mid, wd_ref[0])
    comb = comb_ref[0]
    lane = lax.broadcasted_iota(jnp.int32, comb.shape, 1)
    col = jnp.sum(jnp.where(lane == e + ROUTE_OFF, comb, 0.0), axis=-1, keepdims=True)
    acc_ref[...] += col * y

    @pl.when(e == pl.num_programs(2) - 1)
    def _():
        h = h_ref[0] + mod_ref[0, 5:6, :] * acc_ref[...]
        if final_norm:
            h = _rms(h) * nf_ref[...]
        o_ref[0] = h


def _moe(m, comb, wg, wu, wd, h, mod, nf, final_norm):
    bsz, seq, d = h.shape
    tm = min(MOE_TILE, seq)
    n_exp, _, de = wg.shape
    row = lambda n: pl.BlockSpec((1, tm, n), lambda b, i, e: (b, i, 0))
    return pl.pallas_call(
        functools.partial(_moe_kernel, final_norm=final_norm),
        grid=(bsz, seq // tm, n_exp),
        in_specs=[row(d), row(LANES),
                  pl.BlockSpec((1, d, de), lambda b, i, e: (e, 0, 0)),
                  pl.BlockSpec((1, d, de), lambda b, i, e: (e, 0, 0)),
                  pl.BlockSpec((1, de, d), lambda b, i, e: (e, 0, 0)),
                  row(d), pl.BlockSpec((1, N_MOD, d), lambda b, i, e: (b, 0, 0)),
                  pl.BlockSpec(nf.shape, lambda b, i, e: (0, 0))],
        out_specs=row(d),
        out_shape=jax.ShapeDtypeStruct((bsz, seq, d), F32),
        scratch_shapes=[pltpu.VMEM((tm, d), F32)],
        compiler_params=pltpu.CompilerParams(
            dimension_semantics=("parallel", "parallel", "arbitrary"),
            vmem_limit_bytes=VMEM_LIMIT),
        name="moe",
    )(m, comb, wg, wu, wd, h, mod, nf)


def _moe_scatter_kernel(d1_ref, d2_ref, pad_ref, m_ref, xs_hbm, zero_ref, sem, *, n_exp, k):
    i = pl.program_id(0)
    ts = d1_ref.shape[2]
    trows = zero_ref.shape[0]

    def tok(ref, t):
        return ref.at[pl.ds(pl.multiple_of(t * k, k), k)]

    def tile_copy(j):
        return pltpu.make_async_copy(
            zero_ref, xs_hbm.at[pl.ds(pl.multiple_of(j * trows, trows), trows)], sem.at[0])

    def pad_piece(e, p, act):
        lo = pad_ref[0, e]
        length = pad_ref[1, e] - lo
        start = lo + (length & ~(2 * p - 1))
        copy = pltpu.make_async_copy(
            zero_ref.at[pl.ds(0, p * k)], xs_hbm.at[pl.ds(pl.multiple_of(start * k, k), p * k)], sem.at[0])

        @pl.when((length & p) != 0)
        def _():
            act(copy)

    @pl.when(i == 0)
    def _():
        zero_ref[...] = jnp.zeros_like(zero_ref)
        pieces = [(e, p) for e in range(n_exp) for p in [1 << j for j in range((trows // k).bit_length() - 1)]]
        for e, p in pieces:
            pad_piece(e, p, lambda c: c.start())
        for e, p in pieces:
            pad_piece(e, p, lambda c: c.wait())

        def fill(j, carry):
            tile_copy(j).start()
            return carry

        def drain(j, carry):
            tile_copy(j).wait()
            return carry

        lax.fori_loop(pad_ref[0, n_exp], pad_ref[1, n_exp], fill, 0)
        lax.fori_loop(pad_ref[0, n_exp], pad_ref[1, n_exp], drain, 0)

    def row_copy(t, dst, slot):
        return pltpu.make_async_copy(tok(m_ref, t), tok(xs_hbm, dst), sem.at[1 + slot])

    def issue(t, carry):
        row_copy(t, d1_ref[0, 0, t], 0).start(priority=0)
        row_copy(t, d2_ref[0, 0, t], 1).start(priority=1)
        return carry

    lax.fori_loop(0, ts, issue, 0, unroll=8)
    for slot in range(2):
        pltpu.make_async_copy(m_ref, xs_hbm.at[pl.ds(0, ts * k)], sem.at[1 + slot]).wait()


def _moe_scatter(m, dest1, dest2, pads, n_rows, k):
    n_tok = m.shape[0] // k
    ts = MOE_SCATTER_TILE
    n_exp = pads.shape[1] - 1
    dspec = pl.BlockSpec((1, 1, ts), lambda i: (i, 0, 0), memory_space=pltpu.SMEM)
    return pl.pallas_call(
        functools.partial(_moe_scatter_kernel, n_exp=n_exp, k=k),
        grid=(n_tok // ts,),
        in_specs=[dspec, dspec,
                  pl.BlockSpec(pads.shape, lambda i: (0, 0), memory_space=pltpu.SMEM),
                  pl.BlockSpec((ts * k, LANES), lambda i: (i, 0))],
        out_specs=pl.BlockSpec(memory_space=pl.ANY),
        out_shape=jax.ShapeDtypeStruct((n_rows * k, LANES), m.dtype),
        scratch_shapes=[pltpu.VMEM((MOE_ROWS * k, LANES), m.dtype), pltpu.SemaphoreType.DMA((3,))],
        compiler_params=pltpu.CompilerParams(
            dimension_semantics=("arbitrary",), vmem_limit_bytes=VMEM_LIMIT, has_side_effects=True),
        name="moe_scatter",
    )(dest1.reshape(n_tok // ts, 1, ts), dest2.reshape(n_tok // ts, 1, ts), pads, m)


def _moe_ffn_kernel(te_ref, na_ref, xs_ref, wg_ref, wu_ref, wd_ref, ys_ref, wgb_ref, wub_ref, wdb_ref):
    i = pl.program_id(0)
    active = i < na_ref[0]
    changed = (i == 0) | (te_ref[i] != te_ref[jnp.maximum(i - 1, 0)])

    @pl.when(active & changed)
    def _():
        wgb_ref[...] = wg_ref[0].astype(BF16)
        wub_ref[...] = wu_ref[0].astype(BF16)
        wdb_ref[...] = wd_ref[0].astype(BF16)

    @pl.when(active)
    def _():
        k = wgb_ref.shape[0] // LANES
        x = _load_token_tiles(xs_ref, (), xs_ref.shape[0] // k, k).astype(BF16)
        mid = (_silu(_dot(x, wgb_ref[...])) * _dot(x, wub_ref[...])).astype(BF16)
        _store_token_tiles(ys_ref, (), _dot(mid, wdb_ref[...]))


def _moe_ffn(xs, tile_expert, n_active, wg, wu, wd):
    _, d, de = wg.shape
    tmr = MOE_ROWS * (d // LANES)
    n_rows, dp = xs.shape
    grid_spec = pltpu.PrefetchScalarGridSpec(
        num_scalar_prefetch=2,
        grid=(n_rows // tmr,),
        in_specs=[pl.BlockSpec((tmr, dp), lambda i, te, na: (jnp.minimum(i, na[0] - 1), 0)),
                  pl.BlockSpec((1, d, de), lambda i, te, na: (te[i], 0, 0)),
                  pl.BlockSpec((1, d, de), lambda i, te, na: (te[i], 0, 0)),
                  pl.BlockSpec((1, de, d), lambda i, te, na: (te[i], 0, 0))],
        out_specs=pl.BlockSpec((tmr, dp), lambda i, te, na: (jnp.minimum(i, na[0] - 1), 0)),
        scratch_shapes=[pltpu.VMEM((d, de), BF16), pltpu.VMEM((d, de), BF16), pltpu.VMEM((de, d), BF16)],
    )
    return pl.pallas_call(
        _moe_ffn_kernel,
        grid_spec=grid_spec,
        out_shape=jax.ShapeDtypeStruct((n_rows, dp), F32),
        input_output_aliases={2: 0},
        compiler_params=pltpu.CompilerParams(
            dimension_semantics=("arbitrary",), vmem_limit_bytes=VMEM_LIMIT),
        name="moe_ffn",
    )(tile_expert, n_active, xs, wg, wu, wd)


def _moe_combine_kernel(d1_ref, d2_ref, n1_ref, n2_ref, h_ref, route_ref, mod_ref, nf_ref, ys_hbm, o_ref,
                        b1_ref, b2_ref, sem, *, final_norm):
    tm = h_ref.shape[1]
    k = h_ref.shape[2] // LANES
    g = pl.program_id(0)
    slot = g % 2

    def tok(ref, t):
        return ref.at[pl.ds(pl.multiple_of(t * k, k), k)]

    def gather(i1_ref, i2_ref, s):
        def issue(t, carry):
            pltpu.make_async_copy(tok(ys_hbm, i1_ref[0, 0, t]), tok(b1_ref.at[s], t),
                                  sem.at[s, 0]).start(priority=0)
            pltpu.make_async_copy(tok(ys_hbm, i2_ref[0, 0, t]), tok(b2_ref.at[s], t),
                                  sem.at[s, 1]).start(priority=1)
            return carry

        lax.fori_loop(0, tm, issue, 0, unroll=8)

    @pl.when(g == 0)
    def _():
        gather(d1_ref, d2_ref, 0)

    @pl.when(g + 1 < pl.num_programs(0))
    def _():
        gather(n1_ref, n2_ref, 1 - slot)

    pltpu.make_async_copy(ys_hbm.at[pl.ds(0, tm * k)], b1_ref.at[slot], sem.at[slot, 0]).wait()
    pltpu.make_async_copy(ys_hbm.at[pl.ds(0, tm * k)], b2_ref.at[slot], sem.at[slot, 1]).wait()
    route = route_ref[0]
    moe = (route[:, 2:3] * _load_token_tiles(b1_ref, (slot,), tm, k)
           + route[:, 3:4] * _load_token_tiles(b2_ref, (slot,), tm, k))
    h = h_ref[0] + mod_ref[0, 5:6, :] * moe
    if final_norm:
        h = _rms(h) * nf_ref[...]
    o_ref[0] = h


def _moe_combine(ys, dest1, dest2, h, route, mod, nf, final_norm):
    bsz, seq, d = h.shape
    tm = MOE_COMBINE_TILE
    nt = seq // tm
    steps = bsz * nt
    dspec = pl.BlockSpec((1, 1, tm), lambda g: (g, 0, 0), memory_space=pltpu.SMEM)
    nspec = pl.BlockSpec((1, 1, tm), lambda g: (jnp.minimum(g + 1, steps - 1), 0, 0), memory_space=pltpu.SMEM)
    row = lambda n: pl.BlockSpec((1, tm, n), lambda g: (g // nt, g % nt, 0))
    d1 = dest1.reshape(steps, 1, tm)
    d2 = dest2.reshape(steps, 1, tm)
    return pl.pallas_call(
        functools.partial(_moe_combine_kernel, final_norm=final_norm),
        grid=(steps,),
        in_specs=[dspec, dspec, nspec, nspec, row(d), row(LANES),
                  pl.BlockSpec((1, N_MOD, d), lambda g: (g // nt, 0, 0)),
                  pl.BlockSpec(nf.shape, lambda g: (0, 0)),
                  pl.BlockSpec(memory_space=pl.ANY)],
        out_specs=row(d),
        out_shape=jax.ShapeDtypeStruct((bsz, seq, d), F32),
        scratch_shapes=[pltpu.VMEM((2, tm * (d // LANES), LANES), ys.dtype),
                        pltpu.VMEM((2, tm * (d // LANES), LANES), ys.dtype),
                        pltpu.SemaphoreType.DMA((2, 2))],
        compiler_params=pltpu.CompilerParams(
            dimension_semantics=("arbitrary",), vmem_limit_bytes=VMEM_LIMIT),
        name="moe_combine",
    )(d1, d2, d1, d2, h, route, mod, nf, ys)


def _moe_sparse(m, route, counts, wg, wu, wd, h, mod, nf, final_norm):
    bsz, seq, d = h.shape
    n_tok = bsz * seq
    tmr = MOE_ROWS
    n_exp = wg.shape[0]
    n_tiles = (2 * n_tok) // tmr + n_exp
    i32 = jnp.int32
    r = route.reshape(n_tok, LANES)
    e1, e2, k1, k2 = (r[:, j].astype(i32) for j in (0, 1, 4, 5))
    cnt = counts[-1, 0, ROUTE_OFF:ROUTE_OFF + n_exp].astype(i32)
    ptiles = (cnt + tmr - 1) // tmr
    tend = jnp.cumsum(ptiles)
    gstart = (tend - ptiles) * tmr
    eids = jnp.arange(n_exp, dtype=i32)
    dest1 = k1 + jnp.sum(jnp.where(e1[:, None] == eids[None, :], gstart[None, :], 0), axis=1)
    dest2 = k2 + jnp.sum(jnp.where(e2[:, None] == eids[None, :], gstart[None, :], 0), axis=1)
    n_active = tend[-1:]
    tile_expert = jnp.sum(jnp.arange(n_tiles, dtype=i32)[:, None] >= tend[None, :], axis=1).astype(i32)
    last_expert = jnp.sum(n_active - 1 >= tend).astype(i32)
    tile_expert = jnp.minimum(tile_expert, last_expert)
    pads = jnp.stack([jnp.concatenate([gstart + cnt, n_active]),
                      jnp.concatenate([gstart + ptiles * tmr, jnp.full((1,), n_tiles, i32)])]).astype(i32)
    k = d // LANES
    xs = _moe_scatter(m.reshape(n_tok * k, LANES), dest1, dest2, pads, n_tiles * tmr, k)
    ys = _moe_ffn(xs, tile_expert, n_active.astype(i32), wg, wu, wd)
    return _moe_combine(ys, dest1, dest2, h, route, mod, nf, final_norm)


def kernel(x, c, norm1_g, norm2_g, w_ada, b_ada, w_in, lam_re, lam_im, log_step, s5_b_re, s5_b_im,
           s5_c_re, s5_c_im, s5_d, w_glu, b_glu, conv_w, a_log, dt_bias, gdn_norm_g, w_out,
           w_router_grp, w_router_exp, w_gate, w_up, w_down, normf_g):
    bsz, seq, d = x.shape
    depth = w_ada.shape[0]
    d_s5 = s5_d.shape[1]
    d_gdn = w_out.shape[1] - d_s5
    n_heads = d_gdn // GDN_HEAD_DIM
    assert seq % S5_TILE == 0 or seq < S5_TILE
    assert seq % ROW_TILE == 0 and seq % GDN_CHUNK == 0 and d_s5 % LANES == 0
    assert 2 * n_heads <= LANES and N_EXPERT_GROUPS + N_EXPERTS <= LANES

    h = x
    for l in range(depth):
        mod = _ada(c, w_ada[l], b_ada[l]).reshape(bsz, N_MOD, d)
        wi = w_in[l]
        c0, c1, c2 = d_s5, d_s5 + 3 * d_gdn, d_s5 + 4 * d_gdn
        u, qkv, z, ab = _inproj(h, mod, norm1_g[l].reshape(1, d), wi, c0, c1 - c0, c2 - c1)
        tabs = _s5_tables(lam_re[l], lam_im[l], log_step[l], s5_b_re[l], s5_b_im[l],
                          s5_c_re[l], s5_c_im[l])
        ys = _s5(u, *tabs, s5_d[l])
        yg = _gdn2(qkv, z, ab, conv_w[l], a_log[l], dt_bias[l], gdn_norm_g[l])
        wr = jnp.zeros((d, LANES), F32)
        wr = wr.at[:, :N_EXPERT_GROUPS].set(w_router_grp[l])
        wr = wr.at[:, ROUTE_OFF:ROUTE_OFF + N_EXPERTS].set(w_router_exp[l])
        wr_hi = wr.astype(BF16)
        wr = jnp.concatenate([wr_hi, (wr - wr_hi.astype(F32)).astype(BF16)], axis=1)
        h1, m, route, counts = _outproj(ys, yg, h, mod, w_glu[l], b_glu[l].reshape(1, d_s5), w_out[l],
                                        norm2_g[l].reshape(1, d), wr)
        h = _moe_sparse(m, route, counts, w_gate[l], w_up[l], w_down[l], h1, mod,
                        normf_g.reshape(1, d), final_norm=(l == depth - 1))
    return h
```

```python
import functools
import math

import jax
import jax.numpy as jnp
from jax import lax
from jax.experimental import pallas as pl
from jax.experimental.pallas import tpu as pltpu

F32 = jnp.float32
BF16 = jnp.bfloat16
HIGHEST = lax.Precision.HIGHEST
EPS = 1e-6

S5_GROUP = 16
S5_STATE = 64
GDN_HEAD_DIM = 128
CONV_WIDTH = 4
N_EXPERT_GROUPS = 4
EXPERTS_PER_GROUP = 8
N_EXPERTS = N_EXPERT_GROUPS * EXPERTS_PER_GROUP
N_MOD = 6

LANES = 128
SUBLANES = 8
VMEM_LIMIT = 56 * 1024 * 1024

S5_CHUNK = SUBLANES
S5_TILE = 2048
GDN_CHUNK = 64
GDN_BLOCK = 256
GDN_SIDE = 64
ROW_TILE = 512
MOE_TILE = 1024
MOE_ROWS = 256
MOE_SCATTER_TILE = 1024
MOE_COMBINE_TILE = 1024
ROUTE_OFF = N_EXPERT_GROUPS


def _dot(a, b, **kw):
    return jnp.dot(a, b, preferred_element_type=F32, **kw)


def _silu(v):
    return v * jax.nn.sigmoid(v)


def _rms(v):
    return v * lax.rsqrt(jnp.mean(v * v, axis=-1, keepdims=True) + EPS)


def _store_token_tiles(ref, idx, v):
    n, width = v.shape
    k = width // LANES
    for s in range(k):
        ref[idx + (pl.ds(s, n, stride=k), slice(None))] = v[:, s * LANES:(s + 1) * LANES]


def _load_token_tiles(ref, idx, n, k):
    return jnp.concatenate([ref[idx + (pl.ds(s, n, stride=k), slice(None))] for s in range(k)], axis=1)


def _ada_kernel(c_ref, w_ref, b_ref, o_ref):
    o_ref[...] = _dot(_silu(c_ref[...]), w_ref[...], precision=HIGHEST) + b_ref[...]


def _ada(c, w, b):
    bsz, d = c.shape
    n = w.shape[1]
    return pl.pallas_call(
        _ada_kernel,
        grid=(n // d,),
        in_specs=[pl.BlockSpec((bsz, d), lambda j: (0, 0)),
                  pl.BlockSpec((d, d), lambda j: (0, j)),
                  pl.BlockSpec((1, d), lambda j: (0, j))],
        out_specs=pl.BlockSpec((bsz, d), lambda j: (0, j)),
        out_shape=jax.ShapeDtypeStruct((bsz, n), F32),
        compiler_params=pltpu.CompilerParams(vmem_limit_bytes=VMEM_LIMIT),
        name="ada",
    )(c, w, b.reshape(1, n))


def _inproj_kernel(x_ref, mod_ref, g_ref, w_ref, u_ref, qkv_ref, z_ref, ab_ref,
                   wu_ref, wqkv_ref, wz_ref, wab_ref):
    @pl.when((pl.program_id(0) == 0) & (pl.program_id(1) == 0))
    def _():
        c0 = wu_ref.shape[1]
        c1 = c0 + wqkv_ref.shape[1]
        c2 = c1 + wz_ref.shape[1]
        n_gate = w_ref.shape[1] - c2
        wu_ref[...] = w_ref[:, 0:c0].astype(BF16)
        wqkv_ref[...] = w_ref[:, c0:c1].astype(BF16)
        wz_ref[...] = w_ref[:, c1:c2].astype(BF16)
        wab_ref[...] = jnp.zeros_like(wab_ref)
        wab_ref[:, 0:n_gate] = w_ref[:, c2:c2 + n_gate].astype(BF16)

    x = x_ref[0]
    y = _rms(x) * g_ref[...]
    h = (y * (1.0 + mod_ref[0, 1:2, :]) + mod_ref[0, 0:1, :]).astype(BF16)
    u_ref[0] = _dot(h, wu_ref[...])
    qkv_ref[0] = _dot(h, wqkv_ref[...]).astype(BF16)
    z_ref[0] = _dot(h, wz_ref[...]).astype(BF16)
    ab_ref[0] = _dot(h, wab_ref[...])


def _inproj(x, mod, g, w, n_u, n_qkv, n_z):
    bsz, seq, d = x.shape
    tm = ROW_TILE
    full = lambda a: pl.BlockSpec(a.shape, lambda b, i: (0,) * a.ndim)
    row = lambda n: pl.BlockSpec((1, tm, n), lambda b, i: (b, i, 0))
    return pl.pallas_call(
        _inproj_kernel,
        grid=(bsz, seq // tm),
        in_specs=[row(d), pl.BlockSpec((1, N_MOD, d), lambda b, i: (b, 0, 0)), full(g),
                  pl.BlockSpec(w.shape, lambda b, i: (0, 0), pipeline_mode=pl.Buffered(1))],
        out_specs=[row(n_u), row(n_qkv), row(n_z), row(LANES)],
        out_shape=[jax.ShapeDtypeStruct((bsz, seq, n_u), F32),
                   jax.ShapeDtypeStruct((bsz, seq, n_qkv), BF16),
                   jax.ShapeDtypeStruct((bsz, seq, n_z), BF16),
                   jax.ShapeDtypeStruct((bsz, seq, LANES), F32)],
        scratch_shapes=[pltpu.VMEM((d, n_u), BF16), pltpu.VMEM((d, n_qkv), BF16),
                        pltpu.VMEM((d, n_z), BF16), pltpu.VMEM((d, LANES), BF16)],
        compiler_params=pltpu.CompilerParams(
            dimension_semantics=("arbitrary", "arbitrary"), vmem_limit_bytes=VMEM_LIMIT),
        name="inproj",
    )(x, mod, g, w)


def _s5_tables(lam_re, lam_im, log_step, b_re, b_im, c_re, c_im):
    ch = S5_CHUNK
    n_grp, n_st = lam_re.shape
    gpb = LANES // S5_GROUP
    nblk = n_grp // gpb
    step = jnp.exp(log_step.astype(F32))[:, None]
    lr = lam_re.astype(F32)
    li = lam_im.astype(F32)

    def lam_pow(j):
        mag = jnp.exp(j * lr * step)
        ang = j * li * step
        return mag * jnp.cos(ang), mag * jnp.sin(ang)

    ar, ai = lam_pow(1.0)
    den = lr * lr + li * li
    fr = ((ar - 1.0) * lr + ai * li) / den
    fi = (ai * lr - (ar - 1.0) * li) / den
    bbr = fr[..., None] * b_re - fi[..., None] * b_im
    bbi = fr[..., None] * b_im + fi[..., None] * b_re
    pows = [lam_pow(float(j)) for j in range(ch + 1)]
    pr = jnp.stack([p[0] for p in pows])
    pi = jnp.stack([p[1] for p in pows])
    lbr = pr[:ch, :, :, None] * bbr[None] - pi[:ch, :, :, None] * bbi[None]
    lbi = pr[:ch, :, :, None] * bbi[None] + pi[:ch, :, :, None] * bbr[None]
    kmat = (jnp.einsum('ghp,jgpk->jghk', c_re, lbr, precision=HIGHEST)
            - jnp.einsum('ghp,jgpk->jghk', c_im, lbi, precision=HIGHEST))
    kt = kmat.reshape(ch, nblk, gpb, S5_GROUP, S5_GROUP).transpose(1, 0, 3, 2, 4)
    kt = kt.reshape(nblk, ch, S5_GROUP, LANES)

    wri = jnp.stack([lbr[::-1], lbi[::-1]]).reshape(2, ch, nblk, gpb, n_st, S5_GROUP)
    wt = wri.transpose(2, 1, 0, 4, 3, 5).reshape(nblk, ch, 2 * n_st, LANES)

    pr1 = pr[1:, :, None, :]
    pi1 = pi[1:, :, None, :]
    clr = c_re[None] * pr1 - c_im[None] * pi1
    cli = c_re[None] * pi1 + c_im[None] * pr1
    wo = jnp.stack([clr, -cli]).reshape(2, ch, nblk, gpb, S5_GROUP, n_st)
    ot = wo.transpose(2, 1, 4, 0, 3, 5).reshape(nblk, ch, S5_GROUP, 2 * gpb * n_st)

    half = gpb * n_st
    amat = jnp.stack([pr[ch].reshape(nblk, half), pi[ch].reshape(nblk, half)], axis=1)
    return _s5_expand(kt, wt, ot, n_st) + (amat,)


def _s5_expand_kernel(kt_ref, wt_ref, ot_ref, t_ref, win_ref, wout_ref, *, n_st):
    ch = kt_ref.shape[1]
    grp = S5_GROUP
    tn = (((0,), (0,)), ((), ()))

    def iota(shape, dim):
        return lax.broadcasted_iota(jnp.int32, shape, dim)

    rep_h = jnp.where(iota((grp, LANES), 0) == iota((grp, LANES), 1) % grp, 1.0, 0.0).astype(BF16)
    sdim = win_ref.shape[2]
    ra = iota((2 * n_st, sdim), 0)
    cb = iota((2 * n_st, sdim), 1)
    rep_s = jnp.where((ra // n_st == cb // (sdim // 2)) & (ra % n_st == cb % n_st), 1.0, 0.0).astype(BF16)
    own_t = iota((LANES, LANES), 0) // grp == iota((LANES, LANES), 1) // grp
    own_w = iota((LANES, sdim), 0) // grp == (iota((LANES, sdim), 1) // n_st) % (LANES // grp)
    own_o = (iota((sdim, LANES), 0) // n_st) % (LANES // grp) == iota((sdim, LANES), 1) // grp

    taps = []
    for j in range(ch):
        blk = lax.dot_general(kt_ref[0, j].astype(BF16), rep_h, tn, preferred_element_type=F32)
        taps.append(jnp.where(own_t, blk, 0.0).astype(BF16))
    zero = jnp.zeros((LANES, LANES), BF16)
    for s in range(ch):
        for t in range(ch):
            t_ref[0, s * LANES:(s + 1) * LANES, t * LANES:(t + 1) * LANES] = taps[t - s] if t >= s else zero
        blk = lax.dot_general(wt_ref[0, s].astype(BF16), rep_s, tn, preferred_element_type=F32)
        win_ref[0, s * LANES:(s + 1) * LANES, :] = jnp.where(own_w, blk, 0.0).astype(BF16)
        blk = lax.dot_general(ot_ref[0, s].astype(BF16), rep_h, tn, preferred_element_type=F32)
        wout_ref[0, :, s * LANES:(s + 1) * LANES] = jnp.where(own_o, blk, 0.0).astype(BF16)


def _s5_expand(kt, wt, ot, n_st):
    nblk, ch = kt.shape[:2]
    kdim = ch * LANES
    sdim = ot.shape[3]
    spec = lambda a: pl.BlockSpec((1,) + a.shape[1:], lambda c: (c, 0, 0, 0))
    big = lambda r, cdim: pl.BlockSpec((1, r, cdim), lambda c: (c, 0, 0))
    return pl.pallas_call(
        functools.partial(_s5_expand_kernel, n_st=n_st),
        grid=(nblk,),
        in_specs=[spec(kt), spec(wt), spec(ot)],
        out_specs=[big(kdim, kdim), big(kdim, sdim), big(sdim, kdim)],
        out_shape=[jax.ShapeDtypeStruct((nblk, kdim, kdim), BF16),
                   jax.ShapeDtypeStruct((nblk, kdim, sdim), BF16),
                   jax.ShapeDtypeStruct((nblk, sdim, kdim), BF16)],
        compiler_params=pltpu.CompilerParams(
            dimension_semantics=("arbitrary",), vmem_limit_bytes=VMEM_LIMIT),
        name="s5_tables",
    )(kt, wt, ot)


def _s5_kernel(u_ref, t_ref, win_ref, wout_ref, a_ref, d_ref, y_ref, st_ref, v_ref, xp_ref):
    ch = S5_CHUNK
    n = u_ref.shape[1] // ch
    half = st_ref.shape[1]

    @pl.when(pl.program_id(2) == 0)
    def _():
        st_ref[...] = jnp.zeros_like(st_ref)

    slabs = [u_ref[0, pl.ds(s, n, stride=ch), :] for s in range(ch)]
    ucat = jnp.concatenate(slabs, axis=1).astype(BF16)
    y = _dot(ucat, t_ref[0])
    v_ref[...] = _dot(ucat, win_ref[0])
    a_r = a_ref[0, 0:1, :]
    a_i = a_ref[0, 1:2, :]

    def body(r, carry):
        x_r, x_i = carry
        xp_ref[pl.ds(r, 1), 0:half] = x_r
        xp_ref[pl.ds(r, 1), half:2 * half] = x_i
        v_r = v_ref[pl.ds(r, 1), 0:half]
        v_i = v_ref[pl.ds(r, 1), half:2 * half]
        return a_r * x_r - a_i * x_i + v_r, a_r * x_i + a_i * x_r + v_i

    x_r, x_i = lax.fori_loop(0, n, body, (st_ref[0:1, :], st_ref[1:2, :]), unroll=4)
    st_ref[0:1, :] = x_r
    st_ref[1:2, :] = x_i
    y = y + _dot(xp_ref[...].astype(BF16), wout_ref[0])
    d = d_ref[0]
    for t in range(ch):
        y_ref[0, pl.ds(t, n, stride=ch), :] = y[:, t * LANES:(t + 1) * LANES] + d * slabs[t]


def _s5(u, tmat, win, wout, amat, d_skip):
    bsz, seq, dch = u.shape
    nblk = dch // LANES
    tt = min(S5_TILE, seq)
    n = tt // S5_CHUNK
    kdim = S5_CHUNK * LANES
    sdim = win.shape[2]
    wspec = lambda a: pl.BlockSpec((1,) + a.shape[1:], lambda b, c, i: (c, 0, 0))
    return pl.pallas_call(
        _s5_kernel,
        grid=(bsz, nblk, seq // tt),
        in_specs=[pl.BlockSpec((1, tt, LANES), lambda b, c, i: (b, i, c)),
                  wspec(tmat), wspec(win), wspec(wout), wspec(amat),
                  pl.BlockSpec((1, 1, LANES), lambda b, c, i: (c, 0, 0))],
        out_specs=pl.BlockSpec((1, tt, LANES), lambda b, c, i: (b, i, c)),
        out_shape=jax.ShapeDtypeStruct((bsz, seq, dch), F32),
        scratch_shapes=[pltpu.VMEM((2, sdim // 2), F32),
                        pltpu.VMEM((n, sdim), F32),
                        pltpu.VMEM((n, sdim), F32)],
        compiler_params=pltpu.CompilerParams(
            dimension_semantics=("parallel", "parallel", "arbitrary"),
            vmem_limit_bytes=VMEM_LIMIT),
        name="s5",
    )(u, tmat, win, wout, amat, d_skip.reshape(nblk, 1, LANES))


def _cumsum_rows(v):
    n = v.shape[0]
    row = lax.broadcasted_iota(jnp.int32, v.shape, 0)
    sh = 1
    while sh < n:
        v = v + jnp.where(row >= sh, pltpu.roll(v, sh, axis=0), 0.0)
        sh *= 2
    return v


def _unit_lower_inverse(lm):
    n = lm.shape[0]
    ri = lax.broadcasted_iota(jnp.int32, (n, n), 0)
    ci = lax.broadcasted_iota(jnp.int32, (n, n), 1)
    eye = jnp.where(ri == ci, 1.0, 0.0).astype(F32)
    m = 1
    d = eye
    while m < n:
        mask = ((ri // (2 * m)) == (ci // (2 * m))) & (((ri // m) % 2) == 1) & (((ci // m) % 2) == 0)
        e = jnp.where(mask, lm, 0.0)
        if m == 1:
            d = eye - e
        else:
            de = _dot(d.astype(BF16), e.astype(BF16))
            d = d - _dot(de.astype(BF16), d.astype(BF16))
        m *= 2
    return d


def _gdn_kernel(qkv_ref, z_ref, ab_ref, cw_ref, gp_ref, ng_ref, o_ref, s_ref, xp_ref, *, n_heads):
    ck = GDN_CHUNK
    dh = GDN_HEAD_DIM
    bsz = qkv_ref.shape[0]
    dg = n_heads * dh
    halo = SUBLANES

    @pl.when(pl.program_id(0) == 0)
    def _():
        s_ref[...] = jnp.zeros_like(s_ref)
        xp_ref[:, 0:halo, :] = jnp.zeros((bsz, halo, 3 * dg), F32)

    ri = lax.broadcasted_iota(jnp.int32, (ck, ck), 0)
    ci = lax.broadcasted_iota(jnp.int32, (ck, ck), 1)
    causal = ri >= ci
    strict = ri > ci
    lane = lax.broadcasted_iota(jnp.int32, (ck, LANES), 1)
    a_log = gp_ref[0:1, :]
    dt_bias = gp_ref[1:2, :]
    ng = ng_ref[...]

    for b in range(bsz):
        xp_ref[b, halo:halo + ck, :] = qkv_ref[b].astype(F32)
        conv = cw_ref[0:1, :] * xp_ref[b, halo - 3:halo - 3 + ck, :]
        for j in range(1, CONV_WIDTH):
            conv = conv + cw_ref[j:j + 1, :] * xp_ref[b, halo - 3 + j:halo - 3 + j + ck, :]
        xp_ref[b, 0:halo, :] = xp_ref[b, ck:ck + halo, :]
        act = _silu(conv)

        ab = ab_ref[b]
        sp = jnp.maximum(ab + dt_bias, 0.0) + jnp.log1p(jnp.exp(-jnp.abs(ab + dt_bias)))
        g = jnp.where(lane < n_heads, -jnp.exp(a_log) * sp, 0.0)
        gc = _cumsum_rows(g)
        gct = gc.T
        beta_all = jax.nn.sigmoid(ab)
        zb = z_ref[b].astype(F32)

        for h in range(n_heads):
            q = act[:, h * dh:(h + 1) * dh]
            k = act[:, dg + h * dh:dg + (h + 1) * dh]
            v = act[:, 2 * dg + h * dh:2 * dg + (h + 1) * dh]
            q = q * lax.rsqrt(jnp.sum(q * q, axis=-1, keepdims=True) + EPS) * (dh ** -0.5)
            k = k * lax.rsqrt(jnp.sum(k * k, axis=-1, keepdims=True) + EPS)
            beta = beta_all[:, n_heads + h:n_heads + h + 1]
            gcol = gc[:, h:h + 1]
            grow = gct[h:h + 1, :]
            glast = gc[ck - 1:ck, h:h + 1]
            decay = jnp.exp(jnp.where(causal, gcol - grow, -1e30))
            kb = k * beta
            kq = jnp.concatenate([kb, q], axis=0).astype(BF16)
            m1 = lax.dot_general(kq, k.astype(BF16), (((1,), (1,)), ((), ())),
                                 preferred_element_type=F32)
            lm = jnp.where(strict, m1[0:ck] * decay, 0.0)
            qk = m1[ck:2 * ck] * decay
            tinv = _unit_lower_inverse(lm)
            eg = jnp.exp(gcol)
            rhs = jnp.concatenate([v * beta, kb * eg], axis=1).astype(BF16)
            w = _dot(tinv.astype(BF16), rhs)
            value = w[:, 0:dh]
            kcd = w[:, dh:2 * dh]
            qd = q * eg
            kt = k * jnp.exp(glast - gcol)
            st = s_ref[b * n_heads + h]
            m2 = _dot(jnp.concatenate([kcd, qd], axis=0).astype(BF16), st.astype(BF16))
            v_new = value - m2[0:ck]
            vnb = v_new.astype(BF16)
            o = m2[ck:2 * ck] + _dot(qk.astype(BF16), vnb)
            s_ref[b * n_heads + h] = st * jnp.exp(glast) + lax.dot_general(
                kt.astype(BF16), vnb, (((0,), (0,)), ((), ())), preferred_element_type=F32)
            zh = zb[:, h * dh:(h + 1) * dh]
            o_ref[b, :, h * dh:(h + 1) * dh] = (_rms(o) * ng * _silu(zh)).astype(o_ref.dtype)


def _gdn(qkv, z, ab, conv_w, a_log, dt_bias, norm_g):
    bsz, seq, _ = qkv.shape
    dg = z.shape[2]
    n_heads = dg // GDN_HEAD_DIM
    ck = GDN_CHUNK
    gp = jnp.zeros((2, LANES), F32).at[0, :n_heads].set(a_log).at[1, :n_heads].set(dt_bias)
    blk = lambda n: pl.BlockSpec((bsz, ck, n), lambda i: (0, i, 0))
    full = lambda a: pl.BlockSpec(a.shape, lambda i: (0,) * a.ndim)
    ng = norm_g.reshape(1, GDN_HEAD_DIM)
    return pl.pallas_call(
        functools.partial(_gdn_kernel, n_heads=n_heads),
        grid=(seq // ck,),
        in_specs=[blk(3 * dg), blk(dg), blk(LANES), full(conv_w), full(gp), full(ng)],
        out_specs=blk(dg),
        out_shape=jax.ShapeDtypeStruct((bsz, seq, dg), BF16),
        scratch_shapes=[pltpu.VMEM((bsz * n_heads, GDN_HEAD_DIM, GDN_HEAD_DIM), F32),
                        pltpu.VMEM((bsz, ck + SUBLANES, 3 * dg), F32)],
        compiler_params=pltpu.CompilerParams(
            dimension_semantics=("arbitrary",), vmem_limit_bytes=VMEM_LIMIT),
        name="gdn",
    )(qkv, z, ab, conv_w, gp, ng)


def _gdn2_kernel(qkv_ref, z_ref, ab_ref, cw_ref, gp_ref, ng_ref, o_ref,
                 s_ref, xp_ref, gc_ref, gct_ref, kbq_ref, kn_ref, rhs_ref, cq_ref, qkt_ref,
                 lm_ref, d_ref, db_ref, de_ref, ds_ref, pe_ref, val_ref, vn_ref, *, n_heads):
    tt = qkv_ref.shape[1]
    dh = GDN_HEAD_DIM
    bsz = qkv_ref.shape[0]
    dg = n_heads * dh
    halo = SUBLANES
    sb = ds_ref.shape[1]
    nsb = tt // sb
    pairs = [(b, h) for b in range(bsz) for h in range(n_heads)]

    @pl.when(pl.program_id(0) == 0)
    def _():
        s_ref[...] = jnp.zeros_like(s_ref)
        xp_ref[...] = jnp.zeros_like(xp_ref)

    ri = lax.broadcasted_iota(jnp.int32, (tt, tt), 0)
    ci = lax.broadcasted_iota(jnp.int32, (tt, tt), 1)
    lane = lax.broadcasted_iota(jnp.int32, (tt, LANES), 1)
    a_log = gp_ref[0:1, :]
    dt_bias = gp_ref[1:2, :]
    shift_op = jnp.concatenate(
        [jnp.where(ri == ci + j, 1.0, 0.0) for j in range(1, CONV_WIDTH)], axis=0).astype(BF16)

    for b in range(bsz):
        x = qkv_ref[b]
        x32 = x.astype(F32)
        shifted = _dot(shift_op, x)
        edge = jnp.concatenate([xp_ref[b], x32[0:halo]], axis=0)
        conv = cw_ref[CONV_WIDTH - 1:CONV_WIDTH, :] * x32
        for j in range(1, CONV_WIDTH):
            sh_j = jnp.concatenate([edge[halo - j:2 * halo - j],
                                    shifted[(j - 1) * tt + halo:j * tt]], axis=0)
            conv = conv + cw_ref[CONV_WIDTH - 1 - j:CONV_WIDTH - j, :] * sh_j
        xp_ref[b] = x32[tt - halo:tt]
        act = _silu(conv)

        ab = ab_ref[b]
        sp = jnp.maximum(ab + dt_bias, 0.0) + jnp.log1p(jnp.exp(-jnp.abs(ab + dt_bias)))
        g = jnp.where(lane < n_heads, -jnp.exp(a_log) * sp, 0.0)
        gc = _cumsum_rows(g)
        gc_ref[b] = gc
        gct_ref[b] = gc.T
        glast = gc[tt - 1:tt, :]
        egc = jnp.exp(gc)
        ekt = jnp.exp(glast - gc)
        beta_all = jax.nn.sigmoid(ab)
        for h in range(n_heads):
            p = b * n_heads + h
            q = act[:, h * dh:(h + 1) * dh]
            k = act[:, dg + h * dh:dg + (h + 1) * dh]
            v = act[:, 2 * dg + h * dh:2 * dg + (h + 1) * dh]
            q = q * lax.rsqrt(jnp.sum(q * q, axis=-1, keepdims=True) + EPS) * (dh ** -0.5)
            k = k * lax.rsqrt(jnp.sum(k * k, axis=-1, keepdims=True) + EPS)
            beta = beta_all[:, n_heads + h:n_heads + h + 1]
            eg = egc[:, h:h + 1]
            kb = k * beta
            kbq_ref[p, 0:tt, :] = kb.astype(BF16)
            kbq_ref[p, tt:2 * tt, :] = q.astype(BF16)
            kn_ref[p] = k.astype(BF16)
            rhs_ref[p, :, 0:dh] = (v * beta).astype(BF16)
            rhs_ref[p, :, dh:2 * dh] = (kb * eg).astype(BF16)
            cq_ref[p, tt:2 * tt, :] = (q * eg).astype(BF16)
            qkt_ref[p, tt:tt + dh, :] = (k * ekt[:, h:h + 1]).T.astype(BF16)

    for b, h in pairs:
        p = b * n_heads + h
        m1 = lax.dot_general(kbq_ref[p], kn_ref[p], (((1,), (1,)), ((), ())),
                             preferred_element_type=F32)
        gcol = gc_ref[b, :, h:h + 1]
        grow = gct_ref[b, h:h + 1, :]
        decay = jnp.exp(jnp.where(ri >= ci, gcol - grow, -1e30))
        lm = jnp.where(ri > ci, m1[0:tt] * decay, 0.0)
        lm_ref[p] = lm.astype(BF16)
        qkt_ref[p, 0:tt, :] = (m1[tt:2 * tt] * decay).astype(BF16)
        first = jnp.where(ri == ci + 1, jnp.where((ci & 1) == 0, lm, 0.0), 0.0)
        d = jnp.where(ri == ci, 1.0, 0.0) - first
        ds_ref[p] = sum(d[j * sb:(j + 1) * sb] for j in range(1, nsb)) + d[0:sb]

    in_blk = (ri // sb) == (ci // sb)
    blk_b = jnp.where(in_blk, 1.0, 0.0).astype(BF16)
    m = 2
    sh = 1
    while m < tt:
        rb = ri >> sh
        cb = ci >> sh
        sel = jnp.where(rb == cb + 1, jnp.where((cb & 1) == 0, 1.0, 0.0), 0.0).astype(BF16)
        if 2 * m <= sb:
            for b, h in pairs:
                p = b * n_heads + h
                pe_ref[p] = _dot(ds_ref[p].astype(BF16), lm_ref[p] * sel).astype(BF16)
            for b, h in pairs:
                p = b * n_heads + h
                ds = ds_ref[p]
                dbd = jnp.concatenate([ds.astype(BF16)] * nsb, axis=0) * blk_b
                ds = ds - _dot(pe_ref[p], dbd)
                ds_ref[p] = ds
                if 4 * m > sb:
                    d = jnp.where(in_blk, jnp.concatenate([ds] * nsb, axis=0), 0.0)
                    d_ref[p] = d
                    db_ref[p] = d.astype(BF16)
        else:
            for b, h in pairs:
                p = b * n_heads + h
                de_ref[p] = _dot(db_ref[p], lm_ref[p] * sel).astype(BF16)
            for b, h in pairs:
                p = b * n_heads + h
                d = d_ref[p] - _dot(de_ref[p], db_ref[p])
                d_ref[p] = d
                db_ref[p] = d.astype(BF16)
        m *= 2
        sh += 1

    for b, h in pairs:
        p = b * n_heads + h
        w = _dot(db_ref[p], rhs_ref[p])
        val_ref[p] = w[:, 0:dh]
        cq_ref[p, 0:tt, :] = w[:, dh:2 * dh].astype(BF16)

    for b, h in pairs:
        p = b * n_heads + h
        m2 = _dot(cq_ref[p], s_ref[p].astype(BF16))
        vn_ref[p] = (val_ref[p] - m2[0:tt]).astype(BF16)
        val_ref[p] = m2[tt:2 * tt]
    for b, h in pairs:
        p = b * n_heads + h
        r = _dot(qkt_ref[p], vn_ref[p])
        o = val_ref[p] + r[0:tt]
        egl = jnp.exp(gc_ref[b, tt - 1:tt, h:h + 1])
        s_ref[p] = s_ref[p] * egl + r[tt:tt + dh]
        zh = z_ref[b, :, h * dh:(h + 1) * dh].astype(F32)
        o_ref[b, :, h * dh:(h + 1) * dh] = (_rms(o) * ng_ref[...] * _silu(zh)).astype(o_ref.dtype)


def _gdn2(qkv, z, ab, conv_w, a_log, dt_bias, norm_g):
    bsz, seq, _ = qkv.shape
    dg = z.shape[2]
    dh = GDN_HEAD_DIM
    n_heads = dg // dh
    tt = GDN_BLOCK
    npair = bsz * n_heads
    gp = jnp.zeros((2, LANES), F32).at[0, :n_heads].set(a_log).at[1, :n_heads].set(dt_bias)
    blk = lambda n: pl.BlockSpec((bsz, tt, n), lambda i: (0, i, 0))
    full = lambda a: pl.BlockSpec(a.shape, lambda i: (0,) * a.ndim)
    ng = norm_g.reshape(1, dh)
    return pl.pallas_call(
        functools.partial(_gdn2_kernel, n_heads=n_heads),
        grid=(seq // tt,),
        in_specs=[blk(3 * dg), blk(dg), blk(LANES), full(conv_w), full(gp), full(ng)],
        out_specs=blk(dg),
        out_shape=jax.ShapeDtypeStruct((bsz, seq, dg), BF16),
        scratch_shapes=[pltpu.VMEM((npair, dh, dh), F32),
                        pltpu.VMEM((bsz, SUBLANES, 3 * dg), F32),
                        pltpu.VMEM((bsz, tt, LANES), F32),
                        pltpu.VMEM((bsz, LANES, tt), F32),
                        pltpu.VMEM((npair, 2 * tt, dh), BF16),
                        pltpu.VMEM((npair, tt, dh), BF16),
                        pltpu.VMEM((npair, tt, 2 * dh), BF16),
                        pltpu.VMEM((npair, 2 * tt, dh), BF16),
                        pltpu.VMEM((npair, tt + dh, tt), BF16),
                        pltpu.VMEM((npair, tt, tt), BF16),
                        pltpu.VMEM((npair, tt, tt), F32),
                        pltpu.VMEM((npair, tt, tt), BF16),
                        pltpu.VMEM((npair, tt, tt), BF16),
                        pltpu.VMEM((npair, GDN_SIDE, tt), F32),
                        pltpu.VMEM((npair, GDN_SIDE, tt), BF16),
                        pltpu.VMEM((npair, tt, dh), F32),
                        pltpu.VMEM((npair, tt, dh), BF16)],
        compiler_params=pltpu.CompilerParams(
            dimension_semantics=("arbitrary",), vmem_limit_bytes=VMEM_LIMIT),
        name="gdn",
    )(qkv, z, ab, conv_w, gp, ng)


def _outproj_kernel(ys_ref, yg_ref, x_ref, mod_ref, wglu_ref, bglu_ref, wo_ref,
                    g2_ref, wr_ref, h_ref, m_ref, route_ref, routet_ref, count_ref,
                    cnt_ref, wglub_ref, wob_ref):
    @pl.when((pl.program_id(0) == 0) & (pl.program_id(1) == 0))
    def _():
        wglub_ref[...] = wglu_ref[...].astype(BF16)
        wob_ref[...] = wo_ref[...].astype(BF16)

    d_s5 = wglub_ref.shape[0]
    y = jax.nn.gelu(ys_ref[0])
    gate = jax.nn.sigmoid(_dot(y.astype(BF16), wglub_ref[...]) + bglu_ref[...])
    y = (y * gate).astype(BF16)
    mix = _dot(y, wob_ref[0:d_s5, :]) + _dot(yg_ref[0], wob_ref[d_s5:, :])
    h = x_ref[0] + mod_ref[0, 2:3, :] * mix
    h_ref[0] = h
    m = _rms(h) * g2_ref[...] * (1.0 + mod_ref[0, 4:5, :]) + mod_ref[0, 3:4, :]
    _store_token_tiles(m_ref, (0,), m)

    m_hi = m.astype(BF16)
    m_lo = (m - m_hi.astype(F32)).astype(BF16)
    part = _dot(m_hi, wr_ref[...])
    logits = part[:, 0:LANES] + part[:, LANES:2 * LANES] + _dot(m_lo, wr_ref[:, 0:LANES])
    lane = lax.broadcasted_iota(jnp.int32, logits.shape, 1)
    neg = -1e30
    is_grp = lane < N_EXPERT_GROUPS
    gl = jnp.where(is_grp, logits, neg)
    gmax = jnp.max(gl, axis=-1, keepdims=True)
    gidx = jnp.min(jnp.where(gl == gmax, lane, LANES), axis=-1, keepdims=True)
    p_grp = 1.0 / jnp.sum(jnp.where(is_grp, jnp.exp(gl - gmax), 0.0), axis=-1, keepdims=True)
    lo = ROUTE_OFF + gidx * EXPERTS_PER_GROUP
    el = jnp.where((lane >= lo) & (lane < lo + EXPERTS_PER_GROUP), logits, neg)
    v1 = jnp.max(el, axis=-1, keepdims=True)
    i1 = jnp.min(jnp.where(el == v1, lane, LANES), axis=-1, keepdims=True)
    el2 = jnp.where(lane == i1, neg, el)
    v2 = jnp.max(el2, axis=-1, keepdims=True)
    i2 = jnp.min(jnp.where(el2 == v2, lane, LANES), axis=-1, keepdims=True)
    t = jnp.exp(v2 - v1)
    w1 = p_grp / (1.0 + t)
    w2 = w1 * t

    @pl.when((pl.program_id(0) == 0) & (pl.program_id(1) == 0))
    def _():
        cnt_ref[...] = jnp.zeros_like(cnt_ref)

    tm = logits.shape[0]
    oh1 = lane == i1
    oh2 = lane == i2
    cnt = jnp.where(oh1, 1.0, 0.0) + jnp.where(oh2, 1.0, 0.0)
    earlier = (lax.broadcasted_iota(jnp.int32, (tm, tm), 0) > lax.broadcasted_iota(jnp.int32, (tm, tm), 1))
    before = _dot(jnp.where(earlier, 1.0, 0.0).astype(BF16), cnt.astype(BF16)) + cnt_ref[...]
    r1 = jnp.sum(jnp.where(oh1, before, 0.0), axis=-1, keepdims=True)
    r2 = jnp.sum(jnp.where(oh2, before, 0.0), axis=-1, keepdims=True)
    cnt_ref[...] += jnp.sum(cnt, axis=0, keepdims=True)
    count_ref[0] = cnt_ref[...]
    cols = [(i1 - ROUTE_OFF).astype(F32), (i2 - ROUTE_OFF).astype(F32), w1, w2, r1, r2]
    route = jnp.zeros_like(logits)
    for j, col in enumerate(cols):
        route = jnp.where(lane == j, col, route)
    route_ref[0] = route
    routet_ref[...] = route.T[0:SUBLANES, :]


def _outproj(ys, yg, x, mod, wglu, bglu, wo, g2, wr):
    bsz, seq, d = x.shape
    tm = ROW_TILE
    full = lambda a: pl.BlockSpec(a.shape, lambda b, i: (0,) * a.ndim, pipeline_mode=pl.Buffered(1))
    row = lambda n: pl.BlockSpec((1, tm, n), lambda b, i: (b, i, 0))
    return pl.pallas_call(
        _outproj_kernel,
        grid=(bsz, seq // tm),
        in_specs=[row(ys.shape[2]), row(yg.shape[2]), row(d),
                  pl.BlockSpec((1, N_MOD, d), lambda b, i: (b, 0, 0)),
                  full(wglu), full(bglu), full(wo), full(g2), full(wr)],
        out_specs=[row(d), pl.BlockSpec((1, tm * (d // LANES), LANES), lambda b, i: (b, i, 0)), row(LANES),
                   pl.BlockSpec((SUBLANES, tm), lambda b, i: (0, b * (seq // tm) + i)),
                   pl.BlockSpec((1, 1, LANES), lambda b, i: (b * (seq // tm) + i, 0, 0))],
        out_shape=[jax.ShapeDtypeStruct((bsz, seq, d), F32),
                   jax.ShapeDtypeStruct((bsz, seq * (d // LANES), LANES), F32),
                   jax.ShapeDtypeStruct((bsz, seq, LANES), F32),
                   jax.ShapeDtypeStruct((SUBLANES, bsz * seq), F32),
                   jax.ShapeDtypeStruct((bsz * (seq // tm), 1, LANES), F32)],
        scratch_shapes=[pltpu.VMEM((1, LANES), F32), pltpu.VMEM(wglu.shape, BF16), pltpu.VMEM(wo.shape, BF16)],
        compiler_params=pltpu.CompilerParams(
            dimension_semantics=("arbitrary", "arbitrary"), vmem_limit_bytes=VMEM_LIMIT),
        name="outproj",
    )(ys, yg, x, mod, wglu, bglu, wo, g2, wr)


def _moe_kernel(m_ref, comb_ref, wg_ref, wu_ref, wd_ref, h_ref, mod_ref, nf_ref, o_ref, acc_ref,
                *, final_norm):
    e = pl.program_id(2)

    @pl.when(e == 0)
    def _():
        acc_ref[...] = jnp.zeros_like(acc_ref)

    m = m_ref[0]
    mid = (_silu(_dot(m, wg_ref[0])) * _dot(m, wu_ref[0])).astype(BF16)
    y = _dot(mid, wd_ref[0])
    comb = comb_ref[0]
    lane = lax.broadcasted_iota(jnp.int32, comb.shape, 1)
    col = jnp.sum(jnp.where(lane == e + ROUTE_OFF, comb, 0.0), axis=-1, keepdims=True)
    acc_ref[...] += col * y

    @pl.when(e == pl.num_programs(2) - 1)
    def _():
        h = h_ref[0] + mod_ref[0, 5:6, :] * acc_ref[...]
        if final_norm:
            h = _rms(h) * nf_ref[...]
        o_ref[0] = h


def _moe(m, comb, wg, wu, wd, h, mod, nf, final_norm):
    bsz, seq, d = h.shape
    tm = min(MOE_TILE, seq)
    n_exp, _, de = wg.shape
    row = lambda n: pl.BlockSpec((1, tm, n), lambda b, i, e: (b, i, 0))
    return pl.pallas_call(
        functools.partial(_moe_kernel, final_norm=final_norm),
        grid=(bsz, seq // tm, n_exp),
        in_specs=[row(d), row(LANES),
                  pl.BlockSpec((1, d, de), lambda b, i, e: (e, 0, 0)),
                  pl.BlockSpec((1, d, de), lambda b, i, e: (e, 0, 0)),
                  pl.BlockSpec((1, de, d), lambda b, i, e: (e, 0, 0)),
                  row(d), pl.BlockSpec((1, N_MOD, d), lambda b, i, e: (b, 0, 0)),
                  pl.BlockSpec(nf.shape, lambda b, i, e: (0, 0))],
        out_specs=row(d),
        out_shape=jax.ShapeDtypeStruct((bsz, seq, d), F32),
        scratch_shapes=[pltpu.VMEM((tm, d), F32)],
        compiler_params=pltpu.CompilerParams(
            dimension_semantics=("parallel", "parallel", "arbitrary"),
            vmem_limit_bytes=VMEM_LIMIT),
        name="moe",
    )(m, comb, wg, wu, wd, h, mod, nf)


def _moe_scatter_kernel(d1_ref, d2_ref, pad_ref, m_ref, xs_hbm, zero_ref, sem, *, n_exp, k):
    i = pl.program_id(0)
    ts = d1_ref.shape[2]
    trows = zero_ref.shape[0]

    def tok(ref, t):
        return ref.at[pl.ds(pl.multiple_of(t * k, k), k)]

    def tile_copy(j):
        return pltpu.make_async_copy(
            zero_ref, xs_hbm.at[pl.ds(pl.multiple_of(j * trows, trows), trows)], sem.at[0])

    def pad_piece(e, p, act):
        lo = pad_ref[0, e]
        length = pad_ref[1, e] - lo
        start = lo + (length & ~(2 * p - 1))
        copy = pltpu.make_async_copy(
            zero_ref.at[pl.ds(0, p * k)], xs_hbm.at[pl.ds(pl.multiple_of(start * k, k), p * k)], sem.at[0])

        @pl.when((length & p) != 0)
        def _():
            act(copy)

    @pl.when(i == 0)
    def _():
        zero_ref[...] = jnp.zeros_like(zero_ref)
        pieces = [(e, p) for e in range(n_exp) for p in [1 << j for j in range((trows // k).bit_length() - 1)]]
        for e, p in pieces:
            pad_piece(e, p, lambda c: c.start())
        for e, p in pieces:
            pad_piece(e, p, lambda c: c.wait())

        def fill(j, carry):
            tile_copy(j).start()
            return carry

        def drain(j, carry):
            tile_copy(j).wait()
            return carry

        lax.fori_loop(pad_ref[0, n_exp], pad_ref[1, n_exp], fill, 0)
        lax.fori_loop(pad_ref[0, n_exp], pad_ref[1, n_exp], drain, 0)

    def row_copy(t, dst, slot):
        return pltpu.make_async_copy(tok(m_ref, t), tok(xs_hbm, dst), sem.at[1 + slot])

    def issue(t, carry):
        row_copy(t, d1_ref[0, 0, t], 0).start(priority=0)
        row_copy(t, d2_ref[0, 0, t], 1).start(priority=1)
        return carry

    lax.fori_loop(0, ts, issue, 0, unroll=8)
    for slot in range(2):
        pltpu.make_async_copy(m_ref, xs_hbm.at[pl.ds(0, ts * k)], sem.at[1 + slot]).wait()


def _moe_scatter(m, dest1, dest2, pads, n_rows, k):
    n_tok = m.shape[0] // k
    ts = MOE_SCATTER_TILE
    n_exp = pads.shape[1] - 1
    dspec = pl.BlockSpec((1, 1, ts), lambda i: (i, 0, 0), memory_space=pltpu.SMEM)
    return pl.pallas_call(
        functools.partial(_moe_scatter_kernel, n_exp=n_exp, k=k),
        grid=(n_tok // ts,),
        in_specs=[dspec, dspec,
                  pl.BlockSpec(pads.shape, lambda i: (0, 0), memory_space=pltpu.SMEM),
                  pl.BlockSpec((ts * k, LANES), lambda i: (i, 0))],
        out_specs=pl.BlockSpec(memory_space=pl.ANY),
        out_shape=jax.ShapeDtypeStruct((n_rows * k, LANES), m.dtype),
        scratch_shapes=[pltpu.VMEM((MOE_ROWS * k, LANES), m.dtype), pltpu.SemaphoreType.DMA((3,))],
        compiler_params=pltpu.CompilerParams(
            dimension_semantics=("arbitrary",), vmem_limit_bytes=VMEM_LIMIT, has_side_effects=True),
        name="moe_scatter",
    )(dest1.reshape(n_tok // ts, 1, ts), dest2.reshape(n_tok // ts, 1, ts), pads, m)


def _moe_ffn_kernel(te_ref, na_ref, xs_ref, wg_ref, wu_ref, wd_ref, ys_ref, wgb_ref, wub_ref, wdb_ref):
    i = pl.program_id(0)
    active = i < na_ref[0]
    changed = (i == 0) | (te_ref[i] != te_ref[jnp.maximum(i - 1, 0)])

    @pl.when(active & changed)
    def _():
        wgb_ref[...] = wg_ref[0].astype(BF16)
        wub_ref[...] = wu_ref[0].astype(BF16)
        wdb_ref[...] = wd_ref[0].astype(BF16)

    @pl.when(active)
    def _():
        k = wgb_ref.shape[0] // LANES
        x = _load_token_tiles(xs_ref, (), xs_ref.shape[0] // k, k).astype(BF16)
        mid = (_silu(_dot(x, wgb_ref[...])) * _dot(x, wub_ref[...])).astype(BF16)
        _store_token_tiles(ys_ref, (), _dot(mid, wdb_ref[...]))


def _moe_ffn(xs, tile_expert, n_active, wg, wu, wd):
    _, d, de = wg.shape
    tmr = MOE_ROWS * (d // LANES)
    n_rows, dp = xs.shape
    grid_spec = pltpu.PrefetchScalarGridSpec(
        num_scalar_prefetch=2,
        grid=(n_rows // tmr,),
        in_specs=[pl.BlockSpec((tmr, dp), lambda i, te, na: (jnp.minimum(i, na[0] - 1), 0)),
                  pl.BlockSpec((1, d, de), lambda i, te, na: (te[i], 0, 0)),
                  pl.BlockSpec((1, d, de), lambda i, te, na: (te[i], 0, 0)),
                  pl.BlockSpec((1, de, d), lambda i, te, na: (te[i], 0, 0))],
        out_specs=pl.BlockSpec((tmr, dp), lambda i, te, na: (jnp.minimum(i, na[0] - 1), 0)),
        scratch_shapes=[pltpu.VMEM((d, de), BF16), pltpu.VMEM((d, de), BF16), pltpu.VMEM((de, d), BF16)],
    )
    return pl.pallas_call(
        _moe_ffn_kernel,
        grid_spec=grid_spec,
        out_shape=jax.ShapeDtypeStruct((n_rows, dp), F32),
        input_output_aliases={2: 0},
        compiler_params=pltpu.CompilerParams(
            dimension_semantics=("arbitrary",), vmem_limit_bytes=VMEM_LIMIT),
        name="moe_ffn",
    )(tile_expert, n_active, xs, wg, wu, wd)


def _moe_combine_kernel(d1_ref, d2_ref, n1_ref, n2_ref, h_ref, route_ref, mod_ref, nf_ref, ys_hbm, o_ref,
                        b1_ref, b2_ref, sem, *, final_norm):
    tm = h_ref.shape[1]
    k = h_ref.shape[2] // LANES
    g = pl.program_id(0)
    slot = g % 2

    def tok(ref, t):
        return ref.at[pl.ds(pl.multiple_of(t * k, k), k)]

    def gather(i1_ref, i2_ref, s):
        def issue(t, carry):
            pltpu.make_async_copy(tok(ys_hbm, i1_ref[0, 0, t]), tok(b1_ref.at[s], t),
                                  sem.at[s, 0]).start(priority=0)
            pltpu.make_async_copy(tok(ys_hbm, i2_ref[0, 0, t]), tok(b2_ref.at[s], t),
                                  sem.at[s, 1]).start(priority=1)
            return carry

        lax.fori_loop(0, tm, issue, 0, unroll=8)

    @pl.when(g == 0)
    def _():
        gather(d1_ref, d2_ref, 0)

    @pl.when(g + 1 < pl.num_programs(0))
    def _():
        gather(n1_ref, n2_ref, 1 - slot)

    pltpu.make_async_copy(ys_hbm.at[pl.ds(0, tm * k)], b1_ref.at[slot], sem.at[slot, 0]).wait()
    pltpu.make_async_copy(ys_hbm.at[pl.ds(0, tm * k)], b2_ref.at[slot], sem.at[slot, 1]).wait()
    route = route_ref[0]
    moe = (route[:, 2:3] * _load_token_tiles(b1_ref, (slot,), tm, k)
           + route[:, 3:4] * _load_token_tiles(b2_ref, (slot,), tm, k))
    h = h_ref[0] + mod_ref[0, 5:6, :] * moe
    if final_norm:
        h = _rms(h) * nf_ref[...]
    o_ref[0] = h


def _moe_combine(ys, dest1, dest2, h, route, mod, nf, final_norm):
    bsz, seq, d = h.shape
    tm = MOE_COMBINE_TILE
    nt = seq // tm
    steps = bsz * nt
    dspec = pl.BlockSpec((1, 1, tm), lambda g: (g, 0, 0), memory_space=pltpu.SMEM)
    nspec = pl.BlockSpec((1, 1, tm), lambda g: (jnp.minimum(g + 1, steps - 1), 0, 0), memory_space=pltpu.SMEM)
    row = lambda n: pl.BlockSpec((1, tm, n), lambda g: (g // nt, g % nt, 0))
    d1 = dest1.reshape(steps, 1, tm)
    d2 = dest2.reshape(steps, 1, tm)
    return pl.pallas_call(
        functools.partial(_moe_combine_kernel, final_norm=final_norm),
        grid=(steps,),
        in_specs=[dspec, dspec, nspec, nspec, row(d), row(LANES),
                  pl.BlockSpec((1, N_MOD, d), lambda g: (g // nt, 0, 0)),
                  pl.BlockSpec(nf.shape, lambda g: (0, 0)),
                  pl.BlockSpec(memory_space=pl.ANY)],
        out_specs=row(d),
        out_shape=jax.ShapeDtypeStruct((bsz, seq, d), F32),
        scratch_shapes=[pltpu.VMEM((2, tm * (d // LANES), LANES), ys.dtype),
                        pltpu.VMEM((2, tm * (d // LANES), LANES), ys.dtype),
                        pltpu.SemaphoreType.DMA((2, 2))],
        compiler_params=pltpu.CompilerParams(
            dimension_semantics=("arbitrary",), vmem_limit_bytes=VMEM_LIMIT),
        name="moe_combine",
    )(d1, d2, d1, d2, h, route, mod, nf, ys)


def _moe_sparse(m, route, route_t, counts, wg, wu, wd, h, mod, nf, final_norm):
    bsz, seq, d = h.shape
    n_tok = bsz * seq
    tmr = MOE_ROWS
    n_exp = wg.shape[0]
    n_tiles = (2 * n_tok) // tmr + n_exp
    i32 = jnp.int32
    e1, e2, k1, k2 = (route_t[j].astype(i32) for j in (0, 1, 4, 5))
    cnt = counts[-1, 0, ROUTE_OFF:ROUTE_OFF + n_exp].astype(i32)
    ptiles = (cnt + tmr - 1) // tmr
    tend = jnp.cumsum(ptiles)
    gstart = (tend - ptiles) * tmr
    eids = jnp.arange(n_exp, dtype=i32)
    dest1 = k1 + jnp.sum(jnp.where(e1[:, None] == eids[None, :], gstart[None, :], 0), axis=1)
    dest2 = k2 + jnp.sum(jnp.where(e2[:, None] == eids[None, :], gstart[None, :], 0), axis=1)
    n_active = tend[-1:]
    tile_expert = jnp.sum(jnp.arange(n_tiles, dtype=i32)[:, None] >= tend[None, :], axis=1).astype(i32)
    last_expert = jnp.sum(n_active - 1 >= tend).astype(i32)
    tile_expert = jnp.minimum(tile_expert, last_expert)
    pads = jnp.stack([jnp.concatenate([gstart + cnt, n_active]),
                      jnp.concatenate([gstart + ptiles * tmr, jnp.full((1,), n_tiles, i32)])]).astype(i32)
    k = d // LANES
    xs = _moe_scatter(m.reshape(n_tok * k, LANES), dest1, dest2, pads, n_tiles * tmr, k)
    ys = _moe_ffn(xs, tile_expert, n_active.astype(i32), wg, wu, wd)
    return _moe_combine(ys, dest1, dest2, h, route, mod, nf, final_norm)


def kernel(x, c, norm1_g, norm2_g, w_ada, b_ada, w_in, lam_re, lam_im, log_step, s5_b_re, s5_b_im,
           s5_c_re, s5_c_im, s5_d, w_glu, b_glu, conv_w, a_log, dt_bias, gdn_norm_g, w_out,
           w_router_grp, w_router_exp, w_gate, w_up, w_down, normf_g):
    bsz, seq, d = x.shape
    depth = w_ada.shape[0]
    d_s5 = s5_d.shape[1]
    d_gdn = w_out.shape[1] - d_s5
    n_heads = d_gdn // GDN_HEAD_DIM
    assert seq % S5_TILE == 0 or seq < S5_TILE
    assert seq % ROW_TILE == 0 and seq % GDN_CHUNK == 0 and d_s5 % LANES == 0
    assert 2 * n_heads <= LANES and N_EXPERT_GROUPS + N_EXPERTS <= LANES

    h = x
    for l in range(depth):
        mod = _ada(c, w_ada[l], b_ada[l]).reshape(bsz, N_MOD, d)
        wi = w_in[l]
        c0, c1, c2 = d_s5, d_s5 + 3 * d_gdn, d_s5 + 4 * d_gdn
        u, qkv, z, ab = _inproj(h, mod, norm1_g[l].reshape(1, d), wi, c0, c1 - c0, c2 - c1)
        tabs = _s5_tables(lam_re[l], lam_im[l], log_step[l], s5_b_re[l], s5_b_im[l],
                          s5_c_re[l], s5_c_im[l])
        ys = _s5(u, *tabs, s5_d[l])
        yg = _gdn2(qkv, z, ab, conv_w[l], a_log[l], dt_bias[l], gdn_norm_g[l])
        wr = jnp.zeros((d, LANES), F32)
        wr = wr.at[:, :N_EXPERT_GROUPS].set(w_router_grp[l])
        wr = wr.at[:, ROUTE_OFF:ROUTE_OFF + N_EXPERTS].set(w_router_exp[l])
        wr_hi = wr.astype(BF16)
        wr = jnp.concatenate([wr_hi, (wr - wr_hi.astype(F32)).astype(BF16)], axis=1)
        h1, m, route, route_t, counts = _outproj(ys, yg, h, mod, w_glu[l], b_glu[l].reshape(1, d_s5),
                                                 w_out[l], norm2_g[l].reshape(1, d), wr)
        h = _moe_sparse(m, route, route_t, counts, w_gate[l], w_up[l], w_down[l], h1, mod,
                        normf_g.reshape(1, d), final_norm=(l == depth - 1))
    return h
```

```python
import functools

import jax
import jax.numpy as jnp
from jax import lax
from jax.experimental import pallas as pl
from jax.experimental.pallas import tpu as pltpu

F32 = jnp.float32
BF16 = jnp.bfloat16
HIGHEST = lax.Precision.HIGHEST
EPS = 1e-6

S5_GROUP = 16
GDN_HEAD_DIM = 128
CONV_WIDTH = 4
N_EXPERT_GROUPS = 4
EXPERTS_PER_GROUP = 8
N_EXPERTS = N_EXPERT_GROUPS * EXPERTS_PER_GROUP
N_MOD = 6

LANES = 128
SUBLANES = 8
VMEM_LIMIT = 56 * 1024 * 1024

S5_CHUNK = SUBLANES
S5_TILE = 2048
GDN_BLOCK = 256
GDN_SIDE = 64
ROW_TILE = 512
MOE_ROWS = 256
MOE_SCATTER_TILE = 1024
MOE_COMBINE_TILE = 1024
ROUTE_OFF = N_EXPERT_GROUPS


def _dot(a, b, **kw):
    return jnp.dot(a, b, preferred_element_type=F32, **kw)


def _silu(v):
    return v * jax.nn.sigmoid(v)


def _rms(v):
    return v * lax.rsqrt(jnp.mean(v * v, axis=-1, keepdims=True) + EPS)


def _store_token_tiles(ref, idx, v):
    n, width = v.shape
    k = width // LANES
    for s in range(k):
        ref[idx + (pl.ds(s, n, stride=k), slice(None))] = v[:, s * LANES:(s + 1) * LANES]


def _load_token_tiles(ref, idx, n, k):
    return jnp.concatenate([ref[idx + (pl.ds(s, n, stride=k), slice(None))] for s in range(k)], axis=1)


def _ada_kernel(c_ref, w_ref, b_ref, o_ref):
    o_ref[...] = _dot(_silu(c_ref[...]), w_ref[...], precision=HIGHEST) + b_ref[...]


def _ada(c, w, b):
    bsz, d = c.shape
    n = w.shape[1]
    return pl.pallas_call(
        _ada_kernel,
        grid=(n // d,),
        in_specs=[pl.BlockSpec((bsz, d), lambda j: (0, 0)),
                  pl.BlockSpec((d, d), lambda j: (0, j)),
                  pl.BlockSpec((1, d), lambda j: (0, j))],
        out_specs=pl.BlockSpec((bsz, d), lambda j: (0, j)),
        out_shape=jax.ShapeDtypeStruct((bsz, n), F32),
        compiler_params=pltpu.CompilerParams(vmem_limit_bytes=VMEM_LIMIT),
        name="ada",
    )(c, w, b.reshape(1, n))


def _inproj_kernel(x_ref, mod_ref, g_ref, w_ref, u_ref, qkv_ref, z_ref, ab_ref,
                   wu_ref, wqkv_ref, wz_ref, wab_ref):
    @pl.when((pl.program_id(0) == 0) & (pl.program_id(1) == 0))
    def _():
        c0 = wu_ref.shape[1]
        c1 = c0 + wqkv_ref.shape[1]
        c2 = c1 + wz_ref.shape[1]
        n_gate = w_ref.shape[1] - c2
        wu_ref[...] = w_ref[:, 0:c0].astype(BF16)
        wqkv_ref[...] = w_ref[:, c0:c1].astype(BF16)
        wz_ref[...] = w_ref[:, c1:c2].astype(BF16)
        wab_ref[...] = jnp.zeros_like(wab_ref)
        wab_ref[:, 0:n_gate] = w_ref[:, c2:c2 + n_gate].astype(BF16)

    x = x_ref[0]
    y = _rms(x) * g_ref[...]
    h = (y * (1.0 + mod_ref[0, 1:2, :]) + mod_ref[0, 0:1, :]).astype(BF16)
    u_ref[0] = _dot(h, wu_ref[...])
    qkv_ref[0] = _dot(h, wqkv_ref[...]).astype(BF16)
    z_ref[0] = _dot(h, wz_ref[...]).astype(BF16)
    ab_ref[0] = _dot(h, wab_ref[...])


def _inproj(x, mod, g, w, n_u, n_qkv, n_z):
    bsz, seq, d = x.shape
    tm = ROW_TILE
    full = lambda a: pl.BlockSpec(a.shape, lambda b, i: (0,) * a.ndim)
    row = lambda n: pl.BlockSpec((1, tm, n), lambda b, i: (b, i, 0))
    return pl.pallas_call(
        _inproj_kernel,
        grid=(bsz, seq // tm),
        in_specs=[row(d), pl.BlockSpec((1, N_MOD, d), lambda b, i: (b, 0, 0)), full(g),
                  pl.BlockSpec(w.shape, lambda b, i: (0, 0), pipeline_mode=pl.Buffered(1))],
        out_specs=[row(n_u), row(n_qkv), row(n_z), row(LANES)],
        out_shape=[jax.ShapeDtypeStruct((bsz, seq, n_u), F32),
                   jax.ShapeDtypeStruct((bsz, seq, n_qkv), BF16),
                   jax.ShapeDtypeStruct((bsz, seq, n_z), BF16),
                   jax.ShapeDtypeStruct((bsz, seq, LANES), F32)],
        scratch_shapes=[pltpu.VMEM((d, n_u), BF16), pltpu.VMEM((d, n_qkv), BF16),
                        pltpu.VMEM((d, n_z), BF16), pltpu.VMEM((d, LANES), BF16)],
        compiler_params=pltpu.CompilerParams(
            dimension_semantics=("arbitrary", "arbitrary"), vmem_limit_bytes=VMEM_LIMIT),
        name="inproj",
    )(x, mod, g, w)


def _s5_tables(lam_re, lam_im, log_step, b_re, b_im, c_re, c_im):
    ch = S5_CHUNK
    n_grp, n_st = lam_re.shape
    gpb = LANES // S5_GROUP
    nblk = n_grp // gpb
    step = jnp.exp(log_step.astype(F32))[:, None]
    lr = lam_re.astype(F32)
    li = lam_im.astype(F32)

    def lam_pow(j):
        mag = jnp.exp(j * lr * step)
        ang = j * li * step
        return mag * jnp.cos(ang), mag * jnp.sin(ang)

    ar, ai = lam_pow(1.0)
    den = lr * lr + li * li
    fr = ((ar - 1.0) * lr + ai * li) / den
    fi = (ai * lr - (ar - 1.0) * li) / den
    bbr = fr[..., None] * b_re - fi[..., None] * b_im
    bbi = fr[..., None] * b_im + fi[..., None] * b_re
    pows = [lam_pow(float(j)) for j in range(ch + 1)]
    pr = jnp.stack([p[0] for p in pows])
    pi = jnp.stack([p[1] for p in pows])
    lbr = pr[:ch, :, :, None] * bbr[None] - pi[:ch, :, :, None] * bbi[None]
    lbi = pr[:ch, :, :, None] * bbi[None] + pi[:ch, :, :, None] * bbr[None]
    kmat = (jnp.einsum('ghp,jgpk->jghk', c_re, lbr, precision=HIGHEST)
            - jnp.einsum('ghp,jgpk->jghk', c_im, lbi, precision=HIGHEST))
    kt = kmat.reshape(ch, nblk, gpb, S5_GROUP, S5_GROUP).transpose(1, 0, 3, 2, 4)
    kt = kt.reshape(nblk, ch, S5_GROUP, LANES)

    wri = jnp.stack([lbr[::-1], lbi[::-1]]).reshape(2, ch, nblk, gpb, n_st, S5_GROUP)
    wt = wri.transpose(2, 1, 0, 4, 3, 5).reshape(nblk, ch, 2 * n_st, LANES)

    pr1 = pr[1:, :, None, :]
    pi1 = pi[1:, :, None, :]
    clr = c_re[None] * pr1 - c_im[None] * pi1
    cli = c_re[None] * pi1 + c_im[None] * pr1
    wo = jnp.stack([clr, -cli]).reshape(2, ch, nblk, gpb, S5_GROUP, n_st)
    ot = wo.transpose(2, 1, 4, 0, 3, 5).reshape(nblk, ch, S5_GROUP, 2 * gpb * n_st)

    half = gpb * n_st
    amat = jnp.stack([pr[ch].reshape(nblk, half), pi[ch].reshape(nblk, half)], axis=1)
    return _s5_expand(kt, wt, ot, n_st) + (amat,)


def _s5_expand_kernel(kt_ref, wt_ref, ot_ref, t_ref, win_ref, wout_ref, *, n_st):
    ch = kt_ref.shape[1]
    grp = S5_GROUP
    tn = (((0,), (0,)), ((), ()))

    def iota(shape, dim):
        return lax.broadcasted_iota(jnp.int32, shape, dim)

    rep_h = jnp.where(iota((grp, LANES), 0) == iota((grp, LANES), 1) % grp, 1.0, 0.0).astype(BF16)
    sdim = win_ref.shape[2]
    ra = iota((2 * n_st, sdim), 0)
    cb = iota((2 * n_st, sdim), 1)
    rep_s = jnp.where((ra // n_st == cb // (sdim // 2)) & (ra % n_st == cb % n_st), 1.0, 0.0).astype(BF16)
    own_t = iota((LANES, LANES), 0) // grp == iota((LANES, LANES), 1) // grp
    own_w = iota((LANES, sdim), 0) // grp == (iota((LANES, sdim), 1) // n_st) % (LANES // grp)
    own_o = (iota((sdim, LANES), 0) // n_st) % (LANES // grp) == iota((sdim, LANES), 1) // grp

    taps = []
    for j in range(ch):
        blk = lax.dot_general(kt_ref[0, j].astype(BF16), rep_h, tn, preferred_element_type=F32)
        taps.append(jnp.where(own_t, blk, 0.0).astype(BF16))
    zero = jnp.zeros((LANES, LANES), BF16)
    for s in range(ch):
        for t in range(ch):
            t_ref[0, s * LANES:(s + 1) * LANES, t * LANES:(t + 1) * LANES] = taps[t - s] if t >= s else zero
        blk = lax.dot_general(wt_ref[0, s].astype(BF16), rep_s, tn, preferred_element_type=F32)
        win_ref[0, s * LANES:(s + 1) * LANES, :] = jnp.where(own_w, blk, 0.0).astype(BF16)
        blk = lax.dot_general(ot_ref[0, s].astype(BF16), rep_h, tn, preferred_element_type=F32)
        wout_ref[0, :, s * LANES:(s + 1) * LANES] = jnp.where(own_o, blk, 0.0).astype(BF16)


def _s5_expand(kt, wt, ot, n_st):
    nblk, ch = kt.shape[:2]
    kdim = ch * LANES
    sdim = ot.shape[3]
    spec = lambda a: pl.BlockSpec((1,) + a.shape[1:], lambda c: (c, 0, 0, 0))
    big = lambda r, cdim: pl.BlockSpec((1, r, cdim), lambda c: (c, 0, 0))
    return pl.pallas_call(
        functools.partial(_s5_expand_kernel, n_st=n_st),
        grid=(nblk,),
        in_specs=[spec(kt), spec(wt), spec(ot)],
        out_specs=[big(kdim, kdim), big(kdim, sdim), big(sdim, kdim)],
        out_shape=[jax.ShapeDtypeStruct((nblk, kdim, kdim), BF16),
                   jax.ShapeDtypeStruct((nblk, kdim, sdim), BF16),
                   jax.ShapeDtypeStruct((nblk, sdim, kdim), BF16)],
        compiler_params=pltpu.CompilerParams(
            dimension_semantics=("arbitrary",), vmem_limit_bytes=VMEM_LIMIT),
        name="s5_tables",
    )(kt, wt, ot)


def _s5_kernel(u_ref, t_ref, win_ref, wout_ref, a_ref, d_ref, y_ref, st_ref, v_ref, xp_ref):
    ch = S5_CHUNK
    nb = u_ref.shape[0]
    n = u_ref.shape[1] // ch
    half = st_ref.shape[2]

    @pl.when(pl.program_id(2) == 0)
    def _():
        st_ref[...] = jnp.zeros_like(st_ref)

    def slabs(b):
        return [u_ref[b, pl.ds(s, n, stride=ch), :] for s in range(ch)]

    for b in range(nb):
        v_ref[b] = _dot(jnp.concatenate(slabs(b), axis=1).astype(BF16), win_ref[0])
    a_r = a_ref[0, 0:1, :]
    a_i = a_ref[0, 1:2, :]

    def body(r, carry):
        out = []
        for b in range(nb):
            x_r, x_i = carry[b]
            xp_ref[b, pl.ds(r, 1), 0:half] = x_r
            xp_ref[b, pl.ds(r, 1), half:2 * half] = x_i
            v_r = v_ref[b, pl.ds(r, 1), 0:half]
            v_i = v_ref[b, pl.ds(r, 1), half:2 * half]
            out.append((a_r * x_r - a_i * x_i + v_r, a_r * x_i + a_i * x_r + v_i))
        return tuple(out)

    init = tuple((st_ref[b, 0:1, :], st_ref[b, 1:2, :]) for b in range(nb))
    last = lax.fori_loop(0, n, body, init, unroll=4)
    d = d_ref[0]
    for b in range(nb):
        st_ref[b, 0:1, :] = last[b][0]
        st_ref[b, 1:2, :] = last[b][1]
        sl = slabs(b)
        y = (_dot(jnp.concatenate(sl, axis=1).astype(BF16), t_ref[0])
             + _dot(xp_ref[b].astype(BF16), wout_ref[0]))
        for t in range(ch):
            y_ref[b, pl.ds(t, n, stride=ch), :] = y[:, t * LANES:(t + 1) * LANES] + d * sl[t]


def _s5(u, tmat, win, wout, amat, d_skip):
    bsz, seq, dch = u.shape
    nblk = dch // LANES
    tt = min(S5_TILE, seq)
    n = tt // S5_CHUNK
    sdim = win.shape[2]
    nb = 2 if bsz % 2 == 0 else 1
    wspec = lambda a: pl.BlockSpec((1,) + a.shape[1:], lambda b, c, i: (c, 0, 0))
    return pl.pallas_call(
        _s5_kernel,
        grid=(bsz // nb, nblk, seq // tt),
        in_specs=[pl.BlockSpec((nb, tt, LANES), lambda b, c, i: (b, i, c)),
                  wspec(tmat), wspec(win), wspec(wout), wspec(amat),
                  pl.BlockSpec((1, 1, LANES), lambda b, c, i: (c, 0, 0))],
        out_specs=pl.BlockSpec((nb, tt, LANES), lambda b, c, i: (b, i, c)),
        out_shape=jax.ShapeDtypeStruct((bsz, seq, dch), F32),
        scratch_shapes=[pltpu.VMEM((nb, 2, sdim // 2), F32),
                        pltpu.VMEM((nb, n, sdim), F32),
                        pltpu.VMEM((nb, n, sdim), F32)],
        compiler_params=pltpu.CompilerParams(
            dimension_semantics=("parallel", "parallel", "arbitrary"),
            vmem_limit_bytes=VMEM_LIMIT),
        name="s5",
    )(u, tmat, win, wout, amat, d_skip.reshape(nblk, 1, LANES))


def _cumsum_rows(v):
    n = v.shape[0]
    row = lax.broadcasted_iota(jnp.int32, v.shape, 0)
    sh = 1
    while sh < n:
        v = v + jnp.where(row >= sh, pltpu.roll(v, sh, axis=0), 0.0)
        sh *= 2
    return v


def _gdn2_kernel(qkv_ref, z_ref, ab_ref, cw_ref, gp_ref, ng_ref, o_ref,
                 s_ref, xp_ref, gc_ref, gct_ref, kbq_ref, kn_ref, rhs_ref, cq_ref, qkt_ref,
                 lm_ref, d_ref, db_ref, de_ref, ds_ref, pe_ref, val_ref, vn_ref, *, n_heads):
    tt = qkv_ref.shape[1]
    dh = GDN_HEAD_DIM
    bsz = qkv_ref.shape[0]
    dg = n_heads * dh
    halo = SUBLANES
    sb = ds_ref.shape[1]
    nsb = tt // sb
    pairs = [(b, h) for b in range(bsz) for h in range(n_heads)]

    @pl.when(pl.program_id(0) == 0)
    def _():
        s_ref[...] = jnp.zeros_like(s_ref)
        xp_ref[...] = jnp.zeros_like(xp_ref)

    ri = lax.broadcasted_iota(jnp.int32, (tt, tt), 0)
    ci = lax.broadcasted_iota(jnp.int32, (tt, tt), 1)
    lane = lax.broadcasted_iota(jnp.int32, (tt, LANES), 1)
    a_log = gp_ref[0:1, :]
    dt_bias = gp_ref[1:2, :]
    shift_op = jnp.concatenate(
        [jnp.where(ri == ci + j, 1.0, 0.0) for j in range(1, CONV_WIDTH)], axis=0).astype(BF16)

    for b in range(bsz):
        x = qkv_ref[b]
        x32 = x.astype(F32)
        shifted = _dot(shift_op, x)
        edge = jnp.concatenate([xp_ref[b], x32[0:halo]], axis=0)
        conv = cw_ref[CONV_WIDTH - 1:CONV_WIDTH, :] * x32
        for j in range(1, CONV_WIDTH):
            sh_j = jnp.concatenate([edge[halo - j:2 * halo - j],
                                    shifted[(j - 1) * tt + halo:j * tt]], axis=0)
            conv = conv + cw_ref[CONV_WIDTH - 1 - j:CONV_WIDTH - j, :] * sh_j
        xp_ref[b] = x32[tt - halo:tt]
        act = _silu(conv)

        ab = ab_ref[b]
        sp = jnp.maximum(ab + dt_bias, 0.0) + jnp.log1p(jnp.exp(-jnp.abs(ab + dt_bias)))
        g = jnp.where(lane < n_heads, -jnp.exp(a_log) * sp, 0.0)
        gc = _cumsum_rows(g)
        gc_ref[b] = gc
        gct_ref[b] = gc.T
        glast = gc[tt - 1:tt, :]
        egc = jnp.exp(gc)
        ekt = jnp.exp(glast - gc)
        beta_all = jax.nn.sigmoid(ab)
        for h in range(n_heads):
            p = b * n_heads + h
            q = act[:, h * dh:(h + 1) * dh]
            k = act[:, dg + h * dh:dg + (h + 1) * dh]
            v = act[:, 2 * dg + h * dh:2 * dg + (h + 1) * dh]
            q = q * lax.rsqrt(jnp.sum(q * q, axis=-1, keepdims=True) + EPS) * (dh ** -0.5)
            k = k * lax.rsqrt(jnp.sum(k * k, axis=-1, keepdims=True) + EPS)
            beta = beta_all[:, n_heads + h:n_heads + h + 1]
            eg = egc[:, h:h + 1]
            kb = k * beta
            kbq_ref[p, 0:tt, :] = kb.astype(BF16)
            kbq_ref[p, tt:2 * tt, :] = q.astype(BF16)
            kn_ref[p] = k.astype(BF16)
            rhs_ref[p, :, 0:dh] = (v * beta).astype(BF16)
            rhs_ref[p, :, dh:2 * dh] = (kb * eg).astype(BF16)
            cq_ref[p, tt:2 * tt, :] = (q * eg).astype(BF16)
            qkt_ref[p, tt:tt + dh, :] = (k * ekt[:, h:h + 1]).T.astype(BF16)

    for b, h in pairs:
        p = b * n_heads + h
        m1 = lax.dot_general(kbq_ref[p], kn_ref[p], (((1,), (1,)), ((), ())),
                             preferred_element_type=F32)
        gcol = gc_ref[b, :, h:h + 1]
        grow = gct_ref[b, h:h + 1, :]
        decay = jnp.exp(jnp.where(ri >= ci, gcol - grow, -1e30))
        lm = jnp.where(ri > ci, m1[0:tt] * decay, 0.0)
        lm_ref[p] = lm.astype(BF16)
        qkt_ref[p, 0:tt, :] = (m1[tt:2 * tt] * decay).astype(BF16)
        first = jnp.where(ri == ci + 1, jnp.where((ci & 1) == 0, lm, 0.0), 0.0)
        d = jnp.where(ri == ci, 1.0, 0.0) - first
        ds_ref[p] = sum(d[j * sb:(j + 1) * sb] for j in range(1, nsb)) + d[0:sb]

    in_blk = (ri // sb) == (ci // sb)
    blk_b = jnp.where(in_blk, 1.0, 0.0).astype(BF16)
    m = 2
    sh = 1
    while m < tt:
        rb = ri >> sh
        cb = ci >> sh
        sel = jnp.where(rb == cb + 1, jnp.where((cb & 1) == 0, 1.0, 0.0), 0.0).astype(BF16)
        if 2 * m <= sb:
            for b, h in pairs:
                p = b * n_heads + h
                pe_ref[p] = _dot(ds_ref[p].astype(BF16), lm_ref[p] * sel).astype(BF16)
            for b, h in pairs:
                p = b * n_heads + h
                ds = ds_ref[p]
                dbd = jnp.concatenate([ds.astype(BF16)] * nsb, axis=0) * blk_b
                ds = ds - _dot(pe_ref[p], dbd)
                ds_ref[p] = ds
                if 4 * m > sb:
                    d = jnp.where(in_blk, jnp.concatenate([ds] * nsb, axis=0), 0.0)
                    d_ref[p] = d
                    db_ref[p] = d.astype(BF16)
        else:
            for b, h in pairs:
                p = b * n_heads + h
                de_ref[p] = _dot(db_ref[p], lm_ref[p] * sel).astype(BF16)
            for b, h in pairs:
                p = b * n_heads + h
                d = d_ref[p] - _dot(de_ref[p], db_ref[p])
                d_ref[p] = d
                db_ref[p] = d.astype(BF16)
        m *= 2
        sh += 1

    for b, h in pairs:
        p = b * n_heads + h
        w = _dot(db_ref[p], rhs_ref[p])
        val_ref[p] = w[:, 0:dh]
        cq_ref[p, 0:tt, :] = w[:, dh:2 * dh].astype(BF16)

    for b, h in pairs:
        p = b * n_heads + h
        m2 = _dot(cq_ref[p], s_ref[p].astype(BF16))
        vn_ref[p] = (val_ref[p] - m2[0:tt]).astype(BF16)
        val_ref[p] = m2[tt:2 * tt]
    for b, h in pairs:
        p = b * n_heads + h
        r = _dot(qkt_ref[p], vn_ref[p])
        o = val_ref[p] + r[0:tt]
        egl = jnp.exp(gc_ref[b, tt - 1:tt, h:h + 1])
        s_ref[p] = s_ref[p] * egl + r[tt:tt + dh]
        zh = z_ref[b, :, h * dh:(h + 1) * dh].astype(F32)
        o_ref[b, :, h * dh:(h + 1) * dh] = (_rms(o) * ng_ref[...] * _silu(zh)).astype(o_ref.dtype)


def _gdn2(qkv, z, ab, conv_w, a_log, dt_bias, norm_g):
    bsz, seq, _ = qkv.shape
    dg = z.shape[2]
    dh = GDN_HEAD_DIM
    n_heads = dg // dh
    tt = GDN_BLOCK
    npair = bsz * n_heads
    gp = jnp.zeros((2, LANES), F32).at[0, :n_heads].set(a_log).at[1, :n_heads].set(dt_bias)
    blk = lambda n: pl.BlockSpec((bsz, tt, n), lambda i: (0, i, 0))
    full = lambda a: pl.BlockSpec(a.shape, lambda i: (0,) * a.ndim)
    ng = norm_g.reshape(1, dh)
    return pl.pallas_call(
        functools.partial(_gdn2_kernel, n_heads=n_heads),
        grid=(seq // tt,),
        in_specs=[blk(3 * dg), blk(dg), blk(LANES), full(conv_w), full(gp), full(ng)],
        out_specs=blk(dg),
        out_shape=jax.ShapeDtypeStruct((bsz, seq, dg), BF16),
        scratch_shapes=[pltpu.VMEM((npair, dh, dh), F32),
                        pltpu.VMEM((bsz, SUBLANES, 3 * dg), F32),
                        pltpu.VMEM((bsz, tt, LANES), F32),
                        pltpu.VMEM((bsz, LANES, tt), F32),
                        pltpu.VMEM((npair, 2 * tt, dh), BF16),
                        pltpu.VMEM((npair, tt, dh), BF16),
                        pltpu.VMEM((npair, tt, 2 * dh), BF16),
                        pltpu.VMEM((npair, 2 * tt, dh), BF16),
                        pltpu.VMEM((npair, tt + dh, tt), BF16),
                        pltpu.VMEM((npair, tt, tt), BF16),
                        pltpu.VMEM((npair, tt, tt), F32),
                        pltpu.VMEM((npair, tt, tt), BF16),
                        pltpu.VMEM((npair, tt, tt), BF16),
                        pltpu.VMEM((npair, GDN_SIDE, tt), F32),
                        pltpu.VMEM((npair, GDN_SIDE, tt), BF16),
                        pltpu.VMEM((npair, tt, dh), F32),
                        pltpu.VMEM((npair, tt, dh), BF16)],
        compiler_params=pltpu.CompilerParams(
            dimension_semantics=("arbitrary",), vmem_limit_bytes=VMEM_LIMIT),
        name="gdn",
    )(qkv, z, ab, conv_w, gp, ng)


def _outproj_kernel(ys_ref, yg_ref, x_ref, mod_ref, wglu_ref, bglu_ref, wo_ref,
                    g2_ref, wr_ref, h_ref, m_ref, route_ref, routet_ref, count_ref,
                    cnt_ref, wglub_ref, wob_ref):
    @pl.when((pl.program_id(0) == 0) & (pl.program_id(1) == 0))
    def _():
        wglub_ref[...] = wglu_ref[...].astype(BF16)
        wob_ref[...] = wo_ref[...].astype(BF16)

    d_s5 = wglub_ref.shape[0]
    y = jax.nn.gelu(ys_ref[0])
    gate = jax.nn.sigmoid(_dot(y.astype(BF16), wglub_ref[...]) + bglu_ref[...])
    y = (y * gate).astype(BF16)
    mix = _dot(y, wob_ref[0:d_s5, :]) + _dot(yg_ref[0], wob_ref[d_s5:, :])
    h = x_ref[0] + mod_ref[0, 2:3, :] * mix
    h_ref[0] = h
    m = _rms(h) * g2_ref[...] * (1.0 + mod_ref[0, 4:5, :]) + mod_ref[0, 3:4, :]
    _store_token_tiles(m_ref, (0,), m)

    m_hi = m.astype(BF16)
    m_lo = (m - m_hi.astype(F32)).astype(BF16)
    part = _dot(m_hi, wr_ref[...])
    logits = part[:, 0:LANES] + part[:, LANES:2 * LANES] + _dot(m_lo, wr_ref[:, 0:LANES])
    lane = lax.broadcasted_iota(jnp.int32, logits.shape, 1)
    neg = -1e30
    is_grp = lane < N_EXPERT_GROUPS
    gl = jnp.where(is_grp, logits, neg)
    gmax = jnp.max(gl, axis=-1, keepdims=True)
    gidx = jnp.min(jnp.where(gl == gmax, lane, LANES), axis=-1, keepdims=True)
    p_grp = 1.0 / jnp.sum(jnp.where(is_grp, jnp.exp(gl - gmax), 0.0), axis=-1, keepdims=True)
    lo = ROUTE_OFF + gidx * EXPERTS_PER_GROUP
    el = jnp.where((lane >= lo) & (lane < lo + EXPERTS_PER_GROUP), logits, neg)
    v1 = jnp.max(el, axis=-1, keepdims=True)
    i1 = jnp.min(jnp.where(el == v1, lane, LANES), axis=-1, keepdims=True)
    el2 = jnp.where(lane == i1, neg, el)
    v2 = jnp.max(el2, axis=-1, keepdims=True)
    i2 = jnp.min(jnp.where(el2 == v2, lane, LANES), axis=-1, keepdims=True)
    t = jnp.exp(v2 - v1)
    w1 = p_grp / (1.0 + t)
    w2 = w1 * t

    @pl.when((pl.program_id(0) == 0) & (pl.program_id(1) == 0))
    def _():
        cnt_ref[...] = jnp.zeros_like(cnt_ref)

    tm = logits.shape[0]
    oh1 = lane == i1
    oh2 = lane == i2
    cnt = jnp.where(oh1, 1.0, 0.0) + jnp.where(oh2, 1.0, 0.0)
    earlier = (lax.broadcasted_iota(jnp.int32, (tm, tm), 0) > lax.broadcasted_iota(jnp.int32, (tm, tm), 1))
    before = _dot(jnp.where(earlier, 1.0, 0.0).astype(BF16), cnt.astype(BF16)) + cnt_ref[...]
    r1 = jnp.sum(jnp.where(oh1, before, 0.0), axis=-1, keepdims=True)
    r2 = jnp.sum(jnp.where(oh2, before, 0.0), axis=-1, keepdims=True)
    cnt_ref[...] += jnp.sum(cnt, axis=0, keepdims=True)
    count_ref[0] = cnt_ref[...]
    cols = [(i1 - ROUTE_OFF).astype(F32), (i2 - ROUTE_OFF).astype(F32), w1, w2, r1, r2]
    route = jnp.zeros_like(logits)
    for j, col in enumerate(cols):
        route = jnp.where(lane == j, col, route)
    route_ref[0] = route
    routet_ref[...] = route.T[0:SUBLANES, :]


def _outproj(ys, yg, x, mod, wglu, bglu, wo, g2, wr):
    bsz, seq, d = x.shape
    tm = ROW_TILE
    full = lambda a: pl.BlockSpec(a.shape, lambda b, i: (0,) * a.ndim, pipeline_mode=pl.Buffered(1))
    row = lambda n: pl.BlockSpec((1, tm, n), lambda b, i: (b, i, 0))
    return pl.pallas_call(
        _outproj_kernel,
        grid=(bsz, seq // tm),
        in_specs=[row(ys.shape[2]), row(yg.shape[2]), row(d),
                  pl.BlockSpec((1, N_MOD, d), lambda b, i: (b, 0, 0)),
                  full(wglu), full(bglu), full(wo), full(g2), full(wr)],
        out_specs=[row(d), pl.BlockSpec((1, tm * (d // LANES), LANES), lambda b, i: (b, i, 0)), row(LANES),
                   pl.BlockSpec((SUBLANES, tm), lambda b, i: (0, b * (seq // tm) + i)),
                   pl.BlockSpec((1, 1, LANES), lambda b, i: (b * (seq // tm) + i, 0, 0))],
        out_shape=[jax.ShapeDtypeStruct((bsz, seq, d), F32),
                   jax.ShapeDtypeStruct((bsz, seq * (d // LANES), LANES), F32),
                   jax.ShapeDtypeStruct((bsz, seq, LANES), F32),
                   jax.ShapeDtypeStruct((SUBLANES, bsz * seq), F32),
                   jax.ShapeDtypeStruct((bsz * (seq // tm), 1, LANES), F32)],
        scratch_shapes=[pltpu.VMEM((1, LANES), F32), pltpu.VMEM(wglu.shape, BF16), pltpu.VMEM(wo.shape, BF16)],
        compiler_params=pltpu.CompilerParams(
            dimension_semantics=("arbitrary", "arbitrary"), vmem_limit_bytes=VMEM_LIMIT),
        name="outproj",
    )(ys, yg, x, mod, wglu, bglu, wo, g2, wr)


def _moe_scatter_kernel(d1_ref, d2_ref, pad_ref, m_ref, xs_hbm, zero_ref, sem, *, n_exp, k):
    i = pl.program_id(0)
    ts = d1_ref.shape[2]
    trows = zero_ref.shape[0]

    def tok(ref, t):
        return ref.at[pl.ds(pl.multiple_of(t * k, k), k)]

    def tile_copy(j):
        return pltpu.make_async_copy(
            zero_ref, xs_hbm.at[pl.ds(pl.multiple_of(j * trows, trows), trows)], sem.at[0])

    def pad_piece(e, p, act):
        lo = pad_ref[0, e]
        length = pad_ref[1, e] - lo
        start = lo + (length & ~(2 * p - 1))
        copy = pltpu.make_async_copy(
            zero_ref.at[pl.ds(0, p * k)], xs_hbm.at[pl.ds(pl.multiple_of(start * k, k), p * k)], sem.at[0])

        @pl.when((length & p) != 0)
        def _():
            act(copy)

    @pl.when(i == 0)
    def _():
        zero_ref[...] = jnp.zeros_like(zero_ref)
        pieces = [(e, p) for e in range(n_exp) for p in [1 << j for j in range((trows // k).bit_length() - 1)]]
        for e, p in pieces:
            pad_piece(e, p, lambda c: c.start())
        for e, p in pieces:
            pad_piece(e, p, lambda c: c.wait())

        def fill(j, carry):
            tile_copy(j).start()
            return carry

        def drain(j, carry):
            tile_copy(j).wait()
            return carry

        lax.fori_loop(pad_ref[0, n_exp], pad_ref[1, n_exp], fill, 0)
        lax.fori_loop(pad_ref[0, n_exp], pad_ref[1, n_exp], drain, 0)

    def row_copy(t, dst, slot):
        return pltpu.make_async_copy(tok(m_ref, t), tok(xs_hbm, dst), sem.at[1 + slot])

    def issue(t, carry):
        row_copy(t, d1_ref[0, 0, t], 0).start(priority=0)
        row_copy(t, d2_ref[0, 0, t], 1).start(priority=1)
        return carry

    lax.fori_loop(0, ts, issue, 0, unroll=8)
    for slot in range(2):
        pltpu.make_async_copy(m_ref, xs_hbm.at[pl.ds(0, ts * k)], sem.at[1 + slot]).wait()


def _moe_scatter(m, dest1, dest2, pads, n_rows, k):
    n_tok = m.shape[0] // k
    ts = MOE_SCATTER_TILE
    n_exp = pads.shape[1] - 1
    dspec = pl.BlockSpec((1, 1, ts), lambda i: (i, 0, 0), memory_space=pltpu.SMEM)
    return pl.pallas_call(
        functools.partial(_moe_scatter_kernel, n_exp=n_exp, k=k),
        grid=(n_tok // ts,),
        in_specs=[dspec, dspec,
                  pl.BlockSpec(pads.shape, lambda i: (0, 0), memory_space=pltpu.SMEM),
                  pl.BlockSpec((ts * k, LANES), lambda i: (i, 0))],
        out_specs=pl.BlockSpec(memory_space=pl.ANY),
        out_shape=jax.ShapeDtypeStruct((n_rows * k, LANES), m.dtype),
        scratch_shapes=[pltpu.VMEM((MOE_ROWS * k, LANES), m.dtype), pltpu.SemaphoreType.DMA((3,))],
        compiler_params=pltpu.CompilerParams(
            dimension_semantics=("arbitrary",), vmem_limit_bytes=VMEM_LIMIT, has_side_effects=True),
        name="moe_scatter",
    )(dest1.reshape(n_tok // ts, 1, ts), dest2.reshape(n_tok // ts, 1, ts), pads, m)


def _moe_ffn_kernel(te_ref, na_ref, xs_ref, wg_ref, wu_ref, wd_ref, ys_ref, wgb_ref, wub_ref, wdb_ref):
    i = pl.program_id(0)
    active = i < na_ref[0]
    changed = (i == 0) | (te_ref[i] != te_ref[jnp.maximum(i - 1, 0)])

    @pl.when(active & changed)
    def _():
        wgb_ref[...] = wg_ref[0].astype(BF16)
        wub_ref[...] = wu_ref[0].astype(BF16)
        wdb_ref[...] = wd_ref[0].astype(BF16)

    @pl.when(active)
    def _():
        k = wgb_ref.shape[0] // LANES
        x = _load_token_tiles(xs_ref, (), xs_ref.shape[0] // k, k).astype(BF16)
        mid = (_silu(_dot(x, wgb_ref[...])) * _dot(x, wub_ref[...])).astype(BF16)
        _store_token_tiles(ys_ref, (), _dot(mid, wdb_ref[...]))


def _moe_ffn(xs, tile_expert, n_active, wg, wu, wd):
    _, d, de = wg.shape
    tmr = MOE_ROWS * (d // LANES)
    n_rows, dp = xs.shape
    grid_spec = pltpu.PrefetchScalarGridSpec(
        num_scalar_prefetch=2,
        grid=(n_rows // tmr,),
        in_specs=[pl.BlockSpec((tmr, dp), lambda i, te, na: (jnp.minimum(i, na[0] - 1), 0)),
                  pl.BlockSpec((1, d, de), lambda i, te, na: (te[i], 0, 0)),
                  pl.BlockSpec((1, d, de), lambda i, te, na: (te[i], 0, 0)),
                  pl.BlockSpec((1, de, d), lambda i, te, na: (te[i], 0, 0))],
        out_specs=pl.BlockSpec((tmr, dp), lambda i, te, na: (jnp.minimum(i, na[0] - 1), 0)),
        scratch_shapes=[pltpu.VMEM((d, de), BF16), pltpu.VMEM((d, de), BF16), pltpu.VMEM((de, d), BF16)],
    )
    return pl.pallas_call(
        _moe_ffn_kernel,
        grid_spec=grid_spec,
        out_shape=jax.ShapeDtypeStruct((n_rows, dp), F32),
        input_output_aliases={2: 0},
        compiler_params=pltpu.CompilerParams(
            dimension_semantics=("arbitrary",), vmem_limit_bytes=VMEM_LIMIT),
        name="moe_ffn",
    )(tile_expert, n_active, xs, wg, wu, wd)


def _moe_combine_kernel(d1_ref, d2_ref, n1_ref, n2_ref, h_ref, route_ref, mod_ref, nf_ref, ys_hbm, o_ref,
                        b1_ref, b2_ref, sem, *, final_norm):
    tm = h_ref.shape[1]
    k = h_ref.shape[2] // LANES
    g = pl.program_id(0)
    slot = g % 2

    def tok(ref, t):
        return ref.at[pl.ds(pl.multiple_of(t * k, k), k)]

    def gather(i1_ref, i2_ref, s):
        def issue(t, carry):
            pltpu.make_async_copy(tok(ys_hbm, i1_ref[0, 0, t]), tok(b1_ref.at[s], t),
                                  sem.at[s, 0]).start(priority=0)
            pltpu.make_async_copy(tok(ys_hbm, i2_ref[0, 0, t]), tok(b2_ref.at[s], t),
                                  sem.at[s, 1]).start(priority=1)
            return carry

        lax.fori_loop(0, tm, issue, 0, unroll=8)

    @pl.when(g == 0)
    def _():
        gather(d1_ref, d2_ref, 0)

    @pl.when(g + 1 < pl.num_programs(0))
    def _():
        gather(n1_ref, n2_ref, 1 - slot)

    pltpu.make_async_copy(ys_hbm.at[pl.ds(0, tm * k)], b1_ref.at[slot], sem.at[slot, 0]).wait()
    pltpu.make_async_copy(ys_hbm.at[pl.ds(0, tm * k)], b2_ref.at[slot], sem.at[slot, 1]).wait()
    route = route_ref[0]
    moe = (route[:, 2:3] * _load_token_tiles(b1_ref, (slot,), tm, k)
           + route[:, 3:4] * _load_token_tiles(b2_ref, (slot,), tm, k))
    h = h_ref[0] + mod_ref[0, 5:6, :] * moe
    if final_norm:
        h = _rms(h) * nf_ref[...]
    o_ref[0] = h


def _moe_combine(ys, dest1, dest2, h, route, mod, nf, final_norm):
    bsz, seq, d = h.shape
    tm = MOE_COMBINE_TILE
    nt = seq // tm
    steps = bsz * nt
    dspec = pl.BlockSpec((1, 1, tm), lambda g: (g, 0, 0), memory_space=pltpu.SMEM)
    nspec = pl.BlockSpec((1, 1, tm), lambda g: (jnp.minimum(g + 1, steps - 1), 0, 0), memory_space=pltpu.SMEM)
    row = lambda n: pl.BlockSpec((1, tm, n), lambda g: (g // nt, g % nt, 0))
    d1 = dest1.reshape(steps, 1, tm)
    d2 = dest2.reshape(steps, 1, tm)
    return pl.pallas_call(
        functools.partial(_moe_combine_kernel, final_norm=final_norm),
        grid=(steps,),
        in_specs=[dspec, dspec, nspec, nspec, row(d), row(LANES),
                  pl.BlockSpec((1, N_MOD, d), lambda g: (g // nt, 0, 0)),
                  pl.BlockSpec(nf.shape, lambda g: (0, 0)),
                  pl.BlockSpec(memory_space=pl.ANY)],
        out_specs=row(d),
        out_shape=jax.ShapeDtypeStruct((bsz, seq, d), F32),
        scratch_shapes=[pltpu.VMEM((2, tm * (d // LANES), LANES), ys.dtype),
                        pltpu.VMEM((2, tm * (d // LANES), LANES), ys.dtype),
                        pltpu.SemaphoreType.DMA((2, 2))],
        compiler_params=pltpu.CompilerParams(
            dimension_semantics=("arbitrary",), vmem_limit_bytes=VMEM_LIMIT),
        name="moe_combine",
    )(d1, d2, d1, d2, h, route, mod, nf, ys)


def _moe_sparse(m, route, route_t, counts, wg, wu, wd, h, mod, nf, final_norm):
    bsz, seq, d = h.shape
    n_tok = bsz * seq
    tmr = MOE_ROWS
    n_exp = wg.shape[0]
    n_tiles = (2 * n_tok) // tmr + n_exp
    i32 = jnp.int32
    e1, e2, k1, k2 = (route_t[j].astype(i32) for j in (0, 1, 4, 5))
    cnt = counts[-1, 0, ROUTE_OFF:ROUTE_OFF + n_exp].astype(i32)
    ptiles = (cnt + tmr - 1) // tmr
    tend = jnp.cumsum(ptiles)
    gstart = (tend - ptiles) * tmr
    eids = jnp.arange(n_exp, dtype=i32)
    dest1 = k1 + jnp.sum(jnp.where(e1[:, None] == eids[None, :], gstart[None, :], 0), axis=1)
    dest2 = k2 + jnp.sum(jnp.where(e2[:, None] == eids[None, :], gstart[None, :], 0), axis=1)
    n_active = tend[-1:]
    tile_expert = jnp.sum(jnp.arange(n_tiles, dtype=i32)[:, None] >= tend[None, :], axis=1).astype(i32)
    last_expert = jnp.sum(n_active - 1 >= tend).astype(i32)
    tile_expert = jnp.minimum(tile_expert, last_expert)
    pads = jnp.stack([jnp.concatenate([gstart + cnt, n_active]),
                      jnp.concatenate([gstart + ptiles * tmr, jnp.full((1,), n_tiles, i32)])]).astype(i32)
    k = d // LANES
    xs = _moe_scatter(m.reshape(n_tok * k, LANES), dest1, dest2, pads, n_tiles * tmr, k)
    ys = _moe_ffn(xs, tile_expert, n_active.astype(i32), wg, wu, wd)
    return _moe_combine(ys, dest1, dest2, h, route, mod, nf, final_norm)


def kernel(x, c, norm1_g, norm2_g, w_ada, b_ada, w_in, lam_re, lam_im, log_step, s5_b_re, s5_b_im,
           s5_c_re, s5_c_im, s5_d, w_glu, b_glu, conv_w, a_log, dt_bias, gdn_norm_g, w_out,
           w_router_grp, w_router_exp, w_gate, w_up, w_down, normf_g):
    bsz, seq, d = x.shape
    depth = w_ada.shape[0]
    d_s5 = s5_d.shape[1]
    d_gdn = w_out.shape[1] - d_s5
    n_heads = d_gdn // GDN_HEAD_DIM
    assert seq % min(S5_TILE, seq) == 0 and seq % ROW_TILE == 0 and seq % GDN_BLOCK == 0
    assert seq % MOE_COMBINE_TILE == 0 and (bsz * seq) % MOE_SCATTER_TILE == 0
    assert d % LANES == 0 and d_s5 % LANES == 0 and d_gdn % GDN_HEAD_DIM == 0
    assert 2 * n_heads <= LANES and N_EXPERT_GROUPS + N_EXPERTS <= LANES
    assert w_gate.shape[1] == N_EXPERTS and w_router_grp.shape[2] == N_EXPERT_GROUPS

    h = x
    for l in range(depth):
        mod = _ada(c, w_ada[l], b_ada[l]).reshape(bsz, N_MOD, d)
        wi = w_in[l]
        c0, c1, c2 = d_s5, d_s5 + 3 * d_gdn, d_s5 + 4 * d_gdn
        u, qkv, z, ab = _inproj(h, mod, norm1_g[l].reshape(1, d), wi, c0, c1 - c0, c2 - c1)
        tabs = _s5_tables(lam_re[l], lam_im[l], log_step[l], s5_b_re[l], s5_b_im[l],
                          s5_c_re[l], s5_c_im[l])
        ys = _s5(u, *tabs, s5_d[l])
        yg = _gdn2(qkv, z, ab, conv_w[l], a_log[l], dt_bias[l], gdn_norm_g[l])
        wr = jnp.zeros((d, LANES), F32)
        wr = wr.at[:, :N_EXPERT_GROUPS].set(w_router_grp[l])
        wr = wr.at[:, ROUTE_OFF:ROUTE_OFF + N_EXPERTS].set(w_router_exp[l])
        wr_hi = wr.astype(BF16)
        wr = jnp.concatenate([wr_hi, (wr - wr_hi.astype(F32)).astype(BF16)], axis=1)
        h1, m, route, route_t, counts = _outproj(ys, yg, h, mod, w_glu[l], b_glu[l].reshape(1, d_s5),
                                                 w_out[l], norm2_g[l].reshape(1, d), wr)
        h = _moe_sparse(m, route, route_t, counts, w_gate[l], w_up[l], w_down[l], h1, mod,
                        normf_g.reshape(1, d), final_norm=(l == depth - 1))
    return h
```

```python
import functools

import jax
import jax.numpy as jnp
from jax import lax
from jax.experimental import pallas as pl
from jax.experimental.pallas import tpu as pltpu

F32 = jnp.float32
BF16 = jnp.bfloat16
HIGHEST = lax.Precision.HIGHEST
EPS = 1e-6

S5_GROUP = 16
GDN_HEAD_DIM = 128
CONV_WIDTH = 4
N_EXPERT_GROUPS = 4
EXPERTS_PER_GROUP = 8
N_EXPERTS = N_EXPERT_GROUPS * EXPERTS_PER_GROUP
N_MOD = 6

LANES = 128
SUBLANES = 8
VMEM_LIMIT = 56 * 1024 * 1024

S5_CHUNK = SUBLANES
S5_TILE = 2048
S5_BATCH_ROWS = 4
GDN_BLOCK = 256
GDN_SIDE = 64
ROW_TILE = 512
MOE_ROWS = 512
MOE_SCATTER_TILE = 1024
MOE_COMBINE_TILE = 1024
ROUTE_OFF = N_EXPERT_GROUPS


def _dot(a, b, **kw):
    return jnp.dot(a, b, preferred_element_type=F32, **kw)


def _silu(v):
    return v * jax.nn.sigmoid(v)


def _rms(v):
    return v * lax.rsqrt(jnp.mean(v * v, axis=-1, keepdims=True) + EPS)


def _store_token_tiles(ref, idx, v):
    n, width = v.shape
    k = width // LANES
    for s in range(k):
        ref[idx + (pl.ds(s, n, stride=k), slice(None))] = v[:, s * LANES:(s + 1) * LANES]


def _load_token_tiles(ref, idx, n, k):
    return jnp.concatenate([ref[idx + (pl.ds(s, n, stride=k), slice(None))] for s in range(k)], axis=1)


def _ada_kernel(c_ref, w_ref, b_ref, o_ref):
    o_ref[...] = _dot(_silu(c_ref[...]), w_ref[...], precision=HIGHEST) + b_ref[...]


def _ada(c, w, b):
    bsz, d = c.shape
    n = w.shape[1]
    return pl.pallas_call(
        _ada_kernel,
        grid=(n // d,),
        in_specs=[pl.BlockSpec((bsz, d), lambda j: (0, 0)),
                  pl.BlockSpec((d, d), lambda j: (0, j)),
                  pl.BlockSpec((1, d), lambda j: (0, j))],
        out_specs=pl.BlockSpec((bsz, d), lambda j: (0, j)),
        out_shape=jax.ShapeDtypeStruct((bsz, n), F32),
        compiler_params=pltpu.CompilerParams(vmem_limit_bytes=VMEM_LIMIT),
        name="ada",
    )(c, w, b.reshape(1, n))


def _inproj_kernel(x_ref, mod_ref, g_ref, w_ref, u_ref, qkv_ref, z_ref, ab_ref,
                   wu_ref, wqkv_ref, wz_ref, wab_ref):
    @pl.when((pl.program_id(0) == 0) & (pl.program_id(1) == 0))
    def _():
        c0 = wu_ref.shape[1]
        c1 = c0 + wqkv_ref.shape[1]
        c2 = c1 + wz_ref.shape[1]
        n_gate = w_ref.shape[1] - c2
        wu_ref[...] = w_ref[:, 0:c0].astype(BF16)
        wqkv_ref[...] = w_ref[:, c0:c1].astype(BF16)
        wz_ref[...] = w_ref[:, c1:c2].astype(BF16)
        wab_ref[...] = jnp.zeros_like(wab_ref)
        wab_ref[:, 0:n_gate] = w_ref[:, c2:c2 + n_gate].astype(BF16)

    x = x_ref[0]
    y = _rms(x) * g_ref[...]
    h = (y * (1.0 + mod_ref[0, 1:2, :]) + mod_ref[0, 0:1, :]).astype(BF16)
    u_ref[0] = _dot(h, wu_ref[...])
    qkv_ref[0] = _dot(h, wqkv_ref[...]).astype(BF16)
    z_ref[0] = _dot(h, wz_ref[...]).astype(BF16)
    ab_ref[0] = _dot(h, wab_ref[...])


def _inproj(x, mod, g, w, n_u, n_qkv, n_z):
    bsz, seq, d = x.shape
    tm = ROW_TILE
    full = lambda a: pl.BlockSpec(a.shape, lambda b, i: (0,) * a.ndim)
    row = lambda n: pl.BlockSpec((1, tm, n), lambda b, i: (b, i, 0))
    return pl.pallas_call(
        _inproj_kernel,
        grid=(bsz, seq // tm),
        in_specs=[row(d), pl.BlockSpec((1, N_MOD, d), lambda b, i: (b, 0, 0)), full(g),
                  pl.BlockSpec(w.shape, lambda b, i: (0, 0), pipeline_mode=pl.Buffered(1))],
        out_specs=[row(n_u), row(n_qkv), row(n_z), row(LANES)],
        out_shape=[jax.ShapeDtypeStruct((bsz, seq, n_u), F32),
                   jax.ShapeDtypeStruct((bsz, seq, n_qkv), BF16),
                   jax.ShapeDtypeStruct((bsz, seq, n_z), BF16),
                   jax.ShapeDtypeStruct((bsz, seq, LANES), F32)],
        scratch_shapes=[pltpu.VMEM((d, n_u), BF16), pltpu.VMEM((d, n_qkv), BF16),
                        pltpu.VMEM((d, n_z), BF16), pltpu.VMEM((d, LANES), BF16)],
        compiler_params=pltpu.CompilerParams(
            dimension_semantics=("arbitrary", "arbitrary"), vmem_limit_bytes=VMEM_LIMIT),
        name="inproj",
    )(x, mod, g, w)


def _s5_tables(lam_re, lam_im, log_step, b_re, b_im, c_re, c_im):
    ch = S5_CHUNK
    n_grp, n_st = lam_re.shape
    gpb = LANES // S5_GROUP
    nblk = n_grp // gpb
    step = jnp.exp(log_step.astype(F32))[:, None]
    lr = lam_re.astype(F32)
    li = lam_im.astype(F32)

    def lam_pow(j):
        mag = jnp.exp(j * lr * step)
        ang = j * li * step
        return mag * jnp.cos(ang), mag * jnp.sin(ang)

    ar, ai = lam_pow(1.0)
    den = lr * lr + li * li
    fr = ((ar - 1.0) * lr + ai * li) / den
    fi = (ai * lr - (ar - 1.0) * li) / den
    bbr = fr[..., None] * b_re - fi[..., None] * b_im
    bbi = fr[..., None] * b_im + fi[..., None] * b_re
    pows = [lam_pow(float(j)) for j in range(ch + 1)]
    pr = jnp.stack([p[0] for p in pows])
    pi = jnp.stack([p[1] for p in pows])
    lbr = pr[:ch, :, :, None] * bbr[None] - pi[:ch, :, :, None] * bbi[None]
    lbi = pr[:ch, :, :, None] * bbi[None] + pi[:ch, :, :, None] * bbr[None]
    kmat = (jnp.einsum('ghp,jgpk->jghk', c_re, lbr, precision=HIGHEST)
            - jnp.einsum('ghp,jgpk->jghk', c_im, lbi, precision=HIGHEST))
    kt = kmat.reshape(ch, nblk, gpb, S5_GROUP, S5_GROUP).transpose(1, 0, 3, 2, 4)
    kt = kt.reshape(nblk, ch, S5_GROUP, LANES)

    wri = jnp.stack([lbr[::-1], lbi[::-1]]).reshape(2, ch, nblk, gpb, n_st, S5_GROUP)
    wt = wri.transpose(2, 1, 0, 4, 3, 5).reshape(nblk, ch, 2 * n_st, LANES)

    pr1 = pr[1:, :, None, :]
    pi1 = pi[1:, :, None, :]
    clr = c_re[None] * pr1 - c_im[None] * pi1
    cli = c_re[None] * pi1 + c_im[None] * pr1
    wo = jnp.stack([clr, -cli]).reshape(2, ch, nblk, gpb, S5_GROUP, n_st)
    ot = wo.transpose(2, 1, 4, 0, 3, 5).reshape(nblk, ch, S5_GROUP, 2 * gpb * n_st)

    half = gpb * n_st
    amat = jnp.stack([pr[ch].reshape(nblk, half), pi[ch].reshape(nblk, half)], axis=1)
    return _s5_expand(kt, wt, ot, n_st) + (amat,)


def _s5_expand_kernel(kt_ref, wt_ref, ot_ref, t_ref, win_ref, wout_ref, *, n_st):
    ch = kt_ref.shape[1]
    grp = S5_GROUP
    tn = (((0,), (0,)), ((), ()))

    def iota(shape, dim):
        return lax.broadcasted_iota(jnp.int32, shape, dim)

    rep_h = jnp.where(iota((grp, LANES), 0) == iota((grp, LANES), 1) % grp, 1.0, 0.0).astype(BF16)
    sdim = win_ref.shape[2]
    ra = iota((2 * n_st, sdim), 0)
    cb = iota((2 * n_st, sdim), 1)
    rep_s = jnp.where((ra // n_st == cb // (sdim // 2)) & (ra % n_st == cb % n_st), 1.0, 0.0).astype(BF16)
    own_t = iota((LANES, LANES), 0) // grp == iota((LANES, LANES), 1) // grp
    own_w = iota((LANES, sdim), 0) // grp == (iota((LANES, sdim), 1) // n_st) % (LANES // grp)
    own_o = (iota((sdim, LANES), 0) // n_st) % (LANES // grp) == iota((sdim, LANES), 1) // grp

    taps = []
    for j in range(ch):
        blk = lax.dot_general(kt_ref[0, j].astype(BF16), rep_h, tn, preferred_element_type=F32)
        taps.append(jnp.where(own_t, blk, 0.0).astype(BF16))
    zero = jnp.zeros((LANES, LANES), BF16)
    for s in range(ch):
        for t in range(ch):
            t_ref[0, s * LANES:(s + 1) * LANES, t * LANES:(t + 1) * LANES] = taps[t - s] if t >= s else zero
        blk = lax.dot_general(wt_ref[0, s].astype(BF16), rep_s, tn, preferred_element_type=F32)
        win_ref[0, s * LANES:(s + 1) * LANES, :] = jnp.where(own_w, blk, 0.0).astype(BF16)
        blk = lax.dot_general(ot_ref[0, s].astype(BF16), rep_h, tn, preferred_element_type=F32)
        wout_ref[0, :, s * LANES:(s + 1) * LANES] = jnp.where(own_o, blk, 0.0).astype(BF16)


def _s5_expand(kt, wt, ot, n_st):
    nblk, ch = kt.shape[:2]
    kdim = ch * LANES
    sdim = ot.shape[3]
    spec = lambda a: pl.BlockSpec((1,) + a.shape[1:], lambda c: (c, 0, 0, 0))
    big = lambda r, cdim: pl.BlockSpec((1, r, cdim), lambda c: (c, 0, 0))
    return pl.pallas_call(
        functools.partial(_s5_expand_kernel, n_st=n_st),
        grid=(nblk,),
        in_specs=[spec(kt), spec(wt), spec(ot)],
        out_specs=[big(kdim, kdim), big(kdim, sdim), big(sdim, kdim)],
        out_shape=[jax.ShapeDtypeStruct((nblk, kdim, kdim), BF16),
                   jax.ShapeDtypeStruct((nblk, kdim, sdim), BF16),
                   jax.ShapeDtypeStruct((nblk, sdim, kdim), BF16)],
        compiler_params=pltpu.CompilerParams(
            dimension_semantics=("arbitrary",), vmem_limit_bytes=VMEM_LIMIT),
        name="s5_tables",
    )(kt, wt, ot)


def _s5_kernel(u_ref, t_ref, win_ref, wout_ref, a_ref, d_ref, y_ref, st_ref, v_ref, xp_ref):
    ch = S5_CHUNK
    nb = u_ref.shape[0]
    n = u_ref.shape[1] // ch
    half = st_ref.shape[2]

    @pl.when(pl.program_id(2) == 0)
    def _():
        st_ref[...] = jnp.zeros_like(st_ref)

    def slabs(b):
        return [u_ref[b, pl.ds(s, n, stride=ch), :] for s in range(ch)]

    for b in range(nb):
        v_ref[b] = _dot(jnp.concatenate(slabs(b), axis=1).astype(BF16), win_ref[0])
    a_r = a_ref[0, 0:1, :]
    a_i = a_ref[0, 1:2, :]

    def body(r, carry):
        out = []
        for b in range(nb):
            x_r, x_i = carry[b]
            xp_ref[b, pl.ds(r, 1), 0:half] = x_r
            xp_ref[b, pl.ds(r, 1), half:2 * half] = x_i
            v_r = v_ref[b, pl.ds(r, 1), 0:half]
            v_i = v_ref[b, pl.ds(r, 1), half:2 * half]
            out.append((a_r * x_r - a_i * x_i + v_r, a_r * x_i + a_i * x_r + v_i))
        return tuple(out)

    init = tuple((st_ref[b, 0:1, :], st_ref[b, 1:2, :]) for b in range(nb))
    last = lax.fori_loop(0, n, body, init, unroll=4)
    d = d_ref[0]
    for b in range(nb):
        st_ref[b, 0:1, :] = last[b][0]
        st_ref[b, 1:2, :] = last[b][1]
        sl = slabs(b)
        y = (_dot(jnp.concatenate(sl, axis=1).astype(BF16), t_ref[0])
             + _dot(xp_ref[b].astype(BF16), wout_ref[0]))
        for t in range(ch):
            y_ref[b, pl.ds(t, n, stride=ch), :] = y[:, t * LANES:(t + 1) * LANES] + d * sl[t]


def _s5(u, tmat, win, wout, amat, d_skip):
    bsz, seq, dch = u.shape
    nblk = dch // LANES
    tt = min(S5_TILE, seq)
    n = tt // S5_CHUNK
    sdim = win.shape[2]
    nb = next(k for k in (S5_BATCH_ROWS, 2, 1) if bsz % k == 0)
    wspec = lambda a: pl.BlockSpec((1,) + a.shape[1:], lambda b, c, i: (c, 0, 0))
    return pl.pallas_call(
        _s5_kernel,
        grid=(bsz // nb, nblk, seq // tt),
        in_specs=[pl.BlockSpec((nb, tt, LANES), lambda b, c, i: (b, i, c)),
                  wspec(tmat), wspec(win), wspec(wout), wspec(amat),
                  pl.BlockSpec((1, 1, LANES), lambda b, c, i: (c, 0, 0))],
        out_specs=pl.BlockSpec((nb, tt, LANES), lambda b, c, i: (b, i, c)),
        out_shape=jax.ShapeDtypeStruct((bsz, seq, dch), F32),
        scratch_shapes=[pltpu.VMEM((nb, 2, sdim // 2), F32),
                        pltpu.VMEM((nb, n, sdim), F32),
                        pltpu.VMEM((nb, n, sdim), F32)],
        compiler_params=pltpu.CompilerParams(
            dimension_semantics=("parallel", "parallel", "arbitrary"),
            vmem_limit_bytes=VMEM_LIMIT),
        name="s5",
    )(u, tmat, win, wout, amat, d_skip.reshape(nblk, 1, LANES))


def _cumsum_rows(v):
    n = v.shape[0]
    row = lax.broadcasted_iota(jnp.int32, v.shape, 0)
    sh = 1
    while sh < n:
        v = v + jnp.where(row >= sh, pltpu.roll(v, sh, axis=0), 0.0)
        sh *= 2
    return v


def _gdn2_kernel(qkv_ref, z_ref, ab_ref, cw_ref, gp_ref, ng_ref, o_ref,
                 s_ref, xp_ref, gc_ref, gct_ref, kbq_ref, kn_ref, rhs_ref, cq_ref, qkt_ref,
                 lm_ref, d_ref, db_ref, de_ref, ds_ref, pe_ref, val_ref, vn_ref, *, n_heads):
    tt = qkv_ref.shape[1]
    dh = GDN_HEAD_DIM
    bsz = qkv_ref.shape[0]
    dg = n_heads * dh
    halo = SUBLANES
    sb = ds_ref.shape[1]
    nsb = tt // sb
    pairs = [(b, h) for b in range(bsz) for h in range(n_heads)]

    @pl.when(pl.program_id(0) == 0)
    def _():
        s_ref[...] = jnp.zeros_like(s_ref)
        xp_ref[...] = jnp.zeros_like(xp_ref)

    ri = lax.broadcasted_iota(jnp.int32, (tt, tt), 0)
    ci = lax.broadcasted_iota(jnp.int32, (tt, tt), 1)
    lane = lax.broadcasted_iota(jnp.int32, (tt, LANES), 1)
    a_log = gp_ref[0:1, :]
    dt_bias = gp_ref[1:2, :]
    shift_op = jnp.concatenate(
        [jnp.where(ri == ci + j, 1.0, 0.0) for j in range(1, CONV_WIDTH)], axis=0).astype(BF16)

    for b in range(bsz):
        x = qkv_ref[b]
        x32 = x.astype(F32)
        shifted = _dot(shift_op, x)
        edge = jnp.concatenate([xp_ref[b], x32[0:halo]], axis=0)
        conv = cw_ref[CONV_WIDTH - 1:CONV_WIDTH, :] * x32
        for j in range(1, CONV_WIDTH):
            sh_j = jnp.concatenate([edge[halo - j:2 * halo - j],
                                    shifted[(j - 1) * tt + halo:j * tt]], axis=0)
            conv = conv + cw_ref[CONV_WIDTH - 1 - j:CONV_WIDTH - j, :] * sh_j
        xp_ref[b] = x32[tt - halo:tt]
        act = _silu(conv)

        ab = ab_ref[b]
        sp = jnp.maximum(ab + dt_bias, 0.0) + jnp.log1p(jnp.exp(-jnp.abs(ab + dt_bias)))
        g = jnp.where(lane < n_heads, -jnp.exp(a_log) * sp, 0.0)
        gc = _cumsum_rows(g)
        gc_ref[b] = gc
        gct_ref[b] = gc.T
        glast = gc[tt - 1:tt, :]
        egc = jnp.exp(gc)
        ekt = jnp.exp(glast - gc)
        beta_all = jax.nn.sigmoid(ab)
        for h in range(n_heads):
            p = b * n_heads + h
            q = act[:, h * dh:(h + 1) * dh]
            k = act[:, dg + h * dh:dg + (h + 1) * dh]
            v = act[:, 2 * dg + h * dh:2 * dg + (h + 1) * dh]
            q = q * lax.rsqrt(jnp.sum(q * q, axis=-1, keepdims=True) + EPS) * (dh ** -0.5)
            k = k * lax.rsqrt(jnp.sum(k * k, axis=-1, keepdims=True) + EPS)
            beta = beta_all[:, n_heads + h:n_heads + h + 1]
            eg = egc[:, h:h + 1]
            kb = k * beta
            kbq_ref[p, 0:tt, :] = kb.astype(BF16)
            kbq_ref[p, tt:2 * tt, :] = q.astype(BF16)
            kn_ref[p] = k.astype(BF16)
            rhs_ref[p, :, 0:dh] = (v * beta).astype(BF16)
            rhs_ref[p, :, dh:2 * dh] = (kb * eg).astype(BF16)
            cq_ref[p, tt:2 * tt, :] = (q * eg).astype(BF16)
            qkt_ref[p, tt:tt + dh, :] = (k * ekt[:, h:h + 1]).T.astype(BF16)

    for b, h in pairs:
        p = b * n_heads + h
        m1 = lax.dot_general(kbq_ref[p], kn_ref[p], (((1,), (1,)), ((), ())),
                             preferred_element_type=F32)
        gcol = gc_ref[b, :, h:h + 1]
        grow = gct_ref[b, h:h + 1, :]
        decay = jnp.exp(jnp.where(ri >= ci, gcol - grow, -1e30))
        lm = jnp.where(ri > ci, m1[0:tt] * decay, 0.0)
        lm_ref[p] = lm.astype(BF16)
        qkt_ref[p, 0:tt, :] = (m1[tt:2 * tt] * decay).astype(BF16)
        first = jnp.where(ri == ci + 1, jnp.where((ci & 1) == 0, lm, 0.0), 0.0)
        d = jnp.where(ri == ci, 1.0, 0.0) - first
        ds_ref[p] = sum(d[j * sb:(j + 1) * sb] for j in range(1, nsb)) + d[0:sb]

    in_blk = (ri // sb) == (ci // sb)
    blk_b = jnp.where(in_blk, 1.0, 0.0).astype(BF16)
    m = 2
    sh = 1
    while m < tt:
        rb = ri >> sh
        cb = ci >> sh
        sel = jnp.where(rb == cb + 1, jnp.where((cb & 1) == 0, 1.0, 0.0), 0.0).astype(BF16)
        if 2 * m <= sb:
            for b, h in pairs:
                p = b * n_heads + h
                pe_ref[p] = _dot(ds_ref[p].astype(BF16), lm_ref[p] * sel).astype(BF16)
            for b, h in pairs:
                p = b * n_heads + h
                ds = ds_ref[p]
                dbd = jnp.concatenate([ds.astype(BF16)] * nsb, axis=0) * blk_b
                ds = ds - _dot(pe_ref[p], dbd)
                ds_ref[p] = ds
                if 4 * m > sb:
                    d = jnp.where(in_blk, jnp.concatenate([ds] * nsb, axis=0), 0.0)
                    d_ref[p] = d
                    db_ref[p] = d.astype(BF16)
        else:
            for b, h in pairs:
                p = b * n_heads + h
                de_ref[p] = _dot(db_ref[p], lm_ref[p] * sel).astype(BF16)
            for b, h in pairs:
                p = b * n_heads + h
                d = d_ref[p] - _dot(de_ref[p], db_ref[p])
                d_ref[p] = d
                db_ref[p] = d.astype(BF16)
        m *= 2
        sh += 1

    for b, h in pairs:
        p = b * n_heads + h
        w = _dot(db_ref[p], rhs_ref[p])
        val_ref[p] = w[:, 0:dh]
        cq_ref[p, 0:tt, :] = w[:, dh:2 * dh].astype(BF16)

    for b, h in pairs:
        p = b * n_heads + h
        m2 = _dot(cq_ref[p], s_ref[p].astype(BF16))
        vn_ref[p] = (val_ref[p] - m2[0:tt]).astype(BF16)
        val_ref[p] = m2[tt:2 * tt]
    for b, h in pairs:
        p = b * n_heads + h
        r = _dot(qkt_ref[p], vn_ref[p])
        o = val_ref[p] + r[0:tt]
        egl = jnp.exp(gc_ref[b, tt - 1:tt, h:h + 1])
        s_ref[p] = s_ref[p] * egl + r[tt:tt + dh]
        zh = z_ref[b, :, h * dh:(h + 1) * dh].astype(F32)
        o_ref[b, :, h * dh:(h + 1) * dh] = (_rms(o) * ng_ref[...] * _silu(zh)).astype(o_ref.dtype)


def _gdn2(qkv, z, ab, conv_w, a_log, dt_bias, norm_g):
    bsz, seq, _ = qkv.shape
    dg = z.shape[2]
    dh = GDN_HEAD_DIM
    n_heads = dg // dh
    tt = GDN_BLOCK
    npair = bsz * n_heads
    gp = jnp.zeros((2, LANES), F32).at[0, :n_heads].set(a_log).at[1, :n_heads].set(dt_bias)
    blk = lambda n: pl.BlockSpec((bsz, tt, n), lambda i: (0, i, 0))
    full = lambda a: pl.BlockSpec(a.shape, lambda i: (0,) * a.ndim)
    ng = norm_g.reshape(1, dh)
    return pl.pallas_call(
        functools.partial(_gdn2_kernel, n_heads=n_heads),
        grid=(seq // tt,),
        in_specs=[blk(3 * dg), blk(dg), blk(LANES), full(conv_w), full(gp), full(ng)],
        out_specs=blk(dg),
        out_shape=jax.ShapeDtypeStruct((bsz, seq, dg), BF16),
        scratch_shapes=[pltpu.VMEM((npair, dh, dh), F32),
                        pltpu.VMEM((bsz, SUBLANES, 3 * dg), F32),
                        pltpu.VMEM((bsz, tt, LANES), F32),
                        pltpu.VMEM((bsz, LANES, tt), F32),
                        pltpu.VMEM((npair, 2 * tt, dh), BF16),
                        pltpu.VMEM((npair, tt, dh), BF16),
                        pltpu.VMEM((npair, tt, 2 * dh), BF16),
                        pltpu.VMEM((npair, 2 * tt, dh), BF16),
                        pltpu.VMEM((npair, tt + dh, tt), BF16),
                        pltpu.VMEM((npair, tt, tt), BF16),
                        pltpu.VMEM((npair, tt, tt), F32),
                        pltpu.VMEM((npair, tt, tt), BF16),
                        pltpu.VMEM((npair, tt, tt), BF16),
                        pltpu.VMEM((npair, GDN_SIDE, tt), F32),
                        pltpu.VMEM((npair, GDN_SIDE, tt), BF16),
                        pltpu.VMEM((npair, tt, dh), F32),
                        pltpu.VMEM((npair, tt, dh), BF16)],
        compiler_params=pltpu.CompilerParams(
            dimension_semantics=("arbitrary",), vmem_limit_bytes=VMEM_LIMIT),
        name="gdn",
    )(qkv, z, ab, conv_w, gp, ng)


def _outproj_kernel(ys_ref, yg_ref, x_ref, mod_ref, wglu_ref, bglu_ref, wo_ref,
                    g2_ref, wr_ref, h_ref, m_ref, route_ref, routet_ref, count_ref,
                    cnt_ref, wglub_ref, wob_ref):
    @pl.when((pl.program_id(0) == 0) & (pl.program_id(1) == 0))
    def _():
        wglub_ref[...] = wglu_ref[...].astype(BF16)
        wob_ref[...] = wo_ref[...].astype(BF16)

    d_s5 = wglub_ref.shape[0]
    y = jax.nn.gelu(ys_ref[0])
    gate = jax.nn.sigmoid(_dot(y.astype(BF16), wglub_ref[...]) + bglu_ref[...])
    y = (y * gate).astype(BF16)
    mix = _dot(y, wob_ref[0:d_s5, :]) + _dot(yg_ref[0], wob_ref[d_s5:, :])
    h = x_ref[0] + mod_ref[0, 2:3, :] * mix
    h_ref[0] = h
    m = _rms(h) * g2_ref[...] * (1.0 + mod_ref[0, 4:5, :]) + mod_ref[0, 3:4, :]
    _store_token_tiles(m_ref, (0,), m)

    m_hi = m.astype(BF16)
    m_lo = (m - m_hi.astype(F32)).astype(BF16)
    part = _dot(m_hi, wr_ref[...])
    logits = part[:, 0:LANES] + part[:, LANES:2 * LANES] + _dot(m_lo, wr_ref[:, 0:LANES])
    lane = lax.broadcasted_iota(jnp.int32, logits.shape, 1)
    neg = -1e30
    is_grp = lane < N_EXPERT_GROUPS
    gl = jnp.where(is_grp, logits, neg)
    gmax = jnp.max(gl, axis=-1, keepdims=True)
    gidx = jnp.min(jnp.where(gl == gmax, lane, LANES), axis=-1, keepdims=True)
    p_grp = 1.0 / jnp.sum(jnp.where(is_grp, jnp.exp(gl - gmax), 0.0), axis=-1, keepdims=True)
    lo = ROUTE_OFF + gidx * EXPERTS_PER_GROUP
    el = jnp.where((lane >= lo) & (lane < lo + EXPERTS_PER_GROUP), logits, neg)
    v1 = jnp.max(el, axis=-1, keepdims=True)
    i1 = jnp.min(jnp.where(el == v1, lane, LANES), axis=-1, keepdims=True)
    el2 = jnp.where(lane == i1, neg, el)
    v2 = jnp.max(el2, axis=-1, keepdims=True)
    i2 = jnp.min(jnp.where(el2 == v2, lane, LANES), axis=-1, keepdims=True)
    t = jnp.exp(v2 - v1)
    w1 = p_grp / (1.0 + t)
    w2 = w1 * t

    @pl.when((pl.program_id(0) == 0) & (pl.program_id(1) == 0))
    def _():
        cnt_ref[...] = jnp.zeros_like(cnt_ref)

    tm = logits.shape[0]
    oh1 = lane == i1
    oh2 = lane == i2
    cnt = jnp.where(oh1, 1.0, 0.0) + jnp.where(oh2, 1.0, 0.0)
    earlier = (lax.broadcasted_iota(jnp.int32, (tm, tm), 0) > lax.broadcasted_iota(jnp.int32, (tm, tm), 1))
    before = _dot(jnp.where(earlier, 1.0, 0.0).astype(BF16), cnt.astype(BF16)) + cnt_ref[...]
    r1 = jnp.sum(jnp.where(oh1, before, 0.0), axis=-1, keepdims=True)
    r2 = jnp.sum(jnp.where(oh2, before, 0.0), axis=-1, keepdims=True)
    cnt_ref[...] += jnp.sum(cnt, axis=0, keepdims=True)
    count_ref[0] = cnt_ref[...]
    cols = [(i1 - ROUTE_OFF).astype(F32), (i2 - ROUTE_OFF).astype(F32), w1, w2, r1, r2]
    route = jnp.zeros_like(logits)
    for j, col in enumerate(cols):
        route = jnp.where(lane == j, col, route)
    route_ref[0] = route
    routet_ref[...] = route.T[0:SUBLANES, :]


def _outproj(ys, yg, x, mod, wglu, bglu, wo, g2, wr):
    bsz, seq, d = x.shape
    tm = ROW_TILE
    full = lambda a: pl.BlockSpec(a.shape, lambda b, i: (0,) * a.ndim, pipeline_mode=pl.Buffered(1))
    row = lambda n: pl.BlockSpec((1, tm, n), lambda b, i: (b, i, 0))
    return pl.pallas_call(
        _outproj_kernel,
        grid=(bsz, seq // tm),
        in_specs=[row(ys.shape[2]), row(yg.shape[2]), row(d),
                  pl.BlockSpec((1, N_MOD, d), lambda b, i: (b, 0, 0)),
                  full(wglu), full(bglu), full(wo), full(g2), full(wr)],
        out_specs=[row(d), pl.BlockSpec((1, tm * (d // LANES), LANES), lambda b, i: (b, i, 0)), row(LANES),
                   pl.BlockSpec((SUBLANES, tm), lambda b, i: (0, b * (seq // tm) + i)),
                   pl.BlockSpec((1, 1, LANES), lambda b, i: (b * (seq // tm) + i, 0, 0))],
        out_shape=[jax.ShapeDtypeStruct((bsz, seq, d), F32),
                   jax.ShapeDtypeStruct((bsz, seq * (d // LANES), LANES), F32),
                   jax.ShapeDtypeStruct((bsz, seq, LANES), F32),
                   jax.ShapeDtypeStruct((SUBLANES, bsz * seq), F32),
                   jax.ShapeDtypeStruct((bsz * (seq // tm), 1, LANES), F32)],
        scratch_shapes=[pltpu.VMEM((1, LANES), F32), pltpu.VMEM(wglu.shape, BF16), pltpu.VMEM(wo.shape, BF16)],
        compiler_params=pltpu.CompilerParams(
            dimension_semantics=("arbitrary", "arbitrary"), vmem_limit_bytes=VMEM_LIMIT),
        name="outproj",
    )(ys, yg, x, mod, wglu, bglu, wo, g2, wr)


def _moe_scatter_kernel(d1_ref, d2_ref, pad_ref, m_ref, xs_hbm, zero_ref, sem, *, n_exp, k):
    i = pl.program_id(0)
    ts = d1_ref.shape[2]
    trows = zero_ref.shape[0]

    def tok(ref, t):
        return ref.at[pl.ds(pl.multiple_of(t * k, k), k)]

    def tile_copy(j):
        return pltpu.make_async_copy(
            zero_ref, xs_hbm.at[pl.ds(pl.multiple_of(j * trows, trows), trows)], sem.at[0])

    def pad_piece(e, p, act):
        lo = pad_ref[0, e]
        length = pad_ref[1, e] - lo
        start = lo + (length & ~(2 * p - 1))
        copy = pltpu.make_async_copy(
            zero_ref.at[pl.ds(0, p * k)], xs_hbm.at[pl.ds(pl.multiple_of(start * k, k), p * k)], sem.at[0])

        @pl.when((length & p) != 0)
        def _():
            act(copy)

    @pl.when(i == 0)
    def _():
        zero_ref[...] = jnp.zeros_like(zero_ref)
        pieces = [(e, p) for e in range(n_exp) for p in [1 << j for j in range((trows // k).bit_length() - 1)]]
        for e, p in pieces:
            pad_piece(e, p, lambda c: c.start())
        for e, p in pieces:
            pad_piece(e, p, lambda c: c.wait())

        def fill(j, carry):
            tile_copy(j).start()
            return carry

        def drain(j, carry):
            tile_copy(j).wait()
            return carry

        lax.fori_loop(pad_ref[0, n_exp], pad_ref[1, n_exp], fill, 0)
        lax.fori_loop(pad_ref[0, n_exp], pad_ref[1, n_exp], drain, 0)

    def row_copy(t, dst, slot):
        return pltpu.make_async_copy(tok(m_ref, t), tok(xs_hbm, dst), sem.at[1 + slot])

    def issue(t, carry):
        row_copy(t, d1_ref[0, 0, t], 0).start(priority=0)
        row_copy(t, d2_ref[0, 0, t], 1).start(priority=1)
        return carry

    lax.fori_loop(0, ts, issue, 0, unroll=8)
    for slot in range(2):
        pltpu.make_async_copy(m_ref, xs_hbm.at[pl.ds(0, ts * k)], sem.at[1 + slot]).wait()


def _moe_scatter(m, dest1, dest2, pads, n_rows, k):
    n_tok = m.shape[0] // k
    ts = MOE_SCATTER_TILE
    n_exp = pads.shape[1] - 1
    dspec = pl.BlockSpec((1, 1, ts), lambda i: (i, 0, 0), memory_space=pltpu.SMEM)
    return pl.pallas_call(
        functools.partial(_moe_scatter_kernel, n_exp=n_exp, k=k),
        grid=(n_tok // ts,),
        in_specs=[dspec, dspec,
                  pl.BlockSpec(pads.shape, lambda i: (0, 0), memory_space=pltpu.SMEM),
                  pl.BlockSpec((ts * k, LANES), lambda i: (i, 0))],
        out_specs=pl.BlockSpec(memory_space=pl.ANY),
        out_shape=jax.ShapeDtypeStruct((n_rows * k, LANES), m.dtype),
        scratch_shapes=[pltpu.VMEM((MOE_ROWS * k, LANES), m.dtype), pltpu.SemaphoreType.DMA((3,))],
        compiler_params=pltpu.CompilerParams(
            dimension_semantics=("arbitrary",), vmem_limit_bytes=VMEM_LIMIT, has_side_effects=True),
        name="moe_scatter",
    )(dest1.reshape(n_tok // ts, 1, ts), dest2.reshape(n_tok // ts, 1, ts), pads, m)


def _moe_ffn_kernel(te_ref, na_ref, xs_ref, wg_ref, wu_ref, wd_ref, ys_ref, wgb_ref, wub_ref, wdb_ref):
    i = pl.program_id(0)
    active = i < na_ref[0]
    changed = (i == 0) | (te_ref[i] != te_ref[jnp.maximum(i - 1, 0)])

    @pl.when(active & changed)
    def _():
        wgb_ref[...] = wg_ref[0].astype(BF16)
        wub_ref[...] = wu_ref[0].astype(BF16)
        wdb_ref[...] = wd_ref[0].astype(BF16)

    @pl.when(active)
    def _():
        k = wgb_ref.shape[0] // LANES
        x = _load_token_tiles(xs_ref, (), xs_ref.shape[0] // k, k).astype(BF16)
        mid = (_silu(_dot(x, wgb_ref[...])) * _dot(x, wub_ref[...])).astype(BF16)
        _store_token_tiles(ys_ref, (), _dot(mid, wdb_ref[...]))


def _moe_ffn(xs, tile_expert, n_active, wg, wu, wd):
    _, d, de = wg.shape
    tmr = MOE_ROWS * (d // LANES)
    n_rows, dp = xs.shape
    grid_spec = pltpu.PrefetchScalarGridSpec(
        num_scalar_prefetch=2,
        grid=(n_rows // tmr,),
        in_specs=[pl.BlockSpec((tmr, dp), lambda i, te, na: (jnp.minimum(i, na[0] - 1), 0)),
                  pl.BlockSpec((1, d, de), lambda i, te, na: (te[i], 0, 0)),
                  pl.BlockSpec((1, d, de), lambda i, te, na: (te[i], 0, 0)),
                  pl.BlockSpec((1, de, d), lambda i, te, na: (te[i], 0, 0))],
        out_specs=pl.BlockSpec((tmr, dp), lambda i, te, na: (jnp.minimum(i, na[0] - 1), 0)),
        scratch_shapes=[pltpu.VMEM((d, de), BF16), pltpu.VMEM((d, de), BF16), pltpu.VMEM((de, d), BF16)],
    )
    return pl.pallas_call(
        _moe_ffn_kernel,
        grid_spec=grid_spec,
        out_shape=jax.ShapeDtypeStruct((n_rows, dp), F32),
        input_output_aliases={2: 0},
        compiler_params=pltpu.CompilerParams(
            dimension_semantics=("arbitrary",), vmem_limit_bytes=VMEM_LIMIT),
        name="moe_ffn",
    )(tile_expert, n_active, xs, wg, wu, wd)


def _moe_combine_kernel(d1_ref, d2_ref, n1_ref, n2_ref, h_ref, route_ref, mod_ref, nf_ref, ys_hbm, o_ref,
                        b1_ref, b2_ref, sem, *, final_norm):
    tm = h_ref.shape[1]
    k = h_ref.shape[2] // LANES
    g = pl.program_id(0)
    slot = g % 2

    def tok(ref, t):
        return ref.at[pl.ds(pl.multiple_of(t * k, k), k)]

    def gather(i1_ref, i2_ref, s):
        def issue(t, carry):
            pltpu.make_async_copy(tok(ys_hbm, i1_ref[0, 0, t]), tok(b1_ref.at[s], t),
                                  sem.at[s, 0]).start(priority=0)
            pltpu.make_async_copy(tok(ys_hbm, i2_ref[0, 0, t]), tok(b2_ref.at[s], t),
                                  sem.at[s, 1]).start(priority=1)
            return carry

        lax.fori_loop(0, tm, issue, 0, unroll=8)

    @pl.when(g == 0)
    def _():
        gather(d1_ref, d2_ref, 0)

    @pl.when(g + 1 < pl.num_programs(0))
    def _():
        gather(n1_ref, n2_ref, 1 - slot)

    pltpu.make_async_copy(ys_hbm.at[pl.ds(0, tm * k)], b1_ref.at[slot], sem.at[slot, 0]).wait()
    pltpu.make_async_copy(ys_hbm.at[pl.ds(0, tm * k)], b2_ref.at[slot], sem.at[slot, 1]).wait()
    route = route_ref[0]
    moe = (route[:, 2:3] * _load_token_tiles(b1_ref, (slot,), tm, k)
           + route[:, 3:4] * _load_token_tiles(b2_ref, (slot,), tm, k))
    h = h_ref[0] + mod_ref[0, 5:6, :] * moe
    if final_norm:
        h = _rms(h) * nf_ref[...]
    o_ref[0] = h


def _moe_combine(ys, dest1, dest2, h, route, mod, nf, final_norm):
    bsz, seq, d = h.shape
    tm = MOE_COMBINE_TILE
    nt = seq // tm
    steps = bsz * nt
    dspec = pl.BlockSpec((1, 1, tm), lambda g: (g, 0, 0), memory_space=pltpu.SMEM)
    nspec = pl.BlockSpec((1, 1, tm), lambda g: (jnp.minimum(g + 1, steps - 1), 0, 0), memory_space=pltpu.SMEM)
    row = lambda n: pl.BlockSpec((1, tm, n), lambda g: (g // nt, g % nt, 0))
    d1 = dest1.reshape(steps, 1, tm)
    d2 = dest2.reshape(steps, 1, tm)
    return pl.pallas_call(
        functools.partial(_moe_combine_kernel, final_norm=final_norm),
        grid=(steps,),
        in_specs=[dspec, dspec, nspec, nspec, row(d), row(LANES),
                  pl.BlockSpec((1, N_MOD, d), lambda g: (g // nt, 0, 0)),
                  pl.BlockSpec(nf.shape, lambda g: (0, 0)),
                  pl.BlockSpec(memory_space=pl.ANY)],
        out_specs=row(d),
        out_shape=jax.ShapeDtypeStruct((bsz, seq, d), F32),
        scratch_shapes=[pltpu.VMEM((2, tm * (d // LANES), LANES), ys.dtype),
                        pltpu.VMEM((2, tm * (d // LANES), LANES), ys.dtype),
                        pltpu.SemaphoreType.DMA((2, 2))],
        compiler_params=pltpu.CompilerParams(
            dimension_semantics=("arbitrary",), vmem_limit_bytes=VMEM_LIMIT),
        name="moe_combine",
    )(d1, d2, d1, d2, h, route, mod, nf, ys)


def _moe_sparse(m, route, route_t, counts, wg, wu, wd, h, mod, nf, final_norm):
    bsz, seq, d = h.shape
    n_tok = bsz * seq
    tmr = MOE_ROWS
    n_exp = wg.shape[0]
    n_tiles = (2 * n_tok) // tmr + n_exp
    i32 = jnp.int32
    e1, e2, k1, k2 = (route_t[j].astype(i32) for j in (0, 1, 4, 5))
    cnt = counts[-1, 0, ROUTE_OFF:ROUTE_OFF + n_exp].astype(i32)
    ptiles = (cnt + tmr - 1) // tmr
    tend = jnp.cumsum(ptiles)
    gstart = (tend - ptiles) * tmr
    eids = jnp.arange(n_exp, dtype=i32)
    dest1 = k1 + jnp.sum(jnp.where(e1[:, None] == eids[None, :], gstart[None, :], 0), axis=1)
    dest2 = k2 + jnp.sum(jnp.where(e2[:, None] == eids[None, :], gstart[None, :], 0), axis=1)
    n_active = tend[-1:]
    tile_expert = jnp.sum(jnp.arange(n_tiles, dtype=i32)[:, None] >= tend[None, :], axis=1).astype(i32)
    last_expert = jnp.sum(n_active - 1 >= tend).astype(i32)
    tile_expert = jnp.minimum(tile_expert, last_expert)
    pads = jnp.stack([jnp.concatenate([gstart + cnt, n_active]),
                      jnp.concatenate([gstart + ptiles * tmr, jnp.full((1,), n_tiles, i32)])]).astype(i32)
    k = d // LANES
    xs = _moe_scatter(m.reshape(n_tok * k, LANES), dest1, dest2, pads, n_tiles * tmr, k)
    ys = _moe_ffn(xs, tile_expert, n_active.astype(i32), wg, wu, wd)
    return _moe_combine(ys, dest1, dest2, h, route, mod, nf, final_norm)


def kernel(x, c, norm1_g, norm2_g, w_ada, b_ada, w_in, lam_re, lam_im, log_step, s5_b_re, s5_b_im,
           s5_c_re, s5_c_im, s5_d, w_glu, b_glu, conv_w, a_log, dt_bias, gdn_norm_g, w_out,
           w_router_grp, w_router_exp, w_gate, w_up, w_down, normf_g):
    bsz, seq, d = x.shape
    depth = w_ada.shape[0]
    d_s5 = s5_d.shape[1]
    d_gdn = w_out.shape[1] - d_s5
    n_heads = d_gdn // GDN_HEAD_DIM
    assert seq % min(S5_TILE, seq) == 0 and seq % ROW_TILE == 0 and seq % GDN_BLOCK == 0
    assert seq % MOE_COMBINE_TILE == 0 and (bsz * seq) % MOE_SCATTER_TILE == 0
    assert d % LANES == 0 and d_s5 % LANES == 0 and d_gdn % GDN_HEAD_DIM == 0
    assert 2 * n_heads <= LANES and N_EXPERT_GROUPS + N_EXPERTS <= LANES
    assert w_gate.shape[1] == N_EXPERTS and w_router_grp.shape[2] == N_EXPERT_GROUPS

    h = x
    for l in range(depth):
        mod = _ada(c, w_ada[l], b_ada[l]).reshape(bsz, N_MOD, d)
        wi = w_in[l]
        c0, c1, c2 = d_s5, d_s5 + 3 * d_gdn, d_s5 + 4 * d_gdn
        u, qkv, z, ab = _inproj(h, mod, norm1_g[l].reshape(1, d), wi, c0, c1 - c0, c2 - c1)
        tabs = _s5_tables(lam_re[l], lam_im[l], log_step[l], s5_b_re[l], s5_b_im[l],
                          s5_c_re[l], s5_c_im[l])
        ys = _s5(u, *tabs, s5_d[l])
        yg = _gdn2(qkv, z, ab, conv_w[l], a_log[l], dt_bias[l], gdn_norm_g[l])
        wr = jnp.zeros((d, LANES), F32)
        wr = wr.at[:, :N_EXPERT_GROUPS].set(w_router_grp[l])
        wr = wr.at[:, ROUTE_OFF:ROUTE_OFF + N_EXPERTS].set(w_router_exp[l])
        wr_hi = wr.astype(BF16)
        wr = jnp.concatenate([wr_hi, (wr - wr_hi.astype(F32)).astype(BF16)], axis=1)
        h1, m, route, route_t, counts = _outproj(ys, yg, h, mod, w_glu[l], b_glu[l].reshape(1, d_s5),
                                                 w_out[l], norm2_g[l].reshape(1, d), wr)
        h = _moe_sparse(m, route, route_t, counts, w_gate[l], w_up[l], w_down[l], h1, mod,
                        normf_g.reshape(1, d), final_norm=(l == depth - 1))
    return h
```

```python
import functools

import jax
import jax.numpy as jnp
from jax import lax
from jax.experimental import pallas as pl
from jax.experimental.pallas import tpu as pltpu

F32 = jnp.float32
BF16 = jnp.bfloat16
HIGHEST = lax.Precision.HIGHEST
EPS = 1e-6

S5_GROUP = 16
GDN_HEAD_DIM = 128
CONV_WIDTH = 4
N_EXPERT_GROUPS = 4
EXPERTS_PER_GROUP = 8
N_EXPERTS = N_EXPERT_GROUPS * EXPERTS_PER_GROUP
N_MOD = 6

LANES = 128
SUBLANES = 8
VMEM_LIMIT = 56 * 1024 * 1024

S5_CHUNK = SUBLANES
S5_TILE = 2048
S5_BATCH_ROWS = 4
GDN_BLOCK = 256
GDN_SIDE = 64
ROW_TILE = 512
OUTPROJ_PARTS = 1
MOE_ROWS = 512
MOE_SCATTER_TILE = 1024
MOE_COMBINE_TILE = 1024
ROUTE_OFF = N_EXPERT_GROUPS


def _dot(a, b, **kw):
    return jnp.dot(a, b, preferred_element_type=F32, **kw)


def _silu(v):
    return v * jax.nn.sigmoid(v)


def _rms(v):
    return v * lax.rsqrt(jnp.mean(v * v, axis=-1, keepdims=True) + EPS)


def _store_token_tiles(ref, idx, v, row0=0):
    n, width = v.shape
    k = width // LANES
    for s in range(k):
        ref[idx + (pl.ds(row0 * k + s, n, stride=k), slice(None))] = v[:, s * LANES:(s + 1) * LANES]


def _load_token_tiles(ref, idx, n, k):
    return jnp.concatenate([ref[idx + (pl.ds(s, n, stride=k), slice(None))] for s in range(k)], axis=1)


def _ada_kernel(c_ref, w_ref, b_ref, o_ref):
    o_ref[...] = _dot(_silu(c_ref[...]), w_ref[...], precision=HIGHEST) + b_ref[...]


def _ada(c, w, b):
    bsz, d = c.shape
    n = w.shape[1]
    cols = d
    return pl.pallas_call(
        _ada_kernel,
        grid=(n // cols,),
        in_specs=[pl.BlockSpec((bsz, d), lambda j: (0, 0)),
                  pl.BlockSpec((d, cols), lambda j: (0, j)),
                  pl.BlockSpec((1, cols), lambda j: (0, j))],
        out_specs=pl.BlockSpec((bsz, cols), lambda j: (0, j)),
        out_shape=jax.ShapeDtypeStruct((bsz, n), F32),
        compiler_params=pltpu.CompilerParams(vmem_limit_bytes=VMEM_LIMIT),
        name="ada",
    )(c, w, b.reshape(1, n))


def _inproj_kernel(x_ref, mod_ref, g_ref, w_ref, u_ref, qkv_ref, z_ref, ab_ref,
                   wu_ref, wqkv_ref, wz_ref, wab_ref):
    @pl.when((pl.program_id(0) == 0) & (pl.program_id(1) == 0))
    def _():
        c0 = wu_ref.shape[1]
        c1 = c0 + wqkv_ref.shape[1]
        c2 = c1 + wz_ref.shape[1]
        n_gate = w_ref.shape[1] - c2
        wu_ref[...] = w_ref[:, 0:c0].astype(BF16)
        wqkv_ref[...] = w_ref[:, c0:c1].astype(BF16)
        wz_ref[...] = w_ref[:, c1:c2].astype(BF16)
        wab_ref[...] = jnp.zeros_like(wab_ref)
        wab_ref[:, 0:n_gate] = w_ref[:, c2:c2 + n_gate].astype(BF16)

    x = x_ref[0]
    y = _rms(x) * g_ref[...]
    h = (y * (1.0 + mod_ref[0, 1:2, :]) + mod_ref[0, 0:1, :]).astype(BF16)
    u_ref[0] = _dot(h, wu_ref[...])
    qkv_ref[0] = _dot(h, wqkv_ref[...]).astype(BF16)
    z_ref[0] = _dot(h, wz_ref[...]).astype(BF16)
    ab_ref[0] = _dot(h, wab_ref[...])


def _inproj(x, mod, g, w, n_u, n_qkv, n_z):
    bsz, seq, d = x.shape
    tm = ROW_TILE
    full = lambda a: pl.BlockSpec(a.shape, lambda b, i: (0,) * a.ndim)
    row = lambda n: pl.BlockSpec((1, tm, n), lambda b, i: (b, i, 0))
    return pl.pallas_call(
        _inproj_kernel,
        grid=(bsz, seq // tm),
        in_specs=[row(d), pl.BlockSpec((1, N_MOD, d), lambda b, i: (b, 0, 0)), full(g),
                  pl.BlockSpec(w.shape, lambda b, i: (0, 0), pipeline_mode=pl.Buffered(1))],
        out_specs=[row(n_u), row(n_qkv), row(n_z), row(LANES)],
        out_shape=[jax.ShapeDtypeStruct((bsz, seq, n_u), F32),
                   jax.ShapeDtypeStruct((bsz, seq, n_qkv), BF16),
                   jax.ShapeDtypeStruct((bsz, seq, n_z), BF16),
                   jax.ShapeDtypeStruct((bsz, seq, LANES), F32)],
        scratch_shapes=[pltpu.VMEM((d, n_u), BF16), pltpu.VMEM((d, n_qkv), BF16),
                        pltpu.VMEM((d, n_z), BF16), pltpu.VMEM((d, LANES), BF16)],
        compiler_params=pltpu.CompilerParams(
            dimension_semantics=("arbitrary", "arbitrary"), vmem_limit_bytes=VMEM_LIMIT),
        name="inproj",
    )(x, mod, g, w)


def _s5_tables(lam_re, lam_im, log_step, b_re, b_im, c_re, c_im):
    ch = S5_CHUNK
    n_grp, n_st = lam_re.shape
    gpb = LANES // S5_GROUP
    nblk = n_grp // gpb
    step = jnp.exp(log_step.astype(F32))[:, None]
    lr = lam_re.astype(F32)
    li = lam_im.astype(F32)

    def lam_pow(j):
        mag = jnp.exp(j * lr * step)
        ang = j * li * step
        return mag * jnp.cos(ang), mag * jnp.sin(ang)

    ar, ai = lam_pow(1.0)
    den = lr * lr + li * li
    fr = ((ar - 1.0) * lr + ai * li) / den
    fi = (ai * lr - (ar - 1.0) * li) / den
    bbr = fr[..., None] * b_re - fi[..., None] * b_im
    bbi = fr[..., None] * b_im + fi[..., None] * b_re
    pows = [lam_pow(float(j)) for j in range(ch + 1)]
    pr = jnp.stack([p[0] for p in pows])
    pi = jnp.stack([p[1] for p in pows])
    lbr = pr[:ch, :, :, None] * bbr[None] - pi[:ch, :, :, None] * bbi[None]
    lbi = pr[:ch, :, :, None] * bbi[None] + pi[:ch, :, :, None] * bbr[None]
    kmat = (jnp.einsum('ghp,jgpk->jghk', c_re, lbr, precision=HIGHEST)
            - jnp.einsum('ghp,jgpk->jghk', c_im, lbi, precision=HIGHEST))
    kt = kmat.reshape(ch, nblk, gpb, S5_GROUP, S5_GROUP).transpose(1, 0, 3, 2, 4)
    kt = kt.reshape(nblk, ch, S5_GROUP, LANES)

    wri = jnp.stack([lbr[::-1], lbi[::-1]]).reshape(2, ch, nblk, gpb, n_st, S5_GROUP)
    wt = wri.transpose(2, 1, 0, 4, 3, 5).reshape(nblk, ch, 2 * n_st, LANES)

    pr1 = pr[1:, :, None, :]
    pi1 = pi[1:, :, None, :]
    clr = c_re[None] * pr1 - c_im[None] * pi1
    cli = c_re[None] * pi1 + c_im[None] * pr1
    wo = jnp.stack([clr, -cli]).reshape(2, ch, nblk, gpb, S5_GROUP, n_st)
    ot = wo.transpose(2, 1, 4, 0, 3, 5).reshape(nblk, ch, S5_GROUP, 2 * gpb * n_st)

    half = gpb * n_st
    amat = jnp.stack([pr[ch].reshape(nblk, half), pi[ch].reshape(nblk, half)], axis=1)
    return _s5_expand(kt, wt, ot, n_st) + (amat,)


def _s5_expand_kernel(kt_ref, wt_ref, ot_ref, t_ref, win_ref, wout_ref, *, n_st):
    ch = kt_ref.shape[1]
    grp = S5_GROUP
    tn = (((0,), (0,)), ((), ()))

    def iota(shape, dim):
        return lax.broadcasted_iota(jnp.int32, shape, dim)

    rep_h = jnp.where(iota((grp, LANES), 0) == iota((grp, LANES), 1) % grp, 1.0, 0.0).astype(BF16)
    sdim = win_ref.shape[2]
    ra = iota((2 * n_st, sdim), 0)
    cb = iota((2 * n_st, sdim), 1)
    rep_s = jnp.where((ra // n_st == cb // (sdim // 2)) & (ra % n_st == cb % n_st), 1.0, 0.0).astype(BF16)
    own_t = iota((LANES, LANES), 0) // grp == iota((LANES, LANES), 1) // grp
    own_w = iota((LANES, sdim), 0) // grp == (iota((LANES, sdim), 1) // n_st) % (LANES // grp)
    own_o = (iota((sdim, LANES), 0) // n_st) % (LANES // grp) == iota((sdim, LANES), 1) // grp

    taps = []
    for j in range(ch):
        blk = lax.dot_general(kt_ref[0, j].astype(BF16), rep_h, tn, preferred_element_type=F32)
        taps.append(jnp.where(own_t, blk, 0.0).astype(BF16))
    zero = jnp.zeros((LANES, LANES), BF16)
    for s in range(ch):
        for t in range(ch):
            t_ref[0, s * LANES:(s + 1) * LANES, t * LANES:(t + 1) * LANES] = taps[t - s] if t >= s else zero
        blk = lax.dot_general(wt_ref[0, s].astype(BF16), rep_s, tn, preferred_element_type=F32)
        win_ref[0, s * LANES:(s + 1) * LANES, :] = jnp.where(own_w, blk, 0.0).astype(BF16)
        blk = lax.dot_general(ot_ref[0, s].astype(BF16), rep_h, tn, preferred_element_type=F32)
        wout_ref[0, :, s * LANES:(s + 1) * LANES] = jnp.where(own_o, blk, 0.0).astype(BF16)


def _s5_expand(kt, wt, ot, n_st):
    nblk, ch = kt.shape[:2]
    kdim = ch * LANES
    sdim = ot.shape[3]
    spec = lambda a: pl.BlockSpec((1,) + a.shape[1:], lambda c: (c, 0, 0, 0))
    big = lambda r, cdim: pl.BlockSpec((1, r, cdim), lambda c: (c, 0, 0))
    return pl.pallas_call(
        functools.partial(_s5_expand_kernel, n_st=n_st),
        grid=(nblk,),
        in_specs=[spec(kt), spec(wt), spec(ot)],
        out_specs=[big(kdim, kdim), big(kdim, sdim), big(sdim, kdim)],
        out_shape=[jax.ShapeDtypeStruct((nblk, kdim, kdim), BF16),
                   jax.ShapeDtypeStruct((nblk, kdim, sdim), BF16),
                   jax.ShapeDtypeStruct((nblk, sdim, kdim), BF16)],
        compiler_params=pltpu.CompilerParams(
            dimension_semantics=("arbitrary",), vmem_limit_bytes=VMEM_LIMIT),
        name="s5_tables",
    )(kt, wt, ot)


def _s5_kernel(u_ref, t_ref, win_ref, wout_ref, a_ref, d_ref, y_ref, st_ref, v_ref, xp_ref):
    ch = S5_CHUNK
    nb = u_ref.shape[0]
    n = u_ref.shape[1] // ch
    half = st_ref.shape[2]

    @pl.when(pl.program_id(2) == 0)
    def _():
        st_ref[...] = jnp.zeros_like(st_ref)

    def slabs(b):
        return [u_ref[b, pl.ds(s, n, stride=ch), :] for s in range(ch)]

    for b in range(nb):
        v_ref[b] = _dot(jnp.concatenate(slabs(b), axis=1).astype(BF16), win_ref[0])
    a_r = a_ref[0, 0:1, :]
    a_i = a_ref[0, 1:2, :]

    def body(r, carry):
        out = []
        for b in range(nb):
            x_r, x_i = carry[b]
            xp_ref[b, pl.ds(r, 1), 0:half] = x_r
            xp_ref[b, pl.ds(r, 1), half:2 * half] = x_i
            v_r = v_ref[b, pl.ds(r, 1), 0:half]
            v_i = v_ref[b, pl.ds(r, 1), half:2 * half]
            out.append((a_r * x_r - a_i * x_i + v_r, a_r * x_i + a_i * x_r + v_i))
        return tuple(out)

    init = tuple((st_ref[b, 0:1, :], st_ref[b, 1:2, :]) for b in range(nb))
    last = lax.fori_loop(0, n, body, init, unroll=4)
    d = d_ref[0]
    for b in range(nb):
        st_ref[b, 0:1, :] = last[b][0]
        st_ref[b, 1:2, :] = last[b][1]
        sl = slabs(b)
        y = (_dot(jnp.concatenate(sl, axis=1).astype(BF16), t_ref[0])
             + _dot(xp_ref[b].astype(BF16), wout_ref[0]))
        for t in range(ch):
            y_ref[b, pl.ds(t, n, stride=ch), :] = y[:, t * LANES:(t + 1) * LANES] + d * sl[t]


def _s5(u, tmat, win, wout, amat, d_skip):
    bsz, seq, dch = u.shape
    nblk = dch // LANES
    tt = min(S5_TILE, seq)
    n = tt // S5_CHUNK
    sdim = win.shape[2]
    nb = next(k for k in (S5_BATCH_ROWS, 2, 1) if bsz % k == 0)
    wspec = lambda a: pl.BlockSpec((1,) + a.shape[1:], lambda b, c, i: (c, 0, 0))
    return pl.pallas_call(
        _s5_kernel,
        grid=(bsz // nb, nblk, seq // tt),
        in_specs=[pl.BlockSpec((nb, tt, LANES), lambda b, c, i: (b, i, c)),
                  wspec(tmat), wspec(win), wspec(wout), wspec(amat),
                  pl.BlockSpec((1, 1, LANES), lambda b, c, i: (c, 0, 0))],
        out_specs=pl.BlockSpec((nb, tt, LANES), lambda b, c, i: (b, i, c)),
        out_shape=jax.ShapeDtypeStruct((bsz, seq, dch), F32),
        scratch_shapes=[pltpu.VMEM((nb, 2, sdim // 2), F32),
                        pltpu.VMEM((nb, n, sdim), F32),
                        pltpu.VMEM((nb, n, sdim), F32)],
        compiler_params=pltpu.CompilerParams(
            dimension_semantics=("parallel", "parallel", "arbitrary"),
            vmem_limit_bytes=VMEM_LIMIT),
        name="s5",
    )(u, tmat, win, wout, amat, d_skip.reshape(nblk, 1, LANES))


def _cumsum_rows(v):
    n = v.shape[0]
    row = lax.broadcasted_iota(jnp.int32, v.shape, 0)
    sh = 1
    while sh < n:
        v = v + jnp.where(row >= sh, pltpu.roll(v, sh, axis=0), 0.0)
        sh *= 2
    return v


def _gdn2_kernel(qkv_ref, z_ref, ab_ref, cw_ref, gp_ref, ng_ref, o_ref,
                 s_ref, xp_ref, gc_ref, gct_ref, kbq_ref, kn_ref, rhs_ref, cq_ref, qkt_ref,
                 lm_ref, d_ref, db_ref, de_ref, ds_ref, pe_ref, val_ref, vn_ref, *, n_heads):
    tt = qkv_ref.shape[1]
    dh = GDN_HEAD_DIM
    bsz = qkv_ref.shape[0]
    dg = n_heads * dh
    halo = SUBLANES
    sb = ds_ref.shape[1]
    nsb = tt // sb
    pairs = [(b, h) for b in range(bsz) for h in range(n_heads)]

    @pl.when(pl.program_id(0) == 0)
    def _():
        s_ref[...] = jnp.zeros_like(s_ref)
        xp_ref[...] = jnp.zeros_like(xp_ref)

    ri = lax.broadcasted_iota(jnp.int32, (tt, tt), 0)
    ci = lax.broadcasted_iota(jnp.int32, (tt, tt), 1)
    lane = lax.broadcasted_iota(jnp.int32, (tt, LANES), 1)
    a_log = gp_ref[0:1, :]
    dt_bias = gp_ref[1:2, :]
    shift_op = jnp.concatenate(
        [jnp.where(ri == ci + j, 1.0, 0.0) for j in range(1, CONV_WIDTH)], axis=0).astype(BF16)

    for b in range(bsz):
        x = qkv_ref[b]
        x32 = x.astype(F32)
        shifted = _dot(shift_op, x)
        edge = jnp.concatenate([xp_ref[b], x32[0:halo]], axis=0)
        conv = cw_ref[CONV_WIDTH - 1:CONV_WIDTH, :] * x32
        for j in range(1, CONV_WIDTH):
            sh_j = jnp.concatenate([edge[halo - j:2 * halo - j],
                                    shifted[(j - 1) * tt + halo:j * tt]], axis=0)
            conv = conv + cw_ref[CONV_WIDTH - 1 - j:CONV_WIDTH - j, :] * sh_j
        xp_ref[b] = x32[tt - halo:tt]
        act = _silu(conv)

        ab = ab_ref[b]
        sp = jnp.maximum(ab + dt_bias, 0.0) + jnp.log1p(jnp.exp(-jnp.abs(ab + dt_bias)))
        g = jnp.where(lane < n_heads, -jnp.exp(a_log) * sp, 0.0)
        gc = _cumsum_rows(g)
        gc_ref[b] = gc
        gct_ref[b] = gc.T
        glast = gc[tt - 1:tt, :]
        egc = jnp.exp(gc)
        ekt = jnp.exp(glast - gc)
        beta_all = jax.nn.sigmoid(ab)
        for h in range(n_heads):
            p = b * n_heads + h
            q = act[:, h * dh:(h + 1) * dh]
            k = act[:, dg + h * dh:dg + (h + 1) * dh]
            v = act[:, 2 * dg + h * dh:2 * dg + (h + 1) * dh]
            q = q * lax.rsqrt(jnp.sum(q * q, axis=-1, keepdims=True) + EPS) * (dh ** -0.5)
            k = k * lax.rsqrt(jnp.sum(k * k, axis=-1, keepdims=True) + EPS)
            beta = beta_all[:, n_heads + h:n_heads + h + 1]
            eg = egc[:, h:h + 1]
            kb = k * beta
            kbq_ref[p, 0:tt, :] = kb.astype(BF16)
            kbq_ref[p, tt:2 * tt, :] = q.astype(BF16)
            kn_ref[p] = k.astype(BF16)
            rhs_ref[p, :, 0:dh] = (v * beta).astype(BF16)
            rhs_ref[p, :, dh:2 * dh] = (kb * eg).astype(BF16)
            cq_ref[p, tt:2 * tt, :] = (q * eg).astype(BF16)
            qkt_ref[p, tt:tt + dh, :] = (k * ekt[:, h:h + 1]).T.astype(BF16)

    for b, h in pairs:
        p = b * n_heads + h
        m1 = lax.dot_general(kbq_ref[p], kn_ref[p], (((1,), (1,)), ((), ())),
                             preferred_element_type=F32)
        gcol = gc_ref[b, :, h:h + 1]
        grow = gct_ref[b, h:h + 1, :]
        decay = jnp.exp(jnp.where(ri >= ci, gcol - grow, -1e30))
        lm = jnp.where(ri > ci, m1[0:tt] * decay, 0.0)
        lm_ref[p] = lm.astype(BF16)
        qkt_ref[p, 0:tt, :] = (m1[tt:2 * tt] * decay).astype(BF16)
        first = jnp.where(ri == ci + 1, jnp.where((ci & 1) == 0, lm, 0.0), 0.0)
        d = jnp.where(ri == ci, 1.0, 0.0) - first
        ds_ref[p] = sum(d[j * sb:(j + 1) * sb] for j in range(1, nsb)) + d[0:sb]

    in_blk = (ri // sb) == (ci // sb)
    blk_b = jnp.where(in_blk, 1.0, 0.0).astype(BF16)
    m = 2
    sh = 1
    while m < tt:
        rb = ri >> sh
        cb = ci >> sh
        sel = jnp.where(rb == cb + 1, jnp.where((cb & 1) == 0, 1.0, 0.0), 0.0).astype(BF16)
        if 2 * m <= sb:
            for b, h in pairs:
                p = b * n_heads + h
                pe_ref[p] = _dot(ds_ref[p].astype(BF16), lm_ref[p] * sel).astype(BF16)
            for b, h in pairs:
                p = b * n_heads + h
                ds = ds_ref[p]
                dbd = jnp.concatenate([ds.astype(BF16)] * nsb, axis=0) * blk_b
                ds = ds - _dot(pe_ref[p], dbd)
                ds_ref[p] = ds
                if 4 * m > sb:
                    d = jnp.where(in_blk, jnp.concatenate([ds] * nsb, axis=0), 0.0)
                    d_ref[p] = d
                    db_ref[p] = d.astype(BF16)
        else:
            for b, h in pairs:
                p = b * n_heads + h
                de_ref[p] = _dot(db_ref[p], lm_ref[p] * sel).astype(BF16)
            for b, h in pairs:
                p = b * n_heads + h
                d = d_ref[p] - _dot(de_ref[p], db_ref[p])
                d_ref[p] = d
                db_ref[p] = d.astype(BF16)
        m *= 2
        sh += 1

    for b, h in pairs:
        p = b * n_heads + h
        w = _dot(db_ref[p], rhs_ref[p])
        val_ref[p] = w[:, 0:dh]
        cq_ref[p, 0:tt, :] = w[:, dh:2 * dh].astype(BF16)

    for b, h in pairs:
        p = b * n_heads + h
        m2 = _dot(cq_ref[p], s_ref[p].astype(BF16))
        vn_ref[p] = (val_ref[p] - m2[0:tt]).astype(BF16)
        val_ref[p] = m2[tt:2 * tt]
    for b, h in pairs:
        p = b * n_heads + h
        r = _dot(qkt_ref[p], vn_ref[p])
        o = val_ref[p] + r[0:tt]
        egl = jnp.exp(gc_ref[b, tt - 1:tt, h:h + 1])
        s_ref[p] = s_ref[p] * egl + r[tt:tt + dh]
        zh = z_ref[b, :, h * dh:(h + 1) * dh].astype(F32)
        o_ref[b, :, h * dh:(h + 1) * dh] = (_rms(o) * ng_ref[...] * _silu(zh)).astype(o_ref.dtype)


def _gdn2(qkv, z, ab, conv_w, a_log, dt_bias, norm_g):
    bsz, seq, _ = qkv.shape
    dg = z.shape[2]
    dh = GDN_HEAD_DIM
    n_heads = dg // dh
    tt = GDN_BLOCK
    npair = bsz * n_heads
    gp = jnp.zeros((2, LANES), F32).at[0, :n_heads].set(a_log).at[1, :n_heads].set(dt_bias)
    blk = lambda n: pl.BlockSpec((bsz, tt, n), lambda i: (0, i, 0))
    full = lambda a: pl.BlockSpec(a.shape, lambda i: (0,) * a.ndim)
    ng = norm_g.reshape(1, dh)
    return pl.pallas_call(
        functools.partial(_gdn2_kernel, n_heads=n_heads),
        grid=(seq // tt,),
        in_specs=[blk(3 * dg), blk(dg), blk(LANES), full(conv_w), full(gp), full(ng)],
        out_specs=blk(dg),
        out_shape=jax.ShapeDtypeStruct((bsz, seq, dg), BF16),
        scratch_shapes=[pltpu.VMEM((npair, dh, dh), F32),
                        pltpu.VMEM((bsz, SUBLANES, 3 * dg), F32),
                        pltpu.VMEM((bsz, tt, LANES), F32),
                        pltpu.VMEM((bsz, LANES, tt), F32),
                        pltpu.VMEM((npair, 2 * tt, dh), BF16),
                        pltpu.VMEM((npair, tt, dh), BF16),
                        pltpu.VMEM((npair, tt, 2 * dh), BF16),
                        pltpu.VMEM((npair, 2 * tt, dh), BF16),
                        pltpu.VMEM((npair, tt + dh, tt), BF16),
                        pltpu.VMEM((npair, tt, tt), BF16),
                        pltpu.VMEM((npair, tt, tt), F32),
                        pltpu.VMEM((npair, tt, tt), BF16),
                        pltpu.VMEM((npair, tt, tt), BF16),
                        pltpu.VMEM((npair, GDN_SIDE, tt), F32),
                        pltpu.VMEM((npair, GDN_SIDE, tt), BF16),
                        pltpu.VMEM((npair, tt, dh), F32),
                        pltpu.VMEM((npair, tt, dh), BF16)],
        compiler_params=pltpu.CompilerParams(
            dimension_semantics=("arbitrary",), vmem_limit_bytes=VMEM_LIMIT),
        name="gdn",
    )(qkv, z, ab, conv_w, gp, ng)


def _outproj_kernel(ys_ref, yg_ref, x_ref, mod_ref, wglu_ref, bglu_ref, wo_ref,
                    g2_ref, wr_ref, h_ref, m_ref, route_ref, routet_ref, count_ref,
                    cnt_ref, wglub_ref, wob_ref, mt_ref):
    @pl.when((pl.program_id(0) == 0) & (pl.program_id(1) == 0))
    def _():
        wglub_ref[...] = wglu_ref[...].astype(BF16)
        wob_ref[...] = wo_ref[...].astype(BF16)

    @pl.when((pl.program_id(0) == 0) & (pl.program_id(1) == 0))
    def _():
        cnt_ref[...] = jnp.zeros_like(cnt_ref)

    d_s5 = wglub_ref.shape[0]
    tm = x_ref.shape[1]
    hm = tm // OUTPROJ_PARTS
    earlier = jnp.where(lax.broadcasted_iota(jnp.int32, (hm, hm), 0)
                        > lax.broadcasted_iota(jnp.int32, (hm, hm), 1), 1.0, 0.0).astype(BF16)
    lane = lax.broadcasted_iota(jnp.int32, (hm, LANES), 1)
    neg = -1e30
    is_grp = lane < N_EXPERT_GROUPS
    seen = cnt_ref[...]
    for part in range(OUTPROJ_PARTS):
        rows = slice(part * hm, (part + 1) * hm)
        y = jax.nn.gelu(ys_ref[0, rows, :])
        gate = jax.nn.sigmoid(_dot(y.astype(BF16), wglub_ref[...]) + bglu_ref[...])
        y = (y * gate).astype(BF16)
        mix = _dot(y, wob_ref[0:d_s5, :]) + _dot(yg_ref[0, rows, :], wob_ref[d_s5:, :])
        h = x_ref[0, rows, :] + mod_ref[0, 2:3, :] * mix
        h_ref[0, rows, :] = h
        m = _rms(h) * g2_ref[...] * (1.0 + mod_ref[0, 4:5, :]) + mod_ref[0, 3:4, :]
        _store_token_tiles(mt_ref, (), m, row0=part * hm)

        m_hi = m.astype(BF16)
        m_lo = (m - m_hi.astype(F32)).astype(BF16)
        both = _dot(m_hi, wr_ref[...])
        logits = both[:, 0:LANES] + both[:, LANES:2 * LANES] + _dot(m_lo, wr_ref[:, 0:LANES])
        gl = jnp.where(is_grp, logits, neg)
        gmax = jnp.max(gl, axis=-1, keepdims=True)
        gidx = jnp.min(jnp.where(gl == gmax, lane, LANES), axis=-1, keepdims=True)
        p_grp = 1.0 / jnp.sum(jnp.where(is_grp, jnp.exp(gl - gmax), 0.0), axis=-1, keepdims=True)
        lo = ROUTE_OFF + gidx * EXPERTS_PER_GROUP
        el = jnp.where((lane >= lo) & (lane < lo + EXPERTS_PER_GROUP), logits, neg)
        v1 = jnp.max(el, axis=-1, keepdims=True)
        i1 = jnp.min(jnp.where(el == v1, lane, LANES), axis=-1, keepdims=True)
        el2 = jnp.where(lane == i1, neg, el)
        v2 = jnp.max(el2, axis=-1, keepdims=True)
        i2 = jnp.min(jnp.where(el2 == v2, lane, LANES), axis=-1, keepdims=True)
        t = jnp.exp(v2 - v1)
        w1 = p_grp / (1.0 + t)
        w2 = w1 * t

        oh1 = lane == i1
        oh2 = lane == i2
        cnt = jnp.where(oh1, 1.0, 0.0) + jnp.where(oh2, 1.0, 0.0)
        before = _dot(earlier, cnt.astype(BF16)) + seen
        r1 = jnp.sum(jnp.where(oh1, before, 0.0), axis=-1, keepdims=True)
        r2 = jnp.sum(jnp.where(oh2, before, 0.0), axis=-1, keepdims=True)
        seen = seen + jnp.sum(cnt, axis=0, keepdims=True)
        cols = [(i1 - ROUTE_OFF).astype(F32), (i2 - ROUTE_OFF).astype(F32), w1, w2, r1, r2]
        route = jnp.zeros_like(logits)
        for j, col in enumerate(cols):
            route = jnp.where(lane == j, col, route)
        route_ref[0, rows, :] = route
        routet_ref[:, rows] = route.T[0:SUBLANES, :]
    cnt_ref[...] = seen
    count_ref[0] = seen
    m_ref[0] = mt_ref[...].astype(m_ref.dtype)


def _outproj(ys, yg, x, mod, wglu, bglu, wo, g2, wr):
    bsz, seq, d = x.shape
    tm = ROW_TILE
    full = lambda a: pl.BlockSpec(a.shape, lambda b, i: (0,) * a.ndim, pipeline_mode=pl.Buffered(1))
    row = lambda n: pl.BlockSpec((1, tm, n), lambda b, i: (b, i, 0))
    return pl.pallas_call(
        _outproj_kernel,
        grid=(bsz, seq // tm),
        in_specs=[row(ys.shape[2]), row(yg.shape[2]), row(d),
                  pl.BlockSpec((1, N_MOD, d), lambda b, i: (b, 0, 0)),
                  full(wglu), full(bglu), full(wo), full(g2), full(wr)],
        out_specs=[row(d), pl.BlockSpec((1, tm * (d // LANES), LANES), lambda b, i: (b, i, 0)), row(LANES),
                   pl.BlockSpec((SUBLANES, tm), lambda b, i: (0, b * (seq // tm) + i)),
                   pl.BlockSpec((1, 1, LANES), lambda b, i: (b * (seq // tm) + i, 0, 0))],
        out_shape=[jax.ShapeDtypeStruct((bsz, seq, d), F32),
                   jax.ShapeDtypeStruct((bsz, seq * (d // LANES), LANES), BF16),
                   jax.ShapeDtypeStruct((bsz, seq, LANES), F32),
                   jax.ShapeDtypeStruct((SUBLANES, bsz * seq), F32),
                   jax.ShapeDtypeStruct((bsz * (seq // tm), 1, LANES), F32)],
        scratch_shapes=[pltpu.VMEM((1, LANES), F32), pltpu.VMEM(wglu.shape, BF16), pltpu.VMEM(wo.shape, BF16),
                        pltpu.VMEM((tm * (d // LANES), LANES), F32)],
        compiler_params=pltpu.CompilerParams(
            dimension_semantics=("arbitrary", "arbitrary"), vmem_limit_bytes=VMEM_LIMIT),
        name="outproj",
    )(ys, yg, x, mod, wglu, bglu, wo, g2, wr)


def _moe_scatter_kernel(d1_ref, d2_ref, pad_ref, m_ref, xs_hbm, zero_ref, sem, *, n_exp, k):
    i = pl.program_id(0)
    ts = d1_ref.shape[2]
    trows = zero_ref.shape[0]

    def tok(ref, t):
        return ref.at[pl.ds(pl.multiple_of(t * k, k), k)]

    def tile_copy(j):
        return pltpu.make_async_copy(
            zero_ref, xs_hbm.at[pl.ds(pl.multiple_of(j * trows, trows), trows)], sem.at[0])

    def pad_piece(e, p, act):
        lo = pad_ref[0, e]
        length = pad_ref[1, e] - lo
        start = lo + (length & ~(2 * p - 1))
        copy = pltpu.make_async_copy(
            zero_ref.at[pl.ds(0, p * k)], xs_hbm.at[pl.ds(pl.multiple_of(start * k, k), p * k)], sem.at[0])

        @pl.when((length & p) != 0)
        def _():
            act(copy)

    @pl.when(i == 0)
    def _():
        zero_ref[...] = jnp.zeros_like(zero_ref)
        pieces = [(e, p) for e in range(n_exp) for p in [1 << j for j in range((trows // k).bit_length() - 1)]]
        for e, p in pieces:
            pad_piece(e, p, lambda c: c.start())
        for e, p in pieces:
            pad_piece(e, p, lambda c: c.wait())

        def fill(j, carry):
            tile_copy(j).start()
            return carry

        def drain(j, carry):
            tile_copy(j).wait()
            return carry

        lax.fori_loop(pad_ref[0, n_exp], pad_ref[1, n_exp], fill, 0)
        lax.fori_loop(pad_ref[0, n_exp], pad_ref[1, n_exp], drain, 0)

    def row_copy(t, dst, slot):
        return pltpu.make_async_copy(tok(m_ref, t), tok(xs_hbm, dst), sem.at[1 + slot])

    def issue(t, carry):
        row_copy(t, d1_ref[0, 0, t], 0).start(priority=0)
        row_copy(t, d2_ref[0, 0, t], 1).start(priority=1)
        return carry

    lax.fori_loop(0, ts, issue, 0, unroll=8)
    for slot in range(2):
        pltpu.make_async_copy(m_ref, xs_hbm.at[pl.ds(0, ts * k)], sem.at[1 + slot]).wait()


def _moe_scatter(m, dest1, dest2, pads, n_rows, k):
    n_tok = m.shape[0] // k
    ts = MOE_SCATTER_TILE
    n_exp = pads.shape[1] - 1
    dspec = pl.BlockSpec((1, 1, ts), lambda i: (i, 0, 0), memory_space=pltpu.SMEM)
    return pl.pallas_call(
        functools.partial(_moe_scatter_kernel, n_exp=n_exp, k=k),
        grid=(n_tok // ts,),
        in_specs=[dspec, dspec,
                  pl.BlockSpec(pads.shape, lambda i: (0, 0), memory_space=pltpu.SMEM),
                  pl.BlockSpec((ts * k, LANES), lambda i: (i, 0))],
        out_specs=pl.BlockSpec(memory_space=pl.ANY),
        out_shape=jax.ShapeDtypeStruct((n_rows * k, LANES), m.dtype),
        scratch_shapes=[pltpu.VMEM((MOE_ROWS * k, LANES), m.dtype), pltpu.SemaphoreType.DMA((3,))],
        compiler_params=pltpu.CompilerParams(
            dimension_semantics=("arbitrary",), vmem_limit_bytes=VMEM_LIMIT, has_side_effects=True),
        name="moe_scatter",
    )(dest1.reshape(n_tok // ts, 1, ts), dest2.reshape(n_tok // ts, 1, ts), pads, m)


def _moe_ffn_kernel(te_ref, na_ref, xs_ref, wg_ref, wu_ref, wd_ref, ys_ref, wgb_ref, wub_ref, wdb_ref,
                    tile_ref):
    i = pl.program_id(0)
    active = i < na_ref[0]
    changed = (i == 0) | (te_ref[i] != te_ref[jnp.maximum(i - 1, 0)])

    @pl.when(active & changed)
    def _():
        wgb_ref[...] = wg_ref[0].astype(BF16)
        wub_ref[...] = wu_ref[0].astype(BF16)
        wdb_ref[...] = wd_ref[0].astype(BF16)

    @pl.when(active)
    def _():
        k = wgb_ref.shape[0] // LANES
        tile_ref[...] = xs_ref[...].astype(F32)
        x = _load_token_tiles(tile_ref, (), xs_ref.shape[0] // k, k).astype(BF16)
        mid = (_silu(_dot(x, wgb_ref[...])) * _dot(x, wub_ref[...])).astype(BF16)
        _store_token_tiles(tile_ref, (), _dot(mid, wdb_ref[...]))
        ys_ref[...] = tile_ref[...].astype(ys_ref.dtype)


def _moe_ffn(xs, tile_expert, n_active, wg, wu, wd):
    _, d, de = wg.shape
    tmr = MOE_ROWS * (d // LANES)
    n_rows, dp = xs.shape
    grid_spec = pltpu.PrefetchScalarGridSpec(
        num_scalar_prefetch=2,
        grid=(n_rows // tmr,),
        in_specs=[pl.BlockSpec((tmr, dp), lambda i, te, na: (jnp.minimum(i, na[0] - 1), 0)),
                  pl.BlockSpec((1, d, de), lambda i, te, na: (te[i], 0, 0)),
                  pl.BlockSpec((1, d, de), lambda i, te, na: (te[i], 0, 0)),
                  pl.BlockSpec((1, de, d), lambda i, te, na: (te[i], 0, 0))],
        out_specs=pl.BlockSpec((tmr, dp), lambda i, te, na: (jnp.minimum(i, na[0] - 1), 0)),
        scratch_shapes=[pltpu.VMEM((d, de), BF16), pltpu.VMEM((d, de), BF16), pltpu.VMEM((de, d), BF16),
                        pltpu.VMEM((tmr, dp), F32)],
    )
    return pl.pallas_call(
        _moe_ffn_kernel,
        grid_spec=grid_spec,
        out_shape=jax.ShapeDtypeStruct((n_rows, dp), xs.dtype),
        input_output_aliases={2: 0},
        compiler_params=pltpu.CompilerParams(
            dimension_semantics=("arbitrary",), vmem_limit_bytes=VMEM_LIMIT),
        name="moe_ffn",
    )(tile_expert, n_active, xs, wg, wu, wd)


def _moe_combine_kernel(d1_ref, d2_ref, n1_ref, n2_ref, h_ref, route_ref, mod_ref, nf_ref, ys_hbm, o_ref,
                        b1_ref, b2_ref, tile_ref, sem, *, final_norm):
    tm = h_ref.shape[1]
    k = h_ref.shape[2] // LANES
    g = pl.program_id(0)
    slot = g % 2

    def tok(ref, t):
        return ref.at[pl.ds(pl.multiple_of(t * k, k), k)]

    def gather(i1_ref, i2_ref, s):
        def issue(t, carry):
            pltpu.make_async_copy(tok(ys_hbm, i1_ref[0, 0, t]), tok(b1_ref.at[s], t),
                                  sem.at[s, 0]).start(priority=0)
            pltpu.make_async_copy(tok(ys_hbm, i2_ref[0, 0, t]), tok(b2_ref.at[s], t),
                                  sem.at[s, 1]).start(priority=1)
            return carry

        lax.fori_loop(0, tm, issue, 0, unroll=8)

    @pl.when(g == 0)
    def _():
        gather(d1_ref, d2_ref, 0)

    @pl.when(g + 1 < pl.num_programs(0))
    def _():
        gather(n1_ref, n2_ref, 1 - slot)

    pltpu.make_async_copy(ys_hbm.at[pl.ds(0, tm * k)], b1_ref.at[slot], sem.at[slot, 0]).wait()
    pltpu.make_async_copy(ys_hbm.at[pl.ds(0, tm * k)], b2_ref.at[slot], sem.at[slot, 1]).wait()
    route = route_ref[0]
    tile_ref[...] = b1_ref[slot].astype(F32)
    moe = route[:, 2:3] * _load_token_tiles(tile_ref, (), tm, k)
    tile_ref[...] = b2_ref[slot].astype(F32)
    moe = moe + route[:, 3:4] * _load_token_tiles(tile_ref, (), tm, k)
    h = h_ref[0] + mod_ref[0, 5:6, :] * moe
    if final_norm:
        h = _rms(h) * nf_ref[...]
    o_ref[0] = h


def _moe_combine(ys, dest1, dest2, h, route, mod, nf, final_norm):
    bsz, seq, d = h.shape
    tm = MOE_COMBINE_TILE
    nt = seq // tm
    steps = bsz * nt
    dspec = pl.BlockSpec((1, 1, tm), lambda g: (g, 0, 0), memory_space=pltpu.SMEM)
    nspec = pl.BlockSpec((1, 1, tm), lambda g: (jnp.minimum(g + 1, steps - 1), 0, 0), memory_space=pltpu.SMEM)
    row = lambda n: pl.BlockSpec((1, tm, n), lambda g: (g // nt, g % nt, 0))
    d1 = dest1.reshape(steps, 1, tm)
    d2 = dest2.reshape(steps, 1, tm)
    return pl.pallas_call(
        functools.partial(_moe_combine_kernel, final_norm=final_norm),
        grid=(steps,),
        in_specs=[dspec, dspec, nspec, nspec, row(d), row(LANES),
                  pl.BlockSpec((1, N_MOD, d), lambda g: (g // nt, 0, 0)),
                  pl.BlockSpec(nf.shape, lambda g: (0, 0)),
                  pl.BlockSpec(memory_space=pl.ANY)],
        out_specs=row(d),
        out_shape=jax.ShapeDtypeStruct((bsz, seq, d), F32),
        scratch_shapes=[pltpu.VMEM((2, tm * (d // LANES), LANES), ys.dtype),
                        pltpu.VMEM((2, tm * (d // LANES), LANES), ys.dtype),
                        pltpu.VMEM((tm * (d // LANES), LANES), F32),
                        pltpu.SemaphoreType.DMA((2, 2))],
        compiler_params=pltpu.CompilerParams(
            dimension_semantics=("arbitrary",), vmem_limit_bytes=VMEM_LIMIT),
        name="moe_combine",
    )(d1, d2, d1, d2, h, route, mod, nf, ys)


def _moe_sparse(m, route, route_t, counts, wg, wu, wd, h, mod, nf, final_norm):
    bsz, seq, d = h.shape
    n_tok = bsz * seq
    tmr = MOE_ROWS
    n_exp = wg.shape[0]
    n_tiles = (2 * n_tok) // tmr + n_exp
    i32 = jnp.int32
    e1, e2, k1, k2 = (route_t[j].astype(i32) for j in (0, 1, 4, 5))
    cnt = counts[-1, 0, ROUTE_OFF:ROUTE_OFF + n_exp].astype(i32)
    ptiles = (cnt + tmr - 1) // tmr
    tend = jnp.cumsum(ptiles)
    gstart = (tend - ptiles) * tmr
    eids = jnp.arange(n_exp, dtype=i32)
    dest1 = k1 + jnp.sum(jnp.where(e1[:, None] == eids[None, :], gstart[None, :], 0), axis=1)
    dest2 = k2 + jnp.sum(jnp.where(e2[:, None] == eids[None, :], gstart[None, :], 0), axis=1)
    n_active = tend[-1:]
    tile_expert = jnp.sum(jnp.arange(n_tiles, dtype=i32)[:, None] >= tend[None, :], axis=1).astype(i32)
    last_expert = jnp.sum(n_active - 1 >= tend).astype(i32)
    tile_expert = jnp.minimum(tile_expert, last_expert)
    pads = jnp.stack([jnp.concatenate([gstart + cnt, n_active]),
                      jnp.concatenate([gstart + ptiles * tmr, jnp.full((1,), n_tiles, i32)])]).astype(i32)
    k = d // LANES
    xs = _moe_scatter(m.reshape(n_tok * k, LANES), dest1, dest2, pads, n_tiles * tmr, k)
    ys = _moe_ffn(xs, tile_expert, n_active.astype(i32), wg, wu, wd)
    return _moe_combine(ys, dest1, dest2, h, route, mod, nf, final_norm)


def kernel(x, c, norm1_g, norm2_g, w_ada, b_ada, w_in, lam_re, lam_im, log_step, s5_b_re, s5_b_im,
           s5_c_re, s5_c_im, s5_d, w_glu, b_glu, conv_w, a_log, dt_bias, gdn_norm_g, w_out,
           w_router_grp, w_router_exp, w_gate, w_up, w_down, normf_g):
    bsz, seq, d = x.shape
    depth = w_ada.shape[0]
    d_s5 = s5_d.shape[1]
    d_gdn = w_out.shape[1] - d_s5
    n_heads = d_gdn // GDN_HEAD_DIM
    assert seq % min(S5_TILE, seq) == 0 and seq % ROW_TILE == 0 and seq % GDN_BLOCK == 0
    assert seq % MOE_COMBINE_TILE == 0 and (bsz * seq) % MOE_SCATTER_TILE == 0
    assert d % LANES == 0 and d_s5 % LANES == 0 and d_gdn % GDN_HEAD_DIM == 0
    assert 2 * n_heads <= LANES and N_EXPERT_GROUPS + N_EXPERTS <= LANES
    assert w_gate.shape[1] == N_EXPERTS and w_router_grp.shape[2] == N_EXPERT_GROUPS

    h = x
    for l in range(depth):
        mod = _ada(c, w_ada[l], b_ada[l]).reshape(bsz, N_MOD, d)
        wi = w_in[l]
        c0, c1, c2 = d_s5, d_s5 + 3 * d_gdn, d_s5 + 4 * d_gdn
        u, qkv, z, ab = _inproj(h, mod, norm1_g[l].reshape(1, d), wi, c0, c1 - c0, c2 - c1)
        tabs = _s5_tables(lam_re[l], lam_im[l], log_step[l], s5_b_re[l], s5_b_im[l],
                          s5_c_re[l], s5_c_im[l])
        ys = _s5(u, *tabs, s5_d[l])
        yg = _gdn2(qkv, z, ab, conv_w[l], a_log[l], dt_bias[l], gdn_norm_g[l])
        wr = jnp.zeros((d, LANES), F32)
        wr = wr.at[:, :N_EXPERT_GROUPS].set(w_router_grp[l])
        wr = wr.at[:, ROUTE_OFF:ROUTE_OFF + N_EXPERTS].set(w_router_exp[l])
        wr_hi = wr.astype(BF16)
        wr = jnp.concatenate([wr_hi, (wr - wr_hi.astype(F32)).astype(BF16)], axis=1)
        h1, m, route, route_t, counts = _outproj(ys, yg, h, mod, w_glu[l], b_glu[l].reshape(1, d_s5),
                                                 w_out[l], norm2_g[l].reshape(1, d), wr)
        h = _moe_sparse(m, route, route_t, counts, w_gate[l], w_up[l], w_down[l], h1, mod,
                        normf_g.reshape(1, d), final_norm=(l == depth - 1))
    return h
```

```python
import functools

import jax
import jax.numpy as jnp
from jax import lax
from jax.experimental import pallas as pl
from jax.experimental.pallas import tpu as pltpu

F32 = jnp.float32
BF16 = jnp.bfloat16
HIGHEST = lax.Precision.HIGHEST
EPS = 1e-6

S5_GROUP = 16
GDN_HEAD_DIM = 128
CONV_WIDTH = 4
N_EXPERT_GROUPS = 4
EXPERTS_PER_GROUP = 8
N_EXPERTS = N_EXPERT_GROUPS * EXPERTS_PER_GROUP
N_MOD = 6

LANES = 128
SUBLANES = 8
VMEM_LIMIT = 56 * 1024 * 1024

S5_CHUNK = SUBLANES
S5_TILE = 2048
S5_BATCH_ROWS = 4
GDN_BLOCK = 256
GDN_SIDE = 64
ROW_TILE = 512
OUTPROJ_PARTS = 1
MOE_ROWS = 512
MOE_SCATTER_TILE = 1024
MOE_COMBINE_TILE = 1024
ROUTE_OFF = N_EXPERT_GROUPS


def _dot(a, b, **kw):
    return jnp.dot(a, b, preferred_element_type=F32, **kw)


def _sigmoid(v):
    return 0.5 * jnp.tanh(0.5 * v) + 0.5


def _silu(v):
    return v * _sigmoid(v)


def _rms(v):
    return v * lax.rsqrt(jnp.mean(v * v, axis=-1, keepdims=True) + EPS)


def _store_token_tiles(ref, idx, v, row0=0):
    n, width = v.shape
    k = width // LANES
    for s in range(k):
        ref[idx + (pl.ds(row0 * k + s, n, stride=k), slice(None))] = v[:, s * LANES:(s + 1) * LANES]


def _load_token_tiles(ref, idx, n, k):
    return jnp.concatenate([ref[idx + (pl.ds(s, n, stride=k), slice(None))] for s in range(k)], axis=1)


def _ada_kernel(c_ref, w_ref, b_ref, o_ref):
    o_ref[...] = _dot(_silu(c_ref[...]), w_ref[...], precision=HIGHEST) + b_ref[...]


def _ada(c, w, b):
    bsz, d = c.shape
    n = w.shape[1]
    cols = d
    return pl.pallas_call(
        _ada_kernel,
        grid=(n // cols,),
        in_specs=[pl.BlockSpec((bsz, d), lambda j: (0, 0)),
                  pl.BlockSpec((d, cols), lambda j: (0, j)),
                  pl.BlockSpec((1, cols), lambda j: (0, j))],
        out_specs=pl.BlockSpec((bsz, cols), lambda j: (0, j)),
        out_shape=jax.ShapeDtypeStruct((bsz, n), F32),
        compiler_params=pltpu.CompilerParams(vmem_limit_bytes=VMEM_LIMIT),
        name="ada",
    )(c, w, b.reshape(1, n))


def _inproj_kernel(x_ref, mod_ref, g_ref, w_ref, u_ref, qkv_ref, z_ref, ab_ref,
                   wu_ref, wqkv_ref, wz_ref, wab_ref):
    @pl.when((pl.program_id(0) == 0) & (pl.program_id(1) == 0))
    def _():
        c0 = wu_ref.shape[1]
        c1 = c0 + wqkv_ref.shape[1]
        c2 = c1 + wz_ref.shape[1]
        n_gate = w_ref.shape[1] - c2
        wu_ref[...] = w_ref[:, 0:c0].astype(BF16)
        wqkv_ref[...] = w_ref[:, c0:c1].astype(BF16)
        wz_ref[...] = w_ref[:, c1:c2].astype(BF16)
        wab_ref[...] = jnp.zeros_like(wab_ref)
        wab_ref[:, 0:n_gate] = w_ref[:, c2:c2 + n_gate].astype(BF16)

    x = x_ref[0]
    y = _rms(x) * g_ref[...]
    h = (y * (1.0 + mod_ref[0, 1:2, :]) + mod_ref[0, 0:1, :]).astype(BF16)
    u_ref[0] = _dot(h, wu_ref[...])
    qkv_ref[0] = _dot(h, wqkv_ref[...]).astype(BF16)
    z_ref[0] = _dot(h, wz_ref[...]).astype(BF16)
    ab_ref[0] = _dot(h, wab_ref[...])


def _inproj(x, mod, g, w, n_u, n_qkv, n_z):
    bsz, seq, d = x.shape
    tm = ROW_TILE
    full = lambda a: pl.BlockSpec(a.shape, lambda b, i: (0,) * a.ndim)
    row = lambda n: pl.BlockSpec((1, tm, n), lambda b, i: (b, i, 0))
    return pl.pallas_call(
        _inproj_kernel,
        grid=(bsz, seq // tm),
        in_specs=[row(d), pl.BlockSpec((1, N_MOD, d), lambda b, i: (b, 0, 0)), full(g),
                  pl.BlockSpec(w.shape, lambda b, i: (0, 0), pipeline_mode=pl.Buffered(1))],
        out_specs=[row(n_u), row(n_qkv), row(n_z), row(LANES)],
        out_shape=[jax.ShapeDtypeStruct((bsz, seq, n_u), F32),
                   jax.ShapeDtypeStruct((bsz, seq, n_qkv), BF16),
                   jax.ShapeDtypeStruct((bsz, seq, n_z), BF16),
                   jax.ShapeDtypeStruct((bsz, seq, LANES), F32)],
        scratch_shapes=[pltpu.VMEM((d, n_u), BF16), pltpu.VMEM((d, n_qkv), BF16),
                        pltpu.VMEM((d, n_z), BF16), pltpu.VMEM((d, LANES), BF16)],
        compiler_params=pltpu.CompilerParams(
            dimension_semantics=("arbitrary", "arbitrary"), vmem_limit_bytes=VMEM_LIMIT),
        name="inproj",
    )(x, mod, g, w)


def _s5_tables(lam_re, lam_im, log_step, b_re, b_im, c_re, c_im):
    ch = S5_CHUNK
    n_grp, n_st = lam_re.shape
    gpb = LANES // S5_GROUP
    nblk = n_grp // gpb
    step = jnp.exp(log_step.astype(F32))[:, None]
    lr = lam_re.astype(F32)
    li = lam_im.astype(F32)

    def lam_pow(j):
        mag = jnp.exp(j * lr * step)
        ang = j * li * step
        return mag * jnp.cos(ang), mag * jnp.sin(ang)

    ar, ai = lam_pow(1.0)
    den = lr * lr + li * li
    fr = ((ar - 1.0) * lr + ai * li) / den
    fi = (ai * lr - (ar - 1.0) * li) / den
    bbr = fr[..., None] * b_re - fi[..., None] * b_im
    bbi = fr[..., None] * b_im + fi[..., None] * b_re
    pows = [lam_pow(float(j)) for j in range(ch + 1)]
    pr = jnp.stack([p[0] for p in pows])
    pi = jnp.stack([p[1] for p in pows])
    lbr = pr[:ch, :, :, None] * bbr[None] - pi[:ch, :, :, None] * bbi[None]
    lbi = pr[:ch, :, :, None] * bbi[None] + pi[:ch, :, :, None] * bbr[None]
    kmat = (jnp.einsum('ghp,jgpk->jghk', c_re, lbr, precision=HIGHEST)
            - jnp.einsum('ghp,jgpk->jghk', c_im, lbi, precision=HIGHEST))
    kt = kmat.reshape(ch, nblk, gpb, S5_GROUP, S5_GROUP).transpose(1, 0, 3, 2, 4)
    kt = kt.reshape(nblk, ch, S5_GROUP, LANES)

    wri = jnp.stack([lbr[::-1], lbi[::-1]]).reshape(2, ch, nblk, gpb, n_st, S5_GROUP)
    wt = wri.transpose(2, 1, 0, 4, 3, 5).reshape(nblk, ch, 2 * n_st, LANES)

    pr1 = pr[1:, :, None, :]
    pi1 = pi[1:, :, None, :]
    clr = c_re[None] * pr1 - c_im[None] * pi1
    cli = c_re[None] * pi1 + c_im[None] * pr1
    wo = jnp.stack([clr, -cli]).reshape(2, ch, nblk, gpb, S5_GROUP, n_st)
    ot = wo.transpose(2, 1, 4, 0, 3, 5).reshape(nblk, ch, S5_GROUP, 2 * gpb * n_st)

    half = gpb * n_st
    amat = jnp.stack([pr[ch].reshape(nblk, half), pi[ch].reshape(nblk, half)], axis=1)
    return _s5_expand(kt, wt, ot, n_st) + (amat,)


def _s5_expand_kernel(kt_ref, wt_ref, ot_ref, t_ref, win_ref, wout_ref, *, n_st):
    ch = kt_ref.shape[1]
    grp = S5_GROUP
    tn = (((0,), (0,)), ((), ()))

    def iota(shape, dim):
        return lax.broadcasted_iota(jnp.int32, shape, dim)

    rep_h = jnp.where(iota((grp, LANES), 0) == iota((grp, LANES), 1) % grp, 1.0, 0.0).astype(BF16)
    sdim = win_ref.shape[2]
    ra = iota((2 * n_st, sdim), 0)
    cb = iota((2 * n_st, sdim), 1)
    rep_s = jnp.where((ra // n_st == cb // (sdim // 2)) & (ra % n_st == cb % n_st), 1.0, 0.0).astype(BF16)
    own_t = iota((LANES, LANES), 0) // grp == iota((LANES, LANES), 1) // grp
    own_w = iota((LANES, sdim), 0) // grp == (iota((LANES, sdim), 1) // n_st) % (LANES // grp)
    own_o = (iota((sdim, LANES), 0) // n_st) % (LANES // grp) == iota((sdim, LANES), 1) // grp

    taps = []
    for j in range(ch):
        blk = lax.dot_general(kt_ref[0, j].astype(BF16), rep_h, tn, preferred_element_type=F32)
        taps.append(jnp.where(own_t, blk, 0.0).astype(BF16))
    zero = jnp.zeros((LANES, LANES), BF16)
    for s in range(ch):
        for t in range(ch):
            t_ref[0, s * LANES:(s + 1) * LANES, t * LANES:(t + 1) * LANES] = taps[t - s] if t >= s else zero
        blk = lax.dot_general(wt_ref[0, s].astype(BF16), rep_s, tn, preferred_element_type=F32)
        win_ref[0, s * LANES:(s + 1) * LANES, :] = jnp.where(own_w, blk, 0.0).astype(BF16)
        blk = lax.dot_general(ot_ref[0, s].astype(BF16), rep_h, tn, preferred_element_type=F32)
        wout_ref[0, :, s * LANES:(s + 1) * LANES] = jnp.where(own_o, blk, 0.0).astype(BF16)


def _s5_expand(kt, wt, ot, n_st):
    nblk, ch = kt.shape[:2]
    kdim = ch * LANES
    sdim = ot.shape[3]
    spec = lambda a: pl.BlockSpec((1,) + a.shape[1:], lambda c: (c, 0, 0, 0))
    big = lambda r, cdim: pl.BlockSpec((1, r, cdim), lambda c: (c, 0, 0))
    return pl.pallas_call(
        functools.partial(_s5_expand_kernel, n_st=n_st),
        grid=(nblk,),
        in_specs=[spec(kt), spec(wt), spec(ot)],
        out_specs=[big(kdim, kdim), big(kdim, sdim), big(sdim, kdim)],
        out_shape=[jax.ShapeDtypeStruct((nblk, kdim, kdim), BF16),
                   jax.ShapeDtypeStruct((nblk, kdim, sdim), BF16),
                   jax.ShapeDtypeStruct((nblk, sdim, kdim), BF16)],
        compiler_params=pltpu.CompilerParams(
            dimension_semantics=("arbitrary",), vmem_limit_bytes=VMEM_LIMIT),
        name="s5_tables",
    )(kt, wt, ot)


def _s5_kernel(u_ref, t_ref, win_ref, wout_ref, a_ref, d_ref, y_ref, st_ref, v_ref, xp_ref):
    ch = S5_CHUNK
    nb = u_ref.shape[0]
    n = u_ref.shape[1] // ch
    half = st_ref.shape[2]

    @pl.when(pl.program_id(2) == 0)
    def _():
        st_ref[...] = jnp.zeros_like(st_ref)

    def slabs(b):
        return [u_ref[b, pl.ds(s, n, stride=ch), :] for s in range(ch)]

    for b in range(nb):
        v_ref[b] = _dot(jnp.concatenate(slabs(b), axis=1).astype(BF16), win_ref[0])
    a_r = a_ref[0, 0:1, :]
    a_i = a_ref[0, 1:2, :]

    def body(r, carry):
        out = []
        for b in range(nb):
            x_r, x_i = carry[b]
            xp_ref[b, pl.ds(r, 1), 0:half] = x_r
            xp_ref[b, pl.ds(r, 1), half:2 * half] = x_i
            v_r = v_ref[b, pl.ds(r, 1), 0:half]
            v_i = v_ref[b, pl.ds(r, 1), half:2 * half]
            out.append((a_r * x_r - a_i * x_i + v_r, a_r * x_i + a_i * x_r + v_i))
        return tuple(out)

    init = tuple((st_ref[b, 0:1, :], st_ref[b, 1:2, :]) for b in range(nb))
    last = lax.fori_loop(0, n, body, init, unroll=4)
    d = d_ref[0]
    for b in range(nb):
        st_ref[b, 0:1, :] = last[b][0]
        st_ref[b, 1:2, :] = last[b][1]
        sl = slabs(b)
        ucat = jnp.concatenate(sl, axis=1).astype(BF16)
        kh = ucat.shape[1] // 2
        y = jnp.concatenate([_dot(ucat[:, 0:kh], t_ref[0, 0:kh, 0:kh]), _dot(ucat, t_ref[0, :, kh:])], axis=1)
        y = y + _dot(xp_ref[b].astype(BF16), wout_ref[0])
        for t in range(ch):
            y_ref[b, pl.ds(t, n, stride=ch), :] = y[:, t * LANES:(t + 1) * LANES] + d * sl[t]


def _s5(u, tmat, win, wout, amat, d_skip):
    bsz, seq, dch = u.shape
    nblk = dch // LANES
    tt = min(S5_TILE, seq)
    n = tt // S5_CHUNK
    sdim = win.shape[2]
    nb = next(k for k in (S5_BATCH_ROWS, 2, 1) if bsz % k == 0)
    wspec = lambda a: pl.BlockSpec((1,) + a.shape[1:], lambda b, c, i: (c, 0, 0))
    return pl.pallas_call(
        _s5_kernel,
        grid=(bsz // nb, nblk, seq // tt),
        in_specs=[pl.BlockSpec((nb, tt, LANES), lambda b, c, i: (b, i, c)),
                  wspec(tmat), wspec(win), wspec(wout), wspec(amat),
                  pl.BlockSpec((1, 1, LANES), lambda b, c, i: (c, 0, 0))],
        out_specs=pl.BlockSpec((nb, tt, LANES), lambda b, c, i: (b, i, c)),
        out_shape=jax.ShapeDtypeStruct((bsz, seq, dch), F32),
        scratch_shapes=[pltpu.VMEM((nb, 2, sdim // 2), F32),
                        pltpu.VMEM((nb, n, sdim), F32),
                        pltpu.VMEM((nb, n, sdim), F32)],
        compiler_params=pltpu.CompilerParams(
            dimension_semantics=("parallel", "parallel", "arbitrary"),
            vmem_limit_bytes=VMEM_LIMIT),
        name="s5",
    )(u, tmat, win, wout, amat, d_skip.reshape(nblk, 1, LANES))


def _cumsum_rows(v):
    n = v.shape[0]
    row = lax.broadcasted_iota(jnp.int32, v.shape, 0)
    sh = 1
    while sh < n:
        v = v + jnp.where(row >= sh, pltpu.roll(v, sh, axis=0), 0.0)
        sh *= 2
    return v


def _gdn2_kernel(qkv_ref, z_ref, ab_ref, cw_ref, gp_ref, ng_ref, o_ref,
                 s_ref, xp_ref, gc_ref, gct_ref, kbq_ref, kn_ref, rhs_ref, cq_ref, qkt_ref,
                 lm_ref, d_ref, db_ref, de_ref, ds_ref, pe_ref, val_ref, vn_ref, *, n_heads):
    tt = qkv_ref.shape[1]
    dh = GDN_HEAD_DIM
    bsz = qkv_ref.shape[0]
    dg = n_heads * dh
    halo = SUBLANES
    sb = ds_ref.shape[1]
    nsb = tt // sb
    pairs = [(b, h) for b in range(bsz) for h in range(n_heads)]

    @pl.when(pl.program_id(0) == 0)
    def _():
        s_ref[...] = jnp.zeros_like(s_ref)
        xp_ref[...] = jnp.zeros_like(xp_ref)

    ri = lax.broadcasted_iota(jnp.int32, (tt, tt), 0)
    ci = lax.broadcasted_iota(jnp.int32, (tt, tt), 1)
    lane = lax.broadcasted_iota(jnp.int32, (tt, LANES), 1)
    a_log = gp_ref[0:1, :]
    dt_bias = gp_ref[1:2, :]
    shift_op = jnp.concatenate(
        [jnp.where(ri == ci + j, 1.0, 0.0) for j in range(1, CONV_WIDTH)], axis=0).astype(BF16)

    for b in range(bsz):
        x = qkv_ref[b]
        x32 = x.astype(F32)
        shifted = _dot(shift_op, x)
        edge = jnp.concatenate([xp_ref[b], x32[0:halo]], axis=0)
        conv = cw_ref[CONV_WIDTH - 1:CONV_WIDTH, :] * x32
        for j in range(1, CONV_WIDTH):
            sh_j = jnp.concatenate([edge[halo - j:2 * halo - j],
                                    shifted[(j - 1) * tt + halo:j * tt]], axis=0)
            conv = conv + cw_ref[CONV_WIDTH - 1 - j:CONV_WIDTH - j, :] * sh_j
        xp_ref[b] = x32[tt - halo:tt]
        act = _silu(conv)

        ab = ab_ref[b]
        sp = jnp.maximum(ab + dt_bias, 0.0) + jnp.log1p(jnp.exp(-jnp.abs(ab + dt_bias)))
        g = jnp.where(lane < n_heads, -jnp.exp(a_log) * sp, 0.0)
        gc = _cumsum_rows(g)
        gc_ref[b] = gc
        gct_ref[b] = gc.T
        glast = gc[tt - 1:tt, :]
        egc = jnp.exp(gc)
        ekt = jnp.exp(glast - gc)
        beta_all = _sigmoid(ab)
        for h in range(n_heads):
            p = b * n_heads + h
            q = act[:, h * dh:(h + 1) * dh]
            k = act[:, dg + h * dh:dg + (h + 1) * dh]
            v = act[:, 2 * dg + h * dh:2 * dg + (h + 1) * dh]
            q = q * lax.rsqrt(jnp.sum(q * q, axis=-1, keepdims=True) + EPS) * (dh ** -0.5)
            k = k * lax.rsqrt(jnp.sum(k * k, axis=-1, keepdims=True) + EPS)
            beta = beta_all[:, n_heads + h:n_heads + h + 1]
            eg = egc[:, h:h + 1]
            kb = k * beta
            kbq_ref[p, 0:tt, :] = kb.astype(BF16)
            kbq_ref[p, tt:2 * tt, :] = q.astype(BF16)
            kn_ref[p] = k.astype(BF16)
            rhs_ref[p, :, 0:dh] = (v * beta).astype(BF16)
            rhs_ref[p, :, dh:2 * dh] = (kb * eg).astype(BF16)
            cq_ref[p, tt:2 * tt, :] = (q * eg).astype(BF16)
            qkt_ref[p, tt:tt + dh, :] = (k * ekt[:, h:h + 1]).T.astype(BF16)

    for b, h in pairs:
        p = b * n_heads + h
        m1 = lax.dot_general(kbq_ref[p], kn_ref[p], (((1,), (1,)), ((), ())),
                             preferred_element_type=F32)
        gcol = gc_ref[b, :, h:h + 1]
        grow = gct_ref[b, h:h + 1, :]
        decay = jnp.exp(jnp.where(ri >= ci, gcol - grow, -1e30))
        lm = jnp.where(ri > ci, m1[0:tt] * decay, 0.0)
        lm_ref[p] = lm.astype(BF16)
        qkt_ref[p, 0:tt, :] = (m1[tt:2 * tt] * decay).astype(BF16)
        first = jnp.where(ri == ci + 1, jnp.where((ci & 1) == 0, lm, 0.0), 0.0)
        d = jnp.where(ri == ci, 1.0, 0.0) - first
        ds_ref[p] = sum(d[j * sb:(j + 1) * sb] for j in range(1, nsb)) + d[0:sb]

    in_blk = (ri // sb) == (ci // sb)
    blk_b = jnp.where(in_blk, 1.0, 0.0).astype(BF16)
    m = 2
    sh = 1
    while m < tt:
        rb = ri >> sh
        cb = ci >> sh
        sel = jnp.where(rb == cb + 1, jnp.where((cb & 1) == 0, 1.0, 0.0), 0.0).astype(BF16)
        if 2 * m <= sb:
            for b, h in pairs:
                p = b * n_heads + h
                pe_ref[p] = _dot(ds_ref[p].astype(BF16), lm_ref[p] * sel).astype(BF16)
            for b, h in pairs:
                p = b * n_heads + h
                ds = ds_ref[p]
                dbd = jnp.concatenate([ds.astype(BF16)] * nsb, axis=0) * blk_b
                ds = ds - _dot(pe_ref[p], dbd)
                ds_ref[p] = ds
                if 4 * m > sb:
                    d = jnp.where(in_blk, jnp.concatenate([ds] * nsb, axis=0), 0.0)
                    d_ref[p] = d
                    db_ref[p] = d.astype(BF16)
        else:
            for b, h in pairs:
                p = b * n_heads + h
                de_ref[p] = _dot(db_ref[p], lm_ref[p] * sel).astype(BF16)
            for b, h in pairs:
                p = b * n_heads + h
                d = d_ref[p] - _dot(de_ref[p], db_ref[p])
                d_ref[p] = d
                db_ref[p] = d.astype(BF16)
        m *= 2
        sh += 1

    for b, h in pairs:
        p = b * n_heads + h
        w = _dot(db_ref[p], rhs_ref[p])
        val_ref[p] = w[:, 0:dh]
        cq_ref[p, 0:tt, :] = w[:, dh:2 * dh].astype(BF16)

    for b, h in pairs:
        p = b * n_heads + h
        m2 = _dot(cq_ref[p], s_ref[p].astype(BF16))
        vn_ref[p] = (val_ref[p] - m2[0:tt]).astype(BF16)
        val_ref[p] = m2[tt:2 * tt]
    for b, h in pairs:
        p = b * n_heads + h
        r = _dot(qkt_ref[p], vn_ref[p])
        o = val_ref[p] + r[0:tt]
        egl = jnp.exp(gc_ref[b, tt - 1:tt, h:h + 1])
        s_ref[p] = s_ref[p] * egl + r[tt:tt + dh]
        zh = z_ref[b, :, h * dh:(h + 1) * dh].astype(F32)
        o_ref[b, :, h * dh:(h + 1) * dh] = (_rms(o) * ng_ref[...] * _silu(zh)).astype(o_ref.dtype)


def _gdn2(qkv, z, ab, conv_w, a_log, dt_bias, norm_g):
    bsz, seq, _ = qkv.shape
    dg = z.shape[2]
    dh = GDN_HEAD_DIM
    n_heads = dg // dh
    tt = GDN_BLOCK
    npair = bsz * n_heads
    gp = jnp.zeros((2, LANES), F32).at[0, :n_heads].set(a_log).at[1, :n_heads].set(dt_bias)
    blk = lambda n: pl.BlockSpec((bsz, tt, n), lambda i: (0, i, 0))
    full = lambda a: pl.BlockSpec(a.shape, lambda i: (0,) * a.ndim)
    ng = norm_g.reshape(1, dh)
    return pl.pallas_call(
        functools.partial(_gdn2_kernel, n_heads=n_heads),
        grid=(seq // tt,),
        in_specs=[blk(3 * dg), blk(dg), blk(LANES), full(conv_w), full(gp), full(ng)],
        out_specs=blk(dg),
        out_shape=jax.ShapeDtypeStruct((bsz, seq, dg), BF16),
        scratch_shapes=[pltpu.VMEM((npair, dh, dh), F32),
                        pltpu.VMEM((bsz, SUBLANES, 3 * dg), F32),
                        pltpu.VMEM((bsz, tt, LANES), F32),
                        pltpu.VMEM((bsz, LANES, tt), F32),
                        pltpu.VMEM((npair, 2 * tt, dh), BF16),
                        pltpu.VMEM((npair, tt, dh), BF16),
                        pltpu.VMEM((npair, tt, 2 * dh), BF16),
                        pltpu.VMEM((npair, 2 * tt, dh), BF16),
                        pltpu.VMEM((npair, tt + dh, tt), BF16),
                        pltpu.VMEM((npair, tt, tt), BF16),
                        pltpu.VMEM((npair, tt, tt), F32),
                        pltpu.VMEM((npair, tt, tt), BF16),
                        pltpu.VMEM((npair, tt, tt), BF16),
                        pltpu.VMEM((npair, GDN_SIDE, tt), F32),
                        pltpu.VMEM((npair, GDN_SIDE, tt), BF16),
                        pltpu.VMEM((npair, tt, dh), F32),
                        pltpu.VMEM((npair, tt, dh), BF16)],
        compiler_params=pltpu.CompilerParams(
            dimension_semantics=("arbitrary",), vmem_limit_bytes=VMEM_LIMIT),
        name="gdn",
    )(qkv, z, ab, conv_w, gp, ng)


def _outproj_kernel(ys_ref, yg_ref, x_ref, mod_ref, wglu_ref, bglu_ref, wo_ref,
                    g2_ref, wr_ref, h_ref, m_ref, route_ref, routet_ref, count_ref,
                    cnt_ref, wglub_ref, wob_ref, mt_ref):
    @pl.when((pl.program_id(0) == 0) & (pl.program_id(1) == 0))
    def _():
        wglub_ref[...] = wglu_ref[...].astype(BF16)
        wob_ref[...] = wo_ref[...].astype(BF16)

    @pl.when((pl.program_id(0) == 0) & (pl.program_id(1) == 0))
    def _():
        cnt_ref[...] = jnp.zeros_like(cnt_ref)

    d_s5 = wglub_ref.shape[0]
    tm = x_ref.shape[1]
    hm = tm // OUTPROJ_PARTS
    earlier = jnp.where(lax.broadcasted_iota(jnp.int32, (hm, hm), 0)
                        > lax.broadcasted_iota(jnp.int32, (hm, hm), 1), 1.0, 0.0).astype(BF16)
    lane = lax.broadcasted_iota(jnp.int32, (hm, LANES), 1)
    neg = -1e30
    is_grp = lane < N_EXPERT_GROUPS
    seen = cnt_ref[...]
    for part in range(OUTPROJ_PARTS):
        rows = slice(part * hm, (part + 1) * hm)
        y = jax.nn.gelu(ys_ref[0, rows, :])
        gate = _sigmoid(_dot(y.astype(BF16), wglub_ref[...]) + bglu_ref[...])
        y = (y * gate).astype(BF16)
        mix = _dot(y, wob_ref[0:d_s5, :]) + _dot(yg_ref[0, rows, :], wob_ref[d_s5:, :])
        h = x_ref[0, rows, :] + mod_ref[0, 2:3, :] * mix
        h_ref[0, rows, :] = h
        m = _rms(h) * g2_ref[...] * (1.0 + mod_ref[0, 4:5, :]) + mod_ref[0, 3:4, :]
        _store_token_tiles(mt_ref, (), m, row0=part * hm)

        m_hi = m.astype(BF16)
        m_lo = (m - m_hi.astype(F32)).astype(BF16)
        both = _dot(m_hi, wr_ref[...])
        logits = both[:, 0:LANES] + both[:, LANES:2 * LANES] + _dot(m_lo, wr_ref[:, 0:LANES])
        gl = jnp.where(is_grp, logits, neg)
        gmax = jnp.max(gl, axis=-1, keepdims=True)
        gidx = jnp.min(jnp.where(gl == gmax, lane, LANES), axis=-1, keepdims=True)
        p_grp = 1.0 / jnp.sum(jnp.where(is_grp, jnp.exp(gl - gmax), 0.0), axis=-1, keepdims=True)
        lo = ROUTE_OFF + gidx * EXPERTS_PER_GROUP
        el = jnp.where((lane >= lo) & (lane < lo + EXPERTS_PER_GROUP), logits, neg)
        v1 = jnp.max(el, axis=-1, keepdims=True)
        i1 = jnp.min(jnp.where(el == v1, lane, LANES), axis=-1, keepdims=True)
        el2 = jnp.where(lane == i1, neg, el)
        v2 = jnp.max(el2, axis=-1, keepdims=True)
        i2 = jnp.min(jnp.where(el2 == v2, lane, LANES), axis=-1, keepdims=True)
        t = jnp.exp(v2 - v1)
        w1 = p_grp / (1.0 + t)
        w2 = w1 * t

        oh1 = lane == i1
        oh2 = lane == i2
        cnt = jnp.where(oh1, 1.0, 0.0) + jnp.where(oh2, 1.0, 0.0)
        before = _dot(earlier, cnt.astype(BF16)) + seen
        r1 = jnp.sum(jnp.where(oh1, before, 0.0), axis=-1, keepdims=True)
        r2 = jnp.sum(jnp.where(oh2, before, 0.0), axis=-1, keepdims=True)
        seen = seen + jnp.sum(cnt, axis=0, keepdims=True)
        cols = [(i1 - ROUTE_OFF).astype(F32), (i2 - ROUTE_OFF).astype(F32), w1, w2, r1, r2]
        route = jnp.zeros_like(logits)
        for j, col in enumerate(cols):
            route = jnp.where(lane == j, col, route)
        route_ref[0, rows, :] = route
        routet_ref[:, rows] = route.T[0:SUBLANES, :]
    cnt_ref[...] = seen
    count_ref[0] = seen
    m_ref[0] = mt_ref[...].astype(m_ref.dtype)


def _outproj(ys, yg, x, mod, wglu, bglu, wo, g2, wr):
    bsz, seq, d = x.shape
    tm = ROW_TILE
    full = lambda a: pl.BlockSpec(a.shape, lambda b, i: (0,) * a.ndim, pipeline_mode=pl.Buffered(1))
    row = lambda n: pl.BlockSpec((1, tm, n), lambda b, i: (b, i, 0))
    return pl.pallas_call(
        _outproj_kernel,
        grid=(bsz, seq // tm),
        in_specs=[row(ys.shape[2]), row(yg.shape[2]), row(d),
                  pl.BlockSpec((1, N_MOD, d), lambda b, i: (b, 0, 0)),
                  full(wglu), full(bglu), full(wo), full(g2), full(wr)],
        out_specs=[row(d), pl.BlockSpec((1, tm * (d // LANES), LANES), lambda b, i: (b, i, 0)), row(LANES),
                   pl.BlockSpec((SUBLANES, tm), lambda b, i: (0, b * (seq // tm) + i)),
                   pl.BlockSpec((1, 1, LANES), lambda b, i: (b * (seq // tm) + i, 0, 0))],
        out_shape=[jax.ShapeDtypeStruct((bsz, seq, d), F32),
                   jax.ShapeDtypeStruct((bsz, seq * (d // LANES), LANES), BF16),
                   jax.ShapeDtypeStruct((bsz, seq, LANES), F32),
                   jax.ShapeDtypeStruct((SUBLANES, bsz * seq), F32),
                   jax.ShapeDtypeStruct((bsz * (seq // tm), 1, LANES), F32)],
        scratch_shapes=[pltpu.VMEM((1, LANES), F32), pltpu.VMEM(wglu.shape, BF16), pltpu.VMEM(wo.shape, BF16),
                        pltpu.VMEM((tm * (d // LANES), LANES), F32)],
        compiler_params=pltpu.CompilerParams(
            dimension_semantics=("arbitrary", "arbitrary"), vmem_limit_bytes=VMEM_LIMIT),
        name="outproj",
    )(ys, yg, x, mod, wglu, bglu, wo, g2, wr)


def _moe_scatter_kernel(d1_ref, d2_ref, pad_ref, m_ref, xs_hbm, zero_ref, sem, *, n_exp, k):
    i = pl.program_id(0)
    ts = d1_ref.shape[2]
    trows = zero_ref.shape[0]

    def tok(ref, t):
        return ref.at[pl.ds(pl.multiple_of(t * k, k), k)]

    def tile_copy(j):
        return pltpu.make_async_copy(
            zero_ref, xs_hbm.at[pl.ds(pl.multiple_of(j * trows, trows), trows)], sem.at[0])

    def pad_piece(e, p, act):
        lo = pad_ref[0, e]
        length = pad_ref[1, e] - lo
        start = lo + (length & ~(2 * p - 1))
        copy = pltpu.make_async_copy(
            zero_ref.at[pl.ds(0, p * k)], xs_hbm.at[pl.ds(pl.multiple_of(start * k, k), p * k)], sem.at[0])

        @pl.when((length & p) != 0)
        def _():
            act(copy)

    @pl.when(i == 0)
    def _():
        zero_ref[...] = jnp.zeros_like(zero_ref)
        pieces = [(e, p) for e in range(n_exp) for p in [1 << j for j in range((trows // k).bit_length() - 1)]]
        for e, p in pieces:
            pad_piece(e, p, lambda c: c.start())
        for e, p in pieces:
            pad_piece(e, p, lambda c: c.wait())

        def fill(j, carry):
            tile_copy(j).start()
            return carry

        def drain(j, carry):
            tile_copy(j).wait()
            return carry

        lax.fori_loop(pad_ref[0, n_exp], pad_ref[1, n_exp], fill, 0)
        lax.fori_loop(pad_ref[0, n_exp], pad_ref[1, n_exp], drain, 0)

    def row_copy(t, dst, slot):
        return pltpu.make_async_copy(tok(m_ref, t), tok(xs_hbm, dst), sem.at[1 + slot])

    def issue(t, carry):
        row_copy(t, d1_ref[0, 0, t], 0).start(priority=0)
        row_copy(t, d2_ref[0, 0, t], 1).start(priority=1)
        return carry

    lax.fori_loop(0, ts, issue, 0, unroll=8)
    for slot in range(2):
        pltpu.make_async_copy(m_ref, xs_hbm.at[pl.ds(0, ts * k)], sem.at[1 + slot]).wait()


def _moe_scatter(m, dest1, dest2, pads, n_rows, k):
    n_tok = m.shape[0] // k
    ts = MOE_SCATTER_TILE
    n_exp = pads.shape[1] - 1
    dspec = pl.BlockSpec((1, 1, ts), lambda i: (i, 0, 0), memory_space=pltpu.SMEM)
    return pl.pallas_call(
        functools.partial(_moe_scatter_kernel, n_exp=n_exp, k=k),
        grid=(n_tok // ts,),
        in_specs=[dspec, dspec,
                  pl.BlockSpec(pads.shape, lambda i: (0, 0), memory_space=pltpu.SMEM),
                  pl.BlockSpec((ts * k, LANES), lambda i: (i, 0))],
        out_specs=pl.BlockSpec(memory_space=pl.ANY),
        out_shape=jax.ShapeDtypeStruct((n_rows * k, LANES), m.dtype),
        scratch_shapes=[pltpu.VMEM((MOE_ROWS * k, LANES), m.dtype), pltpu.SemaphoreType.DMA((3,))],
        compiler_params=pltpu.CompilerParams(
            dimension_semantics=("arbitrary",), vmem_limit_bytes=VMEM_LIMIT, has_side_effects=True),
        name="moe_scatter",
    )(dest1.reshape(n_tok // ts, 1, ts), dest2.reshape(n_tok // ts, 1, ts), pads, m)


def _moe_ffn_kernel(te_ref, na_ref, xs_ref, wg_ref, wu_ref, wd_ref, ys_ref, wgb_ref, wub_ref, wdb_ref,
                    tile_ref):
    i = pl.program_id(0)
    active = i < na_ref[0]
    changed = (i == 0) | (te_ref[i] != te_ref[jnp.maximum(i - 1, 0)])

    @pl.when(active & changed)
    def _():
        wgb_ref[...] = wg_ref[0].astype(BF16)
        wub_ref[...] = wu_ref[0].astype(BF16)
        wdb_ref[...] = wd_ref[0].astype(BF16)

    @pl.when(active)
    def _():
        k = wgb_ref.shape[0] // LANES
        tile_ref[...] = xs_ref[...].astype(F32)
        x = _load_token_tiles(tile_ref, (), xs_ref.shape[0] // k, k).astype(BF16)
        mid = (_silu(_dot(x, wgb_ref[...])) * _dot(x, wub_ref[...])).astype(BF16)
        _store_token_tiles(tile_ref, (), _dot(mid, wdb_ref[...]))
        ys_ref[...] = tile_ref[...].astype(ys_ref.dtype)


def _moe_ffn(xs, tile_expert, n_active, wg, wu, wd):
    _, d, de = wg.shape
    tmr = MOE_ROWS * (d // LANES)
    n_rows, dp = xs.shape
    grid_spec = pltpu.PrefetchScalarGridSpec(
        num_scalar_prefetch=2,
        grid=(n_rows // tmr,),
        in_specs=[pl.BlockSpec((tmr, dp), lambda i, te, na: (jnp.minimum(i, na[0] - 1), 0)),
                  pl.BlockSpec((1, d, de), lambda i, te, na: (te[i], 0, 0)),
                  pl.BlockSpec((1, d, de), lambda i, te, na: (te[i], 0, 0)),
                  pl.BlockSpec((1, de, d), lambda i, te, na: (te[i], 0, 0))],
        out_specs=pl.BlockSpec((tmr, dp), lambda i, te, na: (jnp.minimum(i, na[0] - 1), 0)),
        scratch_shapes=[pltpu.VMEM((d, de), BF16), pltpu.VMEM((d, de), BF16), pltpu.VMEM((de, d), BF16),
                        pltpu.VMEM((tmr, dp), F32)],
    )
    return pl.pallas_call(
        _moe_ffn_kernel,
        grid_spec=grid_spec,
        out_shape=jax.ShapeDtypeStruct((n_rows, dp), xs.dtype),
        input_output_aliases={2: 0},
        compiler_params=pltpu.CompilerParams(
            dimension_semantics=("arbitrary",), vmem_limit_bytes=VMEM_LIMIT),
        name="moe_ffn",
    )(tile_expert, n_active, xs, wg, wu, wd)


def _moe_combine_kernel(d1_ref, d2_ref, n1_ref, n2_ref, h_ref, route_ref, mod_ref, nf_ref, ys_hbm, o_ref,
                        b1_ref, b2_ref, tile_ref, sem, *, final_norm):
    tm = h_ref.shape[1]
    k = h_ref.shape[2] // LANES
    g = pl.program_id(0)
    slot = g % 2

    def tok(ref, t):
        return ref.at[pl.ds(pl.multiple_of(t * k, k), k)]

    def gather(i1_ref, i2_ref, s):
        def issue(t, carry):
            pltpu.make_async_copy(tok(ys_hbm, i1_ref[0, 0, t]), tok(b1_ref.at[s], t),
                                  sem.at[s, 0]).start(priority=0)
            pltpu.make_async_copy(tok(ys_hbm, i2_ref[0, 0, t]), tok(b2_ref.at[s], t),
                                  sem.at[s, 1]).start(priority=1)
            return carry

        lax.fori_loop(0, tm, issue, 0, unroll=8)

    @pl.when(g == 0)
    def _():
        gather(d1_ref, d2_ref, 0)

    @pl.when(g + 1 < pl.num_programs(0))
    def _():
        gather(n1_ref, n2_ref, 1 - slot)

    pltpu.make_async_copy(ys_hbm.at[pl.ds(0, tm * k)], b1_ref.at[slot], sem.at[slot, 0]).wait()
    pltpu.make_async_copy(ys_hbm.at[pl.ds(0, tm * k)], b2_ref.at[slot], sem.at[slot, 1]).wait()
    route = route_ref[0]
    tile_ref[...] = b1_ref[slot].astype(F32)
    moe = route[:, 2:3] * _load_token_tiles(tile_ref, (), tm, k)
    tile_ref[...] = b2_ref[slot].astype(F32)
    moe = moe + route[:, 3:4] * _load_token_tiles(tile_ref, (), tm, k)
    h = h_ref[0] + mod_ref[0, 5:6, :] * moe
    if final_norm:
        h = _rms(h) * nf_ref[...]
    o_ref[0] = h


def _moe_combine(ys, dest1, dest2, h, route, mod, nf, final_norm):
    bsz, seq, d = h.shape
    tm = MOE_COMBINE_TILE
    nt = seq // tm
    steps = bsz * nt
    dspec = pl.BlockSpec((1, 1, tm), lambda g: (g, 0, 0), memory_space=pltpu.SMEM)
    nspec = pl.BlockSpec((1, 1, tm), lambda g: (jnp.minimum(g + 1, steps - 1), 0, 0), memory_space=pltpu.SMEM)
    row = lambda n: pl.BlockSpec((1, tm, n), lambda g: (g // nt, g % nt, 0))
    d1 = dest1.reshape(steps, 1, tm)
    d2 = dest2.reshape(steps, 1, tm)
    return pl.pallas_call(
        functools.partial(_moe_combine_kernel, final_norm=final_norm),
        grid=(steps,),
        in_specs=[dspec, dspec, nspec, nspec, row(d), row(LANES),
                  pl.BlockSpec((1, N_MOD, d), lambda g: (g // nt, 0, 0)),
                  pl.BlockSpec(nf.shape, lambda g: (0, 0)),
                  pl.BlockSpec(memory_space=pl.ANY)],
        out_specs=row(d),
        out_shape=jax.ShapeDtypeStruct((bsz, seq, d), F32),
        scratch_shapes=[pltpu.VMEM((2, tm * (d // LANES), LANES), ys.dtype),
                        pltpu.VMEM((2, tm * (d // LANES), LANES), ys.dtype),
                        pltpu.VMEM((tm * (d // LANES), LANES), F32),
                        pltpu.SemaphoreType.DMA((2, 2))],
        compiler_params=pltpu.CompilerParams(
            dimension_semantics=("arbitrary",), vmem_limit_bytes=VMEM_LIMIT),
        name="moe_combine",
    )(d1, d2, d1, d2, h, route, mod, nf, ys)


def _moe_sparse(m, route, route_t, counts, wg, wu, wd, h, mod, nf, final_norm):
    bsz, seq, d = h.shape
    n_tok = bsz * seq
    tmr = MOE_ROWS
    n_exp = wg.shape[0]
    n_tiles = (2 * n_tok) // tmr + n_exp
    i32 = jnp.int32
    e1, e2, k1, k2 = (route_t[j].astype(i32) for j in (0, 1, 4, 5))
    cnt = counts[-1, 0, ROUTE_OFF:ROUTE_OFF + n_exp].astype(i32)
    ptiles = (cnt + tmr - 1) // tmr
    tend = jnp.cumsum(ptiles)
    gstart = (tend - ptiles) * tmr
    eids = jnp.arange(n_exp, dtype=i32)
    dest1 = k1 + jnp.sum(jnp.where(e1[:, None] == eids[None, :], gstart[None, :], 0), axis=1)
    dest2 = k2 + jnp.sum(jnp.where(e2[:, None] == eids[None, :], gstart[None, :], 0), axis=1)
    n_active = tend[-1:]
    tile_expert = jnp.sum(jnp.arange(n_tiles, dtype=i32)[:, None] >= tend[None, :], axis=1).astype(i32)
    last_expert = jnp.sum(n_active - 1 >= tend).astype(i32)
    tile_expert = jnp.minimum(tile_expert, last_expert)
    pads = jnp.stack([jnp.concatenate([gstart + cnt, n_active]),
                      jnp.concatenate([gstart + ptiles * tmr, jnp.full((1,), n_tiles, i32)])]).astype(i32)
    k = d // LANES
    xs = _moe_scatter(m.reshape(n_tok * k, LANES), dest1, dest2, pads, n_tiles * tmr, k)
    ys = _moe_ffn(xs, tile_expert, n_active.astype(i32), wg, wu, wd)
    return _moe_combine(ys, dest1, dest2, h, route, mod, nf, final_norm)


def kernel(x, c, norm1_g, norm2_g, w_ada, b_ada, w_in, lam_re, lam_im, log_step, s5_b_re, s5_b_im,
           s5_c_re, s5_c_im, s5_d, w_glu, b_glu, conv_w, a_log, dt_bias, gdn_norm_g, w_out,
           w_router_grp, w_router_exp, w_gate, w_up, w_down, normf_g):
    bsz, seq, d = x.shape
    depth = w_ada.shape[0]
    d_s5 = s5_d.shape[1]
    d_gdn = w_out.shape[1] - d_s5
    n_heads = d_gdn // GDN_HEAD_DIM
    assert seq % min(S5_TILE, seq) == 0 and seq % ROW_TILE == 0 and seq % GDN_BLOCK == 0
    assert seq % MOE_COMBINE_TILE == 0 and (bsz * seq) % MOE_SCATTER_TILE == 0
    assert d % LANES == 0 and d_s5 % LANES == 0 and d_gdn % GDN_HEAD_DIM == 0
    assert 2 * n_heads <= LANES and N_EXPERT_GROUPS + N_EXPERTS <= LANES
    assert w_gate.shape[1] == N_EXPERTS and w_router_grp.shape[2] == N_EXPERT_GROUPS

    h = x
    for l in range(depth):
        mod = _ada(c, w_ada[l], b_ada[l]).reshape(bsz, N_MOD, d)
        wi = w_in[l]
        c0, c1, c2 = d_s5, d_s5 + 3 * d_gdn, d_s5 + 4 * d_gdn
        u, qkv, z, ab = _inproj(h, mod, norm1_g[l].reshape(1, d), wi, c0, c1 - c0, c2 - c1)
        tabs = _s5_tables(lam_re[l], lam_im[l], log_step[l], s5_b_re[l], s5_b_im[l],
                          s5_c_re[l], s5_c_im[l])
        ys = _s5(u, *tabs, s5_d[l])
        yg = _gdn2(qkv, z, ab, conv_w[l], a_log[l], dt_bias[l], gdn_norm_g[l])
        wr = jnp.zeros((d, LANES), F32)
        wr = wr.at[:, :N_EXPERT_GROUPS].set(w_router_grp[l])
        wr = wr.at[:, ROUTE_OFF:ROUTE_OFF + N_EXPERTS].set(w_router_exp[l])
        wr_hi = wr.astype(BF16)
        wr = jnp.concatenate([wr_hi, (wr - wr_hi.astype(F32)).astype(BF16)], axis=1)
        h1, m, route, route_t, counts = _outproj(ys, yg, h, mod, w_glu[l], b_glu[l].reshape(1, d_s5),
                                                 w_out[l], norm2_g[l].reshape(1, d), wr)
        h = _moe_sparse(m, route, route_t, counts, w_gate[l], w_up[l], w_down[l], h1, mod,
                        normf_g.reshape(1, d), final_norm=(l == depth - 1))
    return h
```

```python
import functools

import jax
import jax.numpy as jnp
from jax import lax
from jax.experimental import pallas as pl
from jax.experimental.pallas import tpu as pltpu

F32 = jnp.float32
BF16 = jnp.bfloat16
HIGHEST = lax.Precision.HIGHEST
EPS = 1e-6

S5_GROUP = 16
GDN_HEAD_DIM = 128
CONV_WIDTH = 4
N_EXPERT_GROUPS = 4
EXPERTS_PER_GROUP = 8
N_EXPERTS = N_EXPERT_GROUPS * EXPERTS_PER_GROUP
N_MOD = 6

LANES = 128
SUBLANES = 8
VMEM_LIMIT = 56 * 1024 * 1024

S5_CHUNK = SUBLANES
S5_TILE = 2048
S5_BATCH_ROWS = 4
GDN_BLOCK = 256
GDN_SIDE = 64
ROW_TILE = 512
OUTPROJ_PARTS = 1
MOE_ROWS = 512
MOE_SCATTER_TILE = 2048
MOE_COMBINE_TILE = 1024
ROUTE_OFF = N_EXPERT_GROUPS


def _dot(a, b, **kw):
    return jnp.dot(a, b, preferred_element_type=F32, **kw)


def _sigmoid(v):
    return 0.5 * jnp.tanh(0.5 * v) + 0.5


def _silu(v):
    return v * _sigmoid(v)


def _rms(v):
    return v * lax.rsqrt(jnp.mean(v * v, axis=-1, keepdims=True) + EPS)


def _store_token_tiles(ref, idx, v, row0=0):
    n, width = v.shape
    k = width // LANES
    for s in range(k):
        ref[idx + (pl.ds(row0 * k + s, n, stride=k), slice(None))] = v[:, s * LANES:(s + 1) * LANES]


def _load_token_tiles(ref, idx, n, k):
    return jnp.concatenate([ref[idx + (pl.ds(s, n, stride=k), slice(None))] for s in range(k)], axis=1)


def _ada_kernel(c_ref, w_ref, b_ref, o_ref):
    o_ref[...] = _dot(_silu(c_ref[...]), w_ref[...], precision=HIGHEST) + b_ref[...]


def _ada(c, w, b):
    bsz, d = c.shape
    n = w.shape[1]
    cols = d
    return pl.pallas_call(
        _ada_kernel,
        grid=(n // cols,),
        in_specs=[pl.BlockSpec((bsz, d), lambda j: (0, 0)),
                  pl.BlockSpec((d, cols), lambda j: (0, j)),
                  pl.BlockSpec((1, cols), lambda j: (0, j))],
        out_specs=pl.BlockSpec((bsz, cols), lambda j: (0, j)),
        out_shape=jax.ShapeDtypeStruct((bsz, n), F32),
        compiler_params=pltpu.CompilerParams(vmem_limit_bytes=VMEM_LIMIT),
        name="ada",
    )(c, w, b.reshape(1, n))


def _inproj_kernel(x_ref, mod_ref, g_ref, w_ref, u_ref, qkv_ref, z_ref, ab_ref,
                   wu_ref, wqkv_ref, wz_ref, wab_ref):
    @pl.when((pl.program_id(0) == 0) & (pl.program_id(1) == 0))
    def _():
        c0 = wu_ref.shape[1]
        c1 = c0 + wqkv_ref.shape[1]
        c2 = c1 + wz_ref.shape[1]
        n_gate = w_ref.shape[1] - c2
        wu_ref[...] = w_ref[:, 0:c0].astype(BF16)
        wqkv_ref[...] = w_ref[:, c0:c1].astype(BF16)
        wz_ref[...] = w_ref[:, c1:c2].astype(BF16)
        wab_ref[...] = jnp.zeros_like(wab_ref)
        wab_ref[:, 0:n_gate] = w_ref[:, c2:c2 + n_gate].astype(BF16)

    x = x_ref[0]
    y = _rms(x) * g_ref[...]
    h = (y * (1.0 + mod_ref[0, 1:2, :]) + mod_ref[0, 0:1, :]).astype(BF16)
    u_ref[0] = _dot(h, wu_ref[...])
    qkv_ref[0] = _dot(h, wqkv_ref[...]).astype(BF16)
    z_ref[0] = _dot(h, wz_ref[...]).astype(BF16)
    ab_ref[0] = _dot(h, wab_ref[...])


def _inproj(x, mod, g, w, n_u, n_qkv, n_z):
    bsz, seq, d = x.shape
    tm = ROW_TILE
    full = lambda a: pl.BlockSpec(a.shape, lambda b, i: (0,) * a.ndim)
    row = lambda n: pl.BlockSpec((1, tm, n), lambda b, i: (b, i, 0))
    return pl.pallas_call(
        _inproj_kernel,
        grid=(bsz, seq // tm),
        in_specs=[row(d), pl.BlockSpec((1, N_MOD, d), lambda b, i: (b, 0, 0)), full(g),
                  pl.BlockSpec(w.shape, lambda b, i: (0, 0), pipeline_mode=pl.Buffered(1))],
        out_specs=[row(n_u), row(n_qkv), row(n_z), row(LANES)],
        out_shape=[jax.ShapeDtypeStruct((bsz, seq, n_u), F32),
                   jax.ShapeDtypeStruct((bsz, seq, n_qkv), BF16),
                   jax.ShapeDtypeStruct((bsz, seq, n_z), BF16),
                   jax.ShapeDtypeStruct((bsz, seq, LANES), F32)],
        scratch_shapes=[pltpu.VMEM((d, n_u), BF16), pltpu.VMEM((d, n_qkv), BF16),
                        pltpu.VMEM((d, n_z), BF16), pltpu.VMEM((d, LANES), BF16)],
        compiler_params=pltpu.CompilerParams(
            dimension_semantics=("arbitrary", "arbitrary"), vmem_limit_bytes=VMEM_LIMIT),
        name="inproj",
    )(x, mod, g, w)


def _s5_tables(lam_re, lam_im, log_step, b_re, b_im, c_re, c_im):
    ch = S5_CHUNK
    n_grp, n_st = lam_re.shape
    gpb = LANES // S5_GROUP
    nblk = n_grp // gpb
    step = jnp.exp(log_step.astype(F32))[:, None]
    lr = lam_re.astype(F32)
    li = lam_im.astype(F32)

    def lam_pow(j):
        mag = jnp.exp(j * lr * step)
        ang = j * li * step
        return mag * jnp.cos(ang), mag * jnp.sin(ang)

    ar, ai = lam_pow(1.0)
    den = lr * lr + li * li
    fr = ((ar - 1.0) * lr + ai * li) / den
    fi = (ai * lr - (ar - 1.0) * li) / den
    bbr = fr[..., None] * b_re - fi[..., None] * b_im
    bbi = fr[..., None] * b_im + fi[..., None] * b_re
    pows = [lam_pow(float(j)) for j in range(ch + 1)]
    pr = jnp.stack([p[0] for p in pows])
    pi = jnp.stack([p[1] for p in pows])
    lbr = pr[:ch, :, :, None] * bbr[None] - pi[:ch, :, :, None] * bbi[None]
    lbi = pr[:ch, :, :, None] * bbi[None] + pi[:ch, :, :, None] * bbr[None]
    kmat = (jnp.einsum('ghp,jgpk->jghk', c_re, lbr, precision=HIGHEST)
            - jnp.einsum('ghp,jgpk->jghk', c_im, lbi, precision=HIGHEST))
    kt = kmat.reshape(ch, nblk, gpb, S5_GROUP, S5_GROUP).transpose(1, 0, 3, 2, 4)
    kt = kt.reshape(nblk, ch, S5_GROUP, LANES)

    wri = jnp.stack([lbr[::-1], lbi[::-1]]).reshape(2, ch, nblk, gpb, n_st, S5_GROUP)
    wt = wri.transpose(2, 1, 0, 4, 3, 5).reshape(nblk, ch, 2 * n_st, LANES)

    pr1 = pr[1:, :, None, :]
    pi1 = pi[1:, :, None, :]
    clr = c_re[None] * pr1 - c_im[None] * pi1
    cli = c_re[None] * pi1 + c_im[None] * pr1
    wo = jnp.stack([clr, -cli]).reshape(2, ch, nblk, gpb, S5_GROUP, n_st)
    ot = wo.transpose(2, 1, 4, 0, 3, 5).reshape(nblk, ch, S5_GROUP, 2 * gpb * n_st)

    half = gpb * n_st
    amat = jnp.stack([pr[ch].reshape(nblk, half), pi[ch].reshape(nblk, half)], axis=1)
    return _s5_expand(kt, wt, ot, n_st) + (amat,)


def _s5_expand_kernel(kt_ref, wt_ref, ot_ref, t_ref, win_ref, wout_ref, *, n_st):
    ch = kt_ref.shape[1]
    grp = S5_GROUP
    tn = (((0,), (0,)), ((), ()))

    def iota(shape, dim):
        return lax.broadcasted_iota(jnp.int32, shape, dim)

    rep_h = jnp.where(iota((grp, LANES), 0) == iota((grp, LANES), 1) % grp, 1.0, 0.0).astype(BF16)
    sdim = win_ref.shape[2]
    ra = iota((2 * n_st, sdim), 0)
    cb = iota((2 * n_st, sdim), 1)
    rep_s = jnp.where((ra // n_st == cb // (sdim // 2)) & (ra % n_st == cb % n_st), 1.0, 0.0).astype(BF16)
    own_t = iota((LANES, LANES), 0) // grp == iota((LANES, LANES), 1) // grp
    own_w = iota((LANES, sdim), 0) // grp == (iota((LANES, sdim), 1) // n_st) % (LANES // grp)
    own_o = (iota((sdim, LANES), 0) // n_st) % (LANES // grp) == iota((sdim, LANES), 1) // grp

    taps = []
    for j in range(ch):
        blk = lax.dot_general(kt_ref[0, j].astype(BF16), rep_h, tn, preferred_element_type=F32)
        taps.append(jnp.where(own_t, blk, 0.0).astype(BF16))
    zero = jnp.zeros((LANES, LANES), BF16)
    for s in range(ch):
        for t in range(ch):
            t_ref[0, s * LANES:(s + 1) * LANES, t * LANES:(t + 1) * LANES] = taps[t - s] if t >= s else zero
        blk = lax.dot_general(wt_ref[0, s].astype(BF16), rep_s, tn, preferred_element_type=F32)
        win_ref[0, s * LANES:(s + 1) * LANES, :] = jnp.where(own_w, blk, 0.0).astype(BF16)
        blk = lax.dot_general(ot_ref[0, s].astype(BF16), rep_h, tn, preferred_element_type=F32)
        wout_ref[0, :, s * LANES:(s + 1) * LANES] = jnp.where(own_o, blk, 0.0).astype(BF16)


def _s5_expand(kt, wt, ot, n_st):
    nblk, ch = kt.shape[:2]
    kdim = ch * LANES
    sdim = ot.shape[3]
    spec = lambda a: pl.BlockSpec((1,) + a.shape[1:], lambda c: (c, 0, 0, 0))
    big = lambda r, cdim: pl.BlockSpec((1, r, cdim), lambda c: (c, 0, 0))
    return pl.pallas_call(
        functools.partial(_s5_expand_kernel, n_st=n_st),
        grid=(nblk,),
        in_specs=[spec(kt), spec(wt), spec(ot)],
        out_specs=[big(kdim, kdim), big(kdim, sdim), big(sdim, kdim)],
        out_shape=[jax.ShapeDtypeStruct((nblk, kdim, kdim), BF16),
                   jax.ShapeDtypeStruct((nblk, kdim, sdim), BF16),
                   jax.ShapeDtypeStruct((nblk, sdim, kdim), BF16)],
        compiler_params=pltpu.CompilerParams(
            dimension_semantics=("arbitrary",), vmem_limit_bytes=VMEM_LIMIT),
        name="s5_tables",
    )(kt, wt, ot)


def _s5_kernel(u_ref, t_ref, win_ref, wout_ref, a_ref, d_ref, y_ref, st_ref, v_ref, xp_ref):
    ch = S5_CHUNK
    nb = u_ref.shape[0]
    n = u_ref.shape[1] // ch
    half = st_ref.shape[2]

    @pl.when(pl.program_id(2) == 0)
    def _():
        st_ref[...] = jnp.zeros_like(st_ref)

    def slabs(b):
        return [u_ref[b, pl.ds(s, n, stride=ch), :] for s in range(ch)]

    for b in range(nb):
        v_ref[b] = _dot(jnp.concatenate(slabs(b), axis=1).astype(BF16), win_ref[0])
    a_r = a_ref[0, 0:1, :]
    a_i = a_ref[0, 1:2, :]

    def body(r, carry):
        out = []
        for b in range(nb):
            x_r, x_i = carry[b]
            xp_ref[b, pl.ds(r, 1), 0:half] = x_r
            xp_ref[b, pl.ds(r, 1), half:2 * half] = x_i
            v_r = v_ref[b, pl.ds(r, 1), 0:half]
            v_i = v_ref[b, pl.ds(r, 1), half:2 * half]
            out.append((a_r * x_r - a_i * x_i + v_r, a_r * x_i + a_i * x_r + v_i))
        return tuple(out)

    init = tuple((st_ref[b, 0:1, :], st_ref[b, 1:2, :]) for b in range(nb))
    last = lax.fori_loop(0, n, body, init, unroll=4)
    d = d_ref[0]
    for b in range(nb):
        st_ref[b, 0:1, :] = last[b][0]
        st_ref[b, 1:2, :] = last[b][1]
        sl = slabs(b)
        ucat = jnp.concatenate(sl, axis=1).astype(BF16)
        kh = ucat.shape[1] // 2
        y = jnp.concatenate([_dot(ucat[:, 0:kh], t_ref[0, 0:kh, 0:kh]), _dot(ucat, t_ref[0, :, kh:])], axis=1)
        y = y + _dot(xp_ref[b].astype(BF16), wout_ref[0])
        for t in range(ch):
            y_ref[b, pl.ds(t, n, stride=ch), :] = y[:, t * LANES:(t + 1) * LANES] + d * sl[t]


def _s5(u, tmat, win, wout, amat, d_skip):
    bsz, seq, dch = u.shape
    nblk = dch // LANES
    tt = min(S5_TILE, seq)
    n = tt // S5_CHUNK
    sdim = win.shape[2]
    nb = next(k for k in (S5_BATCH_ROWS, 2, 1) if bsz % k == 0)
    wspec = lambda a: pl.BlockSpec((1,) + a.shape[1:], lambda b, c, i: (c, 0, 0))
    return pl.pallas_call(
        _s5_kernel,
        grid=(bsz // nb, nblk, seq // tt),
        in_specs=[pl.BlockSpec((nb, tt, LANES), lambda b, c, i: (b, i, c)),
                  wspec(tmat), wspec(win), wspec(wout), wspec(amat),
                  pl.BlockSpec((1, 1, LANES), lambda b, c, i: (c, 0, 0))],
        out_specs=pl.BlockSpec((nb, tt, LANES), lambda b, c, i: (b, i, c)),
        out_shape=jax.ShapeDtypeStruct((bsz, seq, dch), F32),
        scratch_shapes=[pltpu.VMEM((nb, 2, sdim // 2), F32),
                        pltpu.VMEM((nb, n, sdim), F32),
                        pltpu.VMEM((nb, n, sdim), F32)],
        compiler_params=pltpu.CompilerParams(
            dimension_semantics=("parallel", "parallel", "arbitrary"),
            vmem_limit_bytes=VMEM_LIMIT),
        name="s5",
    )(u, tmat, win, wout, amat, d_skip.reshape(nblk, 1, LANES))


def _cumsum_rows(v):
    n = v.shape[0]
    row = lax.broadcasted_iota(jnp.int32, v.shape, 0)
    sh = 1
    while sh < n:
        v = v + jnp.where(row >= sh, pltpu.roll(v, sh, axis=0), 0.0)
        sh *= 2
    return v


def _gdn2_kernel(qkv_ref, z_ref, ab_ref, cw_ref, gp_ref, ng_ref, o_ref,
                 s_ref, xp_ref, gc_ref, gct_ref, kbq_ref, kn_ref, rhs_ref, cq_ref, qkt_ref,
                 lm_ref, d_ref, db_ref, de_ref, ds_ref, pe_ref, val_ref, vn_ref, *, n_heads):
    tt = qkv_ref.shape[1]
    dh = GDN_HEAD_DIM
    bsz = qkv_ref.shape[0]
    dg = n_heads * dh
    halo = SUBLANES
    sb = ds_ref.shape[1]
    nsb = tt // sb
    pairs = [(b, h) for b in range(bsz) for h in range(n_heads)]

    @pl.when(pl.program_id(0) == 0)
    def _():
        s_ref[...] = jnp.zeros_like(s_ref)
        xp_ref[...] = jnp.zeros_like(xp_ref)

    ri = lax.broadcasted_iota(jnp.int32, (tt, tt), 0)
    ci = lax.broadcasted_iota(jnp.int32, (tt, tt), 1)
    lane = lax.broadcasted_iota(jnp.int32, (tt, LANES), 1)
    a_log = gp_ref[0:1, :]
    dt_bias = gp_ref[1:2, :]
    shift_op = jnp.concatenate(
        [jnp.where(ri == ci + j, 1.0, 0.0) for j in range(1, CONV_WIDTH)], axis=0).astype(BF16)

    for b in range(bsz):
        x = qkv_ref[b]
        x32 = x.astype(F32)
        shifted = _dot(shift_op, x)
        edge = jnp.concatenate([xp_ref[b], x32[0:halo]], axis=0)
        conv = cw_ref[CONV_WIDTH - 1:CONV_WIDTH, :] * x32
        for j in range(1, CONV_WIDTH):
            sh_j = jnp.concatenate([edge[halo - j:2 * halo - j],
                                    shifted[(j - 1) * tt + halo:j * tt]], axis=0)
            conv = conv + cw_ref[CONV_WIDTH - 1 - j:CONV_WIDTH - j, :] * sh_j
        xp_ref[b] = x32[tt - halo:tt]
        act = _silu(conv)

        ab = ab_ref[b]
        sp = jnp.maximum(ab + dt_bias, 0.0) + jnp.log1p(jnp.exp(-jnp.abs(ab + dt_bias)))
        g = jnp.where(lane < n_heads, -jnp.exp(a_log) * sp, 0.0)
        gc = _cumsum_rows(g)
        gc_ref[b] = gc
        gct_ref[b] = gc.T
        glast = gc[tt - 1:tt, :]
        egc = jnp.exp(gc)
        ekt = jnp.exp(glast - gc)
        beta_all = _sigmoid(ab)
        for h in range(n_heads):
            p = b * n_heads + h
            q = act[:, h * dh:(h + 1) * dh]
            k = act[:, dg + h * dh:dg + (h + 1) * dh]
            v = act[:, 2 * dg + h * dh:2 * dg + (h + 1) * dh]
            q = q * lax.rsqrt(jnp.sum(q * q, axis=-1, keepdims=True) + EPS) * (dh ** -0.5)
            k = k * lax.rsqrt(jnp.sum(k * k, axis=-1, keepdims=True) + EPS)
            beta = beta_all[:, n_heads + h:n_heads + h + 1]
            eg = egc[:, h:h + 1]
            kb = k * beta
            kbq_ref[p, 0:tt, :] = kb.astype(BF16)
            kbq_ref[p, tt:2 * tt, :] = q.astype(BF16)
            kn_ref[p] = k.astype(BF16)
            rhs_ref[p, :, 0:dh] = (v * beta).astype(BF16)
            rhs_ref[p, :, dh:2 * dh] = (kb * eg).astype(BF16)
            cq_ref[p, tt:2 * tt, :] = (q * eg).astype(BF16)
            qkt_ref[p, tt:tt + dh, :] = (k * ekt[:, h:h + 1]).T.astype(BF16)

    for b, h in pairs:
        p = b * n_heads + h
        m1 = lax.dot_general(kbq_ref[p], kn_ref[p], (((1,), (1,)), ((), ())),
                             preferred_element_type=F32)
        gcol = gc_ref[b, :, h:h + 1]
        grow = gct_ref[b, h:h + 1, :]
        decay = jnp.exp(jnp.where(ri >= ci, gcol - grow, -1e30))
        lm = jnp.where(ri > ci, m1[0:tt] * decay, 0.0)
        lm_ref[p] = lm.astype(BF16)
        qkt_ref[p, 0:tt, :] = (m1[tt:2 * tt] * decay).astype(BF16)
        first = jnp.where(ri == ci + 1, jnp.where((ci & 1) == 0, lm, 0.0), 0.0)
        d = jnp.where(ri == ci, 1.0, 0.0) - first
        ds_ref[p] = sum(d[j * sb:(j + 1) * sb] for j in range(1, nsb)) + d[0:sb]

    in_blk = (ri // sb) == (ci // sb)
    blk_b = jnp.where(in_blk, 1.0, 0.0).astype(BF16)
    m = 2
    sh = 1
    while m < tt:
        rb = ri >> sh
        cb = ci >> sh
        sel = jnp.where(rb == cb + 1, jnp.where((cb & 1) == 0, 1.0, 0.0), 0.0).astype(BF16)
        if 2 * m <= sb:
            for b, h in pairs:
                p = b * n_heads + h
                pe_ref[p] = _dot(ds_ref[p].astype(BF16), lm_ref[p] * sel).astype(BF16)
            for b, h in pairs:
                p = b * n_heads + h
                ds = ds_ref[p]
                dbd = jnp.concatenate([ds.astype(BF16)] * nsb, axis=0) * blk_b
                ds = ds - _dot(pe_ref[p], dbd)
                ds_ref[p] = ds
                if 4 * m > sb:
                    d = jnp.where(in_blk, jnp.concatenate([ds] * nsb, axis=0), 0.0)
                    d_ref[p] = d
                    db_ref[p] = d.astype(BF16)
        else:
            for b, h in pairs:
                p = b * n_heads + h
                de_ref[p] = _dot(db_ref[p], lm_ref[p] * sel).astype(BF16)
            for b, h in pairs:
                p = b * n_heads + h
                d = d_ref[p] - _dot(de_ref[p], db_ref[p])
                d_ref[p] = d
                db_ref[p] = d.astype(BF16)
        m *= 2
        sh += 1

    for b, h in pairs:
        p = b * n_heads + h
        w = _dot(db_ref[p], rhs_ref[p])
        val_ref[p] = w[:, 0:dh]
        cq_ref[p, 0:tt, :] = w[:, dh:2 * dh].astype(BF16)

    for b, h in pairs:
        p = b * n_heads + h
        m2 = _dot(cq_ref[p], s_ref[p].astype(BF16))
        vn_ref[p] = (val_ref[p] - m2[0:tt]).astype(BF16)
        val_ref[p] = m2[tt:2 * tt]
    for b, h in pairs:
        p = b * n_heads + h
        r = _dot(qkt_ref[p], vn_ref[p])
        o = val_ref[p] + r[0:tt]
        egl = jnp.exp(gc_ref[b, tt - 1:tt, h:h + 1])
        s_ref[p] = s_ref[p] * egl + r[tt:tt + dh]
        zh = z_ref[b, :, h * dh:(h + 1) * dh].astype(F32)
        o_ref[b, :, h * dh:(h + 1) * dh] = (_rms(o) * ng_ref[...] * _silu(zh)).astype(o_ref.dtype)


def _gdn2(qkv, z, ab, conv_w, a_log, dt_bias, norm_g):
    bsz, seq, _ = qkv.shape
    dg = z.shape[2]
    dh = GDN_HEAD_DIM
    n_heads = dg // dh
    tt = GDN_BLOCK
    npair = bsz * n_heads
    gp = jnp.zeros((2, LANES), F32).at[0, :n_heads].set(a_log).at[1, :n_heads].set(dt_bias)
    blk = lambda n: pl.BlockSpec((bsz, tt, n), lambda i: (0, i, 0))
    full = lambda a: pl.BlockSpec(a.shape, lambda i: (0,) * a.ndim)
    ng = norm_g.reshape(1, dh)
    return pl.pallas_call(
        functools.partial(_gdn2_kernel, n_heads=n_heads),
        grid=(seq // tt,),
        in_specs=[blk(3 * dg), blk(dg), blk(LANES), full(conv_w), full(gp), full(ng)],
        out_specs=blk(dg),
        out_shape=jax.ShapeDtypeStruct((bsz, seq, dg), BF16),
        scratch_shapes=[pltpu.VMEM((npair, dh, dh), F32),
                        pltpu.VMEM((bsz, SUBLANES, 3 * dg), F32),
                        pltpu.VMEM((bsz, tt, LANES), F32),
                        pltpu.VMEM((bsz, LANES, tt), F32),
                        pltpu.VMEM((npair, 2 * tt, dh), BF16),
                        pltpu.VMEM((npair, tt, dh), BF16),
                        pltpu.VMEM((npair, tt, 2 * dh), BF16),
                        pltpu.VMEM((npair, 2 * tt, dh), BF16),
                        pltpu.VMEM((npair, tt + dh, tt), BF16),
                        pltpu.VMEM((npair, tt, tt), BF16),
                        pltpu.VMEM((npair, tt, tt), F32),
                        pltpu.VMEM((npair, tt, tt), BF16),
                        pltpu.VMEM((npair, tt, tt), BF16),
                        pltpu.VMEM((npair, GDN_SIDE, tt), F32),
                        pltpu.VMEM((npair, GDN_SIDE, tt), BF16),
                        pltpu.VMEM((npair, tt, dh), F32),
                        pltpu.VMEM((npair, tt, dh), BF16)],
        compiler_params=pltpu.CompilerParams(
            dimension_semantics=("arbitrary",), vmem_limit_bytes=VMEM_LIMIT),
        name="gdn",
    )(qkv, z, ab, conv_w, gp, ng)


def _outproj_kernel(ys_ref, yg_ref, x_ref, mod_ref, wglu_ref, bglu_ref, wo_ref,
                    g2_ref, wr_ref, h_ref, m_ref, route_ref, routet_ref, count_ref,
                    cnt_ref, wglub_ref, wob_ref, mt_ref):
    @pl.when((pl.program_id(0) == 0) & (pl.program_id(1) == 0))
    def _():
        wglub_ref[...] = wglu_ref[...].astype(BF16)
        wob_ref[...] = wo_ref[...].astype(BF16)

    @pl.when((pl.program_id(0) == 0) & (pl.program_id(1) == 0))
    def _():
        cnt_ref[...] = jnp.zeros_like(cnt_ref)

    d_s5 = wglub_ref.shape[0]
    tm = x_ref.shape[1]
    hm = tm // OUTPROJ_PARTS
    earlier = jnp.where(lax.broadcasted_iota(jnp.int32, (hm, hm), 0)
                        > lax.broadcasted_iota(jnp.int32, (hm, hm), 1), 1.0, 0.0).astype(BF16)
    lane = lax.broadcasted_iota(jnp.int32, (hm, LANES), 1)
    neg = -1e30
    is_grp = lane < N_EXPERT_GROUPS
    seen = cnt_ref[...]
    for part in range(OUTPROJ_PARTS):
        rows = slice(part * hm, (part + 1) * hm)
        y = jax.nn.gelu(ys_ref[0, rows, :])
        gate = _sigmoid(_dot(y.astype(BF16), wglub_ref[...]) + bglu_ref[...])
        y = (y * gate).astype(BF16)
        mix = _dot(y, wob_ref[0:d_s5, :]) + _dot(yg_ref[0, rows, :], wob_ref[d_s5:, :])
        h = x_ref[0, rows, :] + mod_ref[0, 2:3, :] * mix
        h_ref[0, rows, :] = h
        m = _rms(h) * g2_ref[...] * (1.0 + mod_ref[0, 4:5, :]) + mod_ref[0, 3:4, :]
        _store_token_tiles(mt_ref, (), m, row0=part * hm)

        m_hi = m.astype(BF16)
        m_lo = (m - m_hi.astype(F32)).astype(BF16)
        both = _dot(m_hi, wr_ref[...])
        logits = both[:, 0:LANES] + both[:, LANES:2 * LANES] + _dot(m_lo, wr_ref[:, 0:LANES])
        gl = jnp.where(is_grp, logits, neg)
        gmax = jnp.max(gl, axis=-1, keepdims=True)
        gidx = jnp.min(jnp.where(gl == gmax, lane, LANES), axis=-1, keepdims=True)
        p_grp = 1.0 / jnp.sum(jnp.where(is_grp, jnp.exp(gl - gmax), 0.0), axis=-1, keepdims=True)
        lo = ROUTE_OFF + gidx * EXPERTS_PER_GROUP
        el = jnp.where((lane >= lo) & (lane < lo + EXPERTS_PER_GROUP), logits, neg)
        v1 = jnp.max(el, axis=-1, keepdims=True)
        i1 = jnp.min(jnp.where(el == v1, lane, LANES), axis=-1, keepdims=True)
        el2 = jnp.where(lane == i1, neg, el)
        v2 = jnp.max(el2, axis=-1, keepdims=True)
        i2 = jnp.min(jnp.where(el2 == v2, lane, LANES), axis=-1, keepdims=True)
        t = jnp.exp(v2 - v1)
        w1 = p_grp / (1.0 + t)
        w2 = w1 * t

        oh1 = lane == i1
        oh2 = lane == i2
        cnt = jnp.where(oh1, 1.0, 0.0) + jnp.where(oh2, 1.0, 0.0)
        before = _dot(earlier, cnt.astype(BF16)) + seen
        r1 = jnp.sum(jnp.where(oh1, before, 0.0), axis=-1, keepdims=True)
        r2 = jnp.sum(jnp.where(oh2, before, 0.0), axis=-1, keepdims=True)
        seen = seen + jnp.sum(cnt, axis=0, keepdims=True)
        cols = [(i1 - ROUTE_OFF).astype(F32), (i2 - ROUTE_OFF).astype(F32), w1, w2, r1, r2]
        route = jnp.zeros_like(logits)
        for j, col in enumerate(cols):
            route = jnp.where(lane == j, col, route)
        route_ref[0, rows, :] = route
        routet_ref[:, rows] = route.T[0:SUBLANES, :]
    cnt_ref[...] = seen
    count_ref[0] = seen
    m_ref[0] = mt_ref[...].astype(m_ref.dtype)


def _outproj(ys, yg, x, mod, wglu, bglu, wo, g2, wr):
    bsz, seq, d = x.shape
    tm = ROW_TILE
    full = lambda a: pl.BlockSpec(a.shape, lambda b, i: (0,) * a.ndim, pipeline_mode=pl.Buffered(1))
    row = lambda n: pl.BlockSpec((1, tm, n), lambda b, i: (b, i, 0))
    return pl.pallas_call(
        _outproj_kernel,
        grid=(bsz, seq // tm),
        in_specs=[row(ys.shape[2]), row(yg.shape[2]), row(d),
                  pl.BlockSpec((1, N_MOD, d), lambda b, i: (b, 0, 0)),
                  full(wglu), full(bglu), full(wo), full(g2), full(wr)],
        out_specs=[row(d), pl.BlockSpec((1, tm * (d // LANES), LANES), lambda b, i: (b, i, 0)), row(LANES),
                   pl.BlockSpec((SUBLANES, tm), lambda b, i: (0, b * (seq // tm) + i)),
                   pl.BlockSpec((1, 1, LANES), lambda b, i: (b * (seq // tm) + i, 0, 0))],
        out_shape=[jax.ShapeDtypeStruct((bsz, seq, d), F32),
                   jax.ShapeDtypeStruct((bsz, seq * (d // LANES), LANES), BF16),
                   jax.ShapeDtypeStruct((bsz, seq, LANES), F32),
                   jax.ShapeDtypeStruct((SUBLANES, bsz * seq), F32),
                   jax.ShapeDtypeStruct((bsz * (seq // tm), 1, LANES), F32)],
        scratch_shapes=[pltpu.VMEM((1, LANES), F32), pltpu.VMEM(wglu.shape, BF16), pltpu.VMEM(wo.shape, BF16),
                        pltpu.VMEM((tm * (d // LANES), LANES), F32)],
        compiler_params=pltpu.CompilerParams(
            dimension_semantics=("arbitrary", "arbitrary"), vmem_limit_bytes=VMEM_LIMIT),
        name="outproj",
    )(ys, yg, x, mod, wglu, bglu, wo, g2, wr)


def _moe_scatter_kernel(d1_ref, d2_ref, pad_ref, m_ref, xs_hbm, zero_ref, sem, *, n_exp, k):
    i = pl.program_id(0)
    ts = d1_ref.shape[2]
    trows = zero_ref.shape[0]

    def tok(ref, t):
        return ref.at[pl.ds(pl.multiple_of(t * k, k), k)]

    def tile_copy(j):
        return pltpu.make_async_copy(
            zero_ref, xs_hbm.at[pl.ds(pl.multiple_of(j * trows, trows), trows)], sem.at[0])

    def pad_piece(e, p, act):
        lo = pad_ref[0, e]
        length = pad_ref[1, e] - lo
        start = lo + (length & ~(2 * p - 1))
        copy = pltpu.make_async_copy(
            zero_ref.at[pl.ds(0, p * k)], xs_hbm.at[pl.ds(pl.multiple_of(start * k, k), p * k)], sem.at[0])

        @pl.when((length & p) != 0)
        def _():
            act(copy)

    @pl.when(i == 0)
    def _():
        zero_ref[...] = jnp.zeros_like(zero_ref)
        pieces = [(e, p) for e in range(n_exp) for p in [1 << j for j in range((trows // k).bit_length() - 1)]]
        for e, p in pieces:
            pad_piece(e, p, lambda c: c.start())
        for e, p in pieces:
            pad_piece(e, p, lambda c: c.wait())

        def fill(j, carry):
            tile_copy(j).start()
            return carry

        def drain(j, carry):
            tile_copy(j).wait()
            return carry

        lax.fori_loop(pad_ref[0, n_exp], pad_ref[1, n_exp], fill, 0)
        lax.fori_loop(pad_ref[0, n_exp], pad_ref[1, n_exp], drain, 0)

    def row_copy(t, dst, slot):
        return pltpu.make_async_copy(tok(m_ref, t), tok(xs_hbm, dst), sem.at[1 + slot])

    def issue(t, carry):
        row_copy(t, d1_ref[0, 0, t], 0).start(priority=0)
        row_copy(t, d2_ref[0, 0, t], 1).start(priority=1)
        return carry

    lax.fori_loop(0, ts, issue, 0, unroll=8)
    for slot in range(2):
        pltpu.make_async_copy(m_ref, xs_hbm.at[pl.ds(0, ts * k)], sem.at[1 + slot]).wait()


def _moe_scatter(m, dest1, dest2, pads, n_rows, k):
    n_tok = m.shape[0] // k
    ts = MOE_SCATTER_TILE
    n_exp = pads.shape[1] - 1
    dspec = pl.BlockSpec((1, 1, ts), lambda i: (i, 0, 0), memory_space=pltpu.SMEM)
    return pl.pallas_call(
        functools.partial(_moe_scatter_kernel, n_exp=n_exp, k=k),
        grid=(n_tok // ts,),
        in_specs=[dspec, dspec,
                  pl.BlockSpec(pads.shape, lambda i: (0, 0), memory_space=pltpu.SMEM),
                  pl.BlockSpec((ts * k, LANES), lambda i: (i, 0))],
        out_specs=pl.BlockSpec(memory_space=pl.ANY),
        out_shape=jax.ShapeDtypeStruct((n_rows * k, LANES), m.dtype),
        scratch_shapes=[pltpu.VMEM((MOE_ROWS * k, LANES), m.dtype), pltpu.SemaphoreType.DMA((3,))],
        compiler_params=pltpu.CompilerParams(
            dimension_semantics=("arbitrary",), vmem_limit_bytes=VMEM_LIMIT, has_side_effects=True),
        name="moe_scatter",
    )(dest1.reshape(n_tok // ts, 1, ts), dest2.reshape(n_tok // ts, 1, ts), pads, m)


def _moe_ffn_kernel(te_ref, na_ref, xs_ref, wg_ref, wu_ref, wd_ref, ys_ref, wgb_ref, wub_ref, wdb_ref,
                    tile_ref):
    i = pl.program_id(0)
    active = i < na_ref[0]
    changed = (i == 0) | (te_ref[i] != te_ref[jnp.maximum(i - 1, 0)])

    @pl.when(active & changed)
    def _():
        wgb_ref[...] = wg_ref[0].astype(BF16)
        wub_ref[...] = wu_ref[0].astype(BF16)
        wdb_ref[...] = wd_ref[0].astype(BF16)

    @pl.when(active)
    def _():
        k = wgb_ref.shape[0] // LANES
        tile_ref[...] = xs_ref[...].astype(F32)
        x = _load_token_tiles(tile_ref, (), xs_ref.shape[0] // k, k).astype(BF16)
        mid = (_silu(_dot(x, wgb_ref[...])) * _dot(x, wub_ref[...])).astype(BF16)
        _store_token_tiles(tile_ref, (), _dot(mid, wdb_ref[...]))
        ys_ref[...] = tile_ref[...].astype(ys_ref.dtype)


def _moe_ffn(xs, tile_expert, n_active, wg, wu, wd):
    _, d, de = wg.shape
    tmr = MOE_ROWS * (d // LANES)
    n_rows, dp = xs.shape
    grid_spec = pltpu.PrefetchScalarGridSpec(
        num_scalar_prefetch=2,
        grid=(n_rows // tmr,),
        in_specs=[pl.BlockSpec((tmr, dp), lambda i, te, na: (jnp.minimum(i, na[0] - 1), 0)),
                  pl.BlockSpec((1, d, de), lambda i, te, na: (te[i], 0, 0)),
                  pl.BlockSpec((1, d, de), lambda i, te, na: (te[i], 0, 0)),
                  pl.BlockSpec((1, de, d), lambda i, te, na: (te[i], 0, 0))],
        out_specs=pl.BlockSpec((tmr, dp), lambda i, te, na: (jnp.minimum(i, na[0] - 1), 0)),
        scratch_shapes=[pltpu.VMEM((d, de), BF16), pltpu.VMEM((d, de), BF16), pltpu.VMEM((de, d), BF16),
                        pltpu.VMEM((tmr, dp), F32)],
    )
    return pl.pallas_call(
        _moe_ffn_kernel,
        grid_spec=grid_spec,
        out_shape=jax.ShapeDtypeStruct((n_rows, dp), xs.dtype),
        input_output_aliases={2: 0},
        compiler_params=pltpu.CompilerParams(
            dimension_semantics=("arbitrary",), vmem_limit_bytes=VMEM_LIMIT),
        name="moe_ffn",
    )(tile_expert, n_active, xs, wg, wu, wd)


def _moe_combine_kernel(d1_ref, d2_ref, n1_ref, n2_ref, h_ref, route_ref, mod_ref, nf_ref, ys_hbm, o_ref,
                        b1_ref, b2_ref, tile_ref, sem, *, final_norm):
    tm = h_ref.shape[1]
    k = h_ref.shape[2] // LANES
    g = pl.program_id(0)
    slot = g % 2

    def tok(ref, t):
        return ref.at[pl.ds(pl.multiple_of(t * k, k), k)]

    def gather(i1_ref, i2_ref, s):
        def issue(t, carry):
            pltpu.make_async_copy(tok(ys_hbm, i1_ref[0, 0, t]), tok(b1_ref.at[s], t),
                                  sem.at[s, 0]).start(priority=0)
            pltpu.make_async_copy(tok(ys_hbm, i2_ref[0, 0, t]), tok(b2_ref.at[s], t),
                                  sem.at[s, 1]).start(priority=1)
            return carry

        lax.fori_loop(0, tm, issue, 0, unroll=8)

    @pl.when(g == 0)
    def _():
        gather(d1_ref, d2_ref, 0)

    @pl.when(g + 1 < pl.num_programs(0))
    def _():
        gather(n1_ref, n2_ref, 1 - slot)

    pltpu.make_async_copy(ys_hbm.at[pl.ds(0, tm * k)], b1_ref.at[slot], sem.at[slot, 0]).wait()
    pltpu.make_async_copy(ys_hbm.at[pl.ds(0, tm * k)], b2_ref.at[slot], sem.at[slot, 1]).wait()
    route = route_ref[0]
    tile_ref[...] = b1_ref[slot].astype(F32)
    moe = route[:, 2:3] * _load_token_tiles(tile_ref, (), tm, k)
    tile_ref[...] = b2_ref[slot].astype(F32)
    moe = moe + route[:, 3:4] * _load_token_tiles(tile_ref, (), tm, k)
    h = h_ref[0] + mod_ref[0, 5:6, :] * moe
    if final_norm:
        h = _rms(h) * nf_ref[...]
    o_ref[0] = h


def _moe_combine(ys, dest1, dest2, h, route, mod, nf, final_norm):
    bsz, seq, d = h.shape
    tm = MOE_COMBINE_TILE
    nt = seq // tm
    steps = bsz * nt
    dspec = pl.BlockSpec((1, 1, tm), lambda g: (g, 0, 0), memory_space=pltpu.SMEM)
    nspec = pl.BlockSpec((1, 1, tm), lambda g: (jnp.minimum(g + 1, steps - 1), 0, 0), memory_space=pltpu.SMEM)
    row = lambda n: pl.BlockSpec((1, tm, n), lambda g: (g // nt, g % nt, 0))
    d1 = dest1.reshape(steps, 1, tm)
    d2 = dest2.reshape(steps, 1, tm)
    return pl.pallas_call(
        functools.partial(_moe_combine_kernel, final_norm=final_norm),
        grid=(steps,),
        in_specs=[dspec, dspec, nspec, nspec, row(d), row(LANES),
                  pl.BlockSpec((1, N_MOD, d), lambda g: (g // nt, 0, 0)),
                  pl.BlockSpec(nf.shape, lambda g: (0, 0)),
                  pl.BlockSpec(memory_space=pl.ANY)],
        out_specs=row(d),
        out_shape=jax.ShapeDtypeStruct((bsz, seq, d), F32),
        scratch_shapes=[pltpu.VMEM((2, tm * (d // LANES), LANES), ys.dtype),
                        pltpu.VMEM((2, tm * (d // LANES), LANES), ys.dtype),
                        pltpu.VMEM((tm * (d // LANES), LANES), F32),
                        pltpu.SemaphoreType.DMA((2, 2))],
        compiler_params=pltpu.CompilerParams(
            dimension_semantics=("arbitrary",), vmem_limit_bytes=VMEM_LIMIT),
        name="moe_combine",
    )(d1, d2, d1, d2, h, route, mod, nf, ys)


def _moe_sparse(m, route, route_t, counts, wg, wu, wd, h, mod, nf, final_norm):
    bsz, seq, d = h.shape
    n_tok = bsz * seq
    tmr = MOE_ROWS
    n_exp = wg.shape[0]
    n_tiles = (2 * n_tok) // tmr + n_exp
    i32 = jnp.int32
    e1, e2, k1, k2 = (route_t[j].astype(i32) for j in (0, 1, 4, 5))
    cnt = counts[-1, 0, ROUTE_OFF:ROUTE_OFF + n_exp].astype(i32)
    ptiles = (cnt + tmr - 1) // tmr
    tend = jnp.cumsum(ptiles)
    gstart = (tend - ptiles) * tmr
    eids = jnp.arange(n_exp, dtype=i32)
    dest1 = k1 + jnp.sum(jnp.where(e1[:, None] == eids[None, :], gstart[None, :], 0), axis=1)
    dest2 = k2 + jnp.sum(jnp.where(e2[:, None] == eids[None, :], gstart[None, :], 0), axis=1)
    n_active = tend[-1:]
    tile_expert = jnp.sum(jnp.arange(n_tiles, dtype=i32)[:, None] >= tend[None, :], axis=1).astype(i32)
    last_expert = jnp.sum(n_active - 1 >= tend).astype(i32)
    tile_expert = jnp.minimum(tile_expert, last_expert)
    pads = jnp.stack([jnp.concatenate([gstart + cnt, n_active]),
                      jnp.concatenate([gstart + ptiles * tmr, jnp.full((1,), n_tiles, i32)])]).astype(i32)
    k = d // LANES
    xs = _moe_scatter(m.reshape(n_tok * k, LANES), dest1, dest2, pads, n_tiles * tmr, k)
    ys = _moe_ffn(xs, tile_expert, n_active.astype(i32), wg, wu, wd)
    return _moe_combine(ys, dest1, dest2, h, route, mod, nf, final_norm)


def kernel(x, c, norm1_g, norm2_g, w_ada, b_ada, w_in, lam_re, lam_im, log_step, s5_b_re, s5_b_im,
           s5_c_re, s5_c_im, s5_d, w_glu, b_glu, conv_w, a_log, dt_bias, gdn_norm_g, w_out,
           w_router_grp, w_router_exp, w_gate, w_up, w_down, normf_g):
    bsz, seq, d = x.shape
    depth = w_ada.shape[0]
    d_s5 = s5_d.shape[1]
    d_gdn = w_out.shape[1] - d_s5
    n_heads = d_gdn // GDN_HEAD_DIM
    assert seq % min(S5_TILE, seq) == 0 and seq % ROW_TILE == 0 and seq % GDN_BLOCK == 0
    assert seq % MOE_COMBINE_TILE == 0 and (bsz * seq) % MOE_SCATTER_TILE == 0
    assert d % LANES == 0 and d_s5 % LANES == 0 and d_gdn % GDN_HEAD_DIM == 0
    assert 2 * n_heads <= LANES and N_EXPERT_GROUPS + N_EXPERTS <= LANES
    assert w_gate.shape[1] == N_EXPERTS and w_router_grp.shape[2] == N_EXPERT_GROUPS

    h = x
    for l in range(depth):
        mod = _ada(c, w_ada[l], b_ada[l]).reshape(bsz, N_MOD, d)
        wi = w_in[l]
        c0, c1, c2 = d_s5, d_s5 + 3 * d_gdn, d_s5 + 4 * d_gdn
        u, qkv, z, ab = _inproj(h, mod, norm1_g[l].reshape(1, d), wi, c0, c1 - c0, c2 - c1)
        tabs = _s5_tables(lam_re[l], lam_im[l], log_step[l], s5_b_re[l], s5_b_im[l],
                          s5_c_re[l], s5_c_im[l])
        ys = _s5(u, *tabs, s5_d[l])
        yg = _gdn2(qkv, z, ab, conv_w[l], a_log[l], dt_bias[l], gdn_norm_g[l])
        wr = jnp.zeros((d, LANES), F32)
        wr = wr.at[:, :N_EXPERT_GROUPS].set(w_router_grp[l])
        wr = wr.at[:, ROUTE_OFF:ROUTE_OFF + N_EXPERTS].set(w_router_exp[l])
        wr_hi = wr.astype(BF16)
        wr = jnp.concatenate([wr_hi, (wr - wr_hi.astype(F32)).astype(BF16)], axis=1)
        h1, m, route, route_t, counts = _outproj(ys, yg, h, mod, w_glu[l], b_glu[l].reshape(1, d_s5),
                                                 w_out[l], norm2_g[l].reshape(1, d), wr)
        h = _moe_sparse(m, route, route_t, counts, w_gate[l], w_up[l], w_down[l], h1, mod,
                        normf_g.reshape(1, d), final_norm=(l == depth - 1))
    return h
```
